```python
import jax, jax.numpy as jnp
from jax import lax
import numpy as np

D_MODEL = 1024
BATCH = 16
SEQ = 256
DEPTH = 1
DEC_BATCH = 2
DEC_SEQ = 2048
PAST_LEN = 256

GRID_W = 64
HEAD_SIZE = 64
D_RWKV = D_MODEL // 2
N_HEADS_RWKV = D_RWKV // HEAD_SIZE
D_CONV = D_MODEL // 2
CONV_WIDTH = 3
LORA_DECAY = 64
LORA_ICLR = 64
D_FF = 2816
N_DIR = 2
N_MOD = 9
N_NORMS = 6
EPS_RMS = 1e-6
EPS_GN = 64e-5
HALF_STEP = 0.5
D_IN = 4 * D_RWKV + 3 * D_CONV + 2 * D_MODEL

kernel_name = "bidir_rwkv7_shortconv_macaron_prefix_dit_step"


def rms_norm(x, g):
    xf = x.astype(jnp.float32)
    y = xf * lax.rsqrt(jnp.mean(xf * xf, -1, keepdims=True) + EPS_RMS)
    return (y * g.astype(jnp.float32)).astype(x.dtype)


def modulate(h, shift, scale):
    return h * (1 + scale) + shift


def swiglu(h, w13, w2):
    gt, up = jnp.split(h @ w13, 2, -1)
    return (jax.nn.silu(gt) * up) @ w2


def token_shift(h, direction):
    if direction == 0:
        return jnp.pad(h[:, :-1], ((0, 0), (1, 0), (0, 0)))
    return jnp.pad(h[:, 1:], ((0, 0), (0, 1), (0, 0)))


def conv3_centred(u, w, b, axis):
    n = u.shape[axis]
    pad = [(0, 0)] * u.ndim
    pad[axis] = (1, 1)
    up = jnp.pad(u, pad)
    left = lax.slice_in_dim(up, 0, n, axis=axis)
    mid = lax.slice_in_dim(up, 1, n + 1, axis=axis)
    right = lax.slice_in_dim(up, 2, n + 2, axis=axis)
    return left * w[0] + mid * w[1] + right * w[2] + b


def rwkv7_scan(r, w, k, v, kk, a, s0, reverse):
    def step(S, inp):
        r_t, w_t, k_t, v_t, kk_t, a_t = inp
        sa = jnp.einsum('bhij,bhj->bhi', S, -kk_t)
        S = (S * w_t[:, :, None, :]
             + sa[..., None] * (kk_t * a_t)[:, :, None, :]
             + v_t[..., None] * k_t[:, :, None, :])
        return S, jnp.einsum('bhij,bhj->bhi', S, r_t)
    xs = tuple(jnp.moveaxis(t, 1, 0) for t in (r, w, k, v, kk, a))
    s_final, y = lax.scan(step, s0, xs, reverse=reverse)
    return jnp.moveaxis(y, 0, 1), s_final


def token_mixer(h, s0, p, latent):
    B, T, _ = h.shape
    H, N = N_HEADS_RWKV, HEAD_SIZE
    f32 = jnp.float32
    sizes = (D_RWKV,) * 4 + (D_CONV,) * 3 + (D_MODEL,) * 2
    idx = [int(i) for i in np.cumsum(sizes)[:-1]]
    r, k, v, g, cb, cc, xc, gate_a, gate_b = jnp.split(h @ p['w_in'], idx, -1)

    def heads(t):
        return t.astype(f32).reshape(B, T, H, N)
    r_h, k_h, v_h = heads(r), heads(k), heads(v)
    kk = k_h * p['k_k'].astype(f32).reshape(H, N)
    kk = kk * lax.rsqrt(jnp.sum(kk * kk, -1, keepdims=True) + 1e-12)
    k_a = p['k_a'].astype(f32).reshape(H, N)
    r_k = p['r_k'].astype(f32)
    mu = p['mu_shift']
    ys, finals, bonus = [], [], []
    for d in range(N_DIR):
        sh = token_shift(h, d) - h
        xw = h + mu[d, 0] * sh
        xa = h + mu[d, 1] * sh
        w_log = -jax.nn.softplus(-(p['decay_w0'][d] + jnp.tanh(xw @ p['decay_w1'][d]) @ p['decay_w2'][d]).astype(f32)) - 0.5
        decay = heads(jnp.exp(-jnp.exp(w_log)))
        a_d = heads(jax.nn.sigmoid((p['iclr_a0'][d] + (xa @ p['iclr_a1'][d]) @ p['iclr_a2'][d]).astype(f32)))
        k_d = k_h * (1 + (a_d - 1) * k_a)
        y_d, s_d = rwkv7_scan(r_h, decay, k_d, v_h, kk, a_d, s0[:, d].astype(f32), reverse=(d == 1))
        ys.append(y_d)
        finals.append(s_d)
        bonus.append(jnp.sum(r_h * k_d * r_k, -1, keepdims=True))
    y = ys[0] + ys[1]
    mean = jnp.mean(y, -1, keepdims=True)
    var = jnp.var(y, -1, keepdims=True)
    y = ((y - mean) * lax.rsqrt(var + EPS_GN)).reshape(B, T, D_RWKV)
    y = y * p['gn_gain'].astype(f32) + p['gn_bias'].astype(f32)
    y = y + ((bonus[0] + bonus[1]) * v_h).reshape(B, T, D_RWKV)
    y = y * jax.nn.sigmoid(g.astype(f32))
    y_a = y.astype(h.dtype) @ p['w_branch_a']
    new_state = jnp.stack(finals, 1)

    u = cc * xc
    if latent:
        rows = T // GRID_W
        conv = conv3_centred(u.reshape(B, rows, GRID_W, D_CONV), p['conv_w'], p['conv_b'], 2).reshape(B, T, D_CONV)
    else:
        conv = conv3_centred(u, p['conv_w'], p['conv_b'], 1)
    y_b = (cb * conv) @ p['w_branch_b']

    merged = jax.nn.sigmoid(gate_a) * y_a + jax.nn.sigmoid(gate_b) * y_b
    return merged @ p['w_out'], new_state


def trunk_layer(x, mod, s0, p, latent):
    m = jnp.split(mod, N_MOD, -1)
    gn = p['norm_g']
    h = modulate(rms_norm(x, gn[0]), m[0], m[1])
    x = x + HALF_STEP * m[2] * rms_norm(swiglu(h, p['ffn1_w13'], p['ffn1_w2']), gn[1])
    h = modulate(rms_norm(x, gn[2]), m[3], m[4])
    out, s_fin = token_mixer(h, s0, p, latent)
    x = x + m[5] * rms_norm(out, gn[3])
    h = modulate(rms_norm(x, gn[4]), m[6], m[7])
    x = x + HALF_STEP * m[8] * rms_norm(swiglu(h, p['ffn2_w13'], p['ffn2_w2']), gn[5])
    return x, s_fin


def setup_inputs(seed: int = 0) -> dict:
    key = jax.random.key(seed)
    ks = jax.random.split(key, 32)
    nrm = lambda k, shape, s: jax.random.normal(k, shape, jnp.float32) * s
    L, D = DEPTH, D_MODEL
    return {
        'x_prompt': nrm(ks[0], (BATCH, SEQ, D), 1.0),
        'x_sample': nrm(ks[1], (DEC_BATCH, DEC_SEQ, D), 1.0),
        'c': nrm(ks[2], (DEC_BATCH, D), 1.0),
        'state_rwkv': nrm(ks[3], (DEC_BATCH, L, N_DIR, N_HEADS_RWKV, HEAD_SIZE, HEAD_SIZE), 0.5),
        'c_ctx': nrm(ks[4], (D,), 1.0),
        'w_mod': nrm(ks[5], (L, D, N_MOD * D), 0.5 * D ** -0.5),
        'b_mod': nrm(ks[6], (L, N_MOD * D), 0.02),
        'norm_g': 1.0 + nrm(ks[7], (L, N_NORMS, D), 0.02),
        'ffn1_w13': nrm(ks[8], (L, D, 2 * D_FF), D ** -0.5),
        'ffn1_w2': nrm(ks[9], (L, D_FF, D), D_FF ** -0.5),
        'ffn2_w13': nrm(ks[10], (L, D, 2 * D_FF), D ** -0.5),
        'ffn2_w2': nrm(ks[11], (L, D_FF, D), D_FF ** -0.5),
        'w_in': nrm(ks[12], (L, D, D_IN), D ** -0.5),
        'mu_shift': jax.random.uniform(ks[13], (L, N_DIR, 2, D), jnp.float32),
        'decay_w0': 1.0 + nrm(ks[14], (L, N_DIR, D_RWKV), 0.5),
        'decay_w1': nrm(ks[15], (L, N_DIR, D, LORA_DECAY), 0.3 * D ** -0.5),
        'decay_w2': nrm(ks[16], (L, N_DIR, LORA_DECAY, D_RWKV), 0.3 * LORA_DECAY ** -0.5),
        'iclr_a0': nrm(ks[17], (L, N_DIR, D_RWKV), 0.3),
        'iclr_a1': nrm(ks[18], (L, N_DIR, D, LORA_ICLR), 0.3 * D ** -0.5),
        'iclr_a2': nrm(ks[19], (L, N_DIR, LORA_ICLR, D_RWKV), 0.3 * LORA_ICLR ** -0.5),
        'k_k': 0.85 + nrm(ks[20], (L, D_RWKV), 0.05),
        'k_a': 1.0 + nrm(ks[21], (L, D_RWKV), 0.05),
        'r_k': nrm(ks[22], (L, N_HEADS_RWKV, HEAD_SIZE), 0.1),
        'gn_gain': 1.0 + nrm(ks[23], (L, D_RWKV), 0.02),
        'gn_bias': nrm(ks[24], (L, D_RWKV), 0.02),
        'conv_w': nrm(ks[25], (L, CONV_WIDTH, D_CONV), CONV_WIDTH ** -0.5),
        'conv_b': nrm(ks[26], (L, D_CONV), 0.02),
        'w_branch_a': nrm(ks[27], (L, D_RWKV, D), D_RWKV ** -0.5),
        'w_branch_b': nrm(ks[28], (L, D_CONV, D), D_CONV ** -0.5),
        'w_out': nrm(ks[29], (L, D, D), D ** -0.5),
    }


def reference(x_prompt, x_sample, c, state_rwkv, c_ctx, w_mod, b_mod, norm_g,
              ffn1_w13, ffn1_w2, ffn2_w13, ffn2_w2, w_in, mu_shift,
              decay_w0, decay_w1, decay_w2, iclr_a0, iclr_a1, iclr_a2,
              k_k, k_a, r_k, gn_gain, gn_bias, conv_w, conv_b,
              w_branch_a, w_branch_b, w_out):
    y_prompt, y_sample = x_prompt, x_sample
    ctx_states = []
    for l in range(DEPTH):
        p = {
            'norm_g': norm_g[l], 'ffn1_w13': ffn1_w13[l], 'ffn1_w2': ffn1_w2[l],
            'ffn2_w13': ffn2_w13[l], 'ffn2_w2': ffn2_w2[l], 'w_in': w_in[l],
            'mu_shift': mu_shift[l], 'decay_w0': decay_w0[l], 'decay_w1': decay_w1[l],
            'decay_w2': decay_w2[l], 'iclr_a0': iclr_a0[l], 'iclr_a1': iclr_a1[l],
            'iclr_a2': iclr_a2[l], 'k_k': k_k[l], 'k_a': k_a[l], 'r_k': r_k[l],
            'gn_gain': gn_gain[l], 'gn_bias': gn_bias[l], 'conv_w': conv_w[l],
            'conv_b': conv_b[l], 'w_branch_a': w_branch_a[l], 'w_branch_b': w_branch_b[l],
            'w_out': w_out[l],
        }
        mod_ctx = (jax.nn.silu(c_ctx)[None] @ w_mod[l] + b_mod[l])[:, None, :]
        mod_lat = (jax.nn.silu(c) @ w_mod[l] + b_mod[l])[:, None, :]
        s0_ctx = jnp.zeros((y_prompt.shape[0], N_DIR, N_HEADS_RWKV, HEAD_SIZE, HEAD_SIZE), jnp.float32)
        y_prompt, s_ctx = trunk_layer(y_prompt, mod_ctx, s0_ctx, p, latent=False)
        ctx_states.append(s_ctx)
        y_sample, _ = trunk_layer(y_sample, mod_lat, state_rwkv[:, l], p, latent=True)
    new_state_rwkv = jnp.stack(ctx_states, 1)
    return (y_prompt, y_sample, new_state_rwkv)
```

```python
import functools

import jax
import jax.numpy as jnp
from jax import lax
from jax.experimental import pallas as pl
from jax.experimental.pallas import tpu as pltpu

F32 = jnp.float32
BF16 = jnp.bfloat16

D_MODEL = 1024
D_FF = 2816
D_RWKV = 512
D_CONV = 512
HEAD = 64
HEAD_SHIFT = 6
D_IN = 4 * D_RWKV + 3 * D_CONV + 2 * D_MODEL
N_MOD = 9
EPS_RMS = 1e-6
EPS_GN = 64e-5
HALF_STEP = 0.5
EXP_M05 = 0.6065306597126334

N_CTX_SEQ = 16
CTX_LEN = 256
N_LAT_SEQ = 2
LAT_LEN = 2048
GRID_W = 64
N_TOK = N_CTX_SEQ * CTX_LEN + N_LAT_SEQ * LAT_LEN

TM = 256
N_CTX_TILES = N_CTX_SEQ * CTX_LEN // TM
LAT_TILES = LAT_LEN // TM
N_TILES = N_TOK // TM
CHUNK = 64
N_CHUNK = TM // CHUNK
LANES = 128
N_PAIR = D_RWKV // LANES
MOD_ROWS = 8
VMEM_LIMIT = 56 * 1024 * 1024

NN = (((1,), (0,)), ((), ()))
NT = (((1,), (1,)), ((), ()))
TN = (((0,), (0,)), ((), ()))


def _dot(a, b, dims=NN):
    return lax.dot_general(a.astype(BF16), b.astype(BF16), dims, preferred_element_type=F32)


def _rms(x, g):
    ms = jnp.mean(x * x, axis=-1, keepdims=True)
    return x * lax.rsqrt(ms + EPS_RMS) * g


def _split2(x):
    hi = x.astype(BF16)
    lo = (x - hi.astype(F32)).astype(BF16)
    return hi, lo


def _head_ones():
    i = lax.broadcasted_iota(jnp.int32, (D_RWKV, D_RWKV), 0) >> HEAD_SHIFT
    j = lax.broadcasted_iota(jnp.int32, (D_RWKV, D_RWKV), 1) >> HEAD_SHIFT
    return jnp.where(i == j, 1.0, 0.0).astype(BF16)


def _head_sum(x, ones):
    hi, lo = _split2(x)
    return (jnp.dot(hi, ones, preferred_element_type=F32)
            + jnp.dot(lo, ones, preferred_element_type=F32))


def _tile_info(t):
    is_lat = t >= N_CTX_TILES
    i_in = (t - N_CTX_TILES) % LAT_TILES
    first = jnp.logical_or(jnp.logical_not(is_lat), i_in == 0)
    last = jnp.logical_or(jnp.logical_not(is_lat), i_in == LAT_TILES - 1)
    return is_lat, first, last


def _mod_row(t):
    return jnp.where(t < N_CTX_TILES, 0, 1 + (t - N_CTX_TILES) // LAT_TILES)


def _mirror_tile(t):
    u = t - N_CTX_TILES
    return jnp.where(t < N_CTX_TILES, t,
                     N_CTX_TILES + (u // LAT_TILES) * LAT_TILES + (LAT_TILES - 1 - u % LAT_TILES))


def _seq_id(t):
    return jnp.where(t < N_CTX_TILES, t, N_CTX_TILES + (t - N_CTX_TILES) // LAT_TILES)


def _const_spec(shape):
    nd = len(shape)
    return pl.BlockSpec(shape, lambda *_: (0,) * nd, pipeline_mode=pl.Buffered(1))


def _params(n_axes=1):
    return pltpu.CompilerParams(dimension_semantics=("arbitrary",) * n_axes,
                                vmem_limit_bytes=VMEM_LIMIT)


def _mod_kernel(c_ref, w_ref, b_ref, o_ref):
    c = c_ref[...]
    s = c * jax.nn.sigmoid(c)
    o_ref[...] = _dot(s, w_ref[...]) + b_ref[...]


def _modulation(cvec, w_mod, b_mod):
    nblk = N_MOD
    return pl.pallas_call(
        _mod_kernel,
        grid=(nblk,),
        in_specs=[pl.BlockSpec((MOD_ROWS, D_MODEL), lambda j: (0, 0)),
                  pl.BlockSpec((D_MODEL, D_MODEL), lambda j: (0, j)),
                  pl.BlockSpec((1, D_MODEL), lambda j: (0, j))],
        out_specs=pl.BlockSpec((MOD_ROWS, D_MODEL), lambda j: (0, j)),
        out_shape=jax.ShapeDtypeStruct((MOD_ROWS, N_MOD * D_MODEL), F32),
        compiler_params=_params(),
        name="modulation",
    )(cvec, w_mod, b_mod)


def _ffn_kernel(x_ref, mod_ref, g_ref, w13_ref, w2_ref, o_ref, *, im, ig):
    x = x_ref[...]
    shift = mod_ref[im:im + 1, :]
    scale = mod_ref[im + 1:im + 2, :]
    gate = mod_ref[im + 2:im + 3, :]
    h = _rms(x, g_ref[ig:ig + 1, :]) * (1.0 + scale) + shift
    gu = _dot(h, w13_ref[...])
    gt = gu[:, :D_FF]
    up = gu[:, D_FF:]
    act = gt * jax.nn.sigmoid(gt) * up
    o = _dot(act, w2_ref[...])
    o_ref[...] = x + HALF_STEP * gate * _rms(o, g_ref[ig + 1:ig + 2, :])


def _ffn(x, mod3, norm_g, w13, w2, im, ig, name):
    return pl.pallas_call(
        functools.partial(_ffn_kernel, im=im, ig=ig),
        grid=(N_TILES,),
        in_specs=[pl.BlockSpec((TM, D_MODEL), lambda t: (t, 0)),
                  pl.BlockSpec((None, N_MOD, D_MODEL), lambda t: (_mod_row(t), 0, 0)),
                  _const_spec(norm_g.shape),
                  _const_spec(w13.shape),
                  _const_spec(w2.shape)],
        out_specs=pl.BlockSpec((TM, D_MODEL), lambda t: (t, 0)),
        out_shape=jax.ShapeDtypeStruct((N_TOK, D_MODEL), F32),
        compiler_params=_params(),
        name=name,
    )(x, mod3, norm_g, w13, w2)


def _front_kernel(x_ref, xp_ref, xn_ref, mod_ref, g_ref, win_ref, mu_ref,
                  dw0_ref, dw1_ref, dw2_ref, ia0_ref, ia1_ref, ia2_ref,
                  kkw_ref, ka_ref, rk_ref, cw_ref, cbias_ref, wbb_ref,
                  r_o, v_o, kk_o, lw0_o, lw1_o, b0_o, b1_o, k0_o, k1_o,
                  bv_o, sg_o, sa_o, mb_o):
    t = pl.program_id(0)
    is_lat, first, last = _tile_info(t)
    shift = mod_ref[3:4, :]
    scale = mod_ref[4:5, :]
    g2 = g_ref[2:3, :]

    def pre(x):
        return _rms(x, g2) * (1.0 + scale) + shift

    h = pre(x_ref[...])
    hp = jnp.where(first, 0.0, pre(xp_ref[...])[7:8, :])
    hn = jnp.where(last, 0.0, pre(xn_ref[...])[0:1, :])
    row = lax.broadcasted_iota(jnp.int32, (TM, 1), 0)
    prev = jnp.where(row == 0, hp, pltpu.roll(h, 1, 0))
    nxt = jnp.where(row == TM - 1, hn, pltpu.roll(h, TM - 1, 0))

    proj = _dot(h, win_ref[...])
    r = proj[:, 0:512]
    k = proj[:, 512:1024]
    v = proj[:, 1024:1536]
    g = proj[:, 1536:2048]
    cgate = proj[:, 2048:2560]
    cc = proj[:, 2560:3072]
    xc = proj[:, 3072:3584]
    gate_a = proj[:, 3584:4608]
    gate_b = proj[:, 4608:5632]

    ones = _head_ones()
    kk = k * kkw_ref[...]
    kkn = kk * lax.rsqrt(_head_sum(kk * kk, ones) + 1e-12)
    ka = ka_ref[...]

    r_o[...] = r
    v_o[...] = v
    kk_o[...] = kkn
    sg_o[...] = jax.nn.sigmoid(g)
    sa_o[...] = jax.nn.sigmoid(gate_a)

    ksum = None
    for d, (src, lw_o, b_o, k_o) in enumerate(((prev, lw0_o, b0_o, k0_o), (nxt, lw1_o, b1_o, k1_o))):
        sh = src - h
        xw = h + mu_ref[2 * d:2 * d + 1, :] * sh
        xa = h + mu_ref[2 * d + 1:2 * d + 2, :] * sh
        z = dw0_ref[d:d + 1, :] + _dot(jnp.tanh(_dot(xw, dw1_ref[d])), dw2_ref[d])
        lw_o[...] = -EXP_M05 * jax.nn.sigmoid(z)
        a = jax.nn.sigmoid(ia0_ref[d:d + 1, :] + _dot(_dot(xa, ia1_ref[d]), ia2_ref[d]))
        k_d = k * (1.0 + (a - 1.0) * ka)
        b_o[...] = kkn * a
        k_o[...] = k_d
        ksum = k_d if ksum is None else ksum + k_d

    bv_o[...] = _head_sum(r * ksum * rk_ref[...], ones) * v

    u = cc * xc
    col = row & (GRID_W - 1)
    zl = jnp.logical_or(row == 0, jnp.logical_and(is_lat, col == 0))
    zr = jnp.logical_or(row == TM - 1, jnp.logical_and(is_lat, col == GRID_W - 1))
    left = jnp.where(zl, 0.0, pltpu.roll(u, 1, 0))
    right = jnp.where(zr, 0.0, pltpu.roll(u, TM - 1, 0))
    conv = left * cw_ref[0:1, :] + u * cw_ref[1:2, :] + right * cw_ref[2:3, :] + cbias_ref[...]
    y_b = _dot(cgate * conv, wbb_ref[...])
    mb_o[...] = jax.nn.sigmoid(gate_b) * y_b


def _front(x, mod3, norm_g, w_in, mu, dw0, dw1, dw2, ia0, ia1, ia2, kkw, ka, rk, cw, cbias, wbb):
    tok = lambda t: (t, 0)
    rows8 = TM // 8
    last8 = N_TOK // 8 - 1
    consts = [norm_g, w_in, mu, dw0, dw1, dw2, ia0, ia1, ia2, kkw, ka, rk, cw, cbias, wbb]
    out_r = jax.ShapeDtypeStruct((N_TOK, D_RWKV), F32)
    out_d = jax.ShapeDtypeStruct((N_TOK, D_MODEL), F32)
    spec_r = pl.BlockSpec((TM, D_RWKV), tok)
    spec_d = pl.BlockSpec((TM, D_MODEL), tok)
    return pl.pallas_call(
        _front_kernel,
        grid=(N_TILES,),
        in_specs=[pl.BlockSpec((TM, D_MODEL), tok),
                  pl.BlockSpec((8, D_MODEL), lambda t: (jnp.maximum(t * rows8 - 1, 0), 0)),
                  pl.BlockSpec((8, D_MODEL), lambda t: (jnp.minimum((t + 1) * rows8, last8), 0)),
                  pl.BlockSpec((None, N_MOD, D_MODEL), lambda t: (_mod_row(t), 0, 0))]
                 + [_const_spec(c.shape) for c in consts],
        out_specs=[spec_r] * 11 + [spec_d] * 2,
        out_shape=[out_r] * 11 + [out_d] * 2,
        compiler_params=_params(),
        name="mixer_front",
    )(x, x, x, mod3, *consts)


def _scan_unit(r, v, kk, lw, b, kd, s0, tri, m_strict, m_incl, lane_lo, eye, end_row):
    hi, lo = _split2(lw)
    lo2 = (lw - hi.astype(F32) - lo.astype(F32)).astype(BF16)
    cs = (jnp.dot(tri, hi, preferred_element_type=F32)
          + jnp.dot(tri, lo, preferred_element_type=F32)
          + jnp.dot(tri, lo2, preferred_element_type=F32))
    cs_end = cs[end_row:end_row + 1, :]
    dec_in = jnp.exp(cs)
    dec_inv = jnp.exp(-cs)
    dec_rest = jnp.exp(cs_end - cs)

    def stack(x):
        return jnp.concatenate([jnp.where(lane_lo, x, 0.0), jnp.where(lane_lo, 0.0, x)], axis=0)

    a_s = stack(-kk * jnp.exp(cs - lw))
    r_s = stack(r * dec_in)
    b_s = stack(b * dec_inv)
    k_s = stack(kd * dec_inv)
    bh_s = stack(b * dec_rest)
    kh_s = stack(kd * dec_rest)
    v_s = stack(v)

    gram = _dot(jnp.concatenate([a_s, r_s], axis=0), jnp.concatenate([b_s, k_s], axis=0), NT)
    low = jnp.where(m_strict, gram[:LANES, :LANES], 0.0)
    n_ak = jnp.where(m_strict, gram[:LANES, LANES:], 0.0)
    m_rb = jnp.where(m_incl, gram[LANES:, :LANES], 0.0)
    m_rk = jnp.where(m_incl, gram[LANES:, LANES:], 0.0)

    inv = eye + low
    pw = low
    for _ in range(5):
        pw = _dot(pw, pw)
        inv = inv + _dot(inv, pw)

    nv = _dot(n_ak, v_s)
    x12 = _dot(inv, jnp.concatenate([a_s, nv], axis=1))
    mx = _dot(m_rb, x12)
    g_mat = r_s + mx[:, :LANES]
    w_mat = mx[:, LANES:] + _dot(m_rk, v_s)
    o_s = _dot(g_mat, s0, NT) + w_mat
    out = o_s[:CHUNK, :] + o_s[CHUNK:, :]

    bx = _dot(x12, bh_s, TN)
    p_t = bx[:LANES, :] + jnp.where(eye > 0.0, jnp.exp(cs_end), 0.0)
    q_t = bx[LANES:, :] + _dot(v_s, kh_s, TN)
    s1 = _dot(s0, p_t) + q_t
    return out, s1


def _scan_kernel(rf, vf, kf, lwf, bf_, kdf, rb, vb, kb, lwb, bb, kdb, s0_ref,
                 yf_ref, yb_ref, sout_ref, st_ref):
    t = pl.program_id(0)
    is_lat, first, _ = _tile_info(t)

    @pl.when(first)
    def _():
        st_ref[...] = jnp.where(is_lat, s0_ref[...], 0.0)

    ii = lax.broadcasted_iota(jnp.int32, (LANES, LANES), 0)
    jj = lax.broadcasted_iota(jnp.int32, (LANES, LANES), 1)
    same = (ii >> HEAD_SHIFT) == (jj >> HEAD_SHIFT)
    ti = ii & (CHUNK - 1)
    sj = jj & (CHUNK - 1)
    eye = jnp.where(ii == jj, 1.0, 0.0)
    ci = lax.broadcasted_iota(jnp.int32, (CHUNK, CHUNK), 0)
    cj = lax.broadcasted_iota(jnp.int32, (CHUNK, CHUNK), 1)
    lane_lo = lax.broadcasted_iota(jnp.int32, (CHUNK, LANES), 1) < HEAD
    dirs = (
        (0, (rf, vf, kf, lwf, bf_, kdf), yf_ref, range(N_CHUNK),
         jnp.where(cj <= ci, 1.0, 0.0).astype(BF16),
         jnp.logical_and(same, sj < ti), jnp.logical_and(same, sj <= ti), CHUNK - 1),
        (1, (rb, vb, kb, lwb, bb, kdb), yb_ref, range(N_CHUNK - 1, -1, -1),
         jnp.where(cj >= ci, 1.0, 0.0).astype(BF16),
         jnp.logical_and(same, sj > ti), jnp.logical_and(same, sj >= ti), 0),
    )

    def pair_body(p, carry):
        off = pl.multiple_of(p * LANES, LANES)
        for d, refs, y_ref, order, tri, m_strict, m_incl, end_row in dirs:
            s = st_ref[d, p]
            for c in order:
                rows = slice(c * CHUNK, (c + 1) * CHUNK)
                slabs = [ref[rows, pl.ds(off, LANES)] for ref in refs]
                out, s = _scan_unit(*slabs, s, tri, m_strict, m_incl, lane_lo, eye, end_row)
                y_ref[rows, pl.ds(off, LANES)] = out
            st_ref[d, p] = s
        return carry

    lax.fori_loop(0, N_PAIR, pair_body, 0)
    sout_ref[...] = st_ref[...]


def _scan(r, v, kk, lw0, lw1, b0, b1, k0, k1, s0):
    fwd = pl.BlockSpec((TM, D_RWKV), lambda t: (t, 0))
    bwd = pl.BlockSpec((TM, D_RWKV), lambda t: (_mirror_tile(t), 0))
    st_block = (None, 2, N_PAIR, LANES, LANES)
    out_y = jax.ShapeDtypeStruct((N_TOK, D_RWKV), F32)
    return pl.pallas_call(
        _scan_kernel,
        grid=(N_TILES,),
        in_specs=[fwd] * 6 + [bwd] * 6
                 + [pl.BlockSpec(st_block, lambda t: (jnp.maximum(_seq_id(t) - N_CTX_SEQ, 0), 0, 0, 0, 0))],
        out_specs=[fwd, bwd, pl.BlockSpec(st_block, lambda t: (_seq_id(t), 0, 0, 0, 0))],
        out_shape=[out_y, out_y,
                   jax.ShapeDtypeStruct((N_CTX_SEQ + N_LAT_SEQ, 2, N_PAIR, LANES, LANES), F32)],
        scratch_shapes=[pltpu.VMEM((2, N_PAIR, LANES, LANES), F32)],
        compiler_params=_params(),
        name="rwkv7_scan",
    )(r, v, kk, lw0, b0, k0, r, v, kk, lw1, b1, k1, s0)


def _back_kernel(x_ref, yf_ref, yb_ref, bv_ref, sg_ref, sa_ref, mb_ref, mod_ref, g_ref,
                 gng_ref, gnb_ref, wba_ref, wout_ref, o_ref):
    ones = _head_ones()
    y = yf_ref[...] + yb_ref[...]
    mean = _head_sum(y, ones) * (1.0 / HEAD)
    yc = y - mean
    var = _head_sum(yc * yc, ones) * (1.0 / HEAD)
    yn = yc * lax.rsqrt(var + EPS_GN) * gng_ref[...] + gnb_ref[...]
    ya = _dot((yn + bv_ref[...]) * sg_ref[...], wba_ref[...])
    merged = sa_ref[...] * ya + mb_ref[...]
    out = _dot(merged, wout_ref[...])
    o_ref[...] = x_ref[...] + mod_ref[5:6, :] * _rms(out, g_ref[3:4, :])


def _back(x, yf, yb, bv, sg, sa, mb, mod3, norm_g, gng, gnb, wba, wout):
    tok = lambda t: (t, 0)
    spec_r = pl.BlockSpec((TM, D_RWKV), tok)
    spec_d = pl.BlockSpec((TM, D_MODEL), tok)
    consts = [norm_g, gng, gnb, wba, wout]
    return pl.pallas_call(
        _back_kernel,
        grid=(N_TILES,),
        in_specs=[spec_d, spec_r, spec_r, spec_r, spec_r, spec_d, spec_d,
                  pl.BlockSpec((None, N_MOD, D_MODEL), lambda t: (_mod_row(t), 0, 0))]
                 + [_const_spec(c.shape) for c in consts],
        out_specs=spec_d,
        out_shape=jax.ShapeDtypeStruct((N_TOK, D_MODEL), F32),
        compiler_params=_params(),
        name="mixer_back",
    )(x, yf, yb, bv, sg, sa, mb, mod3, *consts)


def _pair_blockdiag(s):
    lead = s.shape[:-3]
    s = s.reshape(lead + (N_PAIR, 2, HEAD, HEAD))
    z = jnp.zeros_like(s[..., 0, :, :])
    top = jnp.concatenate([s[..., 0, :, :], z], axis=-1)
    bot = jnp.concatenate([z, s[..., 1, :, :]], axis=-1)
    return jnp.concatenate([top, bot], axis=-2)


def _pair_unblock(s):
    lead = s.shape[:-3]
    a = s[..., :HEAD, :HEAD]
    b = s[..., HEAD:, HEAD:]
    return jnp.stack([a, b], axis=-3).reshape(lead + (2 * N_PAIR, HEAD, HEAD))


def kernel(x_prompt, x_sample, c, state_rwkv, c_ctx, w_mod, b_mod, norm_g, ffn1_w13, ffn1_w2,
           ffn2_w13, ffn2_w2, w_in, mu_shift, decay_w0, decay_w1, decay_w2, iclr_a0, iclr_a1,
           iclr_a2, k_k, k_a, r_k, gn_gain, gn_bias, conv_w, conv_b, w_branch_a, w_branch_b, w_out):
    assert x_prompt.shape == (N_CTX_SEQ, CTX_LEN, D_MODEL) and x_sample.shape == (N_LAT_SEQ, LAT_LEN, D_MODEL)
    assert w_mod.shape[0] == 1, "single trunk layer"
    bf = lambda w: w.astype(BF16)

    x = jnp.concatenate([x_prompt.reshape(-1, D_MODEL), x_sample.reshape(-1, D_MODEL)], axis=0)
    cvec = jnp.concatenate([c_ctx[None, :], c, jnp.zeros((MOD_ROWS - 1 - N_LAT_SEQ, D_MODEL), F32)], axis=0)
    mod3 = _modulation(cvec, w_mod[0], b_mod).reshape(MOD_ROWS, N_MOD, D_MODEL)
    g = norm_g[0]

    x = _ffn(x, mod3, g, bf(ffn1_w13[0]), bf(ffn1_w2[0]), 0, 0, "ffn1")

    row = lambda p: p.reshape(1, -1)
    front = _front(x, mod3, g, bf(w_in[0]), mu_shift[0].reshape(4, D_MODEL),
                   decay_w0[0], bf(decay_w1[0]), bf(decay_w2[0]),
                   iclr_a0[0], bf(iclr_a1[0]), bf(iclr_a2[0]),
                   row(k_k[0]), row(k_a[0]), row(r_k[0]), conv_w[0], row(conv_b[0]), bf(w_branch_b[0]))
    r, v, kk, lw0, lw1, b0, b1, k0, k1, bv, sg, sa, mb = front

    s0 = _pair_blockdiag(state_rwkv[:, 0])
    yf, yb, s_fin = _scan(r, v, kk, lw0, lw1, b0, b1, k0, k1, s0)

    x = _back(x, yf, yb, bv, sg, sa, mb, mod3, g, row(gn_gain[0]), row(gn_bias[0]),
              bf(w_branch_a[0]), bf(w_out[0]))
    x = _ffn(x, mod3, g, bf(ffn2_w13[0]), bf(ffn2_w2[0]), 6, 4, "ffn2")

    n_ctx = N_CTX_SEQ * CTX_LEN
    y_prompt = x[:n_ctx].reshape(N_CTX_SEQ, CTX_LEN, D_MODEL)
    y_sample = x[n_ctx:].reshape(N_LAT_SEQ, LAT_LEN, D_MODEL)
    new_state = _pair_unblock(s_fin[:N_CTX_SEQ])[:, None]
    return y_prompt, y_sample, new_state
```

```python
import functools

import jax
import jax.numpy as jnp
from jax import lax
from jax.experimental import pallas as pl
from jax.experimental.pallas import tpu as pltpu

F32 = jnp.float32
BF16 = jnp.bfloat16

D_MODEL = 1024
D_FF = 2816
D_RWKV = 512
D_CONV = 512
HEAD = 64
HEAD_SHIFT = 6
D_IN = 4 * D_RWKV + 3 * D_CONV + 2 * D_MODEL
N_MOD = 9
EPS_RMS = 1e-6
EPS_GN = 64e-5
HALF_STEP = 0.5
EXP_M05 = 0.6065306597126334

N_CTX_SEQ = 16
CTX_LEN = 256
N_LAT_SEQ = 2
LAT_LEN = 2048
GRID_W = 64
N_TOK = N_CTX_SEQ * CTX_LEN + N_LAT_SEQ * LAT_LEN

TM = 256
N_CTX_TILES = N_CTX_SEQ * CTX_LEN // TM
LAT_TILES = LAT_LEN // TM
N_TILES = N_TOK // TM
CHUNK = 64
N_CHUNK = TM // CHUNK
LANES = 128
N_PAIR = D_RWKV // LANES
MOD_ROWS = 8
VMEM_LIMIT = 56 * 1024 * 1024

NN = (((1,), (0,)), ((), ()))
NT = (((1,), (1,)), ((), ()))
TN = (((0,), (0,)), ((), ()))


def _dot(a, b, dims=NN):
    return lax.dot_general(a.astype(BF16), b.astype(BF16), dims, preferred_element_type=F32)


def _rms(x, g):
    ms = jnp.mean(x * x, axis=-1, keepdims=True)
    return x * lax.rsqrt(ms + EPS_RMS) * g


def _split2(x):
    hi = x.astype(BF16)
    lo = (x - hi.astype(F32)).astype(BF16)
    return hi, lo


def _head_ones():
    i = lax.broadcasted_iota(jnp.int32, (D_RWKV, D_RWKV), 0) >> HEAD_SHIFT
    j = lax.broadcasted_iota(jnp.int32, (D_RWKV, D_RWKV), 1) >> HEAD_SHIFT
    return jnp.where(i == j, 1.0, 0.0).astype(BF16)


def _head_sum(x, ones):
    hi, lo = _split2(x)
    return (jnp.dot(hi, ones, preferred_element_type=F32)
            + jnp.dot(lo, ones, preferred_element_type=F32))


def _tile_info(t):
    is_lat = t >= N_CTX_TILES
    i_in = (t - N_CTX_TILES) % LAT_TILES
    first = jnp.logical_or(jnp.logical_not(is_lat), i_in == 0)
    last = jnp.logical_or(jnp.logical_not(is_lat), i_in == LAT_TILES - 1)
    return is_lat, first, last


def _mod_row(t):
    return jnp.where(t < N_CTX_TILES, 0, 1 + (t - N_CTX_TILES) // LAT_TILES)


def _mirror_tile(t):
    u = t - N_CTX_TILES
    return jnp.where(t < N_CTX_TILES, t,
                     N_CTX_TILES + (u // LAT_TILES) * LAT_TILES + (LAT_TILES - 1 - u % LAT_TILES))


def _seq_id(t):
    return jnp.where(t < N_CTX_TILES, t, N_CTX_TILES + (t - N_CTX_TILES) // LAT_TILES)


def _const_spec(shape):
    nd = len(shape)
    return pl.BlockSpec(shape, lambda *_: (0,) * nd, pipeline_mode=pl.Buffered(1))


def _params(n_axes=1):
    return pltpu.CompilerParams(dimension_semantics=("arbitrary",) * n_axes,
                                vmem_limit_bytes=VMEM_LIMIT)


def _mod_kernel(c_ref, w_ref, b_ref, o_ref):
    c = c_ref[...]
    s = c * jax.nn.sigmoid(c)
    o_ref[...] = _dot(s, w_ref[...]) + b_ref[...]


def _modulation(cvec, w_mod, b_mod):
    nblk = N_MOD
    return pl.pallas_call(
        _mod_kernel,
        grid=(nblk,),
        in_specs=[pl.BlockSpec((MOD_ROWS, D_MODEL), lambda j: (0, 0)),
                  pl.BlockSpec((D_MODEL, D_MODEL), lambda j: (0, j)),
                  pl.BlockSpec((1, D_MODEL), lambda j: (0, j))],
        out_specs=pl.BlockSpec((MOD_ROWS, D_MODEL), lambda j: (0, j)),
        out_shape=jax.ShapeDtypeStruct((MOD_ROWS, N_MOD * D_MODEL), F32),
        compiler_params=_params(),
        name="modulation",
    )(cvec, w_mod, b_mod)


def _ffn_kernel(x_ref, mod_ref, g_ref, w13_ref, w2_ref, o_ref, *, im, ig):
    x = x_ref[...]
    shift = mod_ref[im:im + 1, :]
    scale = mod_ref[im + 1:im + 2, :]
    gate = mod_ref[im + 2:im + 3, :]
    h = _rms(x, g_ref[ig:ig + 1, :]) * (1.0 + scale) + shift
    gu = _dot(h, w13_ref[...])
    gt = gu[:, :D_FF]
    up = gu[:, D_FF:]
    act = gt * jax.nn.sigmoid(gt) * up
    o = _dot(act, w2_ref[...])
    o_ref[...] = x + HALF_STEP * gate * _rms(o, g_ref[ig + 1:ig + 2, :])


def _ffn(x, mod3, norm_g, w13, w2, im, ig, name):
    return pl.pallas_call(
        functools.partial(_ffn_kernel, im=im, ig=ig),
        grid=(N_TILES,),
        in_specs=[pl.BlockSpec((TM, D_MODEL), lambda t: (t, 0)),
                  pl.BlockSpec((None, N_MOD, D_MODEL), lambda t: (_mod_row(t), 0, 0)),
                  _const_spec(norm_g.shape),
                  _const_spec(w13.shape),
                  _const_spec(w2.shape)],
        out_specs=pl.BlockSpec((TM, D_MODEL), lambda t: (t, 0)),
        out_shape=jax.ShapeDtypeStruct((N_TOK, D_MODEL), F32),
        compiler_params=_params(),
        name=name,
    )(x, mod3, norm_g, w13, w2)


def _front_kernel(x_ref, xp_ref, xn_ref, mod_ref, g_ref, win_ref, mu_ref,
                  dw0_ref, dw1_ref, dw2_ref, ia0_ref, ia1_ref, ia2_ref,
                  kkw_ref, ka_ref, rk_ref, cw_ref, cbias_ref, wbb_ref,
                  r_o, v_o, kk_o, lw0_o, lw1_o, b0_o, b1_o, k0_o, k1_o,
                  bv_o, sg_o, sa_o, mb_o):
    t = pl.program_id(0)
    is_lat, first, last = _tile_info(t)
    shift = mod_ref[3:4, :]
    scale = mod_ref[4:5, :]
    g2 = g_ref[2:3, :]

    def pre(x):
        return _rms(x, g2) * (1.0 + scale) + shift

    h = pre(x_ref[...])
    hp = jnp.where(first, 0.0, pre(xp_ref[...])[7:8, :])
    hn = jnp.where(last, 0.0, pre(xn_ref[...])[0:1, :])
    row = lax.broadcasted_iota(jnp.int32, (TM, 1), 0)
    prev = jnp.where(row == 0, hp, pltpu.roll(h, 1, 0))
    nxt = jnp.where(row == TM - 1, hn, pltpu.roll(h, TM - 1, 0))

    proj = _dot(h, win_ref[...])
    r = proj[:, 0:512]
    k = proj[:, 512:1024]
    v = proj[:, 1024:1536]
    g = proj[:, 1536:2048]
    cgate = proj[:, 2048:2560]
    cc = proj[:, 2560:3072]
    xc = proj[:, 3072:3584]
    gate_a = proj[:, 3584:4608]
    gate_b = proj[:, 4608:5632]

    ones = _head_ones()
    kk = k * kkw_ref[...]
    kkn = kk * lax.rsqrt(_head_sum(kk * kk, ones) + 1e-12)
    ka = ka_ref[...]

    r_o[...] = r
    v_o[...] = v
    kk_o[...] = kkn
    sg_o[...] = jax.nn.sigmoid(g)
    sa_o[...] = jax.nn.sigmoid(gate_a)

    ksum = None
    for d, (src, lw_o, b_o, k_o) in enumerate(((prev, lw0_o, b0_o, k0_o), (nxt, lw1_o, b1_o, k1_o))):
        sh = src - h
        xw = h + mu_ref[2 * d:2 * d + 1, :] * sh
        xa = h + mu_ref[2 * d + 1:2 * d + 2, :] * sh
        z = dw0_ref[d:d + 1, :] + _dot(jnp.tanh(_dot(xw, dw1_ref[d])), dw2_ref[d])
        lw_o[...] = -EXP_M05 * jax.nn.sigmoid(z)
        a = jax.nn.sigmoid(ia0_ref[d:d + 1, :] + _dot(_dot(xa, ia1_ref[d]), ia2_ref[d]))
        k_d = k * (1.0 + (a - 1.0) * ka)
        b_o[...] = kkn * a
        k_o[...] = k_d
        ksum = k_d if ksum is None else ksum + k_d

    bv_o[...] = _head_sum(r * ksum * rk_ref[...], ones) * v

    u = cc * xc
    col = row & (GRID_W - 1)
    zl = jnp.logical_or(row == 0, jnp.logical_and(is_lat, col == 0))
    zr = jnp.logical_or(row == TM - 1, jnp.logical_and(is_lat, col == GRID_W - 1))
    left = jnp.where(zl, 0.0, pltpu.roll(u, 1, 0))
    right = jnp.where(zr, 0.0, pltpu.roll(u, TM - 1, 0))
    conv = left * cw_ref[0:1, :] + u * cw_ref[1:2, :] + right * cw_ref[2:3, :] + cbias_ref[...]
    y_b = _dot(cgate * conv, wbb_ref[...])
    mb_o[...] = jax.nn.sigmoid(gate_b) * y_b


def _front(x, mod3, norm_g, w_in, mu, dw0, dw1, dw2, ia0, ia1, ia2, kkw, ka, rk, cw, cbias, wbb):
    tok = lambda t: (t, 0)
    rows8 = TM // 8
    last8 = N_TOK // 8 - 1
    consts = [norm_g, w_in, mu, dw0, dw1, dw2, ia0, ia1, ia2, kkw, ka, rk, cw, cbias, wbb]
    out_r = jax.ShapeDtypeStruct((N_TOK, D_RWKV), F32)
    out_d = jax.ShapeDtypeStruct((N_TOK, D_MODEL), F32)
    spec_r = pl.BlockSpec((TM, D_RWKV), tok)
    spec_d = pl.BlockSpec((TM, D_MODEL), tok)
    return pl.pallas_call(
        _front_kernel,
        grid=(N_TILES,),
        in_specs=[pl.BlockSpec((TM, D_MODEL), tok),
                  pl.BlockSpec((8, D_MODEL), lambda t: (jnp.maximum(t * rows8 - 1, 0), 0)),
                  pl.BlockSpec((8, D_MODEL), lambda t: (jnp.minimum((t + 1) * rows8, last8), 0)),
                  pl.BlockSpec((None, N_MOD, D_MODEL), lambda t: (_mod_row(t), 0, 0))]
                 + [_const_spec(c.shape) for c in consts],
        out_specs=[spec_r] * 11 + [spec_d] * 2,
        out_shape=[out_r] * 11 + [out_d] * 2,
        compiler_params=_params(),
        name="mixer_front",
    )(x, x, x, mod3, *consts)


OP_A, OP_R, OP_B, OP_K, OP_BH, OP_KH, OP_V = range(7)
N_OPS = 7


def _scan_prep(refs, tri, end_row, ops_ref, d):
    r, v, kk, lw, b, kd = [ref[...] for ref in refs]
    hi, lo = _split2(lw)
    lo2 = (lw - hi.astype(F32) - lo.astype(F32)).astype(BF16)
    cs = (jnp.dot(tri, hi, preferred_element_type=F32)
          + jnp.dot(tri, lo, preferred_element_type=F32)
          + jnp.dot(tri, lo2, preferred_element_type=F32))
    ends = [cs[c * CHUNK + end_row:c * CHUNK + end_row + 1, :] for c in range(N_CHUNK)]
    cs_end = jnp.concatenate([jnp.broadcast_to(e, (CHUNK, D_RWKV)) for e in ends], axis=0)
    dec_in = jnp.exp(cs)
    dec_inv = jnp.exp(-cs)
    dec_rest = jnp.exp(cs_end - cs)
    ops_ref[d, OP_A] = (-kk * jnp.exp(cs - lw)).astype(BF16)
    ops_ref[d, OP_R] = (r * dec_in).astype(BF16)
    ops_ref[d, OP_B] = (b * dec_inv).astype(BF16)
    ops_ref[d, OP_K] = (kd * dec_inv).astype(BF16)
    ops_ref[d, OP_BH] = (b * dec_rest).astype(BF16)
    ops_ref[d, OP_KH] = (kd * dec_rest).astype(BF16)
    ops_ref[d, OP_V] = v.astype(BF16)
    return ends


def _stack(x):
    lane_lo = lax.broadcasted_iota(jnp.int32, x.shape, 1) < HEAD
    z = jnp.zeros_like(x)
    return jnp.concatenate([jnp.where(lane_lo, x, z), jnp.where(lane_lo, z, x)], axis=0)


def _bdot(a, b, dims=NN):
    return lax.dot_general(a, b, dims, preferred_element_type=F32)


def _scan_kernel(rf, vf, kf, lwf, bf_, kdf, rb, vb, kb, lwb, bb, kdb, s0_ref,
                 yf_ref, yb_ref, sout_ref, st_ref, ops_ref):
    t = pl.program_id(0)
    is_lat, first, _ = _tile_info(t)

    @pl.when(first)
    def _():
        st_ref[...] = jnp.where(is_lat, s0_ref[...], 0.0)

    ri = lax.broadcasted_iota(jnp.int32, (TM, TM), 0)
    rj = lax.broadcasted_iota(jnp.int32, (TM, TM), 1)
    same_chunk = (ri >> HEAD_SHIFT) == (rj >> HEAD_SHIFT)
    tri_f = jnp.where(jnp.logical_and(same_chunk, rj <= ri), 1.0, 0.0).astype(BF16)
    tri_b = jnp.where(jnp.logical_and(same_chunk, rj >= ri), 1.0, 0.0).astype(BF16)
    ends = (_scan_prep((rf, vf, kf, lwf, bf_, kdf), tri_f, CHUNK - 1, ops_ref, 0),
            _scan_prep((rb, vb, kb, lwb, bb, kdb), tri_b, 0, ops_ref, 1))

    ti = lax.broadcasted_iota(jnp.int32, (CHUNK, LANES), 0)
    sj = lax.broadcasted_iota(jnp.int32, (CHUNK, LANES), 1) & (CHUNK - 1)
    m_strict = (sj < ti, sj > ti)
    m_incl = (sj <= ti, sj >= ti)
    eye_cat = jnp.where(sj == ti, 1.0, 0.0)
    bi = lax.broadcasted_iota(jnp.int32, (LANES, LANES), 0)
    bj = lax.broadcasted_iota(jnp.int32, (LANES, LANES), 1)
    blk = (bi >> HEAD_SHIFT) == (bj >> HEAD_SHIFT)
    eye_bd = bi == bj
    zero_bd = jnp.zeros((LANES, LANES), BF16)

    units = [(d, p, c) for d in (0, 1) for p in range(N_PAIR) for c in range(N_CHUNK)]

    def op(i, u):
        d, p, c = u
        return ops_ref[d, i, c * CHUNK:(c + 1) * CHUNK, p * LANES:(p + 1) * LANES]

    low, nak, mrbk = {}, {}, {}
    for u in units:
        lhs = jnp.concatenate([op(OP_A, u), op(OP_R, u)], axis=0)
        rhs = jnp.concatenate([_stack(op(OP_B, u)), _stack(op(OP_K, u))], axis=0)
        gram = _bdot(lhs, rhs, NT)
        d = u[0]
        low[u] = jnp.where(m_strict[d], gram[:CHUNK, :LANES], 0.0)
        nak[u] = jnp.where(m_strict[d], gram[:CHUNK, LANES:], 0.0).astype(BF16)
        mrbk[u] = jnp.concatenate([jnp.where(m_incl[d], gram[CHUNK:, :LANES], 0.0),
                                   jnp.where(m_incl[d], gram[CHUNK:, LANES:], 0.0)], axis=1).astype(BF16)

    inv = {u: eye_cat + low[u] for u in units}
    pwb = {u: low[u].astype(BF16) for u in units}
    spw = {u: _stack(pwb[u]) for u in units}
    for _ in range(5):
        for u in units:
            pwb[u] = _bdot(pwb[u], spw[u]).astype(BF16)
        for u in units:
            spw[u] = _stack(pwb[u])
        for u in units:
            inv[u] = inv[u] + _bdot(inv[u].astype(BF16), spw[u])

    sv = {u: _stack(op(OP_V, u)) for u in units}
    nv = {u: _bdot(nak[u], sv[u]).astype(BF16) for u in units}
    x1, x2 = {}, {}
    for u in units:
        x12 = _bdot(inv[u].astype(BF16), jnp.concatenate([_stack(op(OP_A, u)), _stack(nv[u])], axis=1))
        x1[u] = x12[:, :LANES].astype(BF16)
        x2[u] = x12[:, LANES:].astype(BF16)
    g_mat, w_mat = {}, {}
    for u in units:
        rhs = jnp.concatenate([jnp.concatenate([_stack(x1[u]), _stack(x2[u])], axis=1),
                               jnp.concatenate([zero_bd, sv[u]], axis=1)], axis=0)
        gw = _bdot(mrbk[u], rhs)
        g_mat[u] = (op(OP_R, u).astype(F32) + gw[:, :LANES]).astype(BF16)
        w_mat[u] = gw[:, LANES:]
    p_t, q_t = {}, {}
    for u in units:
        d, p, c = u
        gam = jnp.exp(ends[d][c][:, p * LANES:(p + 1) * LANES])
        p_t[u] = (jnp.where(blk, _bdot(x1[u], op(OP_BH, u), TN), 0.0)
                  + jnp.where(eye_bd, gam, 0.0)).astype(BF16)
        q_t[u] = jnp.where(blk, _bdot(jnp.concatenate([x2[u], op(OP_V, u)], axis=0),
                                      jnp.concatenate([op(OP_BH, u), op(OP_KH, u)], axis=0), TN), 0.0)

    state = {(d, p): st_ref[d, p] for d in (0, 1) for p in range(N_PAIR)}
    y_refs = (yf_ref, yb_ref)
    for step in range(N_CHUNK):
        for d in (0, 1):
            c = step if d == 0 else N_CHUNK - 1 - step
            for p in range(N_PAIR):
                u = (d, p, c)
                sb = state[d, p].astype(BF16)
                y_refs[d][c * CHUNK:(c + 1) * CHUNK, p * LANES:(p + 1) * LANES] = (
                    _bdot(g_mat[u], sb, NT) + w_mat[u])
                state[d, p] = _bdot(sb, p_t[u]) + q_t[u]
    for (d, p), s in state.items():
        st_ref[d, p] = s
    sout_ref[...] = st_ref[...]


def _scan(r, v, kk, lw0, lw1, b0, b1, k0, k1, s0):
    fwd = pl.BlockSpec((TM, D_RWKV), lambda t: (t, 0))
    bwd = pl.BlockSpec((TM, D_RWKV), lambda t: (_mirror_tile(t), 0))
    st_block = (None, 2, N_PAIR, LANES, LANES)
    out_y = jax.ShapeDtypeStruct((N_TOK, D_RWKV), F32)
    return pl.pallas_call(
        _scan_kernel,
        grid=(N_TILES,),
        in_specs=[fwd] * 6 + [bwd] * 6
                 + [pl.BlockSpec(st_block, lambda t: (jnp.maximum(_seq_id(t) - N_CTX_SEQ, 0), 0, 0, 0, 0))],
        out_specs=[fwd, bwd, pl.BlockSpec(st_block, lambda t: (_seq_id(t), 0, 0, 0, 0))],
        out_shape=[out_y, out_y,
                   jax.ShapeDtypeStruct((N_CTX_SEQ + N_LAT_SEQ, 2, N_PAIR, LANES, LANES), F32)],
        scratch_shapes=[pltpu.VMEM((2, N_PAIR, LANES, LANES), F32),
                        pltpu.VMEM((2, N_OPS, TM, D_RWKV), BF16)],
        compiler_params=_params(),
        name="rwkv7_scan",
    )(r, v, kk, lw0, b0, k0, r, v, kk, lw1, b1, k1, s0)


def _back_kernel(x_ref, yf_ref, yb_ref, bv_ref, sg_ref, sa_ref, mb_ref, mod_ref, g_ref,
                 gng_ref, gnb_ref, wba_ref, wout_ref, o_ref):
    ones = _head_ones()
    y = yf_ref[...] + yb_ref[...]
    mean = _head_sum(y, ones) * (1.0 / HEAD)
    yc = y - mean
    var = _head_sum(yc * yc, ones) * (1.0 / HEAD)
    yn = yc * lax.rsqrt(var + EPS_GN) * gng_ref[...] + gnb_ref[...]
    ya = _dot((yn + bv_ref[...]) * sg_ref[...], wba_ref[...])
    merged = sa_ref[...] * ya + mb_ref[...]
    out = _dot(merged, wout_ref[...])
    o_ref[...] = x_ref[...] + mod_ref[5:6, :] * _rms(out, g_ref[3:4, :])


def _back(x, yf, yb, bv, sg, sa, mb, mod3, norm_g, gng, gnb, wba, wout):
    tok = lambda t: (t, 0)
    spec_r = pl.BlockSpec((TM, D_RWKV), tok)
    spec_d = pl.BlockSpec((TM, D_MODEL), tok)
    consts = [norm_g, gng, gnb, wba, wout]
    return pl.pallas_call(
        _back_kernel,
        grid=(N_TILES,),
        in_specs=[spec_d, spec_r, spec_r, spec_r, spec_r, spec_d, spec_d,
                  pl.BlockSpec((None, N_MOD, D_MODEL), lambda t: (_mod_row(t), 0, 0))]
                 + [_const_spec(c.shape) for c in consts],
        out_specs=spec_d,
        out_shape=jax.ShapeDtypeStruct((N_TOK, D_MODEL), F32),
        compiler_params=_params(),
        name="mixer_back",
    )(x, yf, yb, bv, sg, sa, mb, mod3, *consts)


def _pair_blockdiag(s):
    lead = s.shape[:-3]
    s = s.reshape(lead + (N_PAIR, 2, HEAD, HEAD))
    z = jnp.zeros_like(s[..., 0, :, :])
    top = jnp.concatenate([s[..., 0, :, :], z], axis=-1)
    bot = jnp.concatenate([z, s[..., 1, :, :]], axis=-1)
    return jnp.concatenate([top, bot], axis=-2)


def _pair_unblock(s):
    lead = s.shape[:-3]
    a = s[..., :HEAD, :HEAD]
    b = s[..., HEAD:, HEAD:]
    return jnp.stack([a, b], axis=-3).reshape(lead + (2 * N_PAIR, HEAD, HEAD))


def kernel(x_prompt, x_sample, c, state_rwkv, c_ctx, w_mod, b_mod, norm_g, ffn1_w13, ffn1_w2,
           ffn2_w13, ffn2_w2, w_in, mu_shift, decay_w0, decay_w1, decay_w2, iclr_a0, iclr_a1,
           iclr_a2, k_k, k_a, r_k, gn_gain, gn_bias, conv_w, conv_b, w_branch_a, w_branch_b, w_out):
    assert x_prompt.shape == (N_CTX_SEQ, CTX_LEN, D_MODEL) and x_sample.shape == (N_LAT_SEQ, LAT_LEN, D_MODEL)
    assert w_mod.shape[0] == 1, "single trunk layer"
    bf = lambda w: w.astype(BF16)

    x = jnp.concatenate([x_prompt.reshape(-1, D_MODEL), x_sample.reshape(-1, D_MODEL)], axis=0)
    cvec = jnp.concatenate([c_ctx[None, :], c, jnp.zeros((MOD_ROWS - 1 - N_LAT_SEQ, D_MODEL), F32)], axis=0)
    mod3 = _modulation(cvec, w_mod[0], b_mod).reshape(MOD_ROWS, N_MOD, D_MODEL)
    g = norm_g[0]

    x = _ffn(x, mod3, g, bf(ffn1_w13[0]), bf(ffn1_w2[0]), 0, 0, "ffn1")

    row = lambda p: p.reshape(1, -1)
    front = _front(x, mod3, g, bf(w_in[0]), mu_shift[0].reshape(4, D_MODEL),
                   decay_w0[0], bf(decay_w1[0]), bf(decay_w2[0]),
                   iclr_a0[0], bf(iclr_a1[0]), bf(iclr_a2[0]),
                   row(k_k[0]), row(k_a[0]), row(r_k[0]), conv_w[0], row(conv_b[0]), bf(w_branch_b[0]))
    r, v, kk, lw0, lw1, b0, b1, k0, k1, bv, sg, sa, mb = front

    s0 = _pair_blockdiag(state_rwkv[:, 0])
    yf, yb, s_fin = _scan(r, v, kk, lw0, lw1, b0, b1, k0, k1, s0)

    x = _back(x, yf, yb, bv, sg, sa, mb, mod3, g, row(gn_gain[0]), row(gn_bias[0]),
              bf(w_branch_a[0]), bf(w_out[0]))
    x = _ffn(x, mod3, g, bf(ffn2_w13[0]), bf(ffn2_w2[0]), 6, 4, "ffn2")

    n_ctx = N_CTX_SEQ * CTX_LEN
    y_prompt = x[:n_ctx].reshape(N_CTX_SEQ, CTX_LEN, D_MODEL)
    y_sample = x[n_ctx:].reshape(N_LAT_SEQ, LAT_LEN, D_MODEL)
    new_state = _pair_unblock(s_fin[:N_CTX_SEQ])[:, None]
    return y_prompt, y_sample, new_state
```

```python
import functools

import jax
import jax.numpy as jnp
from jax import lax
from jax.experimental import pallas as pl
from jax.experimental.pallas import tpu as pltpu

F32 = jnp.float32
BF16 = jnp.bfloat16

D_MODEL = 1024
D_FF = 2816
D_RWKV = 512
D_CONV = 512
HEAD = 64
HEAD_SHIFT = 6
D_IN = 4 * D_RWKV + 3 * D_CONV + 2 * D_MODEL
N_MOD = 9
EPS_RMS = 1e-6
EPS_GN = 64e-5
HALF_STEP = 0.5
EXP_M05 = 0.6065306597126334

N_CTX_SEQ = 16
CTX_LEN = 256
N_LAT_SEQ = 2
LAT_LEN = 2048
GRID_W = 64
N_TOK = N_CTX_SEQ * CTX_LEN + N_LAT_SEQ * LAT_LEN

TM = 256
N_CTX_TILES = N_CTX_SEQ * CTX_LEN // TM
LAT_TILES = LAT_LEN // TM
N_TILES = N_TOK // TM
CHUNK = 64
N_CHUNK = TM // CHUNK
LANES = 128
N_PAIR = D_RWKV // LANES
MOD_ROWS = 8
VMEM_LIMIT = 56 * 1024 * 1024

OP_A, OP_R, OP_B, OP_K, OP_BH, OP_KH, OP_V = range(7)
N_DIR_OPS = 6

NN = (((1,), (0,)), ((), ()))
NT = (((1,), (1,)), ((), ()))
TN = (((0,), (0,)), ((), ()))


def _dot(a, b, dims=NN):
    return lax.dot_general(a.astype(BF16), b.astype(BF16), dims, preferred_element_type=F32)


def _rms(x, g):
    ms = jnp.mean(x * x, axis=-1, keepdims=True)
    return x * lax.rsqrt(ms + EPS_RMS) * g


def _split2(x):
    hi = x.astype(BF16)
    lo = (x - hi.astype(F32)).astype(BF16)
    return hi, lo


def _head_ones():
    i = lax.broadcasted_iota(jnp.int32, (D_RWKV, D_RWKV), 0) >> HEAD_SHIFT
    j = lax.broadcasted_iota(jnp.int32, (D_RWKV, D_RWKV), 1) >> HEAD_SHIFT
    return jnp.where(i == j, 1.0, 0.0).astype(BF16)


def _head_sum(x, ones):
    hi, lo = _split2(x)
    return (jnp.dot(hi, ones, preferred_element_type=F32)
            + jnp.dot(lo, ones, preferred_element_type=F32))


def _tile_info(t):
    is_lat = t >= N_CTX_TILES
    i_in = (t - N_CTX_TILES) % LAT_TILES
    first = jnp.logical_or(jnp.logical_not(is_lat), i_in == 0)
    last = jnp.logical_or(jnp.logical_not(is_lat), i_in == LAT_TILES - 1)
    return is_lat, first, last


def _mod_row(t):
    return jnp.where(t < N_CTX_TILES, 0, 1 + (t - N_CTX_TILES) // LAT_TILES)


def _mirror_tile(t):
    u = t - N_CTX_TILES
    return jnp.where(t < N_CTX_TILES, t,
                     N_CTX_TILES + (u // LAT_TILES) * LAT_TILES + (LAT_TILES - 1 - u % LAT_TILES))


def _seq_id(t):
    return jnp.where(t < N_CTX_TILES, t, N_CTX_TILES + (t - N_CTX_TILES) // LAT_TILES)


def _const_spec(shape):
    nd = len(shape)
    return pl.BlockSpec(shape, lambda *_: (0,) * nd, pipeline_mode=pl.Buffered(1))


def _params(n_axes=1):
    return pltpu.CompilerParams(dimension_semantics=("arbitrary",) * n_axes,
                                vmem_limit_bytes=VMEM_LIMIT)


def _mod_kernel(c_ref, w_ref, b_ref, o_ref):
    c = c_ref[...]
    s = c * jax.nn.sigmoid(c)
    o_ref[...] = _dot(s, w_ref[...]) + b_ref[...]


def _modulation(cvec, w_mod, b_mod):
    nblk = N_MOD
    return pl.pallas_call(
        _mod_kernel,
        grid=(nblk,),
        in_specs=[pl.BlockSpec((MOD_ROWS, D_MODEL), lambda j: (0, 0)),
                  pl.BlockSpec((D_MODEL, D_MODEL), lambda j: (0, j)),
                  pl.BlockSpec((1, D_MODEL), lambda j: (0, j))],
        out_specs=pl.BlockSpec((MOD_ROWS, D_MODEL), lambda j: (0, j)),
        out_shape=jax.ShapeDtypeStruct((MOD_ROWS, N_MOD * D_MODEL), F32),
        compiler_params=_params(),
        name="modulation",
    )(cvec, w_mod, b_mod)


def _ffn_body(x, mod_ref, g_ref, w13_ref, w2_ref, im, ig):
    shift = mod_ref[im:im + 1, :]
    scale = mod_ref[im + 1:im + 2, :]
    gate = mod_ref[im + 2:im + 3, :]
    h = _rms(x, g_ref[ig:ig + 1, :]) * (1.0 + scale) + shift
    gu = _dot(h, w13_ref[...])
    gt = gu[:, :D_FF]
    up = gu[:, D_FF:]
    act = gt * jax.nn.sigmoid(gt) * up
    o = _dot(act, w2_ref[...])
    return x + HALF_STEP * gate * _rms(o, g_ref[ig + 1:ig + 2, :])


def _ffn_first_kernel(xc_ref, xl_ref, mod_ref, g_ref, w13_ref, w2_ref, o_ref):
    x = jnp.where(pl.program_id(0) < N_CTX_TILES, xc_ref[...], xl_ref[...])
    o_ref[...] = _ffn_body(x, mod_ref, g_ref, w13_ref, w2_ref, 0, 0)


def _ffn_last_kernel(x_ref, mod_ref, g_ref, w13_ref, w2_ref, oc_ref, ol_ref):
    out = _ffn_body(x_ref[...], mod_ref, g_ref, w13_ref, w2_ref, 6, 4)
    t = pl.program_id(0)

    @pl.when(t < N_CTX_TILES)
    def _():
        oc_ref[...] = out

    @pl.when(t >= N_CTX_TILES)
    def _():
        ol_ref[...] = out


_CTX_TILE = lambda t: (jnp.minimum(t, N_CTX_TILES - 1), 0)
_LAT_TILE = lambda t: (jnp.maximum(t - N_CTX_TILES, 0), 0)


def _ffn_first(x_ctx, x_lat, mod3, norm_g, w13, w2):
    return pl.pallas_call(
        _ffn_first_kernel,
        grid=(N_TILES,),
        in_specs=[pl.BlockSpec((TM, D_MODEL), _CTX_TILE),
                  pl.BlockSpec((TM, D_MODEL), _LAT_TILE),
                  pl.BlockSpec((None, N_MOD, D_MODEL), lambda t: (_mod_row(t), 0, 0)),
                  _const_spec(norm_g.shape),
                  _const_spec(w13.shape),
                  _const_spec(w2.shape)],
        out_specs=pl.BlockSpec((TM, D_MODEL), lambda t: (t, 0)),
        out_shape=jax.ShapeDtypeStruct((N_TOK, D_MODEL), F32),
        compiler_params=_params(),
        name="ffn1",
    )(x_ctx, x_lat, mod3, norm_g, w13, w2)


def _ffn_last(x, mod3, norm_g, w13, w2):
    n_ctx = N_CTX_SEQ * CTX_LEN
    return pl.pallas_call(
        _ffn_last_kernel,
        grid=(N_TILES,),
        in_specs=[pl.BlockSpec((TM, D_MODEL), lambda t: (t, 0)),
                  pl.BlockSpec((None, N_MOD, D_MODEL), lambda t: (_mod_row(t), 0, 0)),
                  _const_spec(norm_g.shape),
                  _const_spec(w13.shape),
                  _const_spec(w2.shape)],
        out_specs=[pl.BlockSpec((TM, D_MODEL), _CTX_TILE),
                   pl.BlockSpec((TM, D_MODEL), _LAT_TILE)],
        out_shape=[jax.ShapeDtypeStruct((n_ctx, D_MODEL), F32),
                   jax.ShapeDtypeStruct((N_TOK - n_ctx, D_MODEL), F32)],
        compiler_params=_params(),
        name="ffn2",
    )(x, mod3, norm_g, w13, w2)


def _front_kernel(x_ref, xp_ref, xn_ref, mod_ref, g_ref, win_ref, mu_ref,
                  dw0_ref, dw1_ref, dw2_ref, ia0_ref, ia1_ref, ia2_ref,
                  kkw_ref, ka_ref, rk_ref, cw_ref, cbias_ref, wbb_ref,
                  *outs):
    ops_o = (outs[0:N_DIR_OPS], outs[N_DIR_OPS:2 * N_DIR_OPS])
    v_o, ends0_o, ends1_o, bv_o, sg_o, sa_o, mb_o = outs[2 * N_DIR_OPS:]
    ends_o = (ends0_o, ends1_o)
    t = pl.program_id(0)
    is_lat, first, last = _tile_info(t)
    shift = mod_ref[3:4, :]
    scale = mod_ref[4:5, :]
    g2 = g_ref[2:3, :]

    def pre(x):
        return _rms(x, g2) * (1.0 + scale) + shift

    h = pre(x_ref[...])
    hp = jnp.where(first, 0.0, pre(xp_ref[...])[7:8, :])
    hn = jnp.where(last, 0.0, pre(xn_ref[...])[0:1, :])
    row = lax.broadcasted_iota(jnp.int32, (TM, 1), 0)
    prev = jnp.where(row == 0, hp, pltpu.roll(h, 1, 0))
    nxt = jnp.where(row == TM - 1, hn, pltpu.roll(h, TM - 1, 0))

    proj = _dot(h, win_ref[...])
    r = proj[:, 0:512]
    k = proj[:, 512:1024]
    v = proj[:, 1024:1536]
    g = proj[:, 1536:2048]
    cgate = proj[:, 2048:2560]
    cc = proj[:, 2560:3072]
    xc = proj[:, 3072:3584]
    gate_a = proj[:, 3584:4608]
    gate_b = proj[:, 4608:5632]

    ones = _head_ones()
    kk = k * kkw_ref[...]
    kkn = kk * lax.rsqrt(_head_sum(kk * kk, ones) + 1e-12)
    ka = ka_ref[...]

    v_o[...] = v.astype(BF16)
    sg_o[...] = jax.nn.sigmoid(g)
    sa_o[...] = jax.nn.sigmoid(gate_a)

    rin = row & (CHUNK - 1)
    ksum = None
    for d, src in enumerate((prev, nxt)):
        sh = src - h
        xw = h + mu_ref[2 * d:2 * d + 1, :] * sh
        xa = h + mu_ref[2 * d + 1:2 * d + 2, :] * sh
        z = dw0_ref[d:d + 1, :] + _dot(jnp.tanh(_dot(xw, dw1_ref[d])), dw2_ref[d])
        lw = -EXP_M05 * jax.nn.sigmoid(z)
        a = jax.nn.sigmoid(ia0_ref[d:d + 1, :] + _dot(_dot(xa, ia1_ref[d]), ia2_ref[d]))
        k_d = k * (1.0 + (a - 1.0) * ka)
        b = kkn * a
        ksum = k_d if ksum is None else ksum + k_d

        cs = lw
        for s in (1, 2, 4, 8, 16, 32):
            if d == 0:
                cs = cs + jnp.where(rin >= s, pltpu.roll(cs, s, 0), 0.0)
            else:
                cs = cs + jnp.where(rin < CHUNK - s, pltpu.roll(cs, TM - s, 0), 0.0)
        end_row = CHUNK - 1 if d == 0 else 0
        ends = [cs[c * CHUNK + end_row:c * CHUNK + end_row + 1, :] for c in range(N_CHUNK)]
        for c in range(N_CHUNK):
            ends_o[d][c:c + 1, :] = ends[c]
        cs_end = jnp.concatenate([jnp.broadcast_to(e, (CHUNK, D_RWKV)) for e in ends], axis=0)
        dec_inv = jnp.exp(-cs)
        dec_rest = jnp.exp(cs_end - cs)
        o_a, o_r, o_b, o_k, o_bh, o_kh = ops_o[d]
        o_a[...] = (-kkn * jnp.exp(cs - lw)).astype(BF16)
        o_r[...] = (r * jnp.exp(cs)).astype(BF16)
        o_b[...] = (b * dec_inv).astype(BF16)
        o_k[...] = (k_d * dec_inv).astype(BF16)
        o_bh[...] = (b * dec_rest).astype(BF16)
        o_kh[...] = (k_d * dec_rest).astype(BF16)

    bv_o[...] = _head_sum(r * ksum * rk_ref[...], ones) * v

    u = cc * xc
    col = row & (GRID_W - 1)
    zl = jnp.logical_or(row == 0, jnp.logical_and(is_lat, col == 0))
    zr = jnp.logical_or(row == TM - 1, jnp.logical_and(is_lat, col == GRID_W - 1))
    left = jnp.where(zl, 0.0, pltpu.roll(u, 1, 0))
    right = jnp.where(zr, 0.0, pltpu.roll(u, TM - 1, 0))
    conv = left * cw_ref[0:1, :] + u * cw_ref[1:2, :] + right * cw_ref[2:3, :] + cbias_ref[...]
    y_b = _dot(cgate * conv, wbb_ref[...])
    mb_o[...] = jax.nn.sigmoid(gate_b) * y_b


def _front(x, mod3, norm_g, w_in, mu, dw0, dw1, dw2, ia0, ia1, ia2, kkw, ka, rk, cw, cbias, wbb):
    tok = lambda t: (t, 0)
    rows8 = TM // 8
    last8 = N_TOK // 8 - 1
    consts = [norm_g, w_in, mu, dw0, dw1, dw2, ia0, ia1, ia2, kkw, ka, rk, cw, cbias, wbb]
    out_r = jax.ShapeDtypeStruct((N_TOK, D_RWKV), F32)
    out_rb = jax.ShapeDtypeStruct((N_TOK, D_RWKV), BF16)
    out_d = jax.ShapeDtypeStruct((N_TOK, D_MODEL), F32)
    out_e = jax.ShapeDtypeStruct((N_TILES, N_CHUNK, D_RWKV), F32)
    spec_r = pl.BlockSpec((TM, D_RWKV), tok)
    spec_d = pl.BlockSpec((TM, D_MODEL), tok)
    spec_e = pl.BlockSpec((None, N_CHUNK, D_RWKV), lambda t: (t, 0, 0))
    n_b = 2 * N_DIR_OPS + 1
    return pl.pallas_call(
        _front_kernel,
        grid=(N_TILES,),
        in_specs=[pl.BlockSpec((TM, D_MODEL), tok),
                  pl.BlockSpec((8, D_MODEL), lambda t: (jnp.maximum(t * rows8 - 1, 0), 0)),
                  pl.BlockSpec((8, D_MODEL), lambda t: (jnp.minimum((t + 1) * rows8, last8), 0)),
                  pl.BlockSpec((None, N_MOD, D_MODEL), lambda t: (_mod_row(t), 0, 0))]
                 + [_const_spec(c.shape) for c in consts],
        out_specs=[spec_r] * n_b + [spec_e] * 2 + [spec_r] * 2 + [spec_d] * 2,
        out_shape=[out_rb] * n_b + [out_e] * 2 + [out_r] * 2 + [out_d] * 2,
        compiler_params=_params(),
        name="mixer_front",
    )(x, x, x, mod3, *consts)


def _stack(x):
    lane_lo = lax.broadcasted_iota(jnp.int32, x.shape, 1) < HEAD
    z = jnp.zeros_like(x)
    return jnp.concatenate([jnp.where(lane_lo, x, z), jnp.where(lane_lo, z, x)], axis=0)


def _bdot(a, b, dims=NN):
    return lax.dot_general(a, b, dims, preferred_element_type=F32)


def _scan_kernel(*refs):
    ops = (refs[0:N_DIR_OPS + 1], refs[N_DIR_OPS + 1:2 * N_DIR_OPS + 2])
    ends0_ref, ends1_ref, s0_ref, yf_ref, yb_ref, sout_ref, st_ref = refs[2 * N_DIR_OPS + 2:]
    ends_ref = (ends0_ref, ends1_ref)
    t = pl.program_id(0)
    is_lat, first, _ = _tile_info(t)

    @pl.when(first)
    def _():
        st_ref[...] = jnp.where(is_lat, s0_ref[...], 0.0)

    ti = lax.broadcasted_iota(jnp.int32, (CHUNK, LANES), 0)
    sj = lax.broadcasted_iota(jnp.int32, (CHUNK, LANES), 1) & (CHUNK - 1)
    m_strict = (sj < ti, sj > ti)
    m_incl = (sj <= ti, sj >= ti)
    eye_cat = jnp.where(sj == ti, 1.0, 0.0)
    bi = lax.broadcasted_iota(jnp.int32, (LANES, LANES), 0)
    bj = lax.broadcasted_iota(jnp.int32, (LANES, LANES), 1)
    blk = (bi >> HEAD_SHIFT) == (bj >> HEAD_SHIFT)
    eye_bd = bi == bj
    zero_bd = jnp.zeros((LANES, LANES), BF16)

    units = [(d, p, c) for d in (0, 1) for p in range(N_PAIR) for c in range(N_CHUNK)]

    def op(i, u):
        d, p, c = u
        return ops[d][i][c * CHUNK:(c + 1) * CHUNK, p * LANES:(p + 1) * LANES]

    low, nak, mrbk = {}, {}, {}
    for u in units:
        lhs = jnp.concatenate([op(OP_A, u), op(OP_R, u)], axis=0)
        rhs = jnp.concatenate([_stack(op(OP_B, u)), _stack(op(OP_K, u))], axis=0)
        gram = _bdot(lhs, rhs, NT)
        d = u[0]
        low[u] = jnp.where(m_strict[d], gram[:CHUNK, :LANES], 0.0)
        nak[u] = jnp.where(m_strict[d], gram[:CHUNK, LANES:], 0.0).astype(BF16)
        mrbk[u] = jnp.concatenate([jnp.where(m_incl[d], gram[CHUNK:, :LANES], 0.0),
                                   jnp.where(m_incl[d], gram[CHUNK:, LANES:], 0.0)], axis=1).astype(BF16)

    inv = {u: eye_cat + low[u] for u in units}
    pwb = {u: low[u].astype(BF16) for u in units}
    spw = {u: _stack(pwb[u]) for u in units}
    for _ in range(5):
        for u in units:
            pwb[u] = _bdot(pwb[u], spw[u]).astype(BF16)
        for u in units:
            spw[u] = _stack(pwb[u])
        for u in units:
            inv[u] = inv[u] + _bdot(inv[u].astype(BF16), spw[u])

    sv = {u: _stack(op(OP_V, u)) for u in units}
    nv = {u: _bdot(nak[u], sv[u]).astype(BF16) for u in units}
    x1, x2 = {}, {}
    for u in units:
        x12 = _bdot(inv[u].astype(BF16), jnp.concatenate([_stack(op(OP_A, u)), _stack(nv[u])], axis=1))
        x1[u] = x12[:, :LANES].astype(BF16)
        x2[u] = x12[:, LANES:].astype(BF16)
    g_mat, w_mat = {}, {}
    for u in units:
        rhs = jnp.concatenate([jnp.concatenate([_stack(x1[u]), _stack(x2[u])], axis=1),
                               jnp.concatenate([zero_bd, sv[u]], axis=1)], axis=0)
        gw = _bdot(mrbk[u], rhs)
        g_mat[u] = (op(OP_R, u).astype(F32) + gw[:, :LANES]).astype(BF16)
        w_mat[u] = gw[:, LANES:]
    p_t, q_t = {}, {}
    for u in units:
        d, p, c = u
        gam = jnp.exp(ends_ref[d][c:c + 1, p * LANES:(p + 1) * LANES])
        p_t[u] = (jnp.where(blk, _bdot(x1[u], op(OP_BH, u), TN), 0.0)
                  + jnp.where(eye_bd, gam, 0.0)).astype(BF16)
        q_t[u] = jnp.where(blk, _bdot(jnp.concatenate([x2[u], op(OP_V, u)], axis=0),
                                      jnp.concatenate([op(OP_BH, u), op(OP_KH, u)], axis=0), TN), 0.0)

    state = {(d, p): st_ref[d, p] for d in (0, 1) for p in range(N_PAIR)}
    y_refs = (yf_ref, yb_ref)
    for step in range(N_CHUNK):
        for d in (0, 1):
            c = step if d == 0 else N_CHUNK - 1 - step
            for p in range(N_PAIR):
                u = (d, p, c)
                sb = state[d, p].astype(BF16)
                y_refs[d][c * CHUNK:(c + 1) * CHUNK, p * LANES:(p + 1) * LANES] = (
                    _bdot(g_mat[u], sb, NT) + w_mat[u])
                state[d, p] = _bdot(sb, p_t[u]) + q_t[u]
    for (d, p), s in state.items():
        st_ref[d, p] = s
        sout_ref[d, 2 * p] = s[:HEAD, :HEAD]
        sout_ref[d, 2 * p + 1] = s[HEAD:, HEAD:]


def _scan(ops_f, ops_b, v, ends_f, ends_b, s0):
    fwd = pl.BlockSpec((TM, D_RWKV), lambda t: (t, 0))
    bwd = pl.BlockSpec((TM, D_RWKV), lambda t: (_mirror_tile(t), 0))
    e_block = (None, N_CHUNK, D_RWKV)
    st_block = (None, 2, N_PAIR, LANES, LANES)
    out_y = jax.ShapeDtypeStruct((N_TOK, D_RWKV), F32)
    return pl.pallas_call(
        _scan_kernel,
        grid=(N_TILES,),
        in_specs=[fwd] * (N_DIR_OPS + 1) + [bwd] * (N_DIR_OPS + 1)
                 + [pl.BlockSpec(e_block, lambda t: (t, 0, 0)),
                    pl.BlockSpec(e_block, lambda t: (_mirror_tile(t), 0, 0)),
                    pl.BlockSpec(st_block, lambda t: (jnp.maximum(_seq_id(t) - N_CTX_SEQ, 0), 0, 0, 0, 0))],
        out_specs=[fwd, bwd,
                   pl.BlockSpec((None, 2, 2 * N_PAIR, HEAD, HEAD), lambda t: (_seq_id(t), 0, 0, 0, 0))],
        out_shape=[out_y, out_y,
                   jax.ShapeDtypeStruct((N_CTX_SEQ + N_LAT_SEQ, 2, 2 * N_PAIR, HEAD, HEAD), F32)],
        scratch_shapes=[pltpu.VMEM((2, N_PAIR, LANES, LANES), F32)],
        compiler_params=_params(),
        name="rwkv7_scan",
    )(*ops_f, v, *ops_b, v, ends_f, ends_b, s0)


def _back_kernel(x_ref, yf_ref, yb_ref, bv_ref, sg_ref, sa_ref, mb_ref, mod_ref, g_ref,
                 gng_ref, gnb_ref, wba_ref, wout_ref, o_ref):
    ones = _head_ones()
    y = yf_ref[...] + yb_ref[...]
    mean = _head_sum(y, ones) * (1.0 / HEAD)
    yc = y - mean
    var = _head_sum(yc * yc, ones) * (1.0 / HEAD)
    yn = yc * lax.rsqrt(var + EPS_GN) * gng_ref[...] + gnb_ref[...]
    ya = _dot((yn + bv_ref[...]) * sg_ref[...], wba_ref[...])
    merged = sa_ref[...] * ya + mb_ref[...]
    out = _dot(merged, wout_ref[...])
    o_ref[...] = x_ref[...] + mod_ref[5:6, :] * _rms(out, g_ref[3:4, :])


def _back(x, yf, yb, bv, sg, sa, mb, mod3, norm_g, gng, gnb, wba, wout):
    tok = lambda t: (t, 0)
    spec_r = pl.BlockSpec((TM, D_RWKV), tok)
    spec_d = pl.BlockSpec((TM, D_MODEL), tok)
    consts = [norm_g, gng, gnb, wba, wout]
    return pl.pallas_call(
        _back_kernel,
        grid=(N_TILES,),
        in_specs=[spec_d, spec_r, spec_r, spec_r, spec_r, spec_d, spec_d,
                  pl.BlockSpec((None, N_MOD, D_MODEL), lambda t: (_mod_row(t), 0, 0))]
                 + [_const_spec(c.shape) for c in consts],
        out_specs=spec_d,
        out_shape=jax.ShapeDtypeStruct((N_TOK, D_MODEL), F32),
        compiler_params=_params(),
        name="mixer_back",
    )(x, yf, yb, bv, sg, sa, mb, mod3, *consts)


def _pair_blockdiag(s):
    lead = s.shape[:-3]
    s = s.reshape(lead + (N_PAIR, 2, HEAD, HEAD))
    z = jnp.zeros_like(s[..., 0, :, :])
    top = jnp.concatenate([s[..., 0, :, :], z], axis=-1)
    bot = jnp.concatenate([z, s[..., 1, :, :]], axis=-1)
    return jnp.concatenate([top, bot], axis=-2)


def kernel(x_prompt, x_sample, c, state_rwkv, c_ctx, w_mod, b_mod, norm_g, ffn1_w13, ffn1_w2,
           ffn2_w13, ffn2_w2, w_in, mu_shift, decay_w0, decay_w1, decay_w2, iclr_a0, iclr_a1,
           iclr_a2, k_k, k_a, r_k, gn_gain, gn_bias, conv_w, conv_b, w_branch_a, w_branch_b, w_out):
    assert x_prompt.shape == (N_CTX_SEQ, CTX_LEN, D_MODEL) and x_sample.shape == (N_LAT_SEQ, LAT_LEN, D_MODEL)
    assert w_mod.shape[0] == 1, "single trunk layer"
    bf = lambda w: w.astype(BF16)

    cvec = jnp.concatenate([c_ctx[None, :], c, jnp.zeros((MOD_ROWS - 1 - N_LAT_SEQ, D_MODEL), F32)], axis=0)
    mod3 = _modulation(cvec, w_mod[0], b_mod).reshape(MOD_ROWS, N_MOD, D_MODEL)
    g = norm_g[0]

    x = _ffn_first(x_prompt.reshape(-1, D_MODEL), x_sample.reshape(-1, D_MODEL), mod3, g,
                   bf(ffn1_w13[0]), bf(ffn1_w2[0]))

    row = lambda p: p.reshape(1, -1)
    front = _front(x, mod3, g, bf(w_in[0]), mu_shift[0].reshape(4, D_MODEL),
                   decay_w0[0], bf(decay_w1[0]), bf(decay_w2[0]),
                   iclr_a0[0], bf(iclr_a1[0]), bf(iclr_a2[0]),
                   row(k_k[0]), row(k_a[0]), row(r_k[0]), conv_w[0], row(conv_b[0]), bf(w_branch_b[0]))
    ops_f, ops_b = front[0:N_DIR_OPS], front[N_DIR_OPS:2 * N_DIR_OPS]
    v, ends_f, ends_b, bv, sg, sa, mb = front[2 * N_DIR_OPS:]

    s0 = _pair_blockdiag(state_rwkv[:, 0])
    yf, yb, s_fin = _scan(ops_f, ops_b, v, ends_f, ends_b, s0)

    x = _back(x, yf, yb, bv, sg, sa, mb, mod3, g, row(gn_gain[0]), row(gn_bias[0]),
              bf(w_branch_a[0]), bf(w_out[0]))
    y_ctx, y_lat = _ffn_last(x, mod3, g, bf(ffn2_w13[0]), bf(ffn2_w2[0]))

    y_prompt = y_ctx.reshape(N_CTX_SEQ, CTX_LEN, D_MODEL)
    y_sample = y_lat.reshape(N_LAT_SEQ, LAT_LEN, D_MODEL)
    new_state = s_fin[:N_CTX_SEQ, None]
    return y_prompt, y_sample, new_state
```

```python
import functools

import jax
import jax.numpy as jnp
from jax import lax
from jax.experimental import pallas as pl
from jax.experimental.pallas import tpu as pltpu

F32 = jnp.float32
BF16 = jnp.bfloat16

D_MODEL = 1024
D_FF = 2816
D_RWKV = 512
D_CONV = 512
HEAD = 64
HEAD_SHIFT = 6
D_IN = 4 * D_RWKV + 3 * D_CONV + 2 * D_MODEL
N_MOD = 9
EPS_RMS = 1e-6
EPS_GN = 64e-5
HALF_STEP = 0.5
EXP_M05 = 0.6065306597126334

N_CTX_SEQ = 16
CTX_LEN = 256
N_LAT_SEQ = 2
LAT_LEN = 2048
GRID_W = 64
N_TOK = N_CTX_SEQ * CTX_LEN + N_LAT_SEQ * LAT_LEN

TM = 256
N_CTX_TILES = N_CTX_SEQ * CTX_LEN // TM
LAT_TILES = LAT_LEN // TM
N_TILES = N_TOK // TM
FFN_TM = 512
FFN_GROUPS = 2
N_FFN_TILES = N_TOK // FFN_TM
N_FFN_CTX_TILES = N_CTX_SEQ * CTX_LEN // FFN_TM
CHUNK = 64
N_CHUNK = TM // CHUNK
LANES = 128
N_PAIR = D_RWKV // LANES
MOD_ROWS = 8
VMEM_LIMIT = 56 * 1024 * 1024

OP_A, OP_R, OP_B, OP_K, OP_BH, OP_KH, OP_V = range(7)
N_DIR_OPS = 6

NN = (((1,), (0,)), ((), ()))
NT = (((1,), (1,)), ((), ()))
TN = (((0,), (0,)), ((), ()))


def _dot(a, b, dims=NN):
    return lax.dot_general(a.astype(BF16), b.astype(BF16), dims, preferred_element_type=F32)


def _rms(x, g):
    ms = jnp.mean(x * x, axis=-1, keepdims=True)
    return x * lax.rsqrt(ms + EPS_RMS) * g


def _split2(x):
    hi = x.astype(BF16)
    lo = (x - hi.astype(F32)).astype(BF16)
    return hi, lo


def _head_ones():
    i = lax.broadcasted_iota(jnp.int32, (D_RWKV, D_RWKV), 0) >> HEAD_SHIFT
    j = lax.broadcasted_iota(jnp.int32, (D_RWKV, D_RWKV), 1) >> HEAD_SHIFT
    return jnp.where(i == j, 1.0, 0.0).astype(BF16)


def _head_sum(x, ones):
    hi, lo = _split2(x)
    return (jnp.dot(hi, ones, preferred_element_type=F32)
            + jnp.dot(lo, ones, preferred_element_type=F32))


def _tile_info(t):
    is_lat = t >= N_CTX_TILES
    i_in = (t - N_CTX_TILES) % LAT_TILES
    first = jnp.logical_or(jnp.logical_not(is_lat), i_in == 0)
    last = jnp.logical_or(jnp.logical_not(is_lat), i_in == LAT_TILES - 1)
    return is_lat, first, last


def _mod_row(t):
    return jnp.where(t < N_CTX_TILES, 0, 1 + (t - N_CTX_TILES) // LAT_TILES)


def _mirror_tile(t):
    u = t - N_CTX_TILES
    return jnp.where(t < N_CTX_TILES, t,
                     N_CTX_TILES + (u // LAT_TILES) * LAT_TILES + (LAT_TILES - 1 - u % LAT_TILES))


def _seq_id(t):
    return jnp.where(t < N_CTX_TILES, t, N_CTX_TILES + (t - N_CTX_TILES) // LAT_TILES)


def _const_spec(shape):
    nd = len(shape)
    return pl.BlockSpec(shape, lambda *_: (0,) * nd, pipeline_mode=pl.Buffered(1))


def _params(n_axes=1):
    return pltpu.CompilerParams(dimension_semantics=("arbitrary",) * n_axes,
                                vmem_limit_bytes=VMEM_LIMIT)


def _mod_kernel(c_ref, w_ref, b_ref, o_ref):
    c = c_ref[...]
    s = c * jax.nn.sigmoid(c)
    o_ref[...] = _dot(s, w_ref[...]) + b_ref[...]


def _modulation(cvec, w_mod, b_mod):
    nblk = N_MOD
    return pl.pallas_call(
        _mod_kernel,
        grid=(nblk,),
        in_specs=[pl.BlockSpec((MOD_ROWS, D_MODEL), lambda j: (0, 0)),
                  pl.BlockSpec((D_MODEL, D_MODEL), lambda j: (0, j)),
                  pl.BlockSpec((1, D_MODEL), lambda j: (0, j))],
        out_specs=pl.BlockSpec((MOD_ROWS, D_MODEL), lambda j: (0, j)),
        out_shape=jax.ShapeDtypeStruct((MOD_ROWS, N_MOD * D_MODEL), F32),
        compiler_params=_params(),
        name="modulation",
    )(cvec, w_mod, b_mod)


def _ffn_body(x, mod_ref, g_ref, w13_ref, w2_ref, im, ig):
    shift = mod_ref[im:im + 1, :]
    scale = mod_ref[im + 1:im + 2, :]
    gate = mod_ref[im + 2:im + 3, :]
    rows = FFN_TM // FFN_GROUPS
    xs = [x[i * rows:(i + 1) * rows, :] for i in range(FFN_GROUPS)]
    hs = [(_rms(xi, g_ref[ig:ig + 1, :]) * (1.0 + scale) + shift).astype(BF16) for xi in xs]
    gus = [_dot(hi, w13_ref[...]) for hi in hs]
    acts = []
    for gu in gus:
        gt = gu[:, :D_FF]
        up = gu[:, D_FF:]
        acts.append((gt * jax.nn.sigmoid(gt) * up).astype(BF16))
    os_ = [_dot(ai, w2_ref[...]) for ai in acts]
    outs = [xi + HALF_STEP * gate * _rms(oi, g_ref[ig + 1:ig + 2, :]) for xi, oi in zip(xs, os_)]
    return jnp.concatenate(outs, axis=0)


def _ffn_first_kernel(xc_ref, xl_ref, mod_ref, g_ref, w13_ref, w2_ref, o_ref):
    x = jnp.where(pl.program_id(0) < N_FFN_CTX_TILES, xc_ref[...], xl_ref[...])
    o_ref[...] = _ffn_body(x, mod_ref, g_ref, w13_ref, w2_ref, 0, 0)


def _ffn_last_kernel(x_ref, mod_ref, g_ref, w13_ref, w2_ref, oc_ref, ol_ref):
    out = _ffn_body(x_ref[...], mod_ref, g_ref, w13_ref, w2_ref, 6, 4)
    t = pl.program_id(0)

    @pl.when(t < N_FFN_CTX_TILES)
    def _():
        oc_ref[...] = out

    @pl.when(t >= N_FFN_CTX_TILES)
    def _():
        ol_ref[...] = out


_FFN_CTX_TILE = lambda t: (jnp.minimum(t, N_FFN_CTX_TILES - 1), 0)
_FFN_LAT_TILE = lambda t: (jnp.maximum(t - N_FFN_CTX_TILES, 0), 0)
_FFN_MOD = lambda t: (_mod_row(t * (FFN_TM // TM)), 0, 0)


def _ffn_first(x_ctx, x_lat, mod3, norm_g, w13, w2):
    return pl.pallas_call(
        _ffn_first_kernel,
        grid=(N_FFN_TILES,),
        in_specs=[pl.BlockSpec((FFN_TM, D_MODEL), _FFN_CTX_TILE),
                  pl.BlockSpec((FFN_TM, D_MODEL), _FFN_LAT_TILE),
                  pl.BlockSpec((None, N_MOD, D_MODEL), _FFN_MOD),
                  _const_spec(norm_g.shape),
                  _const_spec(w13.shape),
                  _const_spec(w2.shape)],
        out_specs=pl.BlockSpec((FFN_TM, D_MODEL), lambda t: (t, 0)),
        out_shape=jax.ShapeDtypeStruct((N_TOK, D_MODEL), F32),
        compiler_params=_params(),
        name="ffn1",
    )(x_ctx, x_lat, mod3, norm_g, w13, w2)


def _ffn_last(x, mod3, norm_g, w13, w2):
    n_ctx = N_CTX_SEQ * CTX_LEN
    return pl.pallas_call(
        _ffn_last_kernel,
        grid=(N_FFN_TILES,),
        in_specs=[pl.BlockSpec((FFN_TM, D_MODEL), lambda t: (t, 0)),
                  pl.BlockSpec((None, N_MOD, D_MODEL), _FFN_MOD),
                  _const_spec(norm_g.shape),
                  _const_spec(w13.shape),
                  _const_spec(w2.shape)],
        out_specs=[pl.BlockSpec((FFN_TM, D_MODEL), _FFN_CTX_TILE),
                   pl.BlockSpec((FFN_TM, D_MODEL), _FFN_LAT_TILE)],
        out_shape=[jax.ShapeDtypeStruct((n_ctx, D_MODEL), F32),
                   jax.ShapeDtypeStruct((N_TOK - n_ctx, D_MODEL), F32)],
        compiler_params=_params(),
        name="ffn2",
    )(x, mod3, norm_g, w13, w2)


def _front_kernel(x_ref, xp_ref, xn_ref, mod_ref, g_ref, win_ref, mu_ref,
                  dw0_ref, dw1_ref, dw2_ref, ia0_ref, ia1_ref, ia2_ref,
                  kkw_ref, ka_ref, rk_ref, cw_ref, cbias_ref, wbb_ref,
                  *outs):
    ops_o = (outs[0:N_DIR_OPS], outs[N_DIR_OPS:2 * N_DIR_OPS])
    v_o, ends0_o, ends1_o, bv_o, sg_o, sa_o, mb_o = outs[2 * N_DIR_OPS:]
    ends_o = (ends0_o, ends1_o)
    t = pl.program_id(0)
    is_lat, first, last = _tile_info(t)
    shift = mod_ref[3:4, :]
    scale = mod_ref[4:5, :]
    g2 = g_ref[2:3, :]

    def pre(x):
        return _rms(x, g2) * (1.0 + scale) + shift

    h = pre(x_ref[...])
    hp = jnp.where(first, 0.0, pre(xp_ref[...])[7:8, :])
    hn = jnp.where(last, 0.0, pre(xn_ref[...])[0:1, :])
    row = lax.broadcasted_iota(jnp.int32, (TM, 1), 0)
    prev = jnp.where(row == 0, hp, pltpu.roll(h, 1, 0))
    nxt = jnp.where(row == TM - 1, hn, pltpu.roll(h, TM - 1, 0))

    proj = _dot(h, win_ref[...])
    r = proj[:, 0:512]
    k = proj[:, 512:1024]
    v = proj[:, 1024:1536]
    g = proj[:, 1536:2048]
    cgate = proj[:, 2048:2560]
    cc = proj[:, 2560:3072]
    xc = proj[:, 3072:3584]
    gate_a = proj[:, 3584:4608]
    gate_b = proj[:, 4608:5632]

    ones = _head_ones()
    kk = k * kkw_ref[...]
    kkn = kk * lax.rsqrt(_head_sum(kk * kk, ones) + 1e-12)
    ka = ka_ref[...]

    v_o[...] = v.astype(BF16)
    sg_o[...] = jax.nn.sigmoid(g).astype(BF16)
    sa_o[...] = jax.nn.sigmoid(gate_a).astype(BF16)

    rin = row & (CHUNK - 1)
    ksum = None
    for d, src in enumerate((prev, nxt)):
        sh = src - h
        xw = h + mu_ref[2 * d:2 * d + 1, :] * sh
        xa = h + mu_ref[2 * d + 1:2 * d + 2, :] * sh
        z = dw0_ref[d:d + 1, :] + _dot(jnp.tanh(_dot(xw, dw1_ref[d])), dw2_ref[d])
        lw = -EXP_M05 * jax.nn.sigmoid(z)
        a = jax.nn.sigmoid(ia0_ref[d:d + 1, :] + _dot(_dot(xa, ia1_ref[d]), ia2_ref[d]))
        k_d = k * (1.0 + (a - 1.0) * ka)
        b = kkn * a
        ksum = k_d if ksum is None else ksum + k_d

        cs = lw
        for s in (1, 2, 4, 8, 16, 32):
            if d == 0:
                cs = cs + jnp.where(rin >= s, pltpu.roll(cs, s, 0), 0.0)
            else:
                cs = cs + jnp.where(rin < CHUNK - s, pltpu.roll(cs, TM - s, 0), 0.0)
        end_row = CHUNK - 1 if d == 0 else 0
        ends = [cs[c * CHUNK + end_row:c * CHUNK + end_row + 1, :] for c in range(N_CHUNK)]
        for c in range(N_CHUNK):
            ends_o[d][c:c + 1, :] = ends[c]
        cs_end = jnp.concatenate([jnp.broadcast_to(e, (CHUNK, D_RWKV)) for e in ends], axis=0)
        dec_inv = jnp.exp(-cs)
        dec_rest = jnp.exp(cs_end - cs)
        o_a, o_r, o_b, o_k, o_bh, o_kh = ops_o[d]
        o_a[...] = (-kkn * jnp.exp(cs - lw)).astype(BF16)
        o_r[...] = (r * jnp.exp(cs)).astype(BF16)
        o_b[...] = (b * dec_inv).astype(BF16)
        o_k[...] = (k_d * dec_inv).astype(BF16)
        o_bh[...] = (b * dec_rest).astype(BF16)
        o_kh[...] = (k_d * dec_rest).astype(BF16)

    bv_o[...] = (_head_sum(r * ksum * rk_ref[...], ones) * v).astype(BF16)

    u = cc * xc
    col = row & (GRID_W - 1)
    zl = jnp.logical_or(row == 0, jnp.logical_and(is_lat, col == 0))
    zr = jnp.logical_or(row == TM - 1, jnp.logical_and(is_lat, col == GRID_W - 1))
    left = jnp.where(zl, 0.0, pltpu.roll(u, 1, 0))
    right = jnp.where(zr, 0.0, pltpu.roll(u, TM - 1, 0))
    conv = left * cw_ref[0:1, :] + u * cw_ref[1:2, :] + right * cw_ref[2:3, :] + cbias_ref[...]
    y_b = _dot(cgate * conv, wbb_ref[...])
    mb_o[...] = (jax.nn.sigmoid(gate_b) * y_b).astype(BF16)


def _front(x, mod3, norm_g, w_in, mu, dw0, dw1, dw2, ia0, ia1, ia2, kkw, ka, rk, cw, cbias, wbb):
    tok = lambda t: (t, 0)
    rows8 = TM // 8
    last8 = N_TOK // 8 - 1
    consts = [norm_g, w_in, mu, dw0, dw1, dw2, ia0, ia1, ia2, kkw, ka, rk, cw, cbias, wbb]
    out_rb = jax.ShapeDtypeStruct((N_TOK, D_RWKV), BF16)
    out_db = jax.ShapeDtypeStruct((N_TOK, D_MODEL), BF16)
    out_e = jax.ShapeDtypeStruct((N_TILES, N_CHUNK, D_RWKV), F32)
    spec_r = pl.BlockSpec((TM, D_RWKV), tok)
    spec_d = pl.BlockSpec((TM, D_MODEL), tok)
    spec_e = pl.BlockSpec((None, N_CHUNK, D_RWKV), lambda t: (t, 0, 0))
    n_b = 2 * N_DIR_OPS + 1
    return pl.pallas_call(
        _front_kernel,
        grid=(N_TILES,),
        in_specs=[pl.BlockSpec((TM, D_MODEL), tok),
                  pl.BlockSpec((8, D_MODEL), lambda t: (jnp.maximum(t * rows8 - 1, 0), 0)),
                  pl.BlockSpec((8, D_MODEL), lambda t: (jnp.minimum((t + 1) * rows8, last8), 0)),
                  pl.BlockSpec((None, N_MOD, D_MODEL), lambda t: (_mod_row(t), 0, 0))]
                 + [_const_spec(c.shape) for c in consts],
        out_specs=[spec_r] * n_b + [spec_e] * 2 + [spec_r] * 2 + [spec_d] * 2,
        out_shape=[out_rb] * n_b + [out_e] * 2 + [out_rb] * 2 + [out_db] * 2,
        compiler_params=_params(),
        name="mixer_front",
    )(x, x, x, mod3, *consts)


def _stack(x):
    lane_lo = lax.broadcasted_iota(jnp.int32, x.shape, 1) < HEAD
    z = jnp.zeros_like(x)
    return jnp.concatenate([jnp.where(lane_lo, x, z), jnp.where(lane_lo, z, x)], axis=0)


def _bdot(a, b, dims=NN):
    return lax.dot_general(a, b, dims, preferred_element_type=F32)


def _scan_kernel(*refs):
    ops = (refs[0:N_DIR_OPS + 1], refs[N_DIR_OPS + 1:2 * N_DIR_OPS + 2])
    ends0_ref, ends1_ref, s0_ref, yf_ref, yb_ref, sout_ref, st_ref = refs[2 * N_DIR_OPS + 2:]
    ends_ref = (ends0_ref, ends1_ref)
    t = pl.program_id(0)
    is_lat, first, _ = _tile_info(t)

    @pl.when(first)
    def _():
        st_ref[...] = jnp.where(is_lat, s0_ref[...], 0.0)

    ti = lax.broadcasted_iota(jnp.int32, (CHUNK, LANES), 0)
    sj = lax.broadcasted_iota(jnp.int32, (CHUNK, LANES), 1) & (CHUNK - 1)
    m_strict = (sj < ti, sj > ti)
    m_incl = (sj <= ti, sj >= ti)
    eye_cat = jnp.where(sj == ti, 1.0, 0.0)
    bi = lax.broadcasted_iota(jnp.int32, (LANES, LANES), 0)
    bj = lax.broadcasted_iota(jnp.int32, (LANES, LANES), 1)
    blk = (bi >> HEAD_SHIFT) == (bj >> HEAD_SHIFT)
    eye_bd = bi == bj
    zero_bd = jnp.zeros((LANES, LANES), BF16)

    units = [(d, p, c) for d in (0, 1) for p in range(N_PAIR) for c in range(N_CHUNK)]

    def op(i, u):
        d, p, c = u
        return ops[d][i][c * CHUNK:(c + 1) * CHUNK, p * LANES:(p + 1) * LANES]

    low, nak, mrbk = {}, {}, {}
    for u in units:
        lhs = jnp.concatenate([op(OP_A, u), op(OP_R, u)], axis=0)
        rhs = jnp.concatenate([_stack(op(OP_B, u)), _stack(op(OP_K, u))], axis=0)
        gram = _bdot(lhs, rhs, NT)
        d = u[0]
        low[u] = jnp.where(m_strict[d], gram[:CHUNK, :LANES], 0.0)
        nak[u] = jnp.where(m_strict[d], gram[:CHUNK, LANES:], 0.0).astype(BF16)
        mrbk[u] = jnp.concatenate([jnp.where(m_incl[d], gram[CHUNK:, :LANES], 0.0),
                                   jnp.where(m_incl[d], gram[CHUNK:, LANES:], 0.0)], axis=1).astype(BF16)

    inv = {u: eye_cat + low[u] for u in units}
    pwb = {u: low[u].astype(BF16) for u in units}
    for u in units:
        pwb[u] = _bdot(pwb[u], _stack(pwb[u])).astype(BF16)
    for _ in range(4):
        for u in units:
            both = _bdot(pwb[u], jnp.concatenate([_stack(pwb[u]), _stack(inv[u].astype(BF16))], axis=1))
            pwb[u] = both[:, :LANES].astype(BF16)
            inv[u] = inv[u] + both[:, LANES:]
    for u in units:
        inv[u] = inv[u] + _bdot(pwb[u], _stack(inv[u].astype(BF16)))

    sv = {u: _stack(op(OP_V, u)) for u in units}
    nv = {u: _bdot(nak[u], sv[u]).astype(BF16) for u in units}
    x1, x2 = {}, {}
    for u in units:
        x12 = _bdot(inv[u].astype(BF16), jnp.concatenate([_stack(op(OP_A, u)), _stack(nv[u])], axis=1))
        x1[u] = x12[:, :LANES].astype(BF16)
        x2[u] = x12[:, LANES:].astype(BF16)
    g_mat, w_mat = {}, {}
    for u in units:
        rhs = jnp.concatenate([jnp.concatenate([_stack(x1[u]), _stack(x2[u])], axis=1),
                               jnp.concatenate([zero_bd, sv[u]], axis=1)], axis=0)
        gw = _bdot(mrbk[u], rhs)
        g_mat[u] = (op(OP_R, u).astype(F32) + gw[:, :LANES]).astype(BF16)
        w_mat[u] = gw[:, LANES:]
    p_t, q_t = {}, {}
    for u in units:
        d, p, c = u
        gam = jnp.exp(ends_ref[d][c:c + 1, p * LANES:(p + 1) * LANES])
        p_t[u] = (jnp.where(blk, _bdot(x1[u], op(OP_BH, u), TN), 0.0)
                  + jnp.where(eye_bd, gam, 0.0)).astype(BF16)
        q_t[u] = jnp.where(blk, _bdot(jnp.concatenate([x2[u], op(OP_V, u)], axis=0),
                                      jnp.concatenate([op(OP_BH, u), op(OP_KH, u)], axis=0), TN), 0.0)

    state = {(d, p): st_ref[d, p] for d in (0, 1) for p in range(N_PAIR)}
    y_refs = (yf_ref, yb_ref)
    for step in range(N_CHUNK):
        for d in (0, 1):
            c = step if d == 0 else N_CHUNK - 1 - step
            for p in range(N_PAIR):
                u = (d, p, c)
                sb = state[d, p].astype(BF16)
                y_refs[d][c * CHUNK:(c + 1) * CHUNK, p * LANES:(p + 1) * LANES] = (
                    _bdot(g_mat[u], sb, NT) + w_mat[u])
                state[d, p] = _bdot(sb, p_t[u]) + q_t[u]
    for (d, p), s in state.items():
        st_ref[d, p] = s

    @pl.when(jnp.logical_not(is_lat))
    def _():
        for (d, p), s in state.items():
            sout_ref[0, d, 2 * p] = s[:HEAD, :HEAD]
            sout_ref[0, d, 2 * p + 1] = s[HEAD:, HEAD:]


def _scan(ops_f, ops_b, v, ends_f, ends_b, s0):
    fwd = pl.BlockSpec((TM, D_RWKV), lambda t: (t, 0))
    bwd = pl.BlockSpec((TM, D_RWKV), lambda t: (_mirror_tile(t), 0))
    e_block = (None, N_CHUNK, D_RWKV)
    st_block = (None, 2, N_PAIR, LANES, LANES)
    out_y = jax.ShapeDtypeStruct((N_TOK, D_RWKV), F32)
    return pl.pallas_call(
        _scan_kernel,
        grid=(N_TILES,),
        in_specs=[fwd] * (N_DIR_OPS + 1) + [bwd] * (N_DIR_OPS + 1)
                 + [pl.BlockSpec(e_block, lambda t: (t, 0, 0)),
                    pl.BlockSpec(e_block, lambda t: (_mirror_tile(t), 0, 0)),
                    pl.BlockSpec(st_block, lambda t: (jnp.maximum(_seq_id(t) - N_CTX_SEQ, 0), 0, 0, 0, 0))],
        out_specs=[fwd, bwd,
                   pl.BlockSpec((None, 1, 2, 2 * N_PAIR, HEAD, HEAD),
                                lambda t: (jnp.minimum(t, N_CTX_SEQ - 1), 0, 0, 0, 0, 0))],
        out_shape=[out_y, out_y,
                   jax.ShapeDtypeStruct((N_CTX_SEQ, 1, 2, 2 * N_PAIR, HEAD, HEAD), F32)],
        scratch_shapes=[pltpu.VMEM((2, N_PAIR, LANES, LANES), F32)],
        compiler_params=_params(),
        name="rwkv7_scan",
    )(*ops_f, v, *ops_b, v, ends_f, ends_b, s0)


def _back_kernel(x_ref, yf_ref, yb_ref, bv_ref, sg_ref, sa_ref, mb_ref, mod_ref, g_ref,
                 gng_ref, gnb_ref, wba_ref, wout_ref, o_ref):
    ones = _head_ones()
    y = yf_ref[...] + yb_ref[...]
    mean = _head_sum(y, ones) * (1.0 / HEAD)
    yc = y - mean
    var = _head_sum(yc * yc, ones) * (1.0 / HEAD)
    yn = yc * lax.rsqrt(var + EPS_GN) * gng_ref[...] + gnb_ref[...]
    ya = _dot((yn + bv_ref[...]) * sg_ref[...], wba_ref[...])
    merged = sa_ref[...] * ya + mb_ref[...]
    out = _dot(merged, wout_ref[...])
    o_ref[...] = x_ref[...] + mod_ref[5:6, :] * _rms(out, g_ref[3:4, :])


def _back(x, yf, yb, bv, sg, sa, mb, mod3, norm_g, gng, gnb, wba, wout):
    tok = lambda t: (t, 0)
    spec_r = pl.BlockSpec((TM, D_RWKV), tok)
    spec_d = pl.BlockSpec((TM, D_MODEL), tok)
    consts = [norm_g, gng, gnb, wba, wout]
    return pl.pallas_call(
        _back_kernel,
        grid=(N_TILES,),
        in_specs=[spec_d, spec_r, spec_r, spec_r, spec_r, spec_d, spec_d,
                  pl.BlockSpec((None, N_MOD, D_MODEL), lambda t: (_mod_row(t), 0, 0))]
                 + [_const_spec(c.shape) for c in consts],
        out_specs=spec_d,
        out_shape=jax.ShapeDtypeStruct((N_TOK, D_MODEL), F32),
        compiler_params=_params(),
        name="mixer_back",
    )(x, yf, yb, bv, sg, sa, mb, mod3, *consts)


def _pair_blockdiag(s):
    lead = s.shape[:-3]
    s = s.reshape(lead + (N_PAIR, 2, HEAD, HEAD))
    z = jnp.zeros_like(s[..., 0, :, :])
    top = jnp.concatenate([s[..., 0, :, :], z], axis=-1)
    bot = jnp.concatenate([z, s[..., 1, :, :]], axis=-1)
    return jnp.concatenate([top, bot], axis=-2)


def kernel(x_prompt, x_sample, c, state_rwkv, c_ctx, w_mod, b_mod, norm_g, ffn1_w13, ffn1_w2,
           ffn2_w13, ffn2_w2, w_in, mu_shift, decay_w0, decay_w1, decay_w2, iclr_a0, iclr_a1,
           iclr_a2, k_k, k_a, r_k, gn_gain, gn_bias, conv_w, conv_b, w_branch_a, w_branch_b, w_out):
    assert x_prompt.shape == (N_CTX_SEQ, CTX_LEN, D_MODEL) and x_sample.shape == (N_LAT_SEQ, LAT_LEN, D_MODEL)
    assert w_mod.shape[0] == 1, "single trunk layer"
    bf = lambda w: w.astype(BF16)

    cvec = jnp.concatenate([c_ctx[None, :], c, jnp.zeros((MOD_ROWS - 1 - N_LAT_SEQ, D_MODEL), F32)], axis=0)
    mod3 = _modulation(cvec, w_mod[0], b_mod).reshape(MOD_ROWS, N_MOD, D_MODEL)
    g = norm_g[0]

    x = _ffn_first(x_prompt.reshape(-1, D_MODEL), x_sample.reshape(-1, D_MODEL), mod3, g,
                   bf(ffn1_w13[0]), bf(ffn1_w2[0]))

    row = lambda p: p.reshape(1, -1)
    front = _front(x, mod3, g, bf(w_in[0]), mu_shift[0].reshape(4, D_MODEL),
                   decay_w0[0], bf(decay_w1[0]), bf(decay_w2[0]),
                   iclr_a0[0], bf(iclr_a1[0]), bf(iclr_a2[0]),
                   row(k_k[0]), row(k_a[0]), row(r_k[0]), conv_w[0], row(conv_b[0]), bf(w_branch_b[0]))
    ops_f, ops_b = front[0:N_DIR_OPS], front[N_DIR_OPS:2 * N_DIR_OPS]
    v, ends_f, ends_b, bv, sg, sa, mb = front[2 * N_DIR_OPS:]

    s0 = _pair_blockdiag(state_rwkv[:, 0])
    yf, yb, s_fin = _scan(ops_f, ops_b, v, ends_f, ends_b, s0)

    x = _back(x, yf, yb, bv, sg, sa, mb, mod3, g, row(gn_gain[0]), row(gn_bias[0]),
              bf(w_branch_a[0]), bf(w_out[0]))
    y_ctx, y_lat = _ffn_last(x, mod3, g, bf(ffn2_w13[0]), bf(ffn2_w2[0]))

    y_prompt = y_ctx.reshape(N_CTX_SEQ, CTX_LEN, D_MODEL)
    y_sample = y_lat.reshape(N_LAT_SEQ, LAT_LEN, D_MODEL)
    return y_prompt, y_sample, s_fin
```

```python
import functools

import jax
import jax.numpy as jnp
from jax import lax
from jax.experimental import pallas as pl
from jax.experimental.pallas import tpu as pltpu

F32 = jnp.float32
BF16 = jnp.bfloat16

D_MODEL = 1024
D_FF = 2816
D_RWKV = 512
D_CONV = 512
HEAD = 64
HEAD_SHIFT = 6
D_IN = 4 * D_RWKV + 3 * D_CONV + 2 * D_MODEL
N_MOD = 9
EPS_RMS = 1e-6
EPS_GN = 64e-5
HALF_STEP = 0.5
EXP_M05 = 0.6065306597126334

N_CTX_SEQ = 16
CTX_LEN = 256
N_LAT_SEQ = 2
LAT_LEN = 2048
GRID_W = 64
N_TOK = N_CTX_SEQ * CTX_LEN + N_LAT_SEQ * LAT_LEN

TM = 256
N_CTX_TILES = N_CTX_SEQ * CTX_LEN // TM
LAT_TILES = LAT_LEN // TM
N_TILES = N_TOK // TM
FRONT_TM = 512
FRONT_GROUPS = FRONT_TM // TM
N_FRONT_CTX_TILES = N_CTX_SEQ * CTX_LEN // FRONT_TM
FRONT_LAT_TILES = LAT_LEN // FRONT_TM
N_FRONT_CONSTS = 9
LORA = 64
FFN_TM = 512
FFN_GROUPS = 2
N_FFN_TILES = N_TOK // FFN_TM
N_FFN_CTX_TILES = N_CTX_SEQ * CTX_LEN // FFN_TM
CHUNK = 64
N_CHUNK = TM // CHUNK
LANES = 128
N_PAIR = D_RWKV // LANES
MOD_ROWS = 8
VMEM_LIMIT = 56 * 1024 * 1024

OP_A, OP_R, OP_B, OP_K, OP_BH, OP_KH, OP_V = range(7)
N_DIR_OPS = 6

NN = (((1,), (0,)), ((), ()))
NT = (((1,), (1,)), ((), ()))
TN = (((0,), (0,)), ((), ()))


def _dot(a, b, dims=NN):
    return lax.dot_general(a.astype(BF16), b.astype(BF16), dims, preferred_element_type=F32)


def _rms(x, g):
    ms = jnp.mean(x * x, axis=-1, keepdims=True)
    return x * lax.rsqrt(ms + EPS_RMS) * g


def _split2(x):
    hi = x.astype(BF16)
    lo = (x - hi.astype(F32)).astype(BF16)
    return hi, lo


def _head_ones():
    i = lax.broadcasted_iota(jnp.int32, (D_RWKV, D_RWKV), 0) >> HEAD_SHIFT
    j = lax.broadcasted_iota(jnp.int32, (D_RWKV, D_RWKV), 1) >> HEAD_SHIFT
    return jnp.where(i == j, 1.0, 0.0).astype(BF16)


def _head_sum(x, ones):
    return jnp.dot(x.astype(BF16), ones, preferred_element_type=F32)


def _sigmoid(x):
    return 0.5 * jnp.tanh(0.5 * x) + 0.5


def _tile_info(t):
    is_lat = t >= N_CTX_TILES
    i_in = (t - N_CTX_TILES) % LAT_TILES
    first = jnp.logical_or(jnp.logical_not(is_lat), i_in == 0)
    last = jnp.logical_or(jnp.logical_not(is_lat), i_in == LAT_TILES - 1)
    return is_lat, first, last


def _mod_row(t):
    return jnp.where(t < N_CTX_TILES, 0, 1 + (t - N_CTX_TILES) // LAT_TILES)


def _mirror_tile(t):
    u = t - N_CTX_TILES
    return jnp.where(t < N_CTX_TILES, t,
                     N_CTX_TILES + (u // LAT_TILES) * LAT_TILES + (LAT_TILES - 1 - u % LAT_TILES))


def _seq_id(t):
    return jnp.where(t < N_CTX_TILES, t, N_CTX_TILES + (t - N_CTX_TILES) // LAT_TILES)


def _const_spec(shape):
    nd = len(shape)
    return pl.BlockSpec(shape, lambda *_: (0,) * nd, pipeline_mode=pl.Buffered(1))


def _params(n_axes=1):
    return pltpu.CompilerParams(dimension_semantics=("arbitrary",) * n_axes,
                                vmem_limit_bytes=VMEM_LIMIT)


def _mod_kernel(c_ref, w_ref, b_ref, o_ref):
    c = c_ref[...]
    s = c * jax.nn.sigmoid(c)
    o_ref[...] = _dot(s, w_ref[...]) + b_ref[...]


def _modulation(cvec, w_mod, b_mod):
    nblk = N_MOD
    return pl.pallas_call(
        _mod_kernel,
        grid=(nblk,),
        in_specs=[pl.BlockSpec((MOD_ROWS, D_MODEL), lambda j: (0, 0)),
                  pl.BlockSpec((D_MODEL, D_MODEL), lambda j: (0, j)),
                  pl.BlockSpec((1, D_MODEL), lambda j: (0, j))],
        out_specs=pl.BlockSpec((MOD_ROWS, D_MODEL), lambda j: (0, j)),
        out_shape=jax.ShapeDtypeStruct((MOD_ROWS, N_MOD * D_MODEL), F32),
        compiler_params=_params(),
        name="modulation",
    )(cvec, w_mod, b_mod)


def _ffn_body(x, mod_ref, g_ref, w13_ref, w2_ref, im, ig):
    shift = mod_ref[im:im + 1, :]
    scale = mod_ref[im + 1:im + 2, :]
    gate = mod_ref[im + 2:im + 3, :]
    rows = FFN_TM // FFN_GROUPS
    xs = [x[i * rows:(i + 1) * rows, :] for i in range(FFN_GROUPS)]
    hs = [(_rms(xi, g_ref[ig:ig + 1, :]) * (1.0 + scale) + shift).astype(BF16) for xi in xs]
    gus = [_dot(hi, w13_ref[...]) for hi in hs]
    acts = []
    for gu in gus:
        gt = gu[:, :D_FF]
        up = gu[:, D_FF:]
        acts.append((gt * jax.nn.sigmoid(gt) * up).astype(BF16))
    os_ = [_dot(ai, w2_ref[...]) for ai in acts]
    outs = [xi + HALF_STEP * gate * _rms(oi, g_ref[ig + 1:ig + 2, :]) for xi, oi in zip(xs, os_)]
    return jnp.concatenate(outs, axis=0)


def _ffn_first_kernel(xc_ref, xl_ref, mod_ref, g_ref, w13_ref, w2_ref, o_ref):
    x = jnp.where(pl.program_id(0) < N_FFN_CTX_TILES, xc_ref[...], xl_ref[...])
    o_ref[...] = _ffn_body(x, mod_ref, g_ref, w13_ref, w2_ref, 0, 0)


def _ffn_last_kernel(x_ref, mod_ref, g_ref, w13_ref, w2_ref, oc_ref, ol_ref):
    out = _ffn_body(x_ref[...], mod_ref, g_ref, w13_ref, w2_ref, 6, 4)
    t = pl.program_id(0)

    @pl.when(t < N_FFN_CTX_TILES)
    def _():
        oc_ref[...] = out

    @pl.when(t >= N_FFN_CTX_TILES)
    def _():
        ol_ref[...] = out


_FFN_CTX_TILE = lambda t: (jnp.minimum(t, N_FFN_CTX_TILES - 1), 0)
_FFN_LAT_TILE = lambda t: (jnp.maximum(t - N_FFN_CTX_TILES, 0), 0)
_FFN_MOD = lambda t: (_mod_row(t * (FFN_TM // TM)), 0, 0)


def _ffn_first(x_ctx, x_lat, mod3, norm_g, w13, w2):
    return pl.pallas_call(
        _ffn_first_kernel,
        grid=(N_FFN_TILES,),
        in_specs=[pl.BlockSpec((FFN_TM, D_MODEL), _FFN_CTX_TILE),
                  pl.BlockSpec((FFN_TM, D_MODEL), _FFN_LAT_TILE),
                  pl.BlockSpec((None, N_MOD, D_MODEL), _FFN_MOD),
                  _const_spec(norm_g.shape),
                  _const_spec(w13.shape),
                  _const_spec(w2.shape)],
        out_specs=pl.BlockSpec((FFN_TM, D_MODEL), lambda t: (t, 0)),
        out_shape=jax.ShapeDtypeStruct((N_TOK, D_MODEL), F32),
        compiler_params=_params(),
        name="ffn1",
    )(x_ctx, x_lat, mod3, norm_g, w13, w2)


def _ffn_last(x, mod3, norm_g, w13, w2):
    n_ctx = N_CTX_SEQ * CTX_LEN
    return pl.pallas_call(
        _ffn_last_kernel,
        grid=(N_FFN_TILES,),
        in_specs=[pl.BlockSpec((FFN_TM, D_MODEL), lambda t: (t, 0)),
                  pl.BlockSpec((None, N_MOD, D_MODEL), _FFN_MOD),
                  _const_spec(norm_g.shape),
                  _const_spec(w13.shape),
                  _const_spec(w2.shape)],
        out_specs=[pl.BlockSpec((FFN_TM, D_MODEL), _FFN_CTX_TILE),
                   pl.BlockSpec((FFN_TM, D_MODEL), _FFN_LAT_TILE)],
        out_shape=[jax.ShapeDtypeStruct((n_ctx, D_MODEL), F32),
                   jax.ShapeDtypeStruct((N_TOK - n_ctx, D_MODEL), F32)],
        compiler_params=_params(),
        name="ffn2",
    )(x, mod3, norm_g, w13, w2)


def _lora_prep_kernel(mu_ref, w1_ref, o_ref):
    mu = mu_ref[...]
    w1 = w1_ref[...]
    n = w1.shape[1]
    o_ref[:, :n] = ((1.0 - mu) * w1).astype(BF16)
    o_ref[:, n:] = (mu * w1).astype(BF16)


def _lora_prep(mu_cols, w1_cat):
    n = w1_cat.shape[1]
    return pl.pallas_call(
        _lora_prep_kernel,
        out_shape=jax.ShapeDtypeStruct((D_MODEL, 2 * n), BF16),
        compiler_params=pltpu.CompilerParams(vmem_limit_bytes=VMEM_LIMIT),
        name="lora_prep",
    )(mu_cols, w1_cat)


def _front_group(gi, h, halo_prev, halo_next, is_lat, win_ref, waug_ref, refs, outs):
    w2aug_ref, b2aug_ref, kkw_ref, ka_ref, rk_ref, cw_ref, cbias_ref, wbb_ref, ones_ref = refs
    ops_o, v_o, ends_o, bv_o, sg_o, sa_o, mb_o = outs
    rows = slice(gi * TM, (gi + 1) * TM)
    row = lax.broadcasted_iota(jnp.int32, (TM, 1), 0)
    rin = row & (CHUNK - 1)
    ones = ones_ref[...]

    def proj(lo, hi):
        return _bdot(h, win_ref[:, lo:hi])

    pab = _bdot(h, waug_ref[...])
    rk = proj(0, 2 * D_RWKV)
    r = rk[:, :D_RWKV]
    k = rk[:, D_RWKV:]
    pb = pab[:, 2 * LANES:]
    sh_f = jnp.where(row == 0, halo_prev, pltpu.roll(pb[:, :LANES], 1, 0))
    sh_b = jnp.where(row == TM - 1, halo_next, pltpu.roll(pb[:, LANES:], TM - 1, 0))
    t_in = pab[:, :2 * LANES] + jnp.concatenate([sh_f, sh_b], axis=1)
    lane = lax.broadcasted_iota(jnp.int32, (1, 2 * LANES), 1)
    t_in = jnp.where((lane & HEAD) == 0, jnp.tanh(t_in), t_in)
    za = _dot(t_in, w2aug_ref[...]) + b2aug_ref[...]

    kk = k * kkw_ref[...]
    kkn = kk * lax.rsqrt(_head_sum(kk * kk, ones) + 1e-12)
    ka = ka_ref[...]

    vg = proj(2 * D_RWKV, 4 * D_RWKV)
    v = vg[:, :D_RWKV]
    v_o[rows, :] = v.astype(BF16)
    sg_o[rows, :] = _sigmoid(vg[:, D_RWKV:]).astype(BF16)

    def scan_operands(d):
        lw = -EXP_M05 * _sigmoid(za[:, 2 * d * D_RWKV:(2 * d + 1) * D_RWKV])
        a = _sigmoid(za[:, (2 * d + 1) * D_RWKV:(2 * d + 2) * D_RWKV])
        k_d = k * (1.0 + (a - 1.0) * ka)
        b = kkn * a
        cs = lw
        for s in (1, 2, 4, 8, 16, 32):
            if d == 0:
                cs = cs + jnp.where(rin >= s, pltpu.roll(cs, s, 0), 0.0)
            else:
                cs = cs + jnp.where(rin < CHUNK - s, pltpu.roll(cs, TM - s, 0), 0.0)
        end_row = CHUNK - 1 if d == 0 else 0
        ends = [cs[c * CHUNK + end_row:c * CHUNK + end_row + 1, :] for c in range(N_CHUNK)]
        for c in range(N_CHUNK):
            ends_o[d][gi, c:c + 1, :] = ends[c]
        cs_end = jnp.concatenate([jnp.broadcast_to(e, (CHUNK, D_RWKV)) for e in ends], axis=0)
        dec_inv = jnp.exp(-cs)
        dec_rest = jnp.exp(cs_end - cs)
        o_a, o_r, o_b, o_k, o_bh, o_kh = ops_o[d]
        o_a[rows, :] = (-kkn * jnp.exp(cs - lw)).astype(BF16)
        o_r[rows, :] = (r * jnp.exp(cs)).astype(BF16)
        o_b[rows, :] = (b * dec_inv).astype(BF16)
        o_k[rows, :] = (k_d * dec_inv).astype(BF16)
        o_bh[rows, :] = (b * dec_rest).astype(BF16)
        o_kh[rows, :] = (k_d * dec_rest).astype(BF16)
        return k_d

    conv_in = proj(4 * D_RWKV, 4 * D_RWKV + 3 * D_CONV)
    k_0 = scan_operands(0)
    gate_a = proj(4 * D_RWKV + 3 * D_CONV, 4 * D_RWKV + 3 * D_CONV + D_MODEL)
    sa_o[rows, :] = _sigmoid(gate_a).astype(BF16)
    k_1 = scan_operands(1)
    gate_b = proj(4 * D_RWKV + 3 * D_CONV + D_MODEL, D_IN)

    cgate = conv_in[:, :D_CONV]
    u = conv_in[:, D_CONV:2 * D_CONV] * conv_in[:, 2 * D_CONV:]
    col = row & (GRID_W - 1)
    zl = jnp.logical_or(row == 0, jnp.logical_and(is_lat, col == 0))
    zr = jnp.logical_or(row == TM - 1, jnp.logical_and(is_lat, col == GRID_W - 1))
    left = jnp.where(zl, 0.0, pltpu.roll(u, 1, 0))
    right = jnp.where(zr, 0.0, pltpu.roll(u, TM - 1, 0))
    conv = left * cw_ref[0:1, :] + u * cw_ref[1:2, :] + right * cw_ref[2:3, :] + cbias_ref[...]
    y_b = _dot(cgate * conv, wbb_ref[...])
    mb_o[rows, :] = (_sigmoid(gate_b) * y_b).astype(BF16)

    bv_o[rows, :] = (_head_sum(r * (k_0 + k_1) * rk_ref[...], ones) * v).astype(BF16)


def _front_kernel(x_ref, xp_ref, xn_ref, mod_ref, g_ref, win_ref, waug_ref, *rest):
    refs = rest[:N_FRONT_CONSTS]
    outs = rest[N_FRONT_CONSTS:]
    ops_o = (outs[0:N_DIR_OPS], outs[N_DIR_OPS:2 * N_DIR_OPS])
    v_o, ends0_o, ends1_o, bv_o, sg_o, sa_o, mb_o = outs[2 * N_DIR_OPS:]
    outs = (ops_o, v_o, (ends0_o, ends1_o), bv_o, sg_o, sa_o, mb_o)

    s = pl.program_id(0)
    is_lat = s >= N_FRONT_CTX_TILES
    i_in = (s - N_FRONT_CTX_TILES) % FRONT_LAT_TILES
    lat_first = jnp.logical_and(is_lat, i_in == 0)
    lat_last = jnp.logical_and(is_lat, i_in == FRONT_LAT_TILES - 1)
    shift = mod_ref[3:4, :]
    scale = mod_ref[4:5, :]
    g2 = g_ref[2:3, :]

    def pre(x):
        return (_rms(x, g2) * (1.0 + scale) + shift).astype(BF16)

    x = x_ref[...]
    edge = jnp.concatenate([xp_ref[...], x[TM - 8:TM + 8, :], xn_ref[...]], axis=0)
    edge_b = _bdot(pre(edge), waug_ref[:, 2 * LANES:])
    zero = jnp.zeros((1, LANES), F32)
    halo_prev = (jnp.where(jnp.logical_and(is_lat, jnp.logical_not(lat_first)), edge_b[7:8, :LANES], zero),
                 jnp.where(is_lat, edge_b[15:16, :LANES], zero))
    halo_next = (jnp.where(is_lat, edge_b[16:17, LANES:], zero),
                 jnp.where(jnp.logical_and(is_lat, jnp.logical_not(lat_last)), edge_b[24:25, LANES:], zero))

    for gi in range(FRONT_GROUPS):
        h = pre(x[gi * TM:(gi + 1) * TM, :])
        _front_group(gi, h, halo_prev[gi], halo_next[gi], is_lat, win_ref, waug_ref, refs, outs)


def _front(x, mod3, norm_g, w_in, w_aug, consts):
    tok = lambda t: (t, 0)
    rows8 = FRONT_TM // 8
    last8 = N_TOK // 8 - 1
    assert len(consts) == N_FRONT_CONSTS
    out_rb = jax.ShapeDtypeStruct((N_TOK, D_RWKV), BF16)
    out_db = jax.ShapeDtypeStruct((N_TOK, D_MODEL), BF16)
    out_e = jax.ShapeDtypeStruct((N_TILES, N_CHUNK, D_RWKV), F32)
    spec_r = pl.BlockSpec((FRONT_TM, D_RWKV), tok)
    spec_d = pl.BlockSpec((FRONT_TM, D_MODEL), tok)
    spec_e = pl.BlockSpec((FRONT_GROUPS, N_CHUNK, D_RWKV), lambda t: (t, 0, 0))
    n_b = 2 * N_DIR_OPS + 1
    return pl.pallas_call(
        _front_kernel,
        grid=(N_TOK // FRONT_TM,),
        in_specs=[pl.BlockSpec((FRONT_TM, D_MODEL), tok),
                  pl.BlockSpec((8, D_MODEL), lambda t: (jnp.maximum(t * rows8 - 1, 0), 0)),
                  pl.BlockSpec((8, D_MODEL), lambda t: (jnp.minimum((t + 1) * rows8, last8), 0)),
                  pl.BlockSpec((None, N_MOD, D_MODEL), lambda t: (_mod_row(t * FRONT_GROUPS), 0, 0))]
                 + [_const_spec(c.shape) for c in [norm_g, w_in, w_aug] + list(consts)],
        out_specs=[spec_r] * n_b + [spec_e] * 2 + [spec_r] * 2 + [spec_d] * 2,
        out_shape=[out_rb] * n_b + [out_e] * 2 + [out_rb] * 2 + [out_db] * 2,
        compiler_params=_params(),
        name="mixer_front",
    )(x, x, x, mod3, norm_g, w_in, w_aug, *consts)


def _stack(x):
    lane_lo = lax.broadcasted_iota(jnp.int32, x.shape, 1) < HEAD
    z = jnp.zeros_like(x)
    return jnp.concatenate([jnp.where(lane_lo, x, z), jnp.where(lane_lo, z, x)], axis=0)


def _bdot(a, b, dims=NN):
    return lax.dot_general(a, b, dims, preferred_element_type=F32)


def _scan_kernel(*refs):
    ops = (refs[0:N_DIR_OPS + 1], refs[N_DIR_OPS + 1:2 * N_DIR_OPS + 2])
    ends0_ref, ends1_ref, s0_ref, yf_ref, yb_ref, sout_ref, st_ref = refs[2 * N_DIR_OPS + 2:]
    ends_ref = (ends0_ref, ends1_ref)
    t = pl.program_id(0)
    is_lat, first, _ = _tile_info(t)

    @pl.when(first)
    def _():
        st_ref[...] = jnp.where(is_lat, s0_ref[...], 0.0)

    ti = lax.broadcasted_iota(jnp.int32, (CHUNK, LANES), 0)
    sj = lax.broadcasted_iota(jnp.int32, (CHUNK, LANES), 1) & (CHUNK - 1)
    m_strict = (sj < ti, sj > ti)
    m_incl = (sj <= ti, sj >= ti)
    eye_cat = jnp.where(sj == ti, 1.0, 0.0)
    bi = lax.broadcasted_iota(jnp.int32, (LANES, LANES), 0)
    bj = lax.broadcasted_iota(jnp.int32, (LANES, LANES), 1)
    blk = (bi >> HEAD_SHIFT) == (bj >> HEAD_SHIFT)
    eye_bd = bi == bj
    zero_bd = jnp.zeros((LANES, LANES), BF16)

    units = [(d, p, c) for d in (0, 1) for p in range(N_PAIR) for c in range(N_CHUNK)]

    def op(i, u):
        d, p, c = u
        return ops[d][i][c * CHUNK:(c + 1) * CHUNK, p * LANES:(p + 1) * LANES]

    low, nak, mrbk = {}, {}, {}
    for u in units:
        lhs = jnp.concatenate([op(OP_A, u), op(OP_R, u)], axis=0)
        rhs = jnp.concatenate([_stack(op(OP_B, u)), _stack(op(OP_K, u))], axis=0)
        gram = _bdot(lhs, rhs, NT)
        d = u[0]
        low[u] = jnp.where(m_strict[d], gram[:CHUNK, :LANES], 0.0)
        nak[u] = jnp.where(m_strict[d], gram[:CHUNK, LANES:], 0.0).astype(BF16)
        mrbk[u] = jnp.concatenate([jnp.where(m_incl[d], gram[CHUNK:, :LANES], 0.0),
                                   jnp.where(m_incl[d], gram[CHUNK:, LANES:], 0.0)], axis=1).astype(BF16)

    inv = {u: eye_cat + low[u] for u in units}
    pwb = {u: low[u].astype(BF16) for u in units}
    for u in units:
        pwb[u] = _bdot(pwb[u], _stack(pwb[u])).astype(BF16)
    for _ in range(4):
        for u in units:
            both = _bdot(pwb[u], jnp.concatenate([_stack(pwb[u]), _stack(inv[u].astype(BF16))], axis=1))
            pwb[u] = both[:, :LANES].astype(BF16)
            inv[u] = inv[u] + both[:, LANES:]
    for u in units:
        inv[u] = inv[u] + _bdot(pwb[u], _stack(inv[u].astype(BF16)))

    sv = {u: _stack(op(OP_V, u)) for u in units}
    nv = {u: _bdot(nak[u], sv[u]).astype(BF16) for u in units}
    x1, x2 = {}, {}
    for u in units:
        x12 = _bdot(inv[u].astype(BF16), jnp.concatenate([_stack(op(OP_A, u)), _stack(nv[u])], axis=1))
        x1[u] = x12[:, :LANES].astype(BF16)
        x2[u] = x12[:, LANES:].astype(BF16)
    g_mat, w_mat = {}, {}
    for u in units:
        rhs = jnp.concatenate([jnp.concatenate([_stack(x1[u]), _stack(x2[u])], axis=1),
                               jnp.concatenate([zero_bd, sv[u]], axis=1)], axis=0)
        gw = _bdot(mrbk[u], rhs)
        g_mat[u] = (op(OP_R, u).astype(F32) + gw[:, :LANES]).astype(BF16)
        w_mat[u] = gw[:, LANES:]
    p_t, q_t = {}, {}
    for u in units:
        d, p, c = u
        gam = jnp.exp(ends_ref[d][c:c + 1, p * LANES:(p + 1) * LANES])
        p_t[u] = (jnp.where(blk, _bdot(x1[u], op(OP_BH, u), TN), 0.0)
                  + jnp.where(eye_bd, gam, 0.0)).astype(BF16)
        q_t[u] = jnp.where(blk, _bdot(jnp.concatenate([x2[u], op(OP_V, u)], axis=0),
                                      jnp.concatenate([op(OP_BH, u), op(OP_KH, u)], axis=0), TN), 0.0)

    state = {(d, p): st_ref[d, p] for d in (0, 1) for p in range(N_PAIR)}
    y_refs = (yf_ref, yb_ref)
    for step in range(N_CHUNK):
        for d in (0, 1):
            c = step if d == 0 else N_CHUNK - 1 - step
            for p in range(N_PAIR):
                u = (d, p, c)
                sb = state[d, p].astype(BF16)
                y_refs[d][c * CHUNK:(c + 1) * CHUNK, p * LANES:(p + 1) * LANES] = (
                    _bdot(g_mat[u], sb, NT) + w_mat[u])
                state[d, p] = _bdot(sb, p_t[u]) + q_t[u]
    for (d, p), s in state.items():
        st_ref[d, p] = s

    @pl.when(jnp.logical_not(is_lat))
    def _():
        for (d, p), s in state.items():
            sout_ref[0, d, 2 * p] = s[:HEAD, :HEAD]
            sout_ref[0, d, 2 * p + 1] = s[HEAD:, HEAD:]


def _scan(ops_f, ops_b, v, ends_f, ends_b, s0):
    fwd = pl.BlockSpec((TM, D_RWKV), lambda t: (t, 0))
    bwd = pl.BlockSpec((TM, D_RWKV), lambda t: (_mirror_tile(t), 0))
    e_block = (None, N_CHUNK, D_RWKV)
    st_block = (None, 2, N_PAIR, LANES, LANES)
    out_y = jax.ShapeDtypeStruct((N_TOK, D_RWKV), F32)
    return pl.pallas_call(
        _scan_kernel,
        grid=(N_TILES,),
        in_specs=[fwd] * (N_DIR_OPS + 1) + [bwd] * (N_DIR_OPS + 1)
                 + [pl.BlockSpec(e_block, lambda t: (t, 0, 0)),
                    pl.BlockSpec(e_block, lambda t: (_mirror_tile(t), 0, 0)),
                    pl.BlockSpec(st_block, lambda t: (jnp.maximum(_seq_id(t) - N_CTX_SEQ, 0), 0, 0, 0, 0))],
        out_specs=[fwd, bwd,
                   pl.BlockSpec((None, 1, 2, 2 * N_PAIR, HEAD, HEAD),
                                lambda t: (jnp.minimum(t, N_CTX_SEQ - 1), 0, 0, 0, 0, 0))],
        out_shape=[out_y, out_y,
                   jax.ShapeDtypeStruct((N_CTX_SEQ, 1, 2, 2 * N_PAIR, HEAD, HEAD), F32)],
        scratch_shapes=[pltpu.VMEM((2, N_PAIR, LANES, LANES), F32)],
        compiler_params=_params(),
        name="rwkv7_scan",
    )(*ops_f, v, *ops_b, v, ends_f, ends_b, s0)


def _back_kernel(x_ref, yf_ref, yb_ref, bv_ref, sg_ref, sa_ref, mb_ref, mod_ref, g_ref,
                 gng_ref, gnb_ref, wba_ref, wout_ref, ones_ref, o_ref):
    ones = ones_ref[...]
    y = yf_ref[...] + yb_ref[...]
    mean = _head_sum(y, ones) * (1.0 / HEAD)
    yc = y - mean
    var = _head_sum(yc * yc, ones) * (1.0 / HEAD)
    yn = yc * lax.rsqrt(var + EPS_GN) * gng_ref[...] + gnb_ref[...]
    ya = _dot((yn + bv_ref[...]) * sg_ref[...], wba_ref[...])
    merged = sa_ref[...] * ya + mb_ref[...]
    out = _dot(merged, wout_ref[...])
    o_ref[...] = x_ref[...] + mod_ref[5:6, :] * _rms(out, g_ref[3:4, :])


def _back(x, yf, yb, bv, sg, sa, mb, mod3, norm_g, gng, gnb, wba, wout, ones):
    tok = lambda t: (t, 0)
    spec_r = pl.BlockSpec((TM, D_RWKV), tok)
    spec_d = pl.BlockSpec((TM, D_MODEL), tok)
    consts = [norm_g, gng, gnb, wba, wout, ones]
    return pl.pallas_call(
        _back_kernel,
        grid=(N_TILES,),
        in_specs=[spec_d, spec_r, spec_r, spec_r, spec_r, spec_d, spec_d,
                  pl.BlockSpec((None, N_MOD, D_MODEL), lambda t: (_mod_row(t), 0, 0))]
                 + [_const_spec(c.shape) for c in consts],
        out_specs=spec_d,
        out_shape=jax.ShapeDtypeStruct((N_TOK, D_MODEL), F32),
        compiler_params=_params(),
        name="mixer_back",
    )(x, yf, yb, bv, sg, sa, mb, mod3, *consts)


def _pair_blockdiag(s):
    lead = s.shape[:-3]
    s = s.reshape(lead + (N_PAIR, 2, HEAD, HEAD))
    z = jnp.zeros_like(s[..., 0, :, :])
    top = jnp.concatenate([s[..., 0, :, :], z], axis=-1)
    bot = jnp.concatenate([z, s[..., 1, :, :]], axis=-1)
    return jnp.concatenate([top, bot], axis=-2)


def kernel(x_prompt, x_sample, c, state_rwkv, c_ctx, w_mod, b_mod, norm_g, ffn1_w13, ffn1_w2,
           ffn2_w13, ffn2_w2, w_in, mu_shift, decay_w0, decay_w1, decay_w2, iclr_a0, iclr_a1,
           iclr_a2, k_k, k_a, r_k, gn_gain, gn_bias, conv_w, conv_b, w_branch_a, w_branch_b, w_out):
    assert x_prompt.shape == (N_CTX_SEQ, CTX_LEN, D_MODEL) and x_sample.shape == (N_LAT_SEQ, LAT_LEN, D_MODEL)
    assert w_mod.shape[0] == 1, "single trunk layer"
    bf = lambda w: w.astype(BF16)

    cvec = jnp.concatenate([c_ctx[None, :], c, jnp.zeros((MOD_ROWS - 1 - N_LAT_SEQ, D_MODEL), F32)], axis=0)
    mod3 = _modulation(cvec, w_mod[0], b_mod).reshape(MOD_ROWS, N_MOD, D_MODEL)
    g = norm_g[0]

    x = _ffn_first(x_prompt.reshape(-1, D_MODEL), x_sample.reshape(-1, D_MODEL), mod3, g,
                   bf(ffn1_w13[0]), bf(ffn1_w2[0]))

    row = lambda p: p.reshape(1, -1)
    mu4 = mu_shift[0].reshape(4, D_MODEL)
    w1_cat = jnp.concatenate([decay_w1[0, 0], iclr_a1[0, 0], decay_w1[0, 1], iclr_a1[0, 1]], axis=1)
    w_aug = _lora_prep(jnp.repeat(mu4.T, LORA, axis=1), w1_cat)
    w2_blocks = [decay_w2[0, 0], iclr_a2[0, 0], decay_w2[0, 1], iclr_a2[0, 1]]
    w2_zero = jnp.zeros((LORA, D_RWKV), F32)
    w2_aug = bf(jnp.concatenate(
        [jnp.concatenate([blk if j == i else w2_zero for j in range(4)], axis=1)
         for i, blk in enumerate(w2_blocks)], axis=0))
    b2_aug = jnp.concatenate([decay_w0[0, 0], iclr_a0[0, 0], decay_w0[0, 1], iclr_a0[0, 1]]).reshape(1, -1)
    ones = _head_ones()
    front = _front(x, mod3, g, bf(w_in[0]), w_aug,
                   [w2_aug, b2_aug, row(k_k[0]), row(k_a[0]), row(r_k[0]), conv_w[0], row(conv_b[0]),
                    bf(w_branch_b[0]), ones])
    ops_f, ops_b = front[0:N_DIR_OPS], front[N_DIR_OPS:2 * N_DIR_OPS]
    v, ends_f, ends_b, bv, sg, sa, mb = front[2 * N_DIR_OPS:]

    s0 = _pair_blockdiag(state_rwkv[:, 0])
    yf, yb, s_fin = _scan(ops_f, ops_b, v, ends_f, ends_b, s0)

    x = _back(x, yf, yb, bv, sg, sa, mb, mod3, g, row(gn_gain[0]), row(gn_bias[0]),
              bf(w_branch_a[0]), bf(w_out[0]), ones)
    y_ctx, y_lat = _ffn_last(x, mod3, g, bf(ffn2_w13[0]), bf(ffn2_w2[0]))

    y_prompt = y_ctx.reshape(N_CTX_SEQ, CTX_LEN, D_MODEL)
    y_sample = y_lat.reshape(N_LAT_SEQ, LAT_LEN, D_MODEL)
    return y_prompt, y_sample, s_fin
```

```python
import functools

import jax
import jax.numpy as jnp
from jax import lax
from jax.experimental import pallas as pl
from jax.experimental.pallas import tpu as pltpu

F32 = jnp.float32
BF16 = jnp.bfloat16

D_MODEL = 1024
D_FF = 2816
D_RWKV = 512
D_CONV = 512
HEAD = 64
HEAD_SHIFT = 6
D_IN = 4 * D_RWKV + 3 * D_CONV + 2 * D_MODEL
N_MOD = 9
EPS_RMS = 1e-6
EPS_GN = 64e-5
HALF_STEP = 0.5
EXP_M05 = 0.6065306597126334

N_CTX_SEQ = 16
CTX_LEN = 256
N_LAT_SEQ = 2
LAT_LEN = 2048
GRID_W = 64
N_TOK = N_CTX_SEQ * CTX_LEN + N_LAT_SEQ * LAT_LEN

TM = 256
N_CTX_TILES = N_CTX_SEQ * CTX_LEN // TM
LAT_TILES = LAT_LEN // TM
N_TILES = N_TOK // TM
FRONT_TM = 512
FRONT_GROUPS = FRONT_TM // TM
N_FRONT_CTX_TILES = N_CTX_SEQ * CTX_LEN // FRONT_TM
FRONT_LAT_TILES = LAT_LEN // FRONT_TM
N_FRONT_CONSTS = 9
FRONT_WBB = 7
LORA = 64
FFN_TM = 512
FFN_GROUPS = 2
N_FFN_TILES = N_TOK // FFN_TM
N_FFN_CTX_TILES = N_CTX_SEQ * CTX_LEN // FFN_TM
N_WCHUNK = 11
CHUNK = 64
N_CHUNK = TM // CHUNK
LANES = 128
N_PAIR = D_RWKV // LANES
MOD_ROWS = 8
VMEM_LIMIT = 56 * 1024 * 1024

OP_A, OP_R, OP_B, OP_K, OP_BH, OP_KH, OP_V = range(7)
N_DIR_OPS = 6

NN = (((1,), (0,)), ((), ()))
NT = (((1,), (1,)), ((), ()))
TN = (((0,), (0,)), ((), ()))


def _dot(a, b, dims=NN):
    return lax.dot_general(a.astype(BF16), b.astype(BF16), dims, preferred_element_type=F32)


def _rms(x, g):
    ms = jnp.mean(x * x, axis=-1, keepdims=True)
    return x * lax.rsqrt(ms + EPS_RMS) * g


def _split2(x):
    hi = x.astype(BF16)
    lo = (x - hi.astype(F32)).astype(BF16)
    return hi, lo


def _head_ones():
    i = lax.broadcasted_iota(jnp.int32, (D_RWKV, D_RWKV), 0) >> HEAD_SHIFT
    j = lax.broadcasted_iota(jnp.int32, (D_RWKV, D_RWKV), 1) >> HEAD_SHIFT
    return jnp.where(i == j, 1.0, 0.0).astype(BF16)


def _head_sum(x, ones):
    return jnp.dot(x.astype(BF16), ones, preferred_element_type=F32)


def _sigmoid(x):
    return 0.5 * jnp.tanh(0.5 * x) + 0.5


def _tile_info(t):
    is_lat = t >= N_CTX_TILES
    i_in = (t - N_CTX_TILES) % LAT_TILES
    first = jnp.logical_or(jnp.logical_not(is_lat), i_in == 0)
    last = jnp.logical_or(jnp.logical_not(is_lat), i_in == LAT_TILES - 1)
    return is_lat, first, last


def _mod_row(t):
    return jnp.where(t < N_CTX_TILES, 0, 1 + (t - N_CTX_TILES) // LAT_TILES)


def _mirror_tile(t):
    u = t - N_CTX_TILES
    return jnp.where(t < N_CTX_TILES, t,
                     N_CTX_TILES + (u // LAT_TILES) * LAT_TILES + (LAT_TILES - 1 - u % LAT_TILES))


def _seq_id(t):
    return jnp.where(t < N_CTX_TILES, t, N_CTX_TILES + (t - N_CTX_TILES) // LAT_TILES)


def _const_spec(shape):
    nd = len(shape)
    return pl.BlockSpec(shape, lambda *_: (0,) * nd, pipeline_mode=pl.Buffered(1))


def _params(n_axes=1):
    return pltpu.CompilerParams(dimension_semantics=("arbitrary",) * n_axes,
                                vmem_limit_bytes=VMEM_LIMIT)


def _mod_kernel(c_ref, w_ref, b_ref, o_ref):
    c = c_ref[...]
    s = c * jax.nn.sigmoid(c)
    o_ref[...] = _dot(s, w_ref[...]) + b_ref[...]


def _modulation(cvec, w_mod, b_mod):
    nblk = N_MOD
    return pl.pallas_call(
        _mod_kernel,
        grid=(nblk,),
        in_specs=[pl.BlockSpec((MOD_ROWS, D_MODEL), lambda j: (0, 0)),
                  pl.BlockSpec((D_MODEL, D_MODEL), lambda j: (0, j)),
                  pl.BlockSpec((1, D_MODEL), lambda j: (0, j))],
        out_specs=pl.BlockSpec((MOD_ROWS, D_MODEL), lambda j: (0, j)),
        out_shape=jax.ShapeDtypeStruct((MOD_ROWS, N_MOD * D_MODEL), F32),
        compiler_params=_params(),
        name="modulation",
    )(cvec, w_mod, b_mod)


def _ffn_body(x, mod_ref, g_ref, w13_ref, w2_ref, im, ig):
    shift = mod_ref[im:im + 1, :]
    scale = mod_ref[im + 1:im + 2, :]
    gate = mod_ref[im + 2:im + 3, :]
    rows = FFN_TM // FFN_GROUPS
    xs = [x[i * rows:(i + 1) * rows, :] for i in range(FFN_GROUPS)]
    hs = [(_rms(xi, g_ref[ig:ig + 1, :]) * (1.0 + scale) + shift).astype(BF16) for xi in xs]
    gus = [_dot(hi, w13_ref[...]) for hi in hs]
    acts = []
    for gu in gus:
        gt = gu[:, :D_FF]
        up = gu[:, D_FF:]
        acts.append((gt * jax.nn.sigmoid(gt) * up).astype(BF16))
    os_ = [_dot(ai, w2_ref[...]) for ai in acts]
    outs = [xi + HALF_STEP * gate * _rms(oi, g_ref[ig + 1:ig + 2, :]) for xi, oi in zip(xs, os_)]
    return jnp.concatenate(outs, axis=0)


def _cast_chunk(step, src_ref, dst_ref, axis):
    size = src_ref.shape[axis]

    @pl.when(step < N_WCHUNK)
    def _():
        start = pl.multiple_of(step * size, size)
        if axis == 0:
            dst_ref[pl.ds(start, size), :] = src_ref[...].astype(BF16)
        else:
            dst_ref[:, pl.ds(start, size)] = src_ref[...].astype(BF16)


def _chunk_spec(w, axis):
    block = list(w.shape)
    block[axis] = w.shape[axis] // N_WCHUNK
    clamp = lambda s: jnp.minimum(s, N_WCHUNK - 1)
    index = (lambda s: (clamp(s), 0)) if axis == 0 else (lambda s: (0, clamp(s)))
    return pl.BlockSpec(tuple(block), index)


def _tile_step(s):
    return jnp.maximum(s - N_WCHUNK, 0)


def _ffn_first_kernel(xc_ref, xl_ref, mod_ref, g_ref, w13_ref, w2_ref, o_ref, w13_bf, w2_bf):
    s = pl.program_id(0)
    _cast_chunk(s, w13_ref, w13_bf, 1)
    _cast_chunk(s, w2_ref, w2_bf, 0)

    @pl.when(s >= N_WCHUNK)
    def _():
        x = jnp.where(s - N_WCHUNK < N_FFN_CTX_TILES, xc_ref[...], xl_ref[...])
        o_ref[...] = _ffn_body(x, mod_ref, g_ref, w13_bf, w2_bf, 0, 0)


def _ffn_last_kernel(x_ref, mod_ref, g_ref, w13_ref, w2_ref, oc_ref, ol_ref, w13_bf, w2_bf):
    s = pl.program_id(0)
    _cast_chunk(s, w13_ref, w13_bf, 1)
    _cast_chunk(s, w2_ref, w2_bf, 0)

    @pl.when(s >= N_WCHUNK)
    def _():
        out = _ffn_body(x_ref[...], mod_ref, g_ref, w13_bf, w2_bf, 6, 4)

        @pl.when(s - N_WCHUNK < N_FFN_CTX_TILES)
        def _():
            oc_ref[...] = out

        @pl.when(s - N_WCHUNK >= N_FFN_CTX_TILES)
        def _():
            ol_ref[...] = out


_FFN_CTX_TILE = lambda s: (jnp.minimum(_tile_step(s), N_FFN_CTX_TILES - 1), 0)
_FFN_LAT_TILE = lambda s: (jnp.maximum(_tile_step(s) - N_FFN_CTX_TILES, 0), 0)
_FFN_MOD = lambda s: (_mod_row(_tile_step(s) * (FFN_TM // TM)), 0, 0)
_FFN_SCRATCH = [pltpu.VMEM((D_MODEL, 2 * D_FF), BF16), pltpu.VMEM((D_FF, D_MODEL), BF16)]


def _ffn_first(x_ctx, x_lat, mod3, norm_g, w13, w2):
    return pl.pallas_call(
        _ffn_first_kernel,
        grid=(N_WCHUNK + N_FFN_TILES,),
        in_specs=[pl.BlockSpec((FFN_TM, D_MODEL), _FFN_CTX_TILE),
                  pl.BlockSpec((FFN_TM, D_MODEL), _FFN_LAT_TILE),
                  pl.BlockSpec((None, N_MOD, D_MODEL), _FFN_MOD),
                  _const_spec(norm_g.shape),
                  _chunk_spec(w13, 1),
                  _chunk_spec(w2, 0)],
        out_specs=pl.BlockSpec((FFN_TM, D_MODEL), lambda s: (_tile_step(s), 0)),
        out_shape=jax.ShapeDtypeStruct((N_TOK, D_MODEL), F32),
        scratch_shapes=_FFN_SCRATCH,
        compiler_params=_params(),
        name="ffn1",
    )(x_ctx, x_lat, mod3, norm_g, w13, w2)


def _ffn_last(x, mod3, norm_g, w13, w2):
    n_ctx = N_CTX_SEQ * CTX_LEN
    return pl.pallas_call(
        _ffn_last_kernel,
        grid=(N_WCHUNK + N_FFN_TILES,),
        in_specs=[pl.BlockSpec((FFN_TM, D_MODEL), lambda s: (_tile_step(s), 0)),
                  pl.BlockSpec((None, N_MOD, D_MODEL), _FFN_MOD),
                  _const_spec(norm_g.shape),
                  _chunk_spec(w13, 1),
                  _chunk_spec(w2, 0)],
        out_specs=[pl.BlockSpec((FFN_TM, D_MODEL), _FFN_CTX_TILE),
                   pl.BlockSpec((FFN_TM, D_MODEL), _FFN_LAT_TILE)],
        out_shape=[jax.ShapeDtypeStruct((n_ctx, D_MODEL), F32),
                   jax.ShapeDtypeStruct((N_TOK - n_ctx, D_MODEL), F32)],
        scratch_shapes=_FFN_SCRATCH,
        compiler_params=_params(),
        name="ffn2",
    )(x, mod3, norm_g, w13, w2)


def _lora_prep_kernel(mu_ref, w1_ref, o_ref):
    mu = mu_ref[...]
    w1 = w1_ref[...]
    n = w1.shape[1]
    o_ref[:, :n] = ((1.0 - mu) * w1).astype(BF16)
    o_ref[:, n:] = (mu * w1).astype(BF16)


def _lora_prep(mu_cols, w1_cat):
    n = w1_cat.shape[1]
    return pl.pallas_call(
        _lora_prep_kernel,
        out_shape=jax.ShapeDtypeStruct((D_MODEL, 2 * n), BF16),
        compiler_params=pltpu.CompilerParams(vmem_limit_bytes=VMEM_LIMIT),
        name="lora_prep",
    )(mu_cols, w1_cat)


def _front_group(gi, h, halo_prev, halo_next, is_lat, win_ref, waug_ref, refs, outs):
    w2aug_ref, b2aug_ref, kkw_ref, ka_ref, rk_ref, cw_ref, cbias_ref, wbb_ref, ones_ref = refs
    ops_o, v_o, ends_o, bv_o, sg_o, sa_o, mb_o = outs
    rows = slice(gi * TM, (gi + 1) * TM)
    row = lax.broadcasted_iota(jnp.int32, (TM, 1), 0)
    rin = row & (CHUNK - 1)
    ones = ones_ref[...]

    def proj(lo, hi):
        return _bdot(h, win_ref[:, lo:hi])

    pab = _bdot(h, waug_ref[...])
    rk = proj(0, 2 * D_RWKV)
    r = rk[:, :D_RWKV]
    k = rk[:, D_RWKV:]
    pb = pab[:, 2 * LANES:]
    sh_f = jnp.where(row == 0, halo_prev, pltpu.roll(pb[:, :LANES], 1, 0))
    sh_b = jnp.where(row == TM - 1, halo_next, pltpu.roll(pb[:, LANES:], TM - 1, 0))
    t_in = pab[:, :2 * LANES] + jnp.concatenate([sh_f, sh_b], axis=1)
    lane = lax.broadcasted_iota(jnp.int32, (1, 2 * LANES), 1)
    t_in = jnp.where((lane & HEAD) == 0, jnp.tanh(t_in), t_in)
    za = _dot(t_in, w2aug_ref[...]) + b2aug_ref[...]

    kk = k * kkw_ref[...]
    kkn = kk * lax.rsqrt(_head_sum(kk * kk, ones) + 1e-12)
    ka = ka_ref[...]

    vg = proj(2 * D_RWKV, 4 * D_RWKV)
    v = vg[:, :D_RWKV]
    v_o[rows, :] = v.astype(BF16)
    sg_o[rows, :] = _sigmoid(vg[:, D_RWKV:]).astype(BF16)

    def scan_operands(d):
        lw = -EXP_M05 * _sigmoid(za[:, 2 * d * D_RWKV:(2 * d + 1) * D_RWKV])
        a = _sigmoid(za[:, (2 * d + 1) * D_RWKV:(2 * d + 2) * D_RWKV])
        k_d = k * (1.0 + (a - 1.0) * ka)
        b = kkn * a
        cs = lw
        for s in (1, 2, 4, 8, 16, 32):
            if d == 0:
                cs = cs + jnp.where(rin >= s, pltpu.roll(cs, s, 0), 0.0)
            else:
                cs = cs + jnp.where(rin < CHUNK - s, pltpu.roll(cs, TM - s, 0), 0.0)
        end_row = CHUNK - 1 if d == 0 else 0
        ends = [cs[c * CHUNK + end_row:c * CHUNK + end_row + 1, :] for c in range(N_CHUNK)]
        for c in range(N_CHUNK):
            ends_o[d][gi, c:c + 1, :] = ends[c]
        cs_end = jnp.concatenate([jnp.broadcast_to(e, (CHUNK, D_RWKV)) for e in ends], axis=0)
        dec_inv = jnp.exp(-cs)
        dec_rest = jnp.exp(cs_end - cs)
        o_a, o_r, o_b, o_k, o_bh, o_kh = ops_o[d]
        o_a[rows, :] = (-kkn * jnp.exp(cs - lw)).astype(BF16)
        o_r[rows, :] = (r * jnp.exp(cs)).astype(BF16)
        o_b[rows, :] = (b * dec_inv).astype(BF16)
        o_k[rows, :] = (k_d * dec_inv).astype(BF16)
        o_bh[rows, :] = (b * dec_rest).astype(BF16)
        o_kh[rows, :] = (k_d * dec_rest).astype(BF16)
        return k_d

    conv_in = proj(4 * D_RWKV, 4 * D_RWKV + 3 * D_CONV)
    k_0 = scan_operands(0)
    gate_a = proj(4 * D_RWKV + 3 * D_CONV, 4 * D_RWKV + 3 * D_CONV + D_MODEL)
    sa_o[rows, :] = _sigmoid(gate_a).astype(BF16)
    k_1 = scan_operands(1)
    gate_b = proj(4 * D_RWKV + 3 * D_CONV + D_MODEL, D_IN)

    cgate = conv_in[:, :D_CONV]
    u = conv_in[:, D_CONV:2 * D_CONV] * conv_in[:, 2 * D_CONV:]
    col = row & (GRID_W - 1)
    zl = jnp.logical_or(row == 0, jnp.logical_and(is_lat, col == 0))
    zr = jnp.logical_or(row == TM - 1, jnp.logical_and(is_lat, col == GRID_W - 1))
    left = jnp.where(zl, 0.0, pltpu.roll(u, 1, 0))
    right = jnp.where(zr, 0.0, pltpu.roll(u, TM - 1, 0))
    conv = left * cw_ref[0:1, :] + u * cw_ref[1:2, :] + right * cw_ref[2:3, :] + cbias_ref[...]
    y_b = _dot(cgate * conv, wbb_ref[...])
    mb_o[rows, :] = (_sigmoid(gate_b) * y_b).astype(BF16)

    bv_o[rows, :] = (_head_sum(r * (k_0 + k_1) * rk_ref[...], ones) * v).astype(BF16)


def _front_kernel(x_ref, xp_ref, xn_ref, mod_ref, g_ref, win_ref, waug_ref, *rest):
    refs = list(rest[:N_FRONT_CONSTS])
    outs = rest[N_FRONT_CONSTS:-2]
    win_bf, wbb_bf = rest[-2:]
    ops_o = (outs[0:N_DIR_OPS], outs[N_DIR_OPS:2 * N_DIR_OPS])
    v_o, ends0_o, ends1_o, bv_o, sg_o, sa_o, mb_o = outs[2 * N_DIR_OPS:]
    outs = (ops_o, v_o, (ends0_o, ends1_o), bv_o, sg_o, sa_o, mb_o)

    s = pl.program_id(0)
    _cast_chunk(s, win_ref, win_bf, 1)

    @pl.when(s == 0)
    def _():
        wbb_bf[...] = refs[FRONT_WBB][...].astype(BF16)

    refs[FRONT_WBB] = wbb_bf

    @pl.when(s >= N_WCHUNK)
    def _():
        t = s - N_WCHUNK
        is_lat = t >= N_FRONT_CTX_TILES
        i_in = (t - N_FRONT_CTX_TILES) % FRONT_LAT_TILES
        lat_first = jnp.logical_and(is_lat, i_in == 0)
        lat_last = jnp.logical_and(is_lat, i_in == FRONT_LAT_TILES - 1)
        shift = mod_ref[3:4, :]
        scale = mod_ref[4:5, :]
        g2 = g_ref[2:3, :]

        def pre(x):
            return (_rms(x, g2) * (1.0 + scale) + shift).astype(BF16)

        x = x_ref[...]
        edge = jnp.concatenate([xp_ref[...], x[TM - 8:TM + 8, :], xn_ref[...]], axis=0)
        edge_b = _bdot(pre(edge), waug_ref[:, 2 * LANES:])
        zero = jnp.zeros((1, LANES), F32)
        halo_prev = (jnp.where(jnp.logical_and(is_lat, jnp.logical_not(lat_first)), edge_b[7:8, :LANES], zero),
                     jnp.where(is_lat, edge_b[15:16, :LANES], zero))
        halo_next = (jnp.where(is_lat, edge_b[16:17, LANES:], zero),
                     jnp.where(jnp.logical_and(is_lat, jnp.logical_not(lat_last)), edge_b[24:25, LANES:], zero))

        for gi in range(FRONT_GROUPS):
            h = pre(x[gi * TM:(gi + 1) * TM, :])
            _front_group(gi, h, halo_prev[gi], halo_next[gi], is_lat, win_bf, waug_ref, refs, outs)


def _front(x, mod3, norm_g, w_in, w_aug, consts):
    tok = lambda s: (_tile_step(s), 0)
    rows8 = FRONT_TM // 8
    last8 = N_TOK // 8 - 1
    assert len(consts) == N_FRONT_CONSTS
    out_rb = jax.ShapeDtypeStruct((N_TOK, D_RWKV), BF16)
    out_db = jax.ShapeDtypeStruct((N_TOK, D_MODEL), BF16)
    out_e = jax.ShapeDtypeStruct((N_TILES, N_CHUNK, D_RWKV), F32)
    spec_r = pl.BlockSpec((FRONT_TM, D_RWKV), tok)
    spec_d = pl.BlockSpec((FRONT_TM, D_MODEL), tok)
    spec_e = pl.BlockSpec((FRONT_GROUPS, N_CHUNK, D_RWKV), lambda s: (_tile_step(s), 0, 0))
    n_b = 2 * N_DIR_OPS + 1
    return pl.pallas_call(
        _front_kernel,
        grid=(N_WCHUNK + N_TOK // FRONT_TM,),
        in_specs=[pl.BlockSpec((FRONT_TM, D_MODEL), tok),
                  pl.BlockSpec((8, D_MODEL), lambda s: (jnp.maximum(_tile_step(s) * rows8 - 1, 0), 0)),
                  pl.BlockSpec((8, D_MODEL), lambda s: (jnp.minimum((_tile_step(s) + 1) * rows8, last8), 0)),
                  pl.BlockSpec((None, N_MOD, D_MODEL), lambda s: (_mod_row(_tile_step(s) * FRONT_GROUPS), 0, 0)),
                  _const_spec(norm_g.shape),
                  _chunk_spec(w_in, 1)]
                 + [_const_spec(c.shape) for c in [w_aug] + list(consts)],
        out_specs=[spec_r] * n_b + [spec_e] * 2 + [spec_r] * 2 + [spec_d] * 2,
        out_shape=[out_rb] * n_b + [out_e] * 2 + [out_rb] * 2 + [out_db] * 2,
        scratch_shapes=[pltpu.VMEM((D_MODEL, D_IN), BF16), pltpu.VMEM((D_CONV, D_MODEL), BF16)],
        compiler_params=_params(),
        name="mixer_front",
    )(x, x, x, mod3, norm_g, w_in, w_aug, *consts)


def _stack(x):
    lane_lo = lax.broadcasted_iota(jnp.int32, x.shape, 1) < HEAD
    z = jnp.zeros_like(x)
    return jnp.concatenate([jnp.where(lane_lo, x, z), jnp.where(lane_lo, z, x)], axis=0)


def _bdot(a, b, dims=NN):
    return lax.dot_general(a, b, dims, preferred_element_type=F32)


def _scan_kernel(*refs):
    ops = (refs[0:N_DIR_OPS + 1], refs[N_DIR_OPS + 1:2 * N_DIR_OPS + 2])
    ends0_ref, ends1_ref, s0_ref, yf_ref, yb_ref, sout_ref, st_ref = refs[2 * N_DIR_OPS + 2:]
    ends_ref = (ends0_ref, ends1_ref)
    t = pl.program_id(0)
    is_lat, first, _ = _tile_info(t)

    @pl.when(first)
    def _():
        st_ref[...] = jnp.where(is_lat, s0_ref[...], 0.0)

    ti = lax.broadcasted_iota(jnp.int32, (CHUNK, LANES), 0)
    sj = lax.broadcasted_iota(jnp.int32, (CHUNK, LANES), 1) & (CHUNK - 1)
    m_strict = (sj < ti, sj > ti)
    m_incl = (sj <= ti, sj >= ti)
    eye_cat = jnp.where(sj == ti, 1.0, 0.0)
    bi = lax.broadcasted_iota(jnp.int32, (LANES, LANES), 0)
    bj = lax.broadcasted_iota(jnp.int32, (LANES, LANES), 1)
    blk = (bi >> HEAD_SHIFT) == (bj >> HEAD_SHIFT)
    eye_bd = bi == bj
    zero_bd = jnp.zeros((LANES, LANES), BF16)

    def chunk_of(d, step):
        return step if d == 0 else N_CHUNK - 1 - step

    def op(i, u):
        d, p, c = u
        return ops[d][i][c * CHUNK:(c + 1) * CHUNK, p * LANES:(p + 1) * LANES]

    g_mat, w_mat, p_t, q_t = {}, {}, {}, {}

    def local_stages(units):
        low, nak, mrbk = {}, {}, {}
        for u in units:
            lhs = jnp.concatenate([op(OP_A, u), op(OP_R, u)], axis=0)
            rhs = jnp.concatenate([_stack(op(OP_B, u)), _stack(op(OP_K, u))], axis=0)
            gram = _bdot(lhs, rhs, NT)
            d = u[0]
            low[u] = jnp.where(m_strict[d], gram[:CHUNK, :LANES], 0.0)
            nak[u] = jnp.where(m_strict[d], gram[:CHUNK, LANES:], 0.0).astype(BF16)
            mrbk[u] = jnp.concatenate([jnp.where(m_incl[d], gram[CHUNK:, :LANES], 0.0),
                                       jnp.where(m_incl[d], gram[CHUNK:, LANES:], 0.0)], axis=1).astype(BF16)
        yield

        inv = {u: eye_cat + low[u] for u in units}
        pwb = {u: low[u].astype(BF16) for u in units}
        for u in units:
            pwb[u] = _bdot(pwb[u], _stack(pwb[u])).astype(BF16)
        yield
        for _ in range(4):
            for u in units:
                both = _bdot(pwb[u], jnp.concatenate([_stack(pwb[u]), _stack(inv[u].astype(BF16))], axis=1))
                pwb[u] = both[:, :LANES].astype(BF16)
                inv[u] = inv[u] + both[:, LANES:]
            yield
        for u in units:
            inv[u] = inv[u] + _bdot(pwb[u], _stack(inv[u].astype(BF16)))
        yield

        sv = {u: _stack(op(OP_V, u)) for u in units}
        nv = {u: _bdot(nak[u], sv[u]).astype(BF16) for u in units}
        x1, x2 = {}, {}
        for u in units:
            x12 = _bdot(inv[u].astype(BF16), jnp.concatenate([_stack(op(OP_A, u)), _stack(nv[u])], axis=1))
            x1[u] = x12[:, :LANES].astype(BF16)
            x2[u] = x12[:, LANES:].astype(BF16)
        yield
        for u in units:
            rhs = jnp.concatenate([jnp.concatenate([_stack(x1[u]), _stack(x2[u])], axis=1),
                                   jnp.concatenate([zero_bd, sv[u]], axis=1)], axis=0)
            gw = _bdot(mrbk[u], rhs)
            g_mat[u] = (op(OP_R, u).astype(F32) + gw[:, :LANES]).astype(BF16)
            w_mat[u] = gw[:, LANES:]
        yield
        for u in units:
            d, p, c = u
            gam = jnp.exp(ends_ref[d][c:c + 1, p * LANES:(p + 1) * LANES])
            p_t[u] = (jnp.where(blk, _bdot(x1[u], op(OP_BH, u), TN), 0.0)
                      + jnp.where(eye_bd, gam, 0.0)).astype(BF16)
            q_t[u] = jnp.where(blk, _bdot(jnp.concatenate([x2[u], op(OP_V, u)], axis=0),
                                          jnp.concatenate([op(OP_BH, u), op(OP_KH, u)], axis=0), TN), 0.0)
        yield

    state = {(d, p): st_ref[d, p] for d in (0, 1) for p in range(N_PAIR)}
    y_refs = (yf_ref, yb_ref)

    def chain_step(step):
        for d in (0, 1):
            c = chunk_of(d, step)
            for p in range(N_PAIR):
                u = (d, p, c)
                sb = state[d, p].astype(BF16)
                y_refs[d][c * CHUNK:(c + 1) * CHUNK, p * LANES:(p + 1) * LANES] = (
                    _bdot(g_mat[u], sb, NT) + w_mat[u])
                state[d, p] = _bdot(sb, p_t[u]) + q_t[u]

    for _ in local_stages([(d, p, c) for d in (0, 1) for p in range(N_PAIR) for c in range(N_CHUNK)]):
        pass
    for step in range(N_CHUNK):
        chain_step(step)
    for (d, p), s in state.items():
        st_ref[d, p] = s

    @pl.when(jnp.logical_not(is_lat))
    def _():
        for (d, p), s in state.items():
            sout_ref[0, d, 2 * p] = s[:HEAD, :HEAD]
            sout_ref[0, d, 2 * p + 1] = s[HEAD:, HEAD:]


def _scan(ops_f, ops_b, v, ends_f, ends_b, s0):
    fwd = pl.BlockSpec((TM, D_RWKV), lambda t: (t, 0))
    bwd = pl.BlockSpec((TM, D_RWKV), lambda t: (_mirror_tile(t), 0))
    e_block = (None, N_CHUNK, D_RWKV)
    st_block = (None, 2, N_PAIR, LANES, LANES)
    out_y = jax.ShapeDtypeStruct((N_TOK, D_RWKV), F32)
    return pl.pallas_call(
        _scan_kernel,
        grid=(N_TILES,),
        in_specs=[fwd] * (N_DIR_OPS + 1) + [bwd] * (N_DIR_OPS + 1)
                 + [pl.BlockSpec(e_block, lambda t: (t, 0, 0)),
                    pl.BlockSpec(e_block, lambda t: (_mirror_tile(t), 0, 0)),
                    pl.BlockSpec(st_block, lambda t: (jnp.maximum(_seq_id(t) - N_CTX_SEQ, 0), 0, 0, 0, 0))],
        out_specs=[fwd, bwd,
                   pl.BlockSpec((None, 1, 2, 2 * N_PAIR, HEAD, HEAD),
                                lambda t: (jnp.minimum(t, N_CTX_SEQ - 1), 0, 0, 0, 0, 0))],
        out_shape=[out_y, out_y,
                   jax.ShapeDtypeStruct((N_CTX_SEQ, 1, 2, 2 * N_PAIR, HEAD, HEAD), F32)],
        scratch_shapes=[pltpu.VMEM((2, N_PAIR, LANES, LANES), F32)],
        compiler_params=_params(),
        name="rwkv7_scan",
    )(*ops_f, v, *ops_b, v, ends_f, ends_b, s0)


def _back_kernel(x_ref, yf_ref, yb_ref, bv_ref, sg_ref, sa_ref, mb_ref, mod_ref, g_ref,
                 gng_ref, gnb_ref, wba_f32_ref, wout_f32_ref, ones_ref, o_ref, wba_ref, wout_ref):
    @pl.when(pl.program_id(0) == 0)
    def _():
        wba_ref[...] = wba_f32_ref[...].astype(BF16)
        wout_ref[...] = wout_f32_ref[...].astype(BF16)

    ones = ones_ref[...]
    y = yf_ref[...] + yb_ref[...]
    mean = _head_sum(y, ones) * (1.0 / HEAD)
    yc = y - mean
    var = _head_sum(yc * yc, ones) * (1.0 / HEAD)
    yn = yc * lax.rsqrt(var + EPS_GN) * gng_ref[...] + gnb_ref[...]
    ya = _dot((yn + bv_ref[...]) * sg_ref[...], wba_ref[...])
    merged = sa_ref[...] * ya + mb_ref[...]
    out = _dot(merged, wout_ref[...])
    o_ref[...] = x_ref[...] + mod_ref[5:6, :] * _rms(out, g_ref[3:4, :])


def _back(x, yf, yb, bv, sg, sa, mb, mod3, norm_g, gng, gnb, wba, wout, ones):
    tok = lambda t: (t, 0)
    spec_r = pl.BlockSpec((TM, D_RWKV), tok)
    spec_d = pl.BlockSpec((TM, D_MODEL), tok)
    consts = [norm_g, gng, gnb, wba, wout, ones]
    return pl.pallas_call(
        _back_kernel,
        grid=(N_TILES,),
        in_specs=[spec_d, spec_r, spec_r, spec_r, spec_r, spec_d, spec_d,
                  pl.BlockSpec((None, N_MOD, D_MODEL), lambda t: (_mod_row(t), 0, 0))]
                 + [_const_spec(c.shape) for c in consts],
        out_specs=spec_d,
        out_shape=jax.ShapeDtypeStruct((N_TOK, D_MODEL), F32),
        scratch_shapes=[pltpu.VMEM(wba.shape, BF16), pltpu.VMEM(wout.shape, BF16)],
        compiler_params=_params(),
        name="mixer_back",
    )(x, yf, yb, bv, sg, sa, mb, mod3, *consts)


def _pair_blockdiag(s):
    lead = s.shape[:-3]
    s = s.reshape(lead + (N_PAIR, 2, HEAD, HEAD))
    z = jnp.zeros_like(s[..., 0, :, :])
    top = jnp.concatenate([s[..., 0, :, :], z], axis=-1)
    bot = jnp.concatenate([z, s[..., 1, :, :]], axis=-1)
    return jnp.concatenate([top, bot], axis=-2)


def kernel(x_prompt, x_sample, c, state_rwkv, c_ctx, w_mod, b_mod, norm_g, ffn1_w13, ffn1_w2,
           ffn2_w13, ffn2_w2, w_in, mu_shift, decay_w0, decay_w1, decay_w2, iclr_a0, iclr_a1,
           iclr_a2, k_k, k_a, r_k, gn_gain, gn_bias, conv_w, conv_b, w_branch_a, w_branch_b, w_out):
    assert x_prompt.shape == (N_CTX_SEQ, CTX_LEN, D_MODEL) and x_sample.shape == (N_LAT_SEQ, LAT_LEN, D_MODEL)
    assert w_mod.shape[0] == 1, "single trunk layer"
    bf = lambda w: w.astype(BF16)

    cvec = jnp.concatenate([c_ctx[None, :], c, jnp.zeros((MOD_ROWS - 1 - N_LAT_SEQ, D_MODEL), F32)], axis=0)
    mod3 = _modulation(cvec, w_mod[0], b_mod).reshape(MOD_ROWS, N_MOD, D_MODEL)
    g = norm_g[0]

    x = _ffn_first(x_prompt.reshape(-1, D_MODEL), x_sample.reshape(-1, D_MODEL), mod3, g,
                   ffn1_w13[0], ffn1_w2[0])

    row = lambda p: p.reshape(1, -1)
    mu4 = mu_shift[0].reshape(4, D_MODEL)
    w1_cat = jnp.concatenate([decay_w1[0, 0], iclr_a1[0, 0], decay_w1[0, 1], iclr_a1[0, 1]], axis=1)
    w_aug = _lora_prep(jnp.repeat(mu4.T, LORA, axis=1), w1_cat)
    w2_blocks = [decay_w2[0, 0], iclr_a2[0, 0], decay_w2[0, 1], iclr_a2[0, 1]]
    w2_zero = jnp.zeros((LORA, D_RWKV), F32)
    w2_aug = bf(jnp.concatenate(
        [jnp.concatenate([blk if j == i else w2_zero for j in range(4)], axis=1)
         for i, blk in enumerate(w2_blocks)], axis=0))
    b2_aug = jnp.concatenate([decay_w0[0, 0], iclr_a0[0, 0], decay_w0[0, 1], iclr_a0[0, 1]]).reshape(1, -1)
    ones = _head_ones()
    front = _front(x, mod3, g, w_in[0], w_aug,
                   [w2_aug, b2_aug, row(k_k[0]), row(k_a[0]), row(r_k[0]), conv_w[0], row(conv_b[0]),
                    w_branch_b[0], ones])
    ops_f, ops_b = front[0:N_DIR_OPS], front[N_DIR_OPS:2 * N_DIR_OPS]
    v, ends_f, ends_b, bv, sg, sa, mb = front[2 * N_DIR_OPS:]

    s0 = _pair_blockdiag(state_rwkv[:, 0])
    yf, yb, s_fin = _scan(ops_f, ops_b, v, ends_f, ends_b, s0)

    x = _back(x, yf, yb, bv, sg, sa, mb, mod3, g, row(gn_gain[0]), row(gn_bias[0]),
              w_branch_a[0], w_out[0], ones)
    y_ctx, y_lat = _ffn_last(x, mod3, g, ffn2_w13[0], ffn2_w2[0])

    y_prompt = y_ctx.reshape(N_CTX_SEQ, CTX_LEN, D_MODEL)
    y_sample = y_lat.reshape(N_LAT_SEQ, LAT_LEN, D_MODEL)
    return y_prompt, y_sample, s_fin
```

```python
import jax
import jax.numpy as jnp
from jax import lax
from jax.experimental import pallas as pl
from jax.experimental.pallas import tpu as pltpu

F32 = jnp.float32
BF16 = jnp.bfloat16

D_MODEL = 1024
D_FF = 2816
D_RWKV = 512
D_CONV = 512
HEAD = 64
HEAD_SHIFT = 6
D_IN = 4 * D_RWKV + 3 * D_CONV + 2 * D_MODEL
N_MOD = 9
EPS_RMS = 1e-6
EPS_GN = 64e-5
HALF_STEP = 0.5
EXP_M05 = 0.6065306597126334

N_CTX_SEQ = 16
CTX_LEN = 256
N_LAT_SEQ = 2
LAT_LEN = 2048
GRID_W = 64
N_TOK = N_CTX_SEQ * CTX_LEN + N_LAT_SEQ * LAT_LEN

TM = 256
N_CTX_TILES = N_CTX_SEQ * CTX_LEN // TM
LAT_TILES = LAT_LEN // TM
N_TILES = N_TOK // TM
FRONT_TM = 512
FRONT_GROUPS = FRONT_TM // TM
N_FRONT_CTX_TILES = N_CTX_SEQ * CTX_LEN // FRONT_TM
FRONT_LAT_TILES = LAT_LEN // FRONT_TM
N_FRONT_CONSTS = 9
FRONT_WBB = 7
LORA = 64
BACK_TM = 512
FFN_TM = 512
FFN_GROUPS = 2
N_FFN_TILES = N_TOK // FFN_TM
N_FFN_CTX_TILES = N_CTX_SEQ * CTX_LEN // FFN_TM
N_WCHUNK = 11
CHUNK = 64
N_CHUNK = TM // CHUNK
LANES = 128
N_PAIR = D_RWKV // LANES
MOD_ROWS = 8
MOD_BLOCK = 3
VMEM_LIMIT = 56 * 1024 * 1024

OP_A, OP_R, OP_B, OP_K, OP_BH, OP_KH, OP_V = range(7)
N_DIR_OPS = 6

NN = (((1,), (0,)), ((), ()))
NT = (((1,), (1,)), ((), ()))
TN = (((0,), (0,)), ((), ()))


def _dot(a, b, dims=NN):
    return lax.dot_general(a.astype(BF16), b.astype(BF16), dims, preferred_element_type=F32)


def _bdot(a, b, dims=NN):
    return lax.dot_general(a, b, dims, preferred_element_type=F32)


def _rms(x, g):
    ms = jnp.mean(x * x, axis=-1, keepdims=True)
    return x * lax.rsqrt(ms + EPS_RMS) * g


def _head_ones():
    i = lax.broadcasted_iota(jnp.int32, (D_RWKV, D_RWKV), 0) >> HEAD_SHIFT
    j = lax.broadcasted_iota(jnp.int32, (D_RWKV, D_RWKV), 1) >> HEAD_SHIFT
    return jnp.where(i == j, 1.0, 0.0).astype(BF16)


def _head_sum(x, ones):
    return jnp.dot(x.astype(BF16), ones, preferred_element_type=F32)


def _sigmoid(x):
    return 0.5 * jnp.tanh(0.5 * x) + 0.5


def _tile_info(t):
    is_lat = t >= N_CTX_TILES
    i_in = (t - N_CTX_TILES) % LAT_TILES
    first = jnp.logical_or(jnp.logical_not(is_lat), i_in == 0)
    last = jnp.logical_or(jnp.logical_not(is_lat), i_in == LAT_TILES - 1)
    return is_lat, first, last


def _mod_row(t):
    return jnp.where(t < N_CTX_TILES, 0, 1 + (t - N_CTX_TILES) // LAT_TILES)


def _mirror_tile(t):
    u = t - N_CTX_TILES
    return jnp.where(t < N_CTX_TILES, t,
                     N_CTX_TILES + (u // LAT_TILES) * LAT_TILES + (LAT_TILES - 1 - u % LAT_TILES))


def _seq_id(t):
    return jnp.where(t < N_CTX_TILES, t, N_CTX_TILES + (t - N_CTX_TILES) // LAT_TILES)


def _const_spec(shape):
    nd = len(shape)
    return pl.BlockSpec(shape, lambda *_: (0,) * nd, pipeline_mode=pl.Buffered(1))


def _params(n_axes=1):
    return pltpu.CompilerParams(dimension_semantics=("arbitrary",) * n_axes,
                                vmem_limit_bytes=VMEM_LIMIT)


def _mod_kernel(c_ref, w_ref, b_ref, o_ref):
    c = c_ref[...]
    s = c * jax.nn.sigmoid(c)
    o_ref[...] = _dot(s, w_ref[...]) + b_ref[...]


def _modulation(cvec, w_mod, b_mod):
    width = MOD_BLOCK * D_MODEL
    return pl.pallas_call(
        _mod_kernel,
        grid=(N_MOD // MOD_BLOCK,),
        in_specs=[pl.BlockSpec((MOD_ROWS, D_MODEL), lambda j: (0, 0)),
                  pl.BlockSpec((D_MODEL, width), lambda j: (0, j)),
                  pl.BlockSpec((1, width), lambda j: (0, j))],
        out_specs=pl.BlockSpec((MOD_ROWS, width), lambda j: (0, j)),
        out_shape=jax.ShapeDtypeStruct((MOD_ROWS, N_MOD * D_MODEL), F32),
        compiler_params=_params(),
        name="modulation",
    )(cvec, w_mod, b_mod)


def _ffn_body(x, mod_ref, g_ref, w13_ref, w2_ref, im, ig):
    shift = mod_ref[im:im + 1, :]
    scale = mod_ref[im + 1:im + 2, :]
    gate = mod_ref[im + 2:im + 3, :]
    rows = FFN_TM // FFN_GROUPS
    xs = [x[i * rows:(i + 1) * rows, :] for i in range(FFN_GROUPS)]
    hs = [(_rms(xi, g_ref[ig:ig + 1, :]) * (1.0 + scale) + shift).astype(BF16) for xi in xs]
    gus = [_dot(hi, w13_ref[...]) for hi in hs]
    acts = []
    for gu in gus:
        gt = gu[:, :D_FF]
        up = gu[:, D_FF:]
        acts.append((gt * jax.nn.sigmoid(gt) * up).astype(BF16))
    os_ = [_dot(ai, w2_ref[...]) for ai in acts]
    outs = [xi + HALF_STEP * gate * _rms(oi, g_ref[ig + 1:ig + 2, :]) for xi, oi in zip(xs, os_)]
    return jnp.concatenate(outs, axis=0)


def _cast_chunk(step, src_ref, dst_ref, axis):
    size = src_ref.shape[axis]

    @pl.when(step < N_WCHUNK)
    def _():
        start = pl.multiple_of(step * size, size)
        if axis == 0:
            dst_ref[pl.ds(start, size), :] = src_ref[...].astype(BF16)
        else:
            dst_ref[:, pl.ds(start, size)] = src_ref[...].astype(BF16)


def _chunk_spec(w, axis):
    block = list(w.shape)
    block[axis] = w.shape[axis] // N_WCHUNK
    clamp = lambda s: jnp.minimum(s, N_WCHUNK - 1)
    index = (lambda s: (clamp(s), 0)) if axis == 0 else (lambda s: (0, clamp(s)))
    return pl.BlockSpec(tuple(block), index)


def _tile_step(s):
    return jnp.maximum(s - N_WCHUNK, 0)


def _ffn_first_kernel(xc_ref, xl_ref, mod_ref, g_ref, w13_ref, w2_ref, o_ref, w13_bf, w2_bf):
    s = pl.program_id(0)
    _cast_chunk(s, w13_ref, w13_bf, 1)
    _cast_chunk(s, w2_ref, w2_bf, 0)

    @pl.when(s >= N_WCHUNK)
    def _():
        x = jnp.where(s - N_WCHUNK < N_FFN_CTX_TILES, xc_ref[...], xl_ref[...])
        o_ref[...] = _ffn_body(x, mod_ref, g_ref, w13_bf, w2_bf, 0, 0)


def _ffn_last_kernel(x_ref, mod_ref, g_ref, w13_ref, w2_ref, oc_ref, ol_ref, w13_bf, w2_bf):
    s = pl.program_id(0)
    _cast_chunk(s, w13_ref, w13_bf, 1)
    _cast_chunk(s, w2_ref, w2_bf, 0)

    @pl.when(s >= N_WCHUNK)
    def _():
        out = _ffn_body(x_ref[...], mod_ref, g_ref, w13_bf, w2_bf, 6, 4)

        @pl.when(s - N_WCHUNK < N_FFN_CTX_TILES)
        def _():
            oc_ref[...] = out

        @pl.when(s - N_WCHUNK >= N_FFN_CTX_TILES)
        def _():
            ol_ref[...] = out


_FFN_CTX_TILE = lambda s: (jnp.minimum(_tile_step(s), N_FFN_CTX_TILES - 1), 0)
_FFN_LAT_TILE = lambda s: (jnp.maximum(_tile_step(s) - N_FFN_CTX_TILES, 0), 0)
_FFN_MOD = lambda s: (_mod_row(_tile_step(s) * (FFN_TM // TM)), 0, 0)
_FFN_SCRATCH = [pltpu.VMEM((D_MODEL, 2 * D_FF), BF16), pltpu.VMEM((D_FF, D_MODEL), BF16)]


def _ffn_first(x_ctx, x_lat, mod3, norm_g, w13, w2):
    return pl.pallas_call(
        _ffn_first_kernel,
        grid=(N_WCHUNK + N_FFN_TILES,),
        in_specs=[pl.BlockSpec((FFN_TM, D_MODEL), _FFN_CTX_TILE),
                  pl.BlockSpec((FFN_TM, D_MODEL), _FFN_LAT_TILE),
                  pl.BlockSpec((None, N_MOD, D_MODEL), _FFN_MOD),
                  _const_spec(norm_g.shape),
                  _chunk_spec(w13, 1),
                  _chunk_spec(w2, 0)],
        out_specs=pl.BlockSpec((FFN_TM, D_MODEL), lambda s: (_tile_step(s), 0)),
        out_shape=jax.ShapeDtypeStruct((N_TOK, D_MODEL), F32),
        scratch_shapes=_FFN_SCRATCH,
        compiler_params=_params(),
        name="ffn1",
    )(x_ctx, x_lat, mod3, norm_g, w13, w2)


def _ffn_last(x, mod3, norm_g, w13, w2):
    n_ctx = N_CTX_SEQ * CTX_LEN
    return pl.pallas_call(
        _ffn_last_kernel,
        grid=(N_WCHUNK + N_FFN_TILES,),
        in_specs=[pl.BlockSpec((FFN_TM, D_MODEL), lambda s: (_tile_step(s), 0)),
                  pl.BlockSpec((None, N_MOD, D_MODEL), _FFN_MOD),
                  _const_spec(norm_g.shape),
                  _chunk_spec(w13, 1),
                  _chunk_spec(w2, 0)],
        out_specs=[pl.BlockSpec((FFN_TM, D_MODEL), _FFN_CTX_TILE),
                   pl.BlockSpec((FFN_TM, D_MODEL), _FFN_LAT_TILE)],
        out_shape=[jax.ShapeDtypeStruct((n_ctx, D_MODEL), F32),
                   jax.ShapeDtypeStruct((N_TOK - n_ctx, D_MODEL), F32)],
        scratch_shapes=_FFN_SCRATCH,
        compiler_params=_params(),
        name="ffn2",
    )(x, mod3, norm_g, w13, w2)


def _lora_prep_kernel(mu_ref, w1_ref, o_ref):
    mu = mu_ref[...]
    w1 = w1_ref[...]
    n = w1.shape[1]
    o_ref[:, :n] = ((1.0 - mu) * w1).astype(BF16)
    o_ref[:, n:] = (mu * w1).astype(BF16)


def _lora_prep(mu_cols, w1_cat):
    n = w1_cat.shape[1]
    return pl.pallas_call(
        _lora_prep_kernel,
        out_shape=jax.ShapeDtypeStruct((D_MODEL, 2 * n), BF16),
        compiler_params=pltpu.CompilerParams(vmem_limit_bytes=VMEM_LIMIT),
        name="lora_prep",
    )(mu_cols, w1_cat)


def _front_group(gi, h, halo_prev, halo_next, is_lat, win_ref, waug_ref, refs, outs):
    w2aug_ref, b2aug_ref, kkw_ref, ka_ref, rk_ref, cw_ref, cbias_ref, wbb_ref, ones_ref = refs
    ops_o, v_o, ends_o, bv_o, sg_o, sa_o, mb_o = outs
    rows = slice(gi * TM, (gi + 1) * TM)
    row = lax.broadcasted_iota(jnp.int32, (TM, 1), 0)
    rin = row & (CHUNK - 1)
    ones = ones_ref[...]

    def proj(lo, hi):
        return _bdot(h, win_ref[:, lo:hi])

    pab = _bdot(h, waug_ref[...])
    rk = proj(0, 2 * D_RWKV)
    r = rk[:, :D_RWKV]
    k = rk[:, D_RWKV:]
    pb = pab[:, 2 * LANES:]
    sh_f = jnp.where(row == 0, halo_prev, pltpu.roll(pb[:, :LANES], 1, 0))
    sh_b = jnp.where(row == TM - 1, halo_next, pltpu.roll(pb[:, LANES:], TM - 1, 0))
    t_in = pab[:, :2 * LANES] + jnp.concatenate([sh_f, sh_b], axis=1)
    lane = lax.broadcasted_iota(jnp.int32, (1, 2 * LANES), 1)
    t_in = jnp.where((lane & HEAD) == 0, jnp.tanh(t_in), t_in)
    za = _dot(t_in, w2aug_ref[...]) + b2aug_ref[...]

    kk = k * kkw_ref[...]
    kkn = kk * lax.rsqrt(_head_sum(kk * kk, ones) + 1e-12)
    ka = ka_ref[...]

    vg = proj(2 * D_RWKV, 4 * D_RWKV)
    v = vg[:, :D_RWKV]
    v_o[rows, :] = v.astype(BF16)
    sg_o[rows, :] = _sigmoid(vg[:, D_RWKV:]).astype(BF16)

    def scan_operands(d):
        lw = -EXP_M05 * _sigmoid(za[:, 2 * d * D_RWKV:(2 * d + 1) * D_RWKV])
        a = _sigmoid(za[:, (2 * d + 1) * D_RWKV:(2 * d + 2) * D_RWKV])
        k_d = k * (1.0 + (a - 1.0) * ka)
        b = kkn * a
        cs = lw
        for s in (1, 2, 4, 8, 16, 32):
            if d == 0:
                cs = cs + jnp.where(rin >= s, pltpu.roll(cs, s, 0), 0.0)
            else:
                cs = cs + jnp.where(rin < CHUNK - s, pltpu.roll(cs, TM - s, 0), 0.0)
        end_row = CHUNK - 1 if d == 0 else 0
        ends = [cs[c * CHUNK + end_row:c * CHUNK + end_row + 1, :] for c in range(N_CHUNK)]
        for c in range(N_CHUNK):
            ends_o[d][gi, c:c + 1, :] = ends[c]
        cs_end = jnp.concatenate([jnp.broadcast_to(e, (CHUNK, D_RWKV)) for e in ends], axis=0)
        dec_inv = jnp.exp(-cs)
        dec_rest = jnp.exp(cs_end - cs)
        o_a, o_r, o_b, o_k, o_bh, o_kh = ops_o[d]
        o_a[rows, :] = (-kkn * jnp.exp(cs - lw)).astype(BF16)
        o_r[rows, :] = (r * jnp.exp(cs)).astype(BF16)
        o_b[rows, :] = (b * dec_inv).astype(BF16)
        o_k[rows, :] = (k_d * dec_inv).astype(BF16)
        o_bh[rows, :] = (b * dec_rest).astype(BF16)
        o_kh[rows, :] = (k_d * dec_rest).astype(BF16)
        return k_d

    conv_in = proj(4 * D_RWKV, 4 * D_RWKV + 3 * D_CONV)
    k_0 = scan_operands(0)
    gate_a = proj(4 * D_RWKV + 3 * D_CONV, 4 * D_RWKV + 3 * D_CONV + D_MODEL)
    sa_o[rows, :] = _sigmoid(gate_a).astype(BF16)
    k_1 = scan_operands(1)
    gate_b = proj(4 * D_RWKV + 3 * D_CONV + D_MODEL, D_IN)

    cgate = conv_in[:, :D_CONV]
    u = conv_in[:, D_CONV:2 * D_CONV] * conv_in[:, 2 * D_CONV:]
    col = row & (GRID_W - 1)
    zl = jnp.logical_or(row == 0, jnp.logical_and(is_lat, col == 0))
    zr = jnp.logical_or(row == TM - 1, jnp.logical_and(is_lat, col == GRID_W - 1))
    left = jnp.where(zl, 0.0, pltpu.roll(u, 1, 0))
    right = jnp.where(zr, 0.0, pltpu.roll(u, TM - 1, 0))
    conv = left * cw_ref[0:1, :] + u * cw_ref[1:2, :] + right * cw_ref[2:3, :] + cbias_ref[...]
    y_b = _dot(cgate * conv, wbb_ref[...])
    mb_o[rows, :] = (_sigmoid(gate_b) * y_b).astype(BF16)

    bv_o[rows, :] = (_head_sum(r * (k_0 + k_1) * rk_ref[...], ones) * v).astype(BF16)


def _front_kernel(x_ref, xp_ref, xn_ref, mod_ref, g_ref, win_ref, waug_ref, *rest):
    refs = list(rest[:N_FRONT_CONSTS])
    outs = rest[N_FRONT_CONSTS:-2]
    win_bf, wbb_bf = rest[-2:]
    ops_o = (outs[0:N_DIR_OPS], outs[N_DIR_OPS:2 * N_DIR_OPS])
    v_o, ends0_o, ends1_o, bv_o, sg_o, sa_o, mb_o = outs[2 * N_DIR_OPS:]
    outs = (ops_o, v_o, (ends0_o, ends1_o), bv_o, sg_o, sa_o, mb_o)

    s = pl.program_id(0)
    _cast_chunk(s, win_ref, win_bf, 1)

    @pl.when(s == 0)
    def _():
        wbb_bf[...] = refs[FRONT_WBB][...].astype(BF16)

    refs[FRONT_WBB] = wbb_bf

    @pl.when(s >= N_WCHUNK)
    def _():
        t = s - N_WCHUNK
        is_lat = t >= N_FRONT_CTX_TILES
        i_in = (t - N_FRONT_CTX_TILES) % FRONT_LAT_TILES
        lat_first = jnp.logical_and(is_lat, i_in == 0)
        lat_last = jnp.logical_and(is_lat, i_in == FRONT_LAT_TILES - 1)
        shift = mod_ref[3:4, :]
        scale = mod_ref[4:5, :]
        g2 = g_ref[2:3, :]

        def pre(x):
            return (_rms(x, g2) * (1.0 + scale) + shift).astype(BF16)

        x = x_ref[...]
        edge = jnp.concatenate([xp_ref[...], x[TM - 8:TM + 8, :], xn_ref[...]], axis=0)
        edge_b = _bdot(pre(edge), waug_ref[:, 2 * LANES:])
        zero = jnp.zeros((1, LANES), F32)
        halo_prev = (jnp.where(jnp.logical_and(is_lat, jnp.logical_not(lat_first)), edge_b[7:8, :LANES], zero),
                     jnp.where(is_lat, edge_b[15:16, :LANES], zero))
        halo_next = (jnp.where(is_lat, edge_b[16:17, LANES:], zero),
                     jnp.where(jnp.logical_and(is_lat, jnp.logical_not(lat_last)), edge_b[24:25, LANES:], zero))

        for gi in range(FRONT_GROUPS):
            h = pre(x[gi * TM:(gi + 1) * TM, :])
            _front_group(gi, h, halo_prev[gi], halo_next[gi], is_lat, win_bf, waug_ref, refs, outs)


def _front(x, mod3, norm_g, w_in, w_aug, consts):
    tok = lambda s: (_tile_step(s), 0)
    rows8 = FRONT_TM // 8
    last8 = N_TOK // 8 - 1
    assert len(consts) == N_FRONT_CONSTS
    out_rb = jax.ShapeDtypeStruct((N_TOK, D_RWKV), BF16)
    out_db = jax.ShapeDtypeStruct((N_TOK, D_MODEL), BF16)
    out_e = jax.ShapeDtypeStruct((N_TILES, N_CHUNK, D_RWKV), F32)
    spec_r = pl.BlockSpec((FRONT_TM, D_RWKV), tok)
    spec_d = pl.BlockSpec((FRONT_TM, D_MODEL), tok)
    spec_e = pl.BlockSpec((FRONT_GROUPS, N_CHUNK, D_RWKV), lambda s: (_tile_step(s), 0, 0))
    n_b = 2 * N_DIR_OPS + 1
    return pl.pallas_call(
        _front_kernel,
        grid=(N_WCHUNK + N_TOK // FRONT_TM,),
        in_specs=[pl.BlockSpec((FRONT_TM, D_MODEL), tok),
                  pl.BlockSpec((8, D_MODEL), lambda s: (jnp.maximum(_tile_step(s) * rows8 - 1, 0), 0)),
                  pl.BlockSpec((8, D_MODEL), lambda s: (jnp.minimum((_tile_step(s) + 1) * rows8, last8), 0)),
                  pl.BlockSpec((None, N_MOD, D_MODEL), lambda s: (_mod_row(_tile_step(s) * FRONT_GROUPS), 0, 0)),
                  _const_spec(norm_g.shape),
                  _chunk_spec(w_in, 1)]
                 + [_const_spec(c.shape) for c in [w_aug] + list(consts)],
        out_specs=[spec_r] * n_b + [spec_e] * 2 + [spec_r] * 2 + [spec_d] * 2,
        out_shape=[out_rb] * n_b + [out_e] * 2 + [out_rb] * 2 + [out_db] * 2,
        scratch_shapes=[pltpu.VMEM((D_MODEL, D_IN), BF16), pltpu.VMEM((D_CONV, D_MODEL), BF16)],
        compiler_params=_params(),
        name="mixer_front",
    )(x, x, x, mod3, norm_g, w_in, w_aug, *consts)


def _stack(x):
    lane_lo = lax.broadcasted_iota(jnp.int32, x.shape, 1) < HEAD
    z = jnp.zeros_like(x)
    return jnp.concatenate([jnp.where(lane_lo, x, z), jnp.where(lane_lo, z, x)], axis=0)


def _scan_kernel(*refs):
    ops = (refs[0:N_DIR_OPS + 1], refs[N_DIR_OPS + 1:2 * N_DIR_OPS + 2])
    ends0_ref, ends1_ref, s0_ref, yf_ref, yb_ref, sout_ref, st_ref = refs[2 * N_DIR_OPS + 2:]
    ends_ref = (ends0_ref, ends1_ref)
    t = pl.program_id(0)
    is_lat, first, _ = _tile_info(t)

    @pl.when(first)
    def _():
        st_ref[...] = jnp.where(is_lat, s0_ref[...], 0.0)

    ti = lax.broadcasted_iota(jnp.int32, (CHUNK, LANES), 0)
    sj = lax.broadcasted_iota(jnp.int32, (CHUNK, LANES), 1) & (CHUNK - 1)
    m_strict = (sj < ti, sj > ti)
    m_incl = (sj <= ti, sj >= ti)
    eye_cat = jnp.where(sj == ti, 1.0, 0.0)
    bi = lax.broadcasted_iota(jnp.int32, (LANES, LANES), 0)
    bj = lax.broadcasted_iota(jnp.int32, (LANES, LANES), 1)
    blk = (bi >> HEAD_SHIFT) == (bj >> HEAD_SHIFT)
    eye_bd = bi == bj
    zero_bd = jnp.zeros((LANES, LANES), BF16)

    units = [(d, p, c) for d in (0, 1) for p in range(N_PAIR) for c in range(N_CHUNK)]

    def op(i, u):
        d, p, c = u
        return ops[d][i][c * CHUNK:(c + 1) * CHUNK, p * LANES:(p + 1) * LANES]

    low, nak, mrbk = {}, {}, {}
    for u in units:
        lhs = jnp.concatenate([op(OP_A, u), op(OP_R, u)], axis=0)
        rhs = jnp.concatenate([_stack(op(OP_B, u)), _stack(op(OP_K, u))], axis=0)
        gram = _bdot(lhs, rhs, NT)
        d = u[0]
        low[u] = jnp.where(m_strict[d], gram[:CHUNK, :LANES], 0.0)
        nak[u] = jnp.where(m_strict[d], gram[:CHUNK, LANES:], 0.0).astype(BF16)
        mrbk[u] = jnp.concatenate([jnp.where(m_incl[d], gram[CHUNK:, :LANES], 0.0),
                                   jnp.where(m_incl[d], gram[CHUNK:, LANES:], 0.0)], axis=1).astype(BF16)

    inv = {u: eye_cat + low[u] for u in units}
    pwb = {u: low[u].astype(BF16) for u in units}
    for u in units:
        pwb[u] = _bdot(pwb[u], _stack(pwb[u])).astype(BF16)
    for _ in range(4):
        for u in units:
            both = _bdot(pwb[u], jnp.concatenate([_stack(pwb[u]), _stack(inv[u].astype(BF16))], axis=1))
            pwb[u] = both[:, :LANES].astype(BF16)
            inv[u] = inv[u] + both[:, LANES:]
    for u in units:
        inv[u] = inv[u] + _bdot(pwb[u], _stack(inv[u].astype(BF16)))

    sv = {u: _stack(op(OP_V, u)) for u in units}
    nv = {u: _bdot(nak[u], sv[u]).astype(BF16) for u in units}
    x1, x2 = {}, {}
    for u in units:
        x12 = _bdot(inv[u].astype(BF16), jnp.concatenate([_stack(op(OP_A, u)), _stack(nv[u])], axis=1))
        x1[u] = x12[:, :LANES].astype(BF16)
        x2[u] = x12[:, LANES:].astype(BF16)
    g_mat, w_mat = {}, {}
    for u in units:
        rhs = jnp.concatenate([jnp.concatenate([_stack(x1[u]), _stack(x2[u])], axis=1),
                               jnp.concatenate([zero_bd, sv[u]], axis=1)], axis=0)
        gw = _bdot(mrbk[u], rhs)
        g_mat[u] = (op(OP_R, u).astype(F32) + gw[:, :LANES]).astype(BF16)
        w_mat[u] = gw[:, LANES:]
    p_t, q_t = {}, {}
    for u in units:
        d, p, c = u
        gam = jnp.exp(ends_ref[d][c:c + 1, p * LANES:(p + 1) * LANES])
        p_t[u] = (jnp.where(blk, _bdot(x1[u], op(OP_BH, u), TN), 0.0)
                  + jnp.where(eye_bd, gam, 0.0)).astype(BF16)
        q_t[u] = jnp.where(blk, _bdot(jnp.concatenate([x2[u], op(OP_V, u)], axis=0),
                                      jnp.concatenate([op(OP_BH, u), op(OP_KH, u)], axis=0), TN), 0.0)

    state = {(d, p): st_ref[d, p] for d in (0, 1) for p in range(N_PAIR)}
    y_refs = (yf_ref, yb_ref)
    for step in range(N_CHUNK):
        for d in (0, 1):
            c = step if d == 0 else N_CHUNK - 1 - step
            for p in range(N_PAIR):
                u = (d, p, c)
                sb = state[d, p].astype(BF16)
                y_refs[d][c * CHUNK:(c + 1) * CHUNK, p * LANES:(p + 1) * LANES] = (
                    _bdot(g_mat[u], sb, NT) + w_mat[u])
                state[d, p] = _bdot(sb, p_t[u]) + q_t[u]
    for (d, p), s in state.items():
        st_ref[d, p] = s

    @pl.when(jnp.logical_not(is_lat))
    def _():
        for (d, p), s in state.items():
            sout_ref[0, d, 2 * p] = s[:HEAD, :HEAD]
            sout_ref[0, d, 2 * p + 1] = s[HEAD:, HEAD:]


def _scan(ops_f, ops_b, v, ends_f, ends_b, s0):
    fwd = pl.BlockSpec((TM, D_RWKV), lambda t: (t, 0))
    bwd = pl.BlockSpec((TM, D_RWKV), lambda t: (_mirror_tile(t), 0))
    e_block = (None, N_CHUNK, D_RWKV)
    st_block = (None, 2, N_PAIR, LANES, LANES)
    out_y = jax.ShapeDtypeStruct((N_TOK, D_RWKV), F32)
    return pl.pallas_call(
        _scan_kernel,
        grid=(N_TILES,),
        in_specs=[fwd] * (N_DIR_OPS + 1) + [bwd] * (N_DIR_OPS + 1)
                 + [pl.BlockSpec(e_block, lambda t: (t, 0, 0)),
                    pl.BlockSpec(e_block, lambda t: (_mirror_tile(t), 0, 0)),
                    pl.BlockSpec(st_block, lambda t: (jnp.maximum(_seq_id(t) - N_CTX_SEQ, 0), 0, 0, 0, 0))],
        out_specs=[fwd, bwd,
                   pl.BlockSpec((None, 1, 2, 2 * N_PAIR, HEAD, HEAD),
                                lambda t: (jnp.minimum(t, N_CTX_SEQ - 1), 0, 0, 0, 0, 0))],
        out_shape=[out_y, out_y,
                   jax.ShapeDtypeStruct((N_CTX_SEQ, 1, 2, 2 * N_PAIR, HEAD, HEAD), F32)],
        scratch_shapes=[pltpu.VMEM((2, N_PAIR, LANES, LANES), F32)],
        compiler_params=_params(),
        name="rwkv7_scan",
    )(*ops_f, v, *ops_b, v, ends_f, ends_b, s0)


def _back_kernel(x_ref, yf_ref, yb_ref, bv_ref, sg_ref, sa_ref, mb_ref, mod_ref, g_ref,
                 gng_ref, gnb_ref, wba_f32_ref, wout_f32_ref, ones_ref, o_ref, wba_ref, wout_ref):
    @pl.when(pl.program_id(0) == 0)
    def _():
        wba_ref[...] = wba_f32_ref[...].astype(BF16)
        wout_ref[...] = wout_f32_ref[...].astype(BF16)

    ones = ones_ref[...]
    groups = [slice(i * TM, (i + 1) * TM) for i in range(BACK_TM // TM)]
    ys = [yf_ref[r, :] + yb_ref[r, :] for r in groups]
    ycs = [y - _head_sum(y, ones) * (1.0 / HEAD) for y in ys]
    vs = [_head_sum(yc * yc, ones) * (1.0 / HEAD) for yc in ycs]
    yns = [yc * lax.rsqrt(v + EPS_GN) * gng_ref[...] + gnb_ref[...] for yc, v in zip(ycs, vs)]
    yas = [_dot((yn + bv_ref[r, :]) * sg_ref[r, :], wba_ref[...]) for r, yn in zip(groups, yns)]
    outs = [_dot(sa_ref[r, :] * ya + mb_ref[r, :], wout_ref[...]) for r, ya in zip(groups, yas)]
    for r, out in zip(groups, outs):
        o_ref[r, :] = x_ref[r, :] + mod_ref[5:6, :] * _rms(out, g_ref[3:4, :])


def _back(x, yf, yb, bv, sg, sa, mb, mod3, norm_g, gng, gnb, wba, wout, ones):
    tok = lambda t: (t, 0)
    spec_r = pl.BlockSpec((BACK_TM, D_RWKV), tok)
    spec_d = pl.BlockSpec((BACK_TM, D_MODEL), tok)
    consts = [norm_g, gng, gnb, wba, wout, ones]
    return pl.pallas_call(
        _back_kernel,
        grid=(N_TOK // BACK_TM,),
        in_specs=[spec_d, spec_r, spec_r, spec_r, spec_r, spec_d, spec_d,
                  pl.BlockSpec((None, N_MOD, D_MODEL), lambda t: (_mod_row(t * (BACK_TM // TM)), 0, 0))]
                 + [_const_spec(c.shape) for c in consts],
        out_specs=spec_d,
        out_shape=jax.ShapeDtypeStruct((N_TOK, D_MODEL), F32),
        scratch_shapes=[pltpu.VMEM(wba.shape, BF16), pltpu.VMEM(wout.shape, BF16)],
        compiler_params=_params(),
        name="mixer_back",
    )(x, yf, yb, bv, sg, sa, mb, mod3, *consts)


def _pair_blockdiag(s):
    lead = s.shape[:-3]
    s = s.reshape(lead + (N_PAIR, 2, HEAD, HEAD))
    z = jnp.zeros_like(s[..., 0, :, :])
    top = jnp.concatenate([s[..., 0, :, :], z], axis=-1)
    bot = jnp.concatenate([z, s[..., 1, :, :]], axis=-1)
    return jnp.concatenate([top, bot], axis=-2)


def kernel(x_prompt, x_sample, c, state_rwkv, c_ctx, w_mod, b_mod, norm_g, ffn1_w13, ffn1_w2,
           ffn2_w13, ffn2_w2, w_in, mu_shift, decay_w0, decay_w1, decay_w2, iclr_a0, iclr_a1,
           iclr_a2, k_k, k_a, r_k, gn_gain, gn_bias, conv_w, conv_b, w_branch_a, w_branch_b, w_out):
    assert x_prompt.shape == (N_CTX_SEQ, CTX_LEN, D_MODEL) and x_sample.shape == (N_LAT_SEQ, LAT_LEN, D_MODEL)
    assert w_mod.shape[0] == 1, "single trunk layer"

    cvec = jnp.concatenate([c_ctx[None, :], c, jnp.zeros((MOD_ROWS - 1 - N_LAT_SEQ, D_MODEL), F32)], axis=0)
    mod3 = _modulation(cvec, w_mod[0], b_mod).reshape(MOD_ROWS, N_MOD, D_MODEL)
    g = norm_g[0]

    x = _ffn_first(x_prompt.reshape(-1, D_MODEL), x_sample.reshape(-1, D_MODEL), mod3, g,
                   ffn1_w13[0], ffn1_w2[0])

    row = lambda p: p.reshape(1, -1)
    mu4 = mu_shift[0].reshape(4, D_MODEL)
    w1_cat = jnp.concatenate([decay_w1[0, 0], iclr_a1[0, 0], decay_w1[0, 1], iclr_a1[0, 1]], axis=1)
    w_aug = _lora_prep(jnp.repeat(mu4.T, LORA, axis=1), w1_cat)
    w2_blocks = [decay_w2[0, 0], iclr_a2[0, 0], decay_w2[0, 1], iclr_a2[0, 1]]
    w2_zero = jnp.zeros((LORA, D_RWKV), F32)
    w2_aug = jnp.concatenate(
        [jnp.concatenate([blk if j == i else w2_zero for j in range(4)], axis=1)
         for i, blk in enumerate(w2_blocks)], axis=0).astype(BF16)
    b2_aug = jnp.concatenate([decay_w0[0, 0], iclr_a0[0, 0], decay_w0[0, 1], iclr_a0[0, 1]]).reshape(1, -1)
    ones = _head_ones()
    front = _front(x, mod3, g, w_in[0], w_aug,
                   [w2_aug, b2_aug, row(k_k[0]), row(k_a[0]), row(r_k[0]), conv_w[0], row(conv_b[0]),
                    w_branch_b[0], ones])
    ops_f, ops_b = front[0:N_DIR_OPS], front[N_DIR_OPS:2 * N_DIR_OPS]
    v, ends_f, ends_b, bv, sg, sa, mb = front[2 * N_DIR_OPS:]

    s0 = _pair_blockdiag(state_rwkv[:, 0])
    yf, yb, s_fin = _scan(ops_f, ops_b, v, ends_f, ends_b, s0)

    x = _back(x, yf, yb, bv, sg, sa, mb, mod3, g, row(gn_gain[0]), row(gn_bias[0]),
              w_branch_a[0], w_out[0], ones)
    y_ctx, y_lat = _ffn_last(x, mod3, g, ffn2_w13[0], ffn2_w2[0])

    y_prompt = y_ctx.reshape(N_CTX_SEQ, CTX_LEN, D_MODEL)
    y_sample = y_lat.reshape(N_LAT_SEQ, LAT_LEN, D_MODEL)
    return y_prompt, y_sample, s_fin
```

```python
import jax
import jax.numpy as jnp
from jax import lax
from jax.experimental import pallas as pl
from jax.experimental.pallas import tpu as pltpu

F32 = jnp.float32
BF16 = jnp.bfloat16

D_MODEL = 1024
D_FF = 2816
D_RWKV = 512
D_CONV = 512
HEAD = 64
HEAD_SHIFT = 6
D_IN = 4 * D_RWKV + 3 * D_CONV + 2 * D_MODEL
N_MOD = 9
EPS_RMS = 1e-6
EPS_GN = 64e-5
HALF_STEP = 0.5
EXP_M05 = 0.6065306597126334

N_CTX_SEQ = 16
CTX_LEN = 256
N_LAT_SEQ = 2
LAT_LEN = 2048
GRID_W = 64
N_TOK = N_CTX_SEQ * CTX_LEN + N_LAT_SEQ * LAT_LEN

TM = 256
N_CTX_TILES = N_CTX_SEQ * CTX_LEN // TM
LAT_TILES = LAT_LEN // TM
N_TILES = N_TOK // TM
FRONT_TM = 512
FRONT_GROUPS = FRONT_TM // TM
N_FRONT_CTX_TILES = N_CTX_SEQ * CTX_LEN // FRONT_TM
FRONT_LAT_TILES = LAT_LEN // FRONT_TM
N_FRONT_CONSTS = 9
FRONT_WBB = 7
LORA = 64
SCAN_TM = 512
SCAN_TILES = SCAN_TM // TM
N_SCAN_CTX_STEPS = N_CTX_SEQ * CTX_LEN // SCAN_TM
SCAN_LAT_STEPS = LAT_LEN // SCAN_TM
BACK_TM = 512
FFN_TM = 512
FFN_GROUPS = 2
N_FFN_TILES = N_TOK // FFN_TM
N_FFN_CTX_TILES = N_CTX_SEQ * CTX_LEN // FFN_TM
N_WCHUNK = 11
CHUNK = 64
N_CHUNK = TM // CHUNK
LANES = 128
N_PAIR = D_RWKV // LANES
MOD_ROWS = 8
MOD_BLOCK = 3
VMEM_LIMIT = 56 * 1024 * 1024

OP_A, OP_R, OP_B, OP_K, OP_BH, OP_KH, OP_V = range(7)
N_DIR_OPS = 6

NN = (((1,), (0,)), ((), ()))
NT = (((1,), (1,)), ((), ()))
TN = (((0,), (0,)), ((), ()))


def _dot(a, b, dims=NN):
    return lax.dot_general(a.astype(BF16), b.astype(BF16), dims, preferred_element_type=F32)


def _bdot(a, b, dims=NN):
    return lax.dot_general(a, b, dims, preferred_element_type=F32)


def _rms(x, g):
    ms = jnp.mean(x * x, axis=-1, keepdims=True)
    return x * lax.rsqrt(ms + EPS_RMS) * g


def _head_ones():
    i = lax.broadcasted_iota(jnp.int32, (D_RWKV, D_RWKV), 0) >> HEAD_SHIFT
    j = lax.broadcasted_iota(jnp.int32, (D_RWKV, D_RWKV), 1) >> HEAD_SHIFT
    return jnp.where(i == j, 1.0, 0.0).astype(BF16)


def _head_sum(x, ones):
    return jnp.dot(x.astype(BF16), ones, preferred_element_type=F32)


def _sigmoid(x):
    return 0.5 * jnp.tanh(0.5 * x) + 0.5


def _mod_row(t):
    return jnp.where(t < N_CTX_TILES, 0, 1 + (t - N_CTX_TILES) // LAT_TILES)


def _const_spec(shape):
    nd = len(shape)
    return pl.BlockSpec(shape, lambda *_: (0,) * nd, pipeline_mode=pl.Buffered(1))


def _params(n_axes=1):
    return pltpu.CompilerParams(dimension_semantics=("arbitrary",) * n_axes,
                                vmem_limit_bytes=VMEM_LIMIT)


def _mod_kernel(c_ref, w_ref, b_ref, o_ref):
    c = c_ref[...]
    s = c * jax.nn.sigmoid(c)
    o_ref[...] = _dot(s, w_ref[...]) + b_ref[...]


def _modulation(cvec, w_mod, b_mod):
    width = MOD_BLOCK * D_MODEL
    return pl.pallas_call(
        _mod_kernel,
        grid=(N_MOD // MOD_BLOCK,),
        in_specs=[pl.BlockSpec((MOD_ROWS, D_MODEL), lambda j: (0, 0)),
                  pl.BlockSpec((D_MODEL, width), lambda j: (0, j)),
                  pl.BlockSpec((1, width), lambda j: (0, j))],
        out_specs=pl.BlockSpec((MOD_ROWS, width), lambda j: (0, j)),
        out_shape=jax.ShapeDtypeStruct((MOD_ROWS, N_MOD * D_MODEL), F32),
        compiler_params=_params(),
        name="modulation",
    )(cvec, w_mod, b_mod)


def _ffn_body(x, mod_ref, g_ref, w13_ref, w2_ref, im, ig):
    shift = mod_ref[im:im + 1, :]
    scale = mod_ref[im + 1:im + 2, :]
    gate = mod_ref[im + 2:im + 3, :]
    rows = FFN_TM // FFN_GROUPS
    xs = [x[i * rows:(i + 1) * rows, :] for i in range(FFN_GROUPS)]
    hs = [(_rms(xi, g_ref[ig:ig + 1, :]) * (1.0 + scale) + shift).astype(BF16) for xi in xs]
    gus = [_dot(hi, w13_ref[...]) for hi in hs]
    acts = []
    for gu in gus:
        gt = gu[:, :D_FF]
        up = gu[:, D_FF:]
        acts.append((gt * jax.nn.sigmoid(gt) * up).astype(BF16))
    os_ = [_dot(ai, w2_ref[...]) for ai in acts]
    outs = [xi + HALF_STEP * gate * _rms(oi, g_ref[ig + 1:ig + 2, :]) for xi, oi in zip(xs, os_)]
    return jnp.concatenate(outs, axis=0)


def _cast_chunk(step, src_ref, dst_ref, axis):
    size = src_ref.shape[axis]

    @pl.when(step < N_WCHUNK)
    def _():
        start = pl.multiple_of(step * size, size)
        if axis == 0:
            dst_ref[pl.ds(start, size), :] = src_ref[...].astype(BF16)
        else:
            dst_ref[:, pl.ds(start, size)] = src_ref[...].astype(BF16)


def _chunk_spec(w, axis):
    block = list(w.shape)
    block[axis] = w.shape[axis] // N_WCHUNK
    clamp = lambda s: jnp.minimum(s, N_WCHUNK - 1)
    index = (lambda s: (clamp(s), 0)) if axis == 0 else (lambda s: (0, clamp(s)))
    return pl.BlockSpec(tuple(block), index)


def _tile_step(s):
    return jnp.maximum(s - N_WCHUNK, 0)


def _ffn_first_kernel(xc_ref, xl_ref, mod_ref, g_ref, w13_ref, w2_ref, o_ref, w13_bf, w2_bf):
    s = pl.program_id(0)
    _cast_chunk(s, w13_ref, w13_bf, 1)
    _cast_chunk(s, w2_ref, w2_bf, 0)

    @pl.when(s >= N_WCHUNK)
    def _():
        x = jnp.where(s - N_WCHUNK < N_FFN_CTX_TILES, xc_ref[...], xl_ref[...])
        o_ref[...] = _ffn_body(x, mod_ref, g_ref, w13_bf, w2_bf, 0, 0)


def _ffn_last_kernel(x_ref, mod_ref, g_ref, w13_ref, w2_ref, oc_ref, ol_ref, w13_bf, w2_bf):
    s = pl.program_id(0)
    _cast_chunk(s, w13_ref, w13_bf, 1)
    _cast_chunk(s, w2_ref, w2_bf, 0)

    @pl.when(s >= N_WCHUNK)
    def _():
        out = _ffn_body(x_ref[...], mod_ref, g_ref, w13_bf, w2_bf, 6, 4)

        @pl.when(s - N_WCHUNK < N_FFN_CTX_TILES)
        def _():
            oc_ref[...] = out

        @pl.when(s - N_WCHUNK >= N_FFN_CTX_TILES)
        def _():
            ol_ref[...] = out


_FFN_CTX_TILE = lambda s: (jnp.minimum(_tile_step(s), N_FFN_CTX_TILES - 1), 0)
_FFN_LAT_TILE = lambda s: (jnp.maximum(_tile_step(s) - N_FFN_CTX_TILES, 0), 0)
_FFN_MOD = lambda s: (_mod_row(_tile_step(s) * (FFN_TM // TM)), 0, 0)
_FFN_SCRATCH = [pltpu.VMEM((D_MODEL, 2 * D_FF), BF16), pltpu.VMEM((D_FF, D_MODEL), BF16)]


def _ffn_first(x_ctx, x_lat, mod3, norm_g, w13, w2):
    return pl.pallas_call(
        _ffn_first_kernel,
        grid=(N_WCHUNK + N_FFN_TILES,),
        in_specs=[pl.BlockSpec((FFN_TM, D_MODEL), _FFN_CTX_TILE),
                  pl.BlockSpec((FFN_TM, D_MODEL), _FFN_LAT_TILE),
                  pl.BlockSpec((None, N_MOD, D_MODEL), _FFN_MOD),
                  _const_spec(norm_g.shape),
                  _chunk_spec(w13, 1),
                  _chunk_spec(w2, 0)],
        out_specs=pl.BlockSpec((FFN_TM, D_MODEL), lambda s: (_tile_step(s), 0)),
        out_shape=jax.ShapeDtypeStruct((N_TOK, D_MODEL), F32),
        scratch_shapes=_FFN_SCRATCH,
        compiler_params=_params(),
        name="ffn1",
    )(x_ctx, x_lat, mod3, norm_g, w13, w2)


def _ffn_last(x, mod3, norm_g, w13, w2):
    n_ctx = N_CTX_SEQ * CTX_LEN
    return pl.pallas_call(
        _ffn_last_kernel,
        grid=(N_WCHUNK + N_FFN_TILES,),
        in_specs=[pl.BlockSpec((FFN_TM, D_MODEL), lambda s: (_tile_step(s), 0)),
                  pl.BlockSpec((None, N_MOD, D_MODEL), _FFN_MOD),
                  _const_spec(norm_g.shape),
                  _chunk_spec(w13, 1),
                  _chunk_spec(w2, 0)],
        out_specs=[pl.BlockSpec((FFN_TM, D_MODEL), _FFN_CTX_TILE),
                   pl.BlockSpec((FFN_TM, D_MODEL), _FFN_LAT_TILE)],
        out_shape=[jax.ShapeDtypeStruct((n_ctx, D_MODEL), F32),
                   jax.ShapeDtypeStruct((N_TOK - n_ctx, D_MODEL), F32)],
        scratch_shapes=_FFN_SCRATCH,
        compiler_params=_params(),
        name="ffn2",
    )(x, mod3, norm_g, w13, w2)


def _lora_prep_kernel(mu_ref, w1_ref, o_ref):
    mu = mu_ref[...]
    w1 = w1_ref[...]
    n = w1.shape[1]
    o_ref[:, :n] = ((1.0 - mu) * w1).astype(BF16)
    o_ref[:, n:] = (mu * w1).astype(BF16)


def _lora_prep(mu_cols, w1_cat):
    n = w1_cat.shape[1]
    return pl.pallas_call(
        _lora_prep_kernel,
        out_shape=jax.ShapeDtypeStruct((D_MODEL, 2 * n), BF16),
        compiler_params=pltpu.CompilerParams(vmem_limit_bytes=VMEM_LIMIT),
        name="lora_prep",
    )(mu_cols, w1_cat)


def _front_group(gi, h, halo_prev, halo_next, is_lat, win_ref, waug_ref, refs, outs):
    w2aug_ref, b2aug_ref, kkw_ref, ka_ref, rk_ref, cw_ref, cbias_ref, wbb_ref, ones_ref = refs
    ops_o, v_o, ends_o, bv_o, sg_o, sa_o, mb_o = outs
    rows = slice(gi * TM, (gi + 1) * TM)
    row = lax.broadcasted_iota(jnp.int32, (TM, 1), 0)
    rin = row & (CHUNK - 1)
    ones = ones_ref[...]

    def proj(lo, hi):
        return _bdot(h, win_ref[:, lo:hi])

    pab = _bdot(h, waug_ref[...])
    rk = proj(0, 2 * D_RWKV)
    r = rk[:, :D_RWKV]
    k = rk[:, D_RWKV:]
    pb = pab[:, 2 * LANES:]
    sh_f = jnp.where(row == 0, halo_prev, pltpu.roll(pb[:, :LANES], 1, 0))
    sh_b = jnp.where(row == TM - 1, halo_next, pltpu.roll(pb[:, LANES:], TM - 1, 0))
    t_in = pab[:, :2 * LANES] + jnp.concatenate([sh_f, sh_b], axis=1)
    lane = lax.broadcasted_iota(jnp.int32, (1, 2 * LANES), 1)
    t_in = jnp.where((lane & HEAD) == 0, jnp.tanh(t_in), t_in)
    za = _dot(t_in, w2aug_ref[...]) + b2aug_ref[...]

    kk = k * kkw_ref[...]
    kkn = kk * lax.rsqrt(_head_sum(kk * kk, ones) + 1e-12)
    ka = ka_ref[...]

    vg = proj(2 * D_RWKV, 4 * D_RWKV)
    v = vg[:, :D_RWKV]
    v_o[rows, :] = v.astype(BF16)
    sg_o[rows, :] = _sigmoid(vg[:, D_RWKV:]).astype(BF16)

    def scan_operands(d):
        lw = -EXP_M05 * _sigmoid(za[:, 2 * d * D_RWKV:(2 * d + 1) * D_RWKV])
        a = _sigmoid(za[:, (2 * d + 1) * D_RWKV:(2 * d + 2) * D_RWKV])
        k_d = k * (1.0 + (a - 1.0) * ka)
        b = kkn * a
        cs = lw
        for s in (1, 2, 4, 8, 16, 32):
            if d == 0:
                cs = cs + jnp.where(rin >= s, pltpu.roll(cs, s, 0), 0.0)
            else:
                cs = cs + jnp.where(rin < CHUNK - s, pltpu.roll(cs, TM - s, 0), 0.0)
        end_row = CHUNK - 1 if d == 0 else 0
        ends = [cs[c * CHUNK + end_row:c * CHUNK + end_row + 1, :] for c in range(N_CHUNK)]
        for c in range(N_CHUNK):
            ends_o[d][gi, c:c + 1, :] = ends[c]
        cs_end = jnp.concatenate([jnp.broadcast_to(e, (CHUNK, D_RWKV)) for e in ends], axis=0)
        dec_inv = jnp.exp(-cs)
        dec_rest = jnp.exp(cs_end - cs)
        o_a, o_r, o_b, o_k, o_bh, o_kh = ops_o[d]
        o_a[rows, :] = (-kkn * jnp.exp(cs - lw)).astype(BF16)
        o_r[rows, :] = (r * jnp.exp(cs)).astype(BF16)
        o_b[rows, :] = (b * dec_inv).astype(BF16)
        o_k[rows, :] = (k_d * dec_inv).astype(BF16)
        o_bh[rows, :] = (b * dec_rest).astype(BF16)
        o_kh[rows, :] = (k_d * dec_rest).astype(BF16)
        return k_d

    conv_in = proj(4 * D_RWKV, 4 * D_RWKV + 3 * D_CONV)
    k_0 = scan_operands(0)
    gate_a = proj(4 * D_RWKV + 3 * D_CONV, 4 * D_RWKV + 3 * D_CONV + D_MODEL)
    sa_o[rows, :] = _sigmoid(gate_a).astype(BF16)
    k_1 = scan_operands(1)
    gate_b = proj(4 * D_RWKV + 3 * D_CONV + D_MODEL, D_IN)

    cgate = conv_in[:, :D_CONV]
    u = conv_in[:, D_CONV:2 * D_CONV] * conv_in[:, 2 * D_CONV:]
    col = row & (GRID_W - 1)
    zl = jnp.logical_or(row == 0, jnp.logical_and(is_lat, col == 0))
    zr = jnp.logical_or(row == TM - 1, jnp.logical_and(is_lat, col == GRID_W - 1))
    left = jnp.where(zl, 0.0, pltpu.roll(u, 1, 0))
    right = jnp.where(zr, 0.0, pltpu.roll(u, TM - 1, 0))
    conv = left * cw_ref[0:1, :] + u * cw_ref[1:2, :] + right * cw_ref[2:3, :] + cbias_ref[...]
    y_b = _dot(cgate * conv, wbb_ref[...])
    mb_o[rows, :] = (_sigmoid(gate_b) * y_b).astype(BF16)

    bv_o[rows, :] = (_head_sum(r * (k_0 + k_1) * rk_ref[...], ones) * v).astype(BF16)


def _front_kernel(x_ref, xp_ref, xn_ref, mod_ref, g_ref, win_ref, waug_ref, *rest):
    refs = list(rest[:N_FRONT_CONSTS])
    outs = rest[N_FRONT_CONSTS:-2]
    win_bf, wbb_bf = rest[-2:]
    ops_o = (outs[0:N_DIR_OPS], outs[N_DIR_OPS:2 * N_DIR_OPS])
    v_o, ends0_o, ends1_o, bv_o, sg_o, sa_o, mb_o = outs[2 * N_DIR_OPS:]
    outs = (ops_o, v_o, (ends0_o, ends1_o), bv_o, sg_o, sa_o, mb_o)

    s = pl.program_id(0)
    _cast_chunk(s, win_ref, win_bf, 1)

    @pl.when(s == 0)
    def _():
        wbb_bf[...] = refs[FRONT_WBB][...].astype(BF16)

    refs[FRONT_WBB] = wbb_bf

    @pl.when(s >= N_WCHUNK)
    def _():
        t = s - N_WCHUNK
        is_lat = t >= N_FRONT_CTX_TILES
        i_in = (t - N_FRONT_CTX_TILES) % FRONT_LAT_TILES
        lat_first = jnp.logical_and(is_lat, i_in == 0)
        lat_last = jnp.logical_and(is_lat, i_in == FRONT_LAT_TILES - 1)
        shift = mod_ref[3:4, :]
        scale = mod_ref[4:5, :]
        g2 = g_ref[2:3, :]

        def pre(x):
            return (_rms(x, g2) * (1.0 + scale) + shift).astype(BF16)

        x = x_ref[...]
        edge = jnp.concatenate([xp_ref[...], x[TM - 8:TM + 8, :], xn_ref[...]], axis=0)
        edge_b = _bdot(pre(edge), waug_ref[:, 2 * LANES:])
        zero = jnp.zeros((1, LANES), F32)
        halo_prev = (jnp.where(jnp.logical_and(is_lat, jnp.logical_not(lat_first)), edge_b[7:8, :LANES], zero),
                     jnp.where(is_lat, edge_b[15:16, :LANES], zero))
        halo_next = (jnp.where(is_lat, edge_b[16:17, LANES:], zero),
                     jnp.where(jnp.logical_and(is_lat, jnp.logical_not(lat_last)), edge_b[24:25, LANES:], zero))

        for gi in range(FRONT_GROUPS):
            h = pre(x[gi * TM:(gi + 1) * TM, :])
            _front_group(gi, h, halo_prev[gi], halo_next[gi], is_lat, win_bf, waug_ref, refs, outs)


def _front(x, mod3, norm_g, w_in, w_aug, consts):
    tok = lambda s: (_tile_step(s), 0)
    rows8 = FRONT_TM // 8
    last8 = N_TOK // 8 - 1
    assert len(consts) == N_FRONT_CONSTS
    out_rb = jax.ShapeDtypeStruct((N_TOK, D_RWKV), BF16)
    out_db = jax.ShapeDtypeStruct((N_TOK, D_MODEL), BF16)
    out_e = jax.ShapeDtypeStruct((N_TILES, N_CHUNK, D_RWKV), F32)
    spec_r = pl.BlockSpec((FRONT_TM, D_RWKV), tok)
    spec_d = pl.BlockSpec((FRONT_TM, D_MODEL), tok)
    spec_e = pl.BlockSpec((FRONT_GROUPS, N_CHUNK, D_RWKV), lambda s: (_tile_step(s), 0, 0))
    n_b = 2 * N_DIR_OPS + 1
    return pl.pallas_call(
        _front_kernel,
        grid=(N_WCHUNK + N_TOK // FRONT_TM,),
        in_specs=[pl.BlockSpec((FRONT_TM, D_MODEL), tok),
                  pl.BlockSpec((8, D_MODEL), lambda s: (jnp.maximum(_tile_step(s) * rows8 - 1, 0), 0)),
                  pl.BlockSpec((8, D_MODEL), lambda s: (jnp.minimum((_tile_step(s) + 1) * rows8, last8), 0)),
                  pl.BlockSpec((None, N_MOD, D_MODEL), lambda s: (_mod_row(_tile_step(s) * FRONT_GROUPS), 0, 0)),
                  _const_spec(norm_g.shape),
                  _chunk_spec(w_in, 1)]
                 + [_const_spec(c.shape) for c in [w_aug] + list(consts)],
        out_specs=[spec_r] * n_b + [spec_e] * 2 + [spec_r] * 2 + [spec_d] * 2,
        out_shape=[out_rb] * n_b + [out_e] * 2 + [out_rb] * 2 + [out_db] * 2,
        scratch_shapes=[pltpu.VMEM((D_MODEL, D_IN), BF16), pltpu.VMEM((D_CONV, D_MODEL), BF16)],
        compiler_params=_params(),
        name="mixer_front",
    )(x, x, x, mod3, norm_g, w_in, w_aug, *consts)


def _stack(x):
    lane_lo = lax.broadcasted_iota(jnp.int32, x.shape, 1) < HEAD
    z = jnp.zeros_like(x)
    return jnp.concatenate([jnp.where(lane_lo, x, z), jnp.where(lane_lo, z, x)], axis=0)


def _scan_kernel(*refs):
    ops = (refs[0:N_DIR_OPS + 1], refs[N_DIR_OPS + 1:2 * N_DIR_OPS + 2])
    ends0_ref, ends1_ref, s0_ref, yf_ref, yb_ref, sout_ref, st_ref = refs[2 * N_DIR_OPS + 2:]
    ends_ref = (ends0_ref, ends1_ref)
    step_id = pl.program_id(0)
    is_lat = step_id >= N_SCAN_CTX_STEPS
    lat_first = (step_id - N_SCAN_CTX_STEPS) % SCAN_LAT_STEPS == 0

    @pl.when(jnp.logical_and(is_lat, lat_first))
    def _():
        zero = jnp.zeros((HEAD, HEAD), F32)
        for d in (0, 1):
            for p in range(N_PAIR):
                top = jnp.concatenate([s0_ref[0, d, 2 * p], zero], axis=1)
                bottom = jnp.concatenate([zero, s0_ref[0, d, 2 * p + 1]], axis=1)
                st_ref[d, p] = jnp.concatenate([top, bottom], axis=0)

    ti = lax.broadcasted_iota(jnp.int32, (CHUNK, LANES), 0)
    sj = lax.broadcasted_iota(jnp.int32, (CHUNK, LANES), 1) & (CHUNK - 1)
    m_strict = (sj < ti, sj > ti)
    m_incl = (sj <= ti, sj >= ti)
    eye_cat = jnp.where(sj == ti, 1.0, 0.0)
    bi = lax.broadcasted_iota(jnp.int32, (LANES, LANES), 0)
    bj = lax.broadcasted_iota(jnp.int32, (LANES, LANES), 1)
    blk = (bi >> HEAD_SHIFT) == (bj >> HEAD_SHIFT)
    eye_bd = bi == bj
    zero_bd = jnp.zeros((LANES, LANES), BF16)

    def op(i, u):
        d, p, c = u
        return ops[d][i][c * CHUNK:(c + 1) * CHUNK, p * LANES:(p + 1) * LANES]

    g_mat, w_mat, p_t, q_t = {}, {}, {}, {}

    def local_stages(units):
        low, nak, mrbk = {}, {}, {}
        for u in units:
            lhs = jnp.concatenate([op(OP_A, u), op(OP_R, u)], axis=0)
            rhs = jnp.concatenate([_stack(op(OP_B, u)), _stack(op(OP_K, u))], axis=0)
            gram = _bdot(lhs, rhs, NT)
            d = u[0]
            low[u] = jnp.where(m_strict[d], gram[:CHUNK, :LANES], 0.0)
            nak[u] = jnp.where(m_strict[d], gram[:CHUNK, LANES:], 0.0).astype(BF16)
            mrbk[u] = jnp.concatenate([jnp.where(m_incl[d], gram[CHUNK:, :LANES], 0.0),
                                       jnp.where(m_incl[d], gram[CHUNK:, LANES:], 0.0)], axis=1).astype(BF16)

        inv = {u: eye_cat + low[u] for u in units}
        pwb = {u: low[u].astype(BF16) for u in units}
        for u in units:
            pwb[u] = _bdot(pwb[u], _stack(pwb[u])).astype(BF16)
        for _ in range(4):
            for u in units:
                both = _bdot(pwb[u], jnp.concatenate([_stack(pwb[u]), _stack(inv[u].astype(BF16))], axis=1))
                pwb[u] = both[:, :LANES].astype(BF16)
                inv[u] = inv[u] + both[:, LANES:]
        for u in units:
            inv[u] = inv[u] + _bdot(pwb[u], _stack(inv[u].astype(BF16)))

        sv = {u: _stack(op(OP_V, u)) for u in units}
        nv = {}
        for u in units:
            nv[u] = _bdot(nak[u], sv[u]).astype(BF16)
        x1, x2 = {}, {}
        for u in units:
            x12 = _bdot(inv[u].astype(BF16), jnp.concatenate([_stack(op(OP_A, u)), _stack(nv[u])], axis=1))
            x1[u] = x12[:, :LANES].astype(BF16)
            x2[u] = x12[:, LANES:].astype(BF16)
        for u in units:
            rhs = jnp.concatenate([jnp.concatenate([_stack(x1[u]), _stack(x2[u])], axis=1),
                                   jnp.concatenate([zero_bd, sv[u]], axis=1)], axis=0)
            gw = _bdot(mrbk[u], rhs)
            g_mat[u] = (op(OP_R, u).astype(F32) + gw[:, :LANES]).astype(BF16)
            w_mat[u] = gw[:, LANES:]
        for u in units:
            d, p, c = u
            ct = c % N_CHUNK
            gam = jnp.exp(ends_ref[d][c // N_CHUNK, ct:ct + 1, p * LANES:(p + 1) * LANES])
            p_t[u] = (jnp.where(blk, _bdot(x1[u], op(OP_BH, u), TN), 0.0)
                      + jnp.where(eye_bd, gam, 0.0)).astype(BF16)
            q_t[u] = jnp.where(blk, _bdot(jnp.concatenate([x2[u], op(OP_V, u)], axis=0),
                                          jnp.concatenate([op(OP_BH, u), op(OP_KH, u)], axis=0), TN), 0.0)

    n_c = SCAN_TILES * N_CHUNK
    state = {(d, p): jnp.where(is_lat, st_ref[d, p], 0.0) for d in (0, 1) for p in range(N_PAIR)}
    finals = {}
    y_refs = (yf_ref, yb_ref)

    def chunk_of(d, step):
        return step if d == 0 else n_c - 1 - step

    def chain(step, d, p):
        c = chunk_of(d, step)
        u = (d, p, c)
        sb = state[d, p].astype(BF16)
        y_refs[d][c * CHUNK:(c + 1) * CHUNK, p * LANES:(p + 1) * LANES] = _bdot(g_mat[u], sb, NT) + w_mat[u]
        state[d, p] = _bdot(sb, p_t[u]) + q_t[u]
        if (step + 1) % N_CHUNK == 0:
            finals[d, c // N_CHUNK, p] = state[d, p]
            if step + 1 < n_c:
                state[d, p] = jnp.where(is_lat, state[d, p], 0.0)

    local_stages([(d, p, c) for d in (0, 1) for p in range(N_PAIR) for c in range(n_c)])
    for step in range(n_c):
        for d in (0, 1):
            for p in range(N_PAIR):
                chain(step, d, p)
    for (d, p), s in state.items():
        st_ref[d, p] = s

    @pl.when(jnp.logical_not(is_lat))
    def _():
        for (d, tile, p), s in finals.items():
            sout_ref[tile, 0, d, 2 * p] = s[:HEAD, :HEAD]
            sout_ref[tile, 0, d, 2 * p + 1] = s[HEAD:, HEAD:]


def _scan(ops_f, ops_b, v, ends_f, ends_b, s0):
    def mirror(s):
        u = s - N_SCAN_CTX_STEPS
        return jnp.where(s < N_SCAN_CTX_STEPS, s,
                         N_SCAN_CTX_STEPS + (u // SCAN_LAT_STEPS) * SCAN_LAT_STEPS
                         + (SCAN_LAT_STEPS - 1 - u % SCAN_LAT_STEPS))

    def lat_seq(s):
        return jnp.maximum(s - N_SCAN_CTX_STEPS, 0) // SCAN_LAT_STEPS

    fwd = pl.BlockSpec((SCAN_TM, D_RWKV), lambda s: (s, 0))
    bwd = pl.BlockSpec((SCAN_TM, D_RWKV), lambda s: (mirror(s), 0))
    e_block = (SCAN_TILES, N_CHUNK, D_RWKV)
    st_block = (None, 1, 2, 2 * N_PAIR, HEAD, HEAD)
    out_y = jax.ShapeDtypeStruct((N_TOK, D_RWKV), F32)
    return pl.pallas_call(
        _scan_kernel,
        grid=(N_TOK // SCAN_TM,),
        in_specs=[fwd] * (N_DIR_OPS + 1) + [bwd] * (N_DIR_OPS + 1)
                 + [pl.BlockSpec(e_block, lambda s: (s, 0, 0)),
                    pl.BlockSpec(e_block, lambda s: (mirror(s), 0, 0)),
                    pl.BlockSpec(st_block, lambda s: (lat_seq(s), 0, 0, 0, 0, 0))],
        out_specs=[fwd, bwd,
                   pl.BlockSpec((SCAN_TILES, 1, 2, 2 * N_PAIR, HEAD, HEAD),
                                lambda s: (jnp.minimum(s, N_SCAN_CTX_STEPS - 1), 0, 0, 0, 0, 0))],
        out_shape=[out_y, out_y,
                   jax.ShapeDtypeStruct((N_CTX_SEQ, 1, 2, 2 * N_PAIR, HEAD, HEAD), F32)],
        scratch_shapes=[pltpu.VMEM((2, N_PAIR, LANES, LANES), F32)],
        compiler_params=_params(),
        name="rwkv7_scan",
    )(*ops_f, v, *ops_b, v, ends_f, ends_b, s0)


def _back_kernel(x_ref, yf_ref, yb_ref, bv_ref, sg_ref, sa_ref, mb_ref, mod_ref, g_ref,
                 gng_ref, gnb_ref, wba_f32_ref, wout_f32_ref, ones_ref, o_ref, wba_ref, wout_ref):
    @pl.when(pl.program_id(0) == 0)
    def _():
        wba_ref[...] = wba_f32_ref[...].astype(BF16)
        wout_ref[...] = wout_f32_ref[...].astype(BF16)

    ones = ones_ref[...]
    groups = [slice(i * TM, (i + 1) * TM) for i in range(BACK_TM // TM)]
    ys = [yf_ref[r, :] + yb_ref[r, :] for r in groups]
    ycs = [y - _head_sum(y, ones) * (1.0 / HEAD) for y in ys]
    vs = [_head_sum(yc * yc, ones) * (1.0 / HEAD) for yc in ycs]
    yns = [yc * lax.rsqrt(v + EPS_GN) * gng_ref[...] + gnb_ref[...] for yc, v in zip(ycs, vs)]
    yas = [_dot((yn + bv_ref[r, :]) * sg_ref[r, :], wba_ref[...]) for r, yn in zip(groups, yns)]
    outs = [_dot(sa_ref[r, :] * ya + mb_ref[r, :], wout_ref[...]) for r, ya in zip(groups, yas)]
    for r, out in zip(groups, outs):
        o_ref[r, :] = x_ref[r, :] + mod_ref[5:6, :] * _rms(out, g_ref[3:4, :])


def _back(x, yf, yb, bv, sg, sa, mb, mod3, norm_g, gng, gnb, wba, wout, ones):
    tok = lambda t: (t, 0)
    spec_r = pl.BlockSpec((BACK_TM, D_RWKV), tok)
    spec_d = pl.BlockSpec((BACK_TM, D_MODEL), tok)
    consts = [norm_g, gng, gnb, wba, wout, ones]
    return pl.pallas_call(
        _back_kernel,
        grid=(N_TOK // BACK_TM,),
        in_specs=[spec_d, spec_r, spec_r, spec_r, spec_r, spec_d, spec_d,
                  pl.BlockSpec((None, N_MOD, D_MODEL), lambda t: (_mod_row(t * (BACK_TM // TM)), 0, 0))]
                 + [_const_spec(c.shape) for c in consts],
        out_specs=spec_d,
        out_shape=jax.ShapeDtypeStruct((N_TOK, D_MODEL), F32),
        scratch_shapes=[pltpu.VMEM(wba.shape, BF16), pltpu.VMEM(wout.shape, BF16)],
        compiler_params=_params(),
        name="mixer_back",
    )(x, yf, yb, bv, sg, sa, mb, mod3, *consts)


def kernel(x_prompt, x_sample, c, state_rwkv, c_ctx, w_mod, b_mod, norm_g, ffn1_w13, ffn1_w2,
           ffn2_w13, ffn2_w2, w_in, mu_shift, decay_w0, decay_w1, decay_w2, iclr_a0, iclr_a1,
           iclr_a2, k_k, k_a, r_k, gn_gain, gn_bias, conv_w, conv_b, w_branch_a, w_branch_b, w_out):
    assert x_prompt.shape == (N_CTX_SEQ, CTX_LEN, D_MODEL) and x_sample.shape == (N_LAT_SEQ, LAT_LEN, D_MODEL)
    assert w_mod.shape[0] == 1, "single trunk layer"

    cvec = jnp.concatenate([c_ctx[None, :], c, jnp.zeros((MOD_ROWS - 1 - N_LAT_SEQ, D_MODEL), F32)], axis=0)
    mod3 = _modulation(cvec, w_mod[0], b_mod).reshape(MOD_ROWS, N_MOD, D_MODEL)
    g = norm_g[0]

    x = _ffn_first(x_prompt.reshape(-1, D_MODEL), x_sample.reshape(-1, D_MODEL), mod3, g,
                   ffn1_w13[0], ffn1_w2[0])

    row = lambda p: p.reshape(1, -1)
    mu4 = mu_shift[0].reshape(4, D_MODEL)
    w1_cat = jnp.concatenate([decay_w1[0, 0], iclr_a1[0, 0], decay_w1[0, 1], iclr_a1[0, 1]], axis=1)
    w_aug = _lora_prep(jnp.repeat(mu4.T, LORA, axis=1), w1_cat)
    w2_blocks = [decay_w2[0, 0], iclr_a2[0, 0], decay_w2[0, 1], iclr_a2[0, 1]]
    w2_zero = jnp.zeros((LORA, D_RWKV), F32)
    w2_aug = jnp.concatenate(
        [jnp.concatenate([blk if j == i else w2_zero for j in range(4)], axis=1)
         for i, blk in enumerate(w2_blocks)], axis=0).astype(BF16)
    b2_aug = jnp.concatenate([decay_w0[0, 0], iclr_a0[0, 0], decay_w0[0, 1], iclr_a0[0, 1]]).reshape(1, -1)
    ones = _head_ones()
    front = _front(x, mod3, g, w_in[0], w_aug,
                   [w2_aug, b2_aug, row(k_k[0]), row(k_a[0]), row(r_k[0]), conv_w[0], row(conv_b[0]),
                    w_branch_b[0], ones])
    ops_f, ops_b = front[0:N_DIR_OPS], front[N_DIR_OPS:2 * N_DIR_OPS]
    v, ends_f, ends_b, bv, sg, sa, mb = front[2 * N_DIR_OPS:]

    yf, yb, s_fin = _scan(ops_f, ops_b, v, ends_f, ends_b, state_rwkv)

    x = _back(x, yf, yb, bv, sg, sa, mb, mod3, g, row(gn_gain[0]), row(gn_bias[0]),
              w_branch_a[0], w_out[0], ones)
    y_ctx, y_lat = _ffn_last(x, mod3, g, ffn2_w13[0], ffn2_w2[0])

    y_prompt = y_ctx.reshape(N_CTX_SEQ, CTX_LEN, D_MODEL)
    y_sample = y_lat.reshape(N_LAT_SEQ, LAT_LEN, D_MODEL)
    return y_prompt, y_sample, s_fin
```

```python
import jax
import jax.numpy as jnp
from jax import lax
from jax.experimental import pallas as pl
from jax.experimental.pallas import tpu as pltpu

F32 = jnp.float32
BF16 = jnp.bfloat16

D_MODEL = 1024
D_FF = 2816
D_RWKV = 512
D_CONV = 512
HEAD = 64
HEAD_SHIFT = 6
D_IN = 4 * D_RWKV + 3 * D_CONV + 2 * D_MODEL
N_MOD = 9
EPS_RMS = 1e-6
EPS_GN = 64e-5
HALF_STEP = 0.5
EXP_M05 = 0.6065306597126334

N_CTX_SEQ = 16
CTX_LEN = 256
N_LAT_SEQ = 2
LAT_LEN = 2048
GRID_W = 64
N_TOK = N_CTX_SEQ * CTX_LEN + N_LAT_SEQ * LAT_LEN

TM = 256
N_CTX_TILES = N_CTX_SEQ * CTX_LEN // TM
LAT_TILES = LAT_LEN // TM
N_TILES = N_TOK // TM
FRONT_TM = 512
FRONT_GROUPS = FRONT_TM // TM
N_FRONT_CTX_TILES = N_CTX_SEQ * CTX_LEN // FRONT_TM
FRONT_LAT_TILES = LAT_LEN // FRONT_TM
N_FRONT_CONSTS = 9
FRONT_WBB = 7
LORA = 64
SCAN_TM = 512
SCAN_TILES = SCAN_TM // TM
N_SCAN_CTX_STEPS = N_CTX_SEQ * CTX_LEN // SCAN_TM
SCAN_LAT_STEPS = LAT_LEN // SCAN_TM
BACK_TM = 512
FFN_TM = 512
FFN_GROUPS = 2
N_FFN_TILES = N_TOK // FFN_TM
N_FFN_CTX_TILES = N_CTX_SEQ * CTX_LEN // FFN_TM
N_WCHUNK = 11
CHUNK = 64
N_CHUNK = TM // CHUNK
LANES = 128
N_PAIR = D_RWKV // LANES
MOD_ROWS = 8
MOD_BLOCK = 3
VMEM_LIMIT = 56 * 1024 * 1024

OP_A, OP_R, OP_B, OP_K, OP_BH, OP_KH, OP_V = range(7)
N_DIR_OPS = 6

NN = (((1,), (0,)), ((), ()))
NT = (((1,), (1,)), ((), ()))
TN = (((0,), (0,)), ((), ()))


def _dot(a, b, dims=NN):
    return lax.dot_general(a.astype(BF16), b.astype(BF16), dims, preferred_element_type=F32)


def _bdot(a, b, dims=NN):
    return lax.dot_general(a, b, dims, preferred_element_type=F32)


def _rms(x, g):
    ms = jnp.mean(x * x, axis=-1, keepdims=True)
    return x * lax.rsqrt(ms + EPS_RMS) * g


def _head_ones():
    i = lax.broadcasted_iota(jnp.int32, (D_RWKV, D_RWKV), 0) >> HEAD_SHIFT
    j = lax.broadcasted_iota(jnp.int32, (D_RWKV, D_RWKV), 1) >> HEAD_SHIFT
    return jnp.where(i == j, 1.0, 0.0).astype(BF16)


def _head_sum(x, ones):
    return jnp.dot(x.astype(BF16), ones, preferred_element_type=F32)


def _sigmoid(x):
    return 0.5 * jnp.tanh(0.5 * x) + 0.5


def _mod_row(t):
    return jnp.where(t < N_CTX_TILES, 0, 1 + (t - N_CTX_TILES) // LAT_TILES)


def _const_spec(shape):
    nd = len(shape)
    return pl.BlockSpec(shape, lambda *_: (0,) * nd, pipeline_mode=pl.Buffered(1))


def _params(n_axes=1):
    return pltpu.CompilerParams(dimension_semantics=("arbitrary",) * n_axes,
                                vmem_limit_bytes=VMEM_LIMIT)


def _mod_kernel(c_ref, w_ref, b_ref, o_ref):
    c = c_ref[...]
    s = c * jax.nn.sigmoid(c)
    o_ref[...] = _dot(s, w_ref[...]) + b_ref[...]


def _modulation(cvec, w_mod, b_mod):
    width = MOD_BLOCK * D_MODEL
    return pl.pallas_call(
        _mod_kernel,
        grid=(N_MOD // MOD_BLOCK,),
        in_specs=[pl.BlockSpec((MOD_ROWS, D_MODEL), lambda j: (0, 0)),
                  pl.BlockSpec((D_MODEL, width), lambda j: (0, j)),
                  pl.BlockSpec((1, width), lambda j: (0, j))],
        out_specs=pl.BlockSpec((MOD_ROWS, width), lambda j: (0, j)),
        out_shape=jax.ShapeDtypeStruct((MOD_ROWS, N_MOD * D_MODEL), F32),
        compiler_params=_params(),
        name="modulation",
    )(cvec, w_mod, b_mod)


def _ffn_body(x, mod_ref, g_ref, w13_ref, w2_ref, im, ig):
    shift = mod_ref[im:im + 1, :]
    scale = mod_ref[im + 1:im + 2, :]
    gate = mod_ref[im + 2:im + 3, :]
    rows = FFN_TM // FFN_GROUPS
    xs = [x[i * rows:(i + 1) * rows, :] for i in range(FFN_GROUPS)]
    hs = [(_rms(xi, g_ref[ig:ig + 1, :]) * (1.0 + scale) + shift).astype(BF16) for xi in xs]
    gus = [_dot(hi, w13_ref[...]) for hi in hs]
    acts = []
    for gu in gus:
        gt = gu[:, :D_FF]
        up = gu[:, D_FF:]
        acts.append((gt * jax.nn.sigmoid(gt) * up).astype(BF16))
    os_ = [_dot(ai, w2_ref[...]) for ai in acts]
    outs = [xi + HALF_STEP * gate * _rms(oi, g_ref[ig + 1:ig + 2, :]) for xi, oi in zip(xs, os_)]
    return jnp.concatenate(outs, axis=0)


def _cast_chunk(step, src_ref, dst_ref, axis):
    size = src_ref.shape[axis]

    @pl.when(step < N_WCHUNK)
    def _():
        start = pl.multiple_of(step * size, size)
        if axis == 0:
            dst_ref[pl.ds(start, size), :] = src_ref[...].astype(BF16)
        else:
            dst_ref[:, pl.ds(start, size)] = src_ref[...].astype(BF16)


def _chunk_spec(w, axis):
    block = list(w.shape)
    block[axis] = w.shape[axis] // N_WCHUNK
    clamp = lambda s: jnp.minimum(s, N_WCHUNK - 1)
    index = (lambda s: (clamp(s), 0)) if axis == 0 else (lambda s: (0, clamp(s)))
    return pl.BlockSpec(tuple(block), index)


def _tile_step(s):
    return jnp.maximum(s - N_WCHUNK, 0)


def _slab_spec(w, n_steps, step=lambda s: s):
    rows = w.shape[0] // n_steps
    return pl.BlockSpec((rows, w.shape[1]), lambda s: (step(s), 0))


def _ffn_first_kernel(xc_ref, xl_ref, mod_ref, g_ref, w13_ref, w2_ref, win_ref, o_ref, win_bf_ref,
                      w13_bf, w2_bf):
    s = pl.program_id(0)
    _cast_chunk(s, w13_ref, w13_bf, 1)
    _cast_chunk(s, w2_ref, w2_bf, 0)

    @pl.when(s >= N_WCHUNK)
    def _():
        x = jnp.where(s - N_WCHUNK < N_FFN_CTX_TILES, xc_ref[...], xl_ref[...])
        o_ref[...] = _ffn_body(x, mod_ref, g_ref, w13_bf, w2_bf, 0, 0)
        win_bf_ref[...] = win_ref[...].astype(BF16)


def _ffn_last_kernel(x_ref, mod_ref, g_ref, w13_ref, w2_ref, oc_ref, ol_ref):
    out = _ffn_body(x_ref[...], mod_ref, g_ref, w13_ref, w2_ref, 6, 4)
    t = pl.program_id(0)

    @pl.when(t < N_FFN_CTX_TILES)
    def _():
        oc_ref[...] = out

    @pl.when(t >= N_FFN_CTX_TILES)
    def _():
        ol_ref[...] = out


def _ctx_tile(t):
    return jnp.minimum(t, N_FFN_CTX_TILES - 1), 0


def _lat_tile(t):
    return jnp.maximum(t - N_FFN_CTX_TILES, 0), 0


def _ffn_mod(t):
    return _mod_row(t * (FFN_TM // TM)), 0, 0


def _ffn_first(x_ctx, x_lat, mod3, norm_g, w13, w2, w_in):
    return pl.pallas_call(
        _ffn_first_kernel,
        grid=(N_WCHUNK + N_FFN_TILES,),
        in_specs=[pl.BlockSpec((FFN_TM, D_MODEL), lambda s: _ctx_tile(_tile_step(s))),
                  pl.BlockSpec((FFN_TM, D_MODEL), lambda s: _lat_tile(_tile_step(s))),
                  pl.BlockSpec((None, N_MOD, D_MODEL), lambda s: _ffn_mod(_tile_step(s))),
                  _const_spec(norm_g.shape),
                  _chunk_spec(w13, 1),
                  _chunk_spec(w2, 0),
                  _slab_spec(w_in, N_FFN_TILES, step=_tile_step)],
        out_specs=[pl.BlockSpec((FFN_TM, D_MODEL), lambda s: (_tile_step(s), 0)),
                   _slab_spec(w_in, N_FFN_TILES, step=_tile_step)],
        out_shape=[jax.ShapeDtypeStruct((N_TOK, D_MODEL), F32),
                   jax.ShapeDtypeStruct(w_in.shape, BF16)],
        scratch_shapes=[pltpu.VMEM((D_MODEL, 2 * D_FF), BF16), pltpu.VMEM((D_FF, D_MODEL), BF16)],
        compiler_params=_params(),
        name="ffn1",
    )(x_ctx, x_lat, mod3, norm_g, w13, w2, w_in)


def _ffn_last(x, mod3, norm_g, w13_bf, w2_bf):
    n_ctx = N_CTX_SEQ * CTX_LEN
    return pl.pallas_call(
        _ffn_last_kernel,
        grid=(N_FFN_TILES,),
        in_specs=[pl.BlockSpec((FFN_TM, D_MODEL), lambda t: (t, 0)),
                  pl.BlockSpec((None, N_MOD, D_MODEL), _ffn_mod),
                  _const_spec(norm_g.shape),
                  _const_spec(w13_bf.shape),
                  _const_spec(w2_bf.shape)],
        out_specs=[pl.BlockSpec((FFN_TM, D_MODEL), _ctx_tile),
                   pl.BlockSpec((FFN_TM, D_MODEL), _lat_tile)],
        out_shape=[jax.ShapeDtypeStruct((n_ctx, D_MODEL), F32),
                   jax.ShapeDtypeStruct((N_TOK - n_ctx, D_MODEL), F32)],
        compiler_params=_params(),
        name="ffn2",
    )(x, mod3, norm_g, w13_bf, w2_bf)


def _lora_prep_kernel(mu_ref, w1_ref, o_ref):
    mu = mu_ref[...]
    w1 = w1_ref[...]
    n = w1.shape[1]
    o_ref[:, :n] = ((1.0 - mu) * w1).astype(BF16)
    o_ref[:, n:] = (mu * w1).astype(BF16)


def _lora_prep(mu_cols, w1_cat):
    n = w1_cat.shape[1]
    return pl.pallas_call(
        _lora_prep_kernel,
        out_shape=jax.ShapeDtypeStruct((D_MODEL, 2 * n), BF16),
        compiler_params=pltpu.CompilerParams(vmem_limit_bytes=VMEM_LIMIT),
        name="lora_prep",
    )(mu_cols, w1_cat)


def _front_group(gi, h, halo_prev, halo_next, is_lat, win_ref, waug_ref, refs, outs):
    w2aug_ref, b2aug_ref, kkw_ref, ka_ref, rk_ref, cw_ref, cbias_ref, wbb_ref, ones_ref = refs
    ops_o, v_o, ends_o, bv_o, sg_o, sa_o, mb_o = outs
    rows = slice(gi * TM, (gi + 1) * TM)
    row = lax.broadcasted_iota(jnp.int32, (TM, 1), 0)
    rin = row & (CHUNK - 1)
    ones = ones_ref[...]

    def proj(lo, hi):
        return _bdot(h, win_ref[:, lo:hi])

    pab = _bdot(h, waug_ref[...])
    rk = proj(0, 2 * D_RWKV)
    r = rk[:, :D_RWKV]
    k = rk[:, D_RWKV:]
    pb = pab[:, 2 * LANES:]
    sh_f = jnp.where(row == 0, halo_prev, pltpu.roll(pb[:, :LANES], 1, 0))
    sh_b = jnp.where(row == TM - 1, halo_next, pltpu.roll(pb[:, LANES:], TM - 1, 0))
    t_in = pab[:, :2 * LANES] + jnp.concatenate([sh_f, sh_b], axis=1)
    lane = lax.broadcasted_iota(jnp.int32, (1, 2 * LANES), 1)
    t_in = jnp.where((lane & HEAD) == 0, jnp.tanh(t_in), t_in)
    za = _dot(t_in, w2aug_ref[...]) + b2aug_ref[...]

    kk = k * kkw_ref[...]
    kkn = kk * lax.rsqrt(_head_sum(kk * kk, ones) + 1e-12)
    ka = ka_ref[...]

    vg = proj(2 * D_RWKV, 4 * D_RWKV)
    v = vg[:, :D_RWKV]
    v_o[rows, :] = v.astype(BF16)
    sg_o[rows, :] = _sigmoid(vg[:, D_RWKV:]).astype(BF16)

    def scan_operands(d):
        lw = -EXP_M05 * _sigmoid(za[:, 2 * d * D_RWKV:(2 * d + 1) * D_RWKV])
        a = _sigmoid(za[:, (2 * d + 1) * D_RWKV:(2 * d + 2) * D_RWKV])
        k_d = k * (1.0 + (a - 1.0) * ka)
        b = kkn * a
        cs = lw
        for s in (1, 2, 4, 8, 16, 32):
            if d == 0:
                cs = cs + jnp.where(rin >= s, pltpu.roll(cs, s, 0), 0.0)
            else:
                cs = cs + jnp.where(rin < CHUNK - s, pltpu.roll(cs, TM - s, 0), 0.0)
        end_row = CHUNK - 1 if d == 0 else 0
        ends = [cs[c * CHUNK + end_row:c * CHUNK + end_row + 1, :] for c in range(N_CHUNK)]
        for c in range(N_CHUNK):
            ends_o[d][gi, c:c + 1, :] = ends[c]
        cs_end = jnp.concatenate([jnp.broadcast_to(e, (CHUNK, D_RWKV)) for e in ends], axis=0)
        dec_inv = jnp.exp(-cs)
        dec_rest = jnp.exp(cs_end - cs)
        o_a, o_r, o_b, o_k, o_bh, o_kh = ops_o[d]
        o_a[rows, :] = (-kkn * jnp.exp(cs - lw)).astype(BF16)
        o_r[rows, :] = (r * jnp.exp(cs)).astype(BF16)
        o_b[rows, :] = (b * dec_inv).astype(BF16)
        o_k[rows, :] = (k_d * dec_inv).astype(BF16)
        o_bh[rows, :] = (b * dec_rest).astype(BF16)
        o_kh[rows, :] = (k_d * dec_rest).astype(BF16)
        return k_d

    conv_in = proj(4 * D_RWKV, 4 * D_RWKV + 3 * D_CONV)
    k_0 = scan_operands(0)
    gate_a = proj(4 * D_RWKV + 3 * D_CONV, 4 * D_RWKV + 3 * D_CONV + D_MODEL)
    sa_o[rows, :] = _sigmoid(gate_a).astype(BF16)
    k_1 = scan_operands(1)
    gate_b = proj(4 * D_RWKV + 3 * D_CONV + D_MODEL, D_IN)

    cgate = conv_in[:, :D_CONV]
    u = conv_in[:, D_CONV:2 * D_CONV] * conv_in[:, 2 * D_CONV:]
    col = row & (GRID_W - 1)
    zl = jnp.logical_or(row == 0, jnp.logical_and(is_lat, col == 0))
    zr = jnp.logical_or(row == TM - 1, jnp.logical_and(is_lat, col == GRID_W - 1))
    left = jnp.where(zl, 0.0, pltpu.roll(u, 1, 0))
    right = jnp.where(zr, 0.0, pltpu.roll(u, TM - 1, 0))
    conv = left * cw_ref[0:1, :] + u * cw_ref[1:2, :] + right * cw_ref[2:3, :] + cbias_ref[...]
    y_b = _dot(cgate * conv, wbb_ref[...])
    mb_o[rows, :] = (_sigmoid(gate_b) * y_b).astype(BF16)

    bv_o[rows, :] = (_head_sum(r * (k_0 + k_1) * rk_ref[...], ones) * v).astype(BF16)


def _front_kernel(x_ref, xp_ref, xn_ref, mod_ref, g_ref, win_ref, waug_ref, *rest):
    refs = list(rest[:N_FRONT_CONSTS])
    outs = rest[N_FRONT_CONSTS:-1]
    wbb_bf = rest[-1]
    ops_o = (outs[0:N_DIR_OPS], outs[N_DIR_OPS:2 * N_DIR_OPS])
    v_o, ends0_o, ends1_o, bv_o, sg_o, sa_o, mb_o = outs[2 * N_DIR_OPS:]
    outs = (ops_o, v_o, (ends0_o, ends1_o), bv_o, sg_o, sa_o, mb_o)

    t = pl.program_id(0)

    @pl.when(t == 0)
    def _():
        wbb_bf[...] = refs[FRONT_WBB][...].astype(BF16)

    refs[FRONT_WBB] = wbb_bf

    is_lat = t >= N_FRONT_CTX_TILES
    i_in = (t - N_FRONT_CTX_TILES) % FRONT_LAT_TILES
    lat_first = jnp.logical_and(is_lat, i_in == 0)
    lat_last = jnp.logical_and(is_lat, i_in == FRONT_LAT_TILES - 1)
    shift = mod_ref[3:4, :]
    scale = mod_ref[4:5, :]
    g2 = g_ref[2:3, :]

    def pre(x):
        return (_rms(x, g2) * (1.0 + scale) + shift).astype(BF16)

    x = x_ref[...]
    edge = jnp.concatenate([xp_ref[...], x[TM - 8:TM + 8, :], xn_ref[...]], axis=0)
    edge_b = _bdot(pre(edge), waug_ref[:, 2 * LANES:])
    zero = jnp.zeros((1, LANES), F32)
    halo_prev = (jnp.where(jnp.logical_and(is_lat, jnp.logical_not(lat_first)), edge_b[7:8, :LANES], zero),
                 jnp.where(is_lat, edge_b[15:16, :LANES], zero))
    halo_next = (jnp.where(is_lat, edge_b[16:17, LANES:], zero),
                 jnp.where(jnp.logical_and(is_lat, jnp.logical_not(lat_last)), edge_b[24:25, LANES:], zero))

    for gi in range(FRONT_GROUPS):
        h = pre(x[gi * TM:(gi + 1) * TM, :])
        _front_group(gi, h, halo_prev[gi], halo_next[gi], is_lat, win_ref, waug_ref, refs, outs)


def _front(x, mod3, norm_g, w_in_bf, w_aug, consts):
    tok = lambda t: (t, 0)
    rows8 = FRONT_TM // 8
    last8 = N_TOK // 8 - 1
    assert len(consts) == N_FRONT_CONSTS
    out_rb = jax.ShapeDtypeStruct((N_TOK, D_RWKV), BF16)
    out_db = jax.ShapeDtypeStruct((N_TOK, D_MODEL), BF16)
    out_e = jax.ShapeDtypeStruct((N_TILES, N_CHUNK, D_RWKV), F32)
    spec_r = pl.BlockSpec((FRONT_TM, D_RWKV), tok)
    spec_d = pl.BlockSpec((FRONT_TM, D_MODEL), tok)
    spec_e = pl.BlockSpec((FRONT_GROUPS, N_CHUNK, D_RWKV), lambda t: (t, 0, 0))
    n_b = 2 * N_DIR_OPS + 1
    return pl.pallas_call(
        _front_kernel,
        grid=(N_TOK // FRONT_TM,),
        in_specs=[pl.BlockSpec((FRONT_TM, D_MODEL), tok),
                  pl.BlockSpec((8, D_MODEL), lambda t: (jnp.maximum(t * rows8 - 1, 0), 0)),
                  pl.BlockSpec((8, D_MODEL), lambda t: (jnp.minimum((t + 1) * rows8, last8), 0)),
                  pl.BlockSpec((None, N_MOD, D_MODEL), lambda t: (_mod_row(t * FRONT_GROUPS), 0, 0))]
                 + [_const_spec(c.shape) for c in [norm_g, w_in_bf, w_aug] + list(consts)],
        out_specs=[spec_r] * n_b + [spec_e] * 2 + [spec_r] * 2 + [spec_d] * 2,
        out_shape=[out_rb] * n_b + [out_e] * 2 + [out_rb] * 2 + [out_db] * 2,
        scratch_shapes=[pltpu.VMEM((D_CONV, D_MODEL), BF16)],
        compiler_params=_params(),
        name="mixer_front",
    )(x, x, x, mod3, norm_g, w_in_bf, w_aug, *consts)


def _stack(x):
    lane_lo = lax.broadcasted_iota(jnp.int32, x.shape, 1) < HEAD
    z = jnp.zeros_like(x)
    return jnp.concatenate([jnp.where(lane_lo, x, z), jnp.where(lane_lo, z, x)], axis=0)


def _scan_kernel(*refs):
    ops = (refs[0:N_DIR_OPS + 1], refs[N_DIR_OPS + 1:2 * N_DIR_OPS + 2])
    (ends0_ref, ends1_ref, s0_ref, w13_ref, w2_ref,
     yf_ref, yb_ref, sout_ref, w13_bf_ref, w2_bf_ref, st_ref) = refs[2 * N_DIR_OPS + 2:]
    ends_ref = (ends0_ref, ends1_ref)
    w13_bf_ref[...] = w13_ref[...].astype(BF16)
    w2_bf_ref[...] = w2_ref[...].astype(BF16)
    step_id = pl.program_id(0)
    is_lat = step_id >= N_SCAN_CTX_STEPS
    lat_first = (step_id - N_SCAN_CTX_STEPS) % SCAN_LAT_STEPS == 0

    @pl.when(jnp.logical_and(is_lat, lat_first))
    def _():
        zero = jnp.zeros((HEAD, HEAD), F32)
        for d in (0, 1):
            for p in range(N_PAIR):
                top = jnp.concatenate([s0_ref[0, d, 2 * p], zero], axis=1)
                bottom = jnp.concatenate([zero, s0_ref[0, d, 2 * p + 1]], axis=1)
                st_ref[d, p] = jnp.concatenate([top, bottom], axis=0)

    ti = lax.broadcasted_iota(jnp.int32, (CHUNK, LANES), 0)
    sj = lax.broadcasted_iota(jnp.int32, (CHUNK, LANES), 1) & (CHUNK - 1)
    m_strict = (sj < ti, sj > ti)
    m_incl = (sj <= ti, sj >= ti)
    eye_cat = jnp.where(sj == ti, 1.0, 0.0)
    bi = lax.broadcasted_iota(jnp.int32, (LANES, LANES), 0)
    bj = lax.broadcasted_iota(jnp.int32, (LANES, LANES), 1)
    blk = (bi >> HEAD_SHIFT) == (bj >> HEAD_SHIFT)
    eye_bd = bi == bj
    zero_bd = jnp.zeros((LANES, LANES), BF16)

    def op(i, u):
        d, p, c = u
        return ops[d][i][c * CHUNK:(c + 1) * CHUNK, p * LANES:(p + 1) * LANES]

    g_mat, w_mat, p_t, q_t = {}, {}, {}, {}

    def local_stages(units):
        low, nak, mrbk = {}, {}, {}
        for u in units:
            lhs = jnp.concatenate([op(OP_A, u), op(OP_R, u)], axis=0)
            rhs = jnp.concatenate([_stack(op(OP_B, u)), _stack(op(OP_K, u))], axis=0)
            gram = _bdot(lhs, rhs, NT)
            d = u[0]
            low[u] = jnp.where(m_strict[d], gram[:CHUNK, :LANES], 0.0)
            nak[u] = jnp.where(m_strict[d], gram[:CHUNK, LANES:], 0.0).astype(BF16)
            mrbk[u] = jnp.concatenate([jnp.where(m_incl[d], gram[CHUNK:, :LANES], 0.0),
                                       jnp.where(m_incl[d], gram[CHUNK:, LANES:], 0.0)], axis=1).astype(BF16)

        inv = {u: eye_cat + low[u] for u in units}
        pwb = {u: low[u].astype(BF16) for u in units}
        for u in units:
            pwb[u] = _bdot(pwb[u], _stack(pwb[u])).astype(BF16)
        for _ in range(4):
            for u in units:
                both = _bdot(pwb[u], jnp.concatenate([_stack(pwb[u]), _stack(inv[u].astype(BF16))], axis=1))
                pwb[u] = both[:, :LANES].astype(BF16)
                inv[u] = inv[u] + both[:, LANES:]
        for u in units:
            inv[u] = inv[u] + _bdot(pwb[u], _stack(inv[u].astype(BF16)))

        sv = {u: _stack(op(OP_V, u)) for u in units}
        nv = {}
        for u in units:
            nv[u] = _bdot(nak[u], sv[u]).astype(BF16)
        x1, x2 = {}, {}
        for u in units:
            x12 = _bdot(inv[u].astype(BF16), jnp.concatenate([_stack(op(OP_A, u)), _stack(nv[u])], axis=1))
            x1[u] = x12[:, :LANES].astype(BF16)
            x2[u] = x12[:, LANES:].astype(BF16)
        for u in units:
            rhs = jnp.concatenate([jnp.concatenate([_stack(x1[u]), _stack(x2[u])], axis=1),
                                   jnp.concatenate([zero_bd, sv[u]], axis=1)], axis=0)
            gw = _bdot(mrbk[u], rhs)
            g_mat[u] = (op(OP_R, u).astype(F32) + gw[:, :LANES]).astype(BF16)
            w_mat[u] = gw[:, LANES:]
        for u in units:
            d, p, c = u
            ct = c % N_CHUNK
            gam = jnp.exp(ends_ref[d][c // N_CHUNK, ct:ct + 1, p * LANES:(p + 1) * LANES])
            p_t[u] = (jnp.where(blk, _bdot(x1[u], op(OP_BH, u), TN), 0.0)
                      + jnp.where(eye_bd, gam, 0.0)).astype(BF16)
            q_t[u] = jnp.where(blk, _bdot(jnp.concatenate([x2[u], op(OP_V, u)], axis=0),
                                          jnp.concatenate([op(OP_BH, u), op(OP_KH, u)], axis=0), TN), 0.0)

    n_c = SCAN_TILES * N_CHUNK
    state = {(d, p): jnp.where(is_lat, st_ref[d, p], 0.0) for d in (0, 1) for p in range(N_PAIR)}
    finals = {}
    y_refs = (yf_ref, yb_ref)

    def chunk_of(d, step):
        return step if d == 0 else n_c - 1 - step

    def chain(step, d, p):
        c = chunk_of(d, step)
        u = (d, p, c)
        sb = state[d, p].astype(BF16)
        y_refs[d][c * CHUNK:(c + 1) * CHUNK, p * LANES:(p + 1) * LANES] = _bdot(g_mat[u], sb, NT) + w_mat[u]
        state[d, p] = _bdot(sb, p_t[u]) + q_t[u]
        if (step + 1) % N_CHUNK == 0:
            finals[d, c // N_CHUNK, p] = state[d, p]
            if step + 1 < n_c:
                state[d, p] = jnp.where(is_lat, state[d, p], 0.0)

    local_stages([(d, p, c) for d in (0, 1) for p in range(N_PAIR) for c in range(n_c)])
    for step in range(n_c):
        for d in (0, 1):
            for p in range(N_PAIR):
                chain(step, d, p)
    for (d, p), s in state.items():
        st_ref[d, p] = s

    @pl.when(jnp.logical_not(is_lat))
    def _():
        for (d, tile, p), s in finals.items():
            sout_ref[tile, 0, d, 2 * p] = s[:HEAD, :HEAD]
            sout_ref[tile, 0, d, 2 * p + 1] = s[HEAD:, HEAD:]


def _scan(ops_f, ops_b, v, ends_f, ends_b, s0, w13, w2):
    def mirror(s):
        u = s - N_SCAN_CTX_STEPS
        return jnp.where(s < N_SCAN_CTX_STEPS, s,
                         N_SCAN_CTX_STEPS + (u // SCAN_LAT_STEPS) * SCAN_LAT_STEPS
                         + (SCAN_LAT_STEPS - 1 - u % SCAN_LAT_STEPS))

    def lat_seq(s):
        return jnp.maximum(s - N_SCAN_CTX_STEPS, 0) // SCAN_LAT_STEPS

    fwd = pl.BlockSpec((SCAN_TM, D_RWKV), lambda s: (s, 0))
    bwd = pl.BlockSpec((SCAN_TM, D_RWKV), lambda s: (mirror(s), 0))
    e_block = (SCAN_TILES, N_CHUNK, D_RWKV)
    st_block = (None, 1, 2, 2 * N_PAIR, HEAD, HEAD)
    out_y = jax.ShapeDtypeStruct((N_TOK, D_RWKV), F32)
    n_steps = N_TOK // SCAN_TM
    return pl.pallas_call(
        _scan_kernel,
        grid=(n_steps,),
        in_specs=[fwd] * (N_DIR_OPS + 1) + [bwd] * (N_DIR_OPS + 1)
                 + [pl.BlockSpec(e_block, lambda s: (s, 0, 0)),
                    pl.BlockSpec(e_block, lambda s: (mirror(s), 0, 0)),
                    pl.BlockSpec(st_block, lambda s: (lat_seq(s), 0, 0, 0, 0, 0)),
                    _slab_spec(w13, n_steps),
                    _slab_spec(w2, n_steps)],
        out_specs=[fwd, bwd,
                   pl.BlockSpec((SCAN_TILES, 1, 2, 2 * N_PAIR, HEAD, HEAD),
                                lambda s: (jnp.minimum(s, N_SCAN_CTX_STEPS - 1), 0, 0, 0, 0, 0)),
                   _slab_spec(w13, n_steps),
                   _slab_spec(w2, n_steps)],
        out_shape=[out_y, out_y,
                   jax.ShapeDtypeStruct((N_CTX_SEQ, 1, 2, 2 * N_PAIR, HEAD, HEAD), F32),
                   jax.ShapeDtypeStruct(w13.shape, BF16),
                   jax.ShapeDtypeStruct(w2.shape, BF16)],
        scratch_shapes=[pltpu.VMEM((2, N_PAIR, LANES, LANES), F32)],
        compiler_params=_params(),
        name="rwkv7_scan",
    )(*ops_f, v, *ops_b, v, ends_f, ends_b, s0, w13, w2)


def _back_kernel(x_ref, yf_ref, yb_ref, bv_ref, sg_ref, sa_ref, mb_ref, mod_ref, g_ref,
                 gng_ref, gnb_ref, wba_f32_ref, wout_f32_ref, ones_ref, o_ref, wba_ref, wout_ref):
    @pl.when(pl.program_id(0) == 0)
    def _():
        wba_ref[...] = wba_f32_ref[...].astype(BF16)
        wout_ref[...] = wout_f32_ref[...].astype(BF16)

    ones = ones_ref[...]
    groups = [slice(i * TM, (i + 1) * TM) for i in range(BACK_TM // TM)]
    ys = [yf_ref[r, :] + yb_ref[r, :] for r in groups]
    ycs = [y - _head_sum(y, ones) * (1.0 / HEAD) for y in ys]
    vs = [_head_sum(yc * yc, ones) * (1.0 / HEAD) for yc in ycs]
    yns = [yc * lax.rsqrt(v + EPS_GN) * gng_ref[...] + gnb_ref[...] for yc, v in zip(ycs, vs)]
    yas = [_dot((yn + bv_ref[r, :]) * sg_ref[r, :], wba_ref[...]) for r, yn in zip(groups, yns)]
    outs = [_dot(sa_ref[r, :] * ya + mb_ref[r, :], wout_ref[...]) for r, ya in zip(groups, yas)]
    for r, out in zip(groups, outs):
        o_ref[r, :] = x_ref[r, :] + mod_ref[5:6, :] * _rms(out, g_ref[3:4, :])


def _back(x, yf, yb, bv, sg, sa, mb, mod3, norm_g, gng, gnb, wba, wout, ones):
    tok = lambda t: (t, 0)
    spec_r = pl.BlockSpec((BACK_TM, D_RWKV), tok)
    spec_d = pl.BlockSpec((BACK_TM, D_MODEL), tok)
    consts = [norm_g, gng, gnb, wba, wout, ones]
    return pl.pallas_call(
        _back_kernel,
        grid=(N_TOK // BACK_TM,),
        in_specs=[spec_d, spec_r, spec_r, spec_r, spec_r, spec_d, spec_d,
                  pl.BlockSpec((None, N_MOD, D_MODEL), lambda t: (_mod_row(t * (BACK_TM // TM)), 0, 0))]
                 + [_const_spec(c.shape) for c in consts],
        out_specs=spec_d,
        out_shape=jax.ShapeDtypeStruct((N_TOK, D_MODEL), F32),
        scratch_shapes=[pltpu.VMEM(wba.shape, BF16), pltpu.VMEM(wout.shape, BF16)],
        compiler_params=_params(),
        name="mixer_back",
    )(x, yf, yb, bv, sg, sa, mb, mod3, *consts)


def kernel(x_prompt, x_sample, c, state_rwkv, c_ctx, w_mod, b_mod, norm_g, ffn1_w13, ffn1_w2,
           ffn2_w13, ffn2_w2, w_in, mu_shift, decay_w0, decay_w1, decay_w2, iclr_a0, iclr_a1,
           iclr_a2, k_k, k_a, r_k, gn_gain, gn_bias, conv_w, conv_b, w_branch_a, w_branch_b, w_out):
    assert x_prompt.shape == (N_CTX_SEQ, CTX_LEN, D_MODEL) and x_sample.shape == (N_LAT_SEQ, LAT_LEN, D_MODEL)
    assert w_mod.shape[0] == 1, "single trunk layer"

    cvec = jnp.concatenate([c_ctx[None, :], c, jnp.zeros((MOD_ROWS - 1 - N_LAT_SEQ, D_MODEL), F32)], axis=0)
    mod3 = _modulation(cvec, w_mod[0], b_mod).reshape(MOD_ROWS, N_MOD, D_MODEL)
    g = norm_g[0]

    x, w_in_bf = _ffn_first(x_prompt.reshape(-1, D_MODEL), x_sample.reshape(-1, D_MODEL), mod3, g,
                            ffn1_w13[0], ffn1_w2[0], w_in[0])

    row = lambda p: p.reshape(1, -1)
    mu4 = mu_shift[0].reshape(4, D_MODEL)
    w1_cat = jnp.concatenate([decay_w1[0, 0], iclr_a1[0, 0], decay_w1[0, 1], iclr_a1[0, 1]], axis=1)
    w_aug = _lora_prep(jnp.repeat(mu4.T, LORA, axis=1), w1_cat)
    w2_blocks = [decay_w2[0, 0], iclr_a2[0, 0], decay_w2[0, 1], iclr_a2[0, 1]]
    w2_zero = jnp.zeros((LORA, D_RWKV), F32)
    w2_aug = jnp.concatenate(
        [jnp.concatenate([blk if j == i else w2_zero for j in range(4)], axis=1)
         for i, blk in enumerate(w2_blocks)], axis=0).astype(BF16)
    b2_aug = jnp.concatenate([decay_w0[0, 0], iclr_a0[0, 0], decay_w0[0, 1], iclr_a0[0, 1]]).reshape(1, -1)
    ones = _head_ones()
    front = _front(x, mod3, g, w_in_bf, w_aug,
                   [w2_aug, b2_aug, row(k_k[0]), row(k_a[0]), row(r_k[0]), conv_w[0], row(conv_b[0]),
                    w_branch_b[0], ones])
    ops_f, ops_b = front[0:N_DIR_OPS], front[N_DIR_OPS:2 * N_DIR_OPS]
    v, ends_f, ends_b, bv, sg, sa, mb = front[2 * N_DIR_OPS:]

    yf, yb, s_fin, w13_bf, w2_bf = _scan(ops_f, ops_b, v, ends_f, ends_b, state_rwkv,
                                         ffn2_w13[0], ffn2_w2[0])

    x = _back(x, yf, yb, bv, sg, sa, mb, mod3, g, row(gn_gain[0]), row(gn_bias[0]),
              w_branch_a[0], w_out[0], ones)
    y_ctx, y_lat = _ffn_last(x, mod3, g, w13_bf, w2_bf)

    y_prompt = y_ctx.reshape(N_CTX_SEQ, CTX_LEN, D_MODEL)
    y_sample = y_lat.reshape(N_LAT_SEQ, LAT_LEN, D_MODEL)
    return y_prompt, y_sample, s_fin
```

```python
import jax
import jax.numpy as jnp
from jax import lax
from jax.experimental import pallas as pl
from jax.experimental.pallas import tpu as pltpu

F32 = jnp.float32
BF16 = jnp.bfloat16

D_MODEL = 1024
D_FF = 2816
D_RWKV = 512
D_CONV = 512
HEAD = 64
HEAD_SHIFT = 6
D_IN = 4 * D_RWKV + 3 * D_CONV + 2 * D_MODEL
N_MOD = 9
EPS_RMS = 1e-6
EPS_GN = 64e-5
HALF_STEP = 0.5
EXP_M05 = 0.6065306597126334

N_CTX_SEQ = 16
CTX_LEN = 256
N_LAT_SEQ = 2
LAT_LEN = 2048
GRID_W = 64
N_TOK = N_CTX_SEQ * CTX_LEN + N_LAT_SEQ * LAT_LEN

TM = 256
N_CTX_TILES = N_CTX_SEQ * CTX_LEN // TM
LAT_TILES = LAT_LEN // TM
N_TILES = N_TOK // TM
FRONT_TM = 512
FRONT_GROUPS = FRONT_TM // TM
N_FRONT_CTX_TILES = N_CTX_SEQ * CTX_LEN // FRONT_TM
FRONT_LAT_TILES = LAT_LEN // FRONT_TM
N_FRONT_CONSTS = 9
FRONT_WBB = 7
LORA = 64
SCAN_TM = 512
SCAN_TILES = SCAN_TM // TM
N_SCAN_CTX_STEPS = N_CTX_SEQ * CTX_LEN // SCAN_TM
SCAN_LAT_STEPS = LAT_LEN // SCAN_TM
BACK_TM = 512
FFN_TM = 512
FFN_GROUPS = 2
N_FFN_TILES = N_TOK // FFN_TM
N_FFN_CTX_TILES = N_CTX_SEQ * CTX_LEN // FFN_TM
N_WCHUNK = 11
CHUNK = 64
N_CHUNK = TM // CHUNK
LANES = 128
N_PAIR = D_RWKV // LANES
MOD_ROWS = 8
MOD_HEAD = 3
MOD_TAIL = N_MOD - MOD_HEAD
MOD_SIDE = 512
N_MOD_SIDE = MOD_TAIL * D_MODEL // MOD_SIDE
MOD_FRONT, MOD_BACK, MOD_FFN2 = 0, 2, 3
VMEM_LIMIT = 56 * 1024 * 1024

OP_A, OP_R, OP_B, OP_K, OP_BH, OP_KH, OP_V = range(7)
N_DIR_OPS = 6

NN = (((1,), (0,)), ((), ()))
NT = (((1,), (1,)), ((), ()))
TN = (((0,), (0,)), ((), ()))


def _dot(a, b, dims=NN):
    return lax.dot_general(a.astype(BF16), b.astype(BF16), dims, preferred_element_type=F32)


def _bdot(a, b, dims=NN):
    return lax.dot_general(a, b, dims, preferred_element_type=F32)


def _rms(x, g):
    ms = jnp.mean(x * x, axis=-1, keepdims=True)
    return x * lax.rsqrt(ms + EPS_RMS) * g


def _head_ones():
    i = lax.broadcasted_iota(jnp.int32, (D_RWKV, D_RWKV), 0) >> HEAD_SHIFT
    j = lax.broadcasted_iota(jnp.int32, (D_RWKV, D_RWKV), 1) >> HEAD_SHIFT
    return jnp.where(i == j, 1.0, 0.0).astype(BF16)


def _head_sum(x, ones):
    return jnp.dot(x.astype(BF16), ones, preferred_element_type=F32)


def _sigmoid(x):
    return 0.5 * jnp.tanh(0.5 * x) + 0.5


def _mod_row(t):
    return jnp.where(t < N_CTX_TILES, 0, 1 + (t - N_CTX_TILES) // LAT_TILES)


def _const_spec(shape):
    nd = len(shape)
    return pl.BlockSpec(shape, lambda *_: (0,) * nd, pipeline_mode=pl.Buffered(1))


def _params(n_axes=1):
    return pltpu.CompilerParams(dimension_semantics=("arbitrary",) * n_axes,
                                vmem_limit_bytes=VMEM_LIMIT)


def _mod_kernel(c_ref, w_ref, b_ref, o_ref):
    c = c_ref[...]
    s = c * jax.nn.sigmoid(c)
    o_ref[...] = _dot(s, w_ref[...]) + b_ref[...]


def _modulation(cvec, w_mod, b_mod):
    width = MOD_HEAD * D_MODEL
    return pl.pallas_call(
        _mod_kernel,
        grid=(1,),
        in_specs=[pl.BlockSpec((MOD_ROWS, D_MODEL), lambda j: (0, 0)),
                  pl.BlockSpec((D_MODEL, width), lambda j: (0, 0)),
                  pl.BlockSpec((1, width), lambda j: (0, 0))],
        out_specs=pl.BlockSpec((MOD_ROWS, width), lambda j: (0, 0)),
        out_shape=jax.ShapeDtypeStruct((MOD_ROWS, width), F32),
        compiler_params=_params(),
        name="modulation",
    )(cvec, w_mod, b_mod)


def _ffn_body(x, mod_ref, g_ref, w13_ref, w2_ref, im, ig):
    shift = mod_ref[im:im + 1, :]
    scale = mod_ref[im + 1:im + 2, :]
    gate = mod_ref[im + 2:im + 3, :]
    rows = FFN_TM // FFN_GROUPS
    xs = [x[i * rows:(i + 1) * rows, :] for i in range(FFN_GROUPS)]
    hs = [(_rms(xi, g_ref[ig:ig + 1, :]) * (1.0 + scale) + shift).astype(BF16) for xi in xs]
    gus = [_dot(hi, w13_ref[...]) for hi in hs]
    acts = []
    for gu in gus:
        gt = gu[:, :D_FF]
        up = gu[:, D_FF:]
        acts.append((gt * jax.nn.sigmoid(gt) * up).astype(BF16))
    os_ = [_dot(ai, w2_ref[...]) for ai in acts]
    outs = [xi + HALF_STEP * gate * _rms(oi, g_ref[ig + 1:ig + 2, :]) for xi, oi in zip(xs, os_)]
    return jnp.concatenate(outs, axis=0)


def _cast_chunk(step, src_ref, dst_ref, axis):
    size = src_ref.shape[axis]

    @pl.when(step < N_WCHUNK)
    def _():
        start = pl.multiple_of(step * size, size)
        if axis == 0:
            dst_ref[pl.ds(start, size), :] = src_ref[...].astype(BF16)
        else:
            dst_ref[:, pl.ds(start, size)] = src_ref[...].astype(BF16)


def _chunk_spec(w, axis):
    block = list(w.shape)
    block[axis] = w.shape[axis] // N_WCHUNK
    clamp = lambda s: jnp.minimum(s, N_WCHUNK - 1)
    index = (lambda s: (clamp(s), 0)) if axis == 0 else (lambda s: (0, clamp(s)))
    return pl.BlockSpec(tuple(block), index)


def _tile_step(s):
    return jnp.maximum(s - N_WCHUNK, 0)


def _slab_spec(w, n_steps, step=lambda s: s):
    rows = w.shape[0] // n_steps
    return pl.BlockSpec((rows, w.shape[1]), lambda s: (step(s), 0))


def _ffn_first_kernel(xc_ref, xl_ref, mod_ref, g_ref, w13_ref, w2_ref, win_ref, c_ref, wmod_ref, bmod_ref,
                      o_ref, win_bf_ref, mod_tail_ref, w13_bf, w2_bf):
    s = pl.program_id(0)
    _cast_chunk(s, w13_ref, w13_bf, 1)
    _cast_chunk(s, w2_ref, w2_bf, 0)

    @pl.when(s >= N_WCHUNK)
    def _():
        x = jnp.where(s - N_WCHUNK < N_FFN_CTX_TILES, xc_ref[...], xl_ref[...])
        o_ref[...] = _ffn_body(x, mod_ref, g_ref, w13_bf, w2_bf, 0, 0)
        win_bf_ref[...] = win_ref[...].astype(BF16)

        @pl.when(s - N_WCHUNK < N_MOD_SIDE)
        def _():
            _mod_kernel(c_ref, wmod_ref, bmod_ref, mod_tail_ref)


def _ffn_last_kernel(x_ref, mod_ref, g_ref, w13_ref, w2_ref, oc_ref, ol_ref):
    out = _ffn_body(x_ref[...], mod_ref, g_ref, w13_ref, w2_ref, MOD_FFN2, 4)
    t = pl.program_id(0)

    @pl.when(t < N_FFN_CTX_TILES)
    def _():
        oc_ref[...] = out

    @pl.when(t >= N_FFN_CTX_TILES)
    def _():
        ol_ref[...] = out


def _ctx_tile(t):
    return jnp.minimum(t, N_FFN_CTX_TILES - 1), 0


def _lat_tile(t):
    return jnp.maximum(t - N_FFN_CTX_TILES, 0), 0


def _ffn_mod(t):
    return _mod_row(t * (FFN_TM // TM)), 0, 0


def _ffn_first(x_ctx, x_lat, mod_head, norm_g, w13, w2, w_in, cvec, w_mod, b_mod):
    side = lambda s: jnp.minimum(_tile_step(s), N_MOD_SIDE - 1)
    tail_col = MOD_HEAD * D_MODEL // MOD_SIDE
    return pl.pallas_call(
        _ffn_first_kernel,
        grid=(N_WCHUNK + N_FFN_TILES,),
        in_specs=[pl.BlockSpec((FFN_TM, D_MODEL), lambda s: _ctx_tile(_tile_step(s))),
                  pl.BlockSpec((FFN_TM, D_MODEL), lambda s: _lat_tile(_tile_step(s))),
                  pl.BlockSpec((None, MOD_HEAD, D_MODEL), lambda s: _ffn_mod(_tile_step(s))),
                  _const_spec(norm_g.shape),
                  _chunk_spec(w13, 1),
                  _chunk_spec(w2, 0),
                  _slab_spec(w_in, N_FFN_TILES, step=_tile_step),
                  _const_spec(cvec.shape),
                  pl.BlockSpec((D_MODEL, MOD_SIDE), lambda s: (0, tail_col + side(s))),
                  pl.BlockSpec((1, MOD_SIDE), lambda s: (0, tail_col + side(s)))],
        out_specs=[pl.BlockSpec((FFN_TM, D_MODEL), lambda s: (_tile_step(s), 0)),
                   _slab_spec(w_in, N_FFN_TILES, step=_tile_step),
                   pl.BlockSpec((MOD_ROWS, MOD_SIDE), lambda s: (0, side(s)))],
        out_shape=[jax.ShapeDtypeStruct((N_TOK, D_MODEL), F32),
                   jax.ShapeDtypeStruct(w_in.shape, BF16),
                   jax.ShapeDtypeStruct((MOD_ROWS, MOD_TAIL * D_MODEL), F32)],
        scratch_shapes=[pltpu.VMEM((D_MODEL, 2 * D_FF), BF16), pltpu.VMEM((D_FF, D_MODEL), BF16)],
        compiler_params=_params(),
        name="ffn1",
    )(x_ctx, x_lat, mod_head, norm_g, w13, w2, w_in, cvec, w_mod, b_mod)


def _ffn_last(x, mod3, norm_g, w13_bf, w2_bf):
    n_ctx = N_CTX_SEQ * CTX_LEN
    return pl.pallas_call(
        _ffn_last_kernel,
        grid=(N_FFN_TILES,),
        in_specs=[pl.BlockSpec((FFN_TM, D_MODEL), lambda t: (t, 0)),
                  pl.BlockSpec((None, MOD_TAIL, D_MODEL), _ffn_mod),
                  _const_spec(norm_g.shape),
                  _const_spec(w13_bf.shape),
                  _const_spec(w2_bf.shape)],
        out_specs=[pl.BlockSpec((FFN_TM, D_MODEL), _ctx_tile),
                   pl.BlockSpec((FFN_TM, D_MODEL), _lat_tile)],
        out_shape=[jax.ShapeDtypeStruct((n_ctx, D_MODEL), F32),
                   jax.ShapeDtypeStruct((N_TOK - n_ctx, D_MODEL), F32)],
        compiler_params=_params(),
        name="ffn2",
    )(x, mod3, norm_g, w13_bf, w2_bf)


def _lora_prep_kernel(mu_ref, w1_ref, o_ref):
    mu = mu_ref[...]
    w1 = w1_ref[...]
    n = w1.shape[1]
    o_ref[:, :n] = ((1.0 - mu) * w1).astype(BF16)
    o_ref[:, n:] = (mu * w1).astype(BF16)


def _lora_prep(mu_cols, w1_cat):
    n = w1_cat.shape[1]
    return pl.pallas_call(
        _lora_prep_kernel,
        out_shape=jax.ShapeDtypeStruct((D_MODEL, 2 * n), BF16),
        compiler_params=pltpu.CompilerParams(vmem_limit_bytes=VMEM_LIMIT),
        name="lora_prep",
    )(mu_cols, w1_cat)


def _front_group(gi, h, halo_prev, halo_next, is_lat, win_ref, waug_ref, refs, outs):
    w2aug_ref, b2aug_ref, kkw_ref, ka_ref, rk_ref, cw_ref, cbias_ref, wbb_ref, ones_ref = refs
    ops_o, v_o, ends_o, bv_o, sg_o, sa_o, mb_o = outs
    rows = slice(gi * TM, (gi + 1) * TM)
    row = lax.broadcasted_iota(jnp.int32, (TM, 1), 0)
    rin = row & (CHUNK - 1)
    ones = ones_ref[...]

    def proj(lo, hi):
        return _bdot(h, win_ref[:, lo:hi])

    pab = _bdot(h, waug_ref[...])
    rk = proj(0, 2 * D_RWKV)
    r = rk[:, :D_RWKV]
    k = rk[:, D_RWKV:]
    pb = pab[:, 2 * LANES:]
    sh_f = jnp.where(row == 0, halo_prev, pltpu.roll(pb[:, :LANES], 1, 0))
    sh_b = jnp.where(row == TM - 1, halo_next, pltpu.roll(pb[:, LANES:], TM - 1, 0))
    t_in = pab[:, :2 * LANES] + jnp.concatenate([sh_f, sh_b], axis=1)
    lane = lax.broadcasted_iota(jnp.int32, (1, 2 * LANES), 1)
    t_in = jnp.where((lane & HEAD) == 0, jnp.tanh(t_in), t_in)
    za = _dot(t_in, w2aug_ref[...]) + b2aug_ref[...]

    kk = k * kkw_ref[...]
    kkn = kk * lax.rsqrt(_head_sum(kk * kk, ones) + 1e-12)
    ka = ka_ref[...]

    vg = proj(2 * D_RWKV, 4 * D_RWKV)
    v = vg[:, :D_RWKV]
    v_o[rows, :] = v.astype(BF16)
    sg_o[rows, :] = _sigmoid(vg[:, D_RWKV:]).astype(BF16)

    def scan_operands(d):
        lw = -EXP_M05 * _sigmoid(za[:, 2 * d * D_RWKV:(2 * d + 1) * D_RWKV])
        a = _sigmoid(za[:, (2 * d + 1) * D_RWKV:(2 * d + 2) * D_RWKV])
        k_d = k * (1.0 + (a - 1.0) * ka)
        b = kkn * a
        cs = lw
        for s in (1, 2, 4, 8, 16, 32):
            if d == 0:
                cs = cs + jnp.where(rin >= s, pltpu.roll(cs, s, 0), 0.0)
            else:
                cs = cs + jnp.where(rin < CHUNK - s, pltpu.roll(cs, TM - s, 0), 0.0)
        end_row = CHUNK - 1 if d == 0 else 0
        ends = [cs[c * CHUNK + end_row:c * CHUNK + end_row + 1, :] for c in range(N_CHUNK)]
        for c in range(N_CHUNK):
            ends_o[d][gi, c:c + 1, :] = ends[c]
        cs_end = jnp.concatenate([jnp.broadcast_to(e, (CHUNK, D_RWKV)) for e in ends], axis=0)
        dec_inv = jnp.exp(-cs)
        dec_rest = jnp.exp(cs_end - cs)
        o_a, o_r, o_b, o_k, o_bh, o_kh = ops_o[d]
        o_a[rows, :] = (-kkn * jnp.exp(cs - lw)).astype(BF16)
        o_r[rows, :] = (r * jnp.exp(cs)).astype(BF16)
        o_b[rows, :] = (b * dec_inv).astype(BF16)
        o_k[rows, :] = (k_d * dec_inv).astype(BF16)
        o_bh[rows, :] = (b * dec_rest).astype(BF16)
        o_kh[rows, :] = (k_d * dec_rest).astype(BF16)
        return k_d

    conv_in = proj(4 * D_RWKV, 4 * D_RWKV + 3 * D_CONV)
    k_0 = scan_operands(0)
    gate_a = proj(4 * D_RWKV + 3 * D_CONV, 4 * D_RWKV + 3 * D_CONV + D_MODEL)
    sa_o[rows, :] = _sigmoid(gate_a).astype(BF16)
    k_1 = scan_operands(1)
    gate_b = proj(4 * D_RWKV + 3 * D_CONV + D_MODEL, D_IN)

    cgate = conv_in[:, :D_CONV]
    u = conv_in[:, D_CONV:2 * D_CONV] * conv_in[:, 2 * D_CONV:]
    col = row & (GRID_W - 1)
    zl = jnp.logical_or(row == 0, jnp.logical_and(is_lat, col == 0))
    zr = jnp.logical_or(row == TM - 1, jnp.logical_and(is_lat, col == GRID_W - 1))
    left = jnp.where(zl, 0.0, pltpu.roll(u, 1, 0))
    right = jnp.where(zr, 0.0, pltpu.roll(u, TM - 1, 0))
    conv = left * cw_ref[0:1, :] + u * cw_ref[1:2, :] + right * cw_ref[2:3, :] + cbias_ref[...]
    y_b = _dot(cgate * conv, wbb_ref[...])
    mb_o[rows, :] = (_sigmoid(gate_b) * y_b).astype(BF16)

    bv_o[rows, :] = (_head_sum(r * (k_0 + k_1) * rk_ref[...], ones) * v).astype(BF16)


def _front_kernel(x_ref, xp_ref, xn_ref, mod_ref, g_ref, win_ref, waug_ref, *rest):
    refs = list(rest[:N_FRONT_CONSTS])
    outs = rest[N_FRONT_CONSTS:-1]
    wbb_bf = rest[-1]
    ops_o = (outs[0:N_DIR_OPS], outs[N_DIR_OPS:2 * N_DIR_OPS])
    v_o, ends0_o, ends1_o, bv_o, sg_o, sa_o, mb_o = outs[2 * N_DIR_OPS:]
    outs = (ops_o, v_o, (ends0_o, ends1_o), bv_o, sg_o, sa_o, mb_o)

    t = pl.program_id(0)

    @pl.when(t == 0)
    def _():
        wbb_bf[...] = refs[FRONT_WBB][...].astype(BF16)

    refs[FRONT_WBB] = wbb_bf

    is_lat = t >= N_FRONT_CTX_TILES
    i_in = (t - N_FRONT_CTX_TILES) % FRONT_LAT_TILES
    lat_first = jnp.logical_and(is_lat, i_in == 0)
    lat_last = jnp.logical_and(is_lat, i_in == FRONT_LAT_TILES - 1)
    shift = mod_ref[MOD_FRONT:MOD_FRONT + 1, :]
    scale = mod_ref[MOD_FRONT + 1:MOD_FRONT + 2, :]
    g2 = g_ref[2:3, :]

    def pre(x):
        return (_rms(x, g2) * (1.0 + scale) + shift).astype(BF16)

    x = x_ref[...]
    edge = jnp.concatenate([xp_ref[...], x[TM - 8:TM + 8, :], xn_ref[...]], axis=0)
    edge_b = _bdot(pre(edge), waug_ref[:, 2 * LANES:])
    zero = jnp.zeros((1, LANES), F32)
    halo_prev = (jnp.where(jnp.logical_and(is_lat, jnp.logical_not(lat_first)), edge_b[7:8, :LANES], zero),
                 jnp.where(is_lat, edge_b[15:16, :LANES], zero))
    halo_next = (jnp.where(is_lat, edge_b[16:17, LANES:], zero),
                 jnp.where(jnp.logical_and(is_lat, jnp.logical_not(lat_last)), edge_b[24:25, LANES:], zero))

    for gi in range(FRONT_GROUPS):
        h = pre(x[gi * TM:(gi + 1) * TM, :])
        _front_group(gi, h, halo_prev[gi], halo_next[gi], is_lat, win_ref, waug_ref, refs, outs)


def _front(x, mod3, norm_g, w_in_bf, w_aug, consts):
    tok = lambda t: (t, 0)
    rows8 = FRONT_TM // 8
    last8 = N_TOK // 8 - 1
    assert len(consts) == N_FRONT_CONSTS
    out_rb = jax.ShapeDtypeStruct((N_TOK, D_RWKV), BF16)
    out_db = jax.ShapeDtypeStruct((N_TOK, D_MODEL), BF16)
    out_e = jax.ShapeDtypeStruct((N_TILES, N_CHUNK, D_RWKV), F32)
    spec_r = pl.BlockSpec((FRONT_TM, D_RWKV), tok)
    spec_d = pl.BlockSpec((FRONT_TM, D_MODEL), tok)
    spec_e = pl.BlockSpec((FRONT_GROUPS, N_CHUNK, D_RWKV), lambda t: (t, 0, 0))
    n_b = 2 * N_DIR_OPS + 1
    return pl.pallas_call(
        _front_kernel,
        grid=(N_TOK // FRONT_TM,),
        in_specs=[pl.BlockSpec((FRONT_TM, D_MODEL), tok),
                  pl.BlockSpec((8, D_MODEL), lambda t: (jnp.maximum(t * rows8 - 1, 0), 0)),
                  pl.BlockSpec((8, D_MODEL), lambda t: (jnp.minimum((t + 1) * rows8, last8), 0)),
                  pl.BlockSpec((None, MOD_TAIL, D_MODEL), lambda t: (_mod_row(t * FRONT_GROUPS), 0, 0))]
                 + [_const_spec(c.shape) for c in [norm_g, w_in_bf, w_aug] + list(consts)],
        out_specs=[spec_r] * n_b + [spec_e] * 2 + [spec_r] * 2 + [spec_d] * 2,
        out_shape=[out_rb] * n_b + [out_e] * 2 + [out_rb] * 2 + [out_db] * 2,
        scratch_shapes=[pltpu.VMEM((D_CONV, D_MODEL), BF16)],
        compiler_params=_params(),
        name="mixer_front",
    )(x, x, x, mod3, norm_g, w_in_bf, w_aug, *consts)


def _stack(x):
    lane_lo = lax.broadcasted_iota(jnp.int32, x.shape, 1) < HEAD
    z = jnp.zeros_like(x)
    return jnp.concatenate([jnp.where(lane_lo, x, z), jnp.where(lane_lo, z, x)], axis=0)


def _scan_kernel(*refs):
    ops = (refs[0:N_DIR_OPS + 1], refs[N_DIR_OPS + 1:2 * N_DIR_OPS + 2])
    (ends0_ref, ends1_ref, s0_ref, w13_ref, w2_ref,
     yf_ref, yb_ref, sout_ref, w13_bf_ref, w2_bf_ref, st_ref) = refs[2 * N_DIR_OPS + 2:]
    ends_ref = (ends0_ref, ends1_ref)
    w13_bf_ref[...] = w13_ref[...].astype(BF16)
    w2_bf_ref[...] = w2_ref[...].astype(BF16)
    step_id = pl.program_id(0)
    is_lat = step_id >= N_SCAN_CTX_STEPS
    lat_first = (step_id - N_SCAN_CTX_STEPS) % SCAN_LAT_STEPS == 0

    @pl.when(jnp.logical_and(is_lat, lat_first))
    def _():
        zero = jnp.zeros((HEAD, HEAD), F32)
        for d in (0, 1):
            for p in range(N_PAIR):
                top = jnp.concatenate([s0_ref[0, d, 2 * p], zero], axis=1)
                bottom = jnp.concatenate([zero, s0_ref[0, d, 2 * p + 1]], axis=1)
                st_ref[d, p] = jnp.concatenate([top, bottom], axis=0)

    ti = lax.broadcasted_iota(jnp.int32, (CHUNK, LANES), 0)
    sj = lax.broadcasted_iota(jnp.int32, (CHUNK, LANES), 1) & (CHUNK - 1)
    m_strict = (sj < ti, sj > ti)
    m_incl = (sj <= ti, sj >= ti)
    eye_cat = jnp.where(sj == ti, 1.0, 0.0)
    bi = lax.broadcasted_iota(jnp.int32, (LANES, LANES), 0)
    bj = lax.broadcasted_iota(jnp.int32, (LANES, LANES), 1)
    blk = (bi >> HEAD_SHIFT) == (bj >> HEAD_SHIFT)
    eye_bd = bi == bj
    zero_bd = jnp.zeros((LANES, LANES), BF16)

    def op(i, u):
        d, p, c = u
        return ops[d][i][c * CHUNK:(c + 1) * CHUNK, p * LANES:(p + 1) * LANES]

    g_mat, w_mat, p_t, q_t = {}, {}, {}, {}

    def local_stages(units):
        low, nak, mrbk = {}, {}, {}
        for u in units:
            lhs = jnp.concatenate([op(OP_A, u), op(OP_R, u)], axis=0)
            rhs = jnp.concatenate([_stack(op(OP_B, u)), _stack(op(OP_K, u))], axis=0)
            gram = _bdot(lhs, rhs, NT)
            d = u[0]
            low[u] = jnp.where(m_strict[d], gram[:CHUNK, :LANES], 0.0)
            nak[u] = jnp.where(m_strict[d], gram[:CHUNK, LANES:], 0.0).astype(BF16)
            mrbk[u] = jnp.concatenate([jnp.where(m_incl[d], gram[CHUNK:, :LANES], 0.0),
                                       jnp.where(m_incl[d], gram[CHUNK:, LANES:], 0.0)], axis=1).astype(BF16)

        inv = {u: eye_cat + low[u] for u in units}
        pwb = {u: low[u].astype(BF16) for u in units}
        for u in units:
            pwb[u] = _bdot(pwb[u], _stack(pwb[u])).astype(BF16)
        for _ in range(4):
            for u in units:
                both = _bdot(pwb[u], jnp.concatenate([_stack(pwb[u]), _stack(inv[u].astype(BF16))], axis=1))
                pwb[u] = both[:, :LANES].astype(BF16)
                inv[u] = inv[u] + both[:, LANES:]
        for u in units:
            inv[u] = inv[u] + _bdot(pwb[u], _stack(inv[u].astype(BF16)))

        sv = {u: _stack(op(OP_V, u)) for u in units}
        nv = {}
        for u in units:
            nv[u] = _bdot(nak[u], sv[u]).astype(BF16)
        x1, x2 = {}, {}
        for u in units:
            x12 = _bdot(inv[u].astype(BF16), jnp.concatenate([_stack(op(OP_A, u)), _stack(nv[u])], axis=1))
            x1[u] = x12[:, :LANES].astype(BF16)
            x2[u] = x12[:, LANES:].astype(BF16)
        for u in units:
            rhs = jnp.concatenate([jnp.concatenate([_stack(x1[u]), _stack(x2[u])], axis=1),
                                   jnp.concatenate([zero_bd, sv[u]], axis=1)], axis=0)
            gw = _bdot(mrbk[u], rhs)
            g_mat[u] = (op(OP_R, u).astype(F32) + gw[:, :LANES]).astype(BF16)
            w_mat[u] = gw[:, LANES:]
        for u in units:
            d, p, c = u
            ct = c % N_CHUNK
            gam = jnp.exp(ends_ref[d][c // N_CHUNK, ct:ct + 1, p * LANES:(p + 1) * LANES])
            p_t[u] = (jnp.where(blk, _bdot(x1[u], op(OP_BH, u), TN), 0.0)
                      + jnp.where(eye_bd, gam, 0.0)).astype(BF16)
            q_t[u] = jnp.where(blk, _bdot(jnp.concatenate([x2[u], op(OP_V, u)], axis=0),
                                          jnp.concatenate([op(OP_BH, u), op(OP_KH, u)], axis=0), TN), 0.0)

    n_c = SCAN_TILES * N_CHUNK
    state = {(d, p): jnp.where(is_lat, st_ref[d, p], 0.0) for d in (0, 1) for p in range(N_PAIR)}
    finals = {}
    y_refs = (yf_ref, yb_ref)

    def chunk_of(d, step):
        return step if d == 0 else n_c - 1 - step

    def chain(step, d, p):
        c = chunk_of(d, step)
        u = (d, p, c)
        sb = state[d, p].astype(BF16)
        y_refs[d][c * CHUNK:(c + 1) * CHUNK, p * LANES:(p + 1) * LANES] = _bdot(g_mat[u], sb, NT) + w_mat[u]
        state[d, p] = _bdot(sb, p_t[u]) + q_t[u]
        if (step + 1) % N_CHUNK == 0:
            finals[d, c // N_CHUNK, p] = state[d, p]
            if step + 1 < n_c:
                state[d, p] = jnp.where(is_lat, state[d, p], 0.0)

    local_stages([(d, p, c) for d in (0, 1) for p in range(N_PAIR) for c in range(n_c)])
    for step in range(n_c):
        for d in (0, 1):
            for p in range(N_PAIR):
                chain(step, d, p)
    for (d, p), s in state.items():
        st_ref[d, p] = s

    @pl.when(jnp.logical_not(is_lat))
    def _():
        for (d, tile, p), s in finals.items():
            sout_ref[tile, 0, d, 2 * p] = s[:HEAD, :HEAD]
            sout_ref[tile, 0, d, 2 * p + 1] = s[HEAD:, HEAD:]


def _scan(ops_f, ops_b, v, ends_f, ends_b, s0, w13, w2):
    def mirror(s):
        u = s - N_SCAN_CTX_STEPS
        return jnp.where(s < N_SCAN_CTX_STEPS, s,
                         N_SCAN_CTX_STEPS + (u // SCAN_LAT_STEPS) * SCAN_LAT_STEPS
                         + (SCAN_LAT_STEPS - 1 - u % SCAN_LAT_STEPS))

    def lat_seq(s):
        return jnp.maximum(s - N_SCAN_CTX_STEPS, 0) // SCAN_LAT_STEPS

    fwd = pl.BlockSpec((SCAN_TM, D_RWKV), lambda s: (s, 0))
    bwd = pl.BlockSpec((SCAN_TM, D_RWKV), lambda s: (mirror(s), 0))
    e_block = (SCAN_TILES, N_CHUNK, D_RWKV)
    st_block = (None, 1, 2, 2 * N_PAIR, HEAD, HEAD)
    out_y = jax.ShapeDtypeStruct((N_TOK, D_RWKV), F32)
    n_steps = N_TOK // SCAN_TM
    return pl.pallas_call(
        _scan_kernel,
        grid=(n_steps,),
        in_specs=[fwd] * (N_DIR_OPS + 1) + [bwd] * (N_DIR_OPS + 1)
                 + [pl.BlockSpec(e_block, lambda s: (s, 0, 0)),
                    pl.BlockSpec(e_block, lambda s: (mirror(s), 0, 0)),
                    pl.BlockSpec(st_block, lambda s: (lat_seq(s), 0, 0, 0, 0, 0)),
                    _slab_spec(w13, n_steps),
                    _slab_spec(w2, n_steps)],
        out_specs=[fwd, bwd,
                   pl.BlockSpec((SCAN_TILES, 1, 2, 2 * N_PAIR, HEAD, HEAD),
                                lambda s: (jnp.minimum(s, N_SCAN_CTX_STEPS - 1), 0, 0, 0, 0, 0)),
                   _slab_spec(w13, n_steps),
                   _slab_spec(w2, n_steps)],
        out_shape=[out_y, out_y,
                   jax.ShapeDtypeStruct((N_CTX_SEQ, 1, 2, 2 * N_PAIR, HEAD, HEAD), F32),
                   jax.ShapeDtypeStruct(w13.shape, BF16),
                   jax.ShapeDtypeStruct(w2.shape, BF16)],
        scratch_shapes=[pltpu.VMEM((2, N_PAIR, LANES, LANES), F32)],
        compiler_params=_params(),
        name="rwkv7_scan",
    )(*ops_f, v, *ops_b, v, ends_f, ends_b, s0, w13, w2)


def _back_kernel(x_ref, yf_ref, yb_ref, bv_ref, sg_ref, sa_ref, mb_ref, mod_ref, g_ref,
                 gng_ref, gnb_ref, wba_f32_ref, wout_f32_ref, ones_ref, o_ref, wba_ref, wout_ref):
    @pl.when(pl.program_id(0) == 0)
    def _():
        wba_ref[...] = wba_f32_ref[...].astype(BF16)
        wout_ref[...] = wout_f32_ref[...].astype(BF16)

    ones = ones_ref[...]
    groups = [slice(i * TM, (i + 1) * TM) for i in range(BACK_TM // TM)]
    ys = [yf_ref[r, :] + yb_ref[r, :] for r in groups]
    ycs = [y - _head_sum(y, ones) * (1.0 / HEAD) for y in ys]
    vs = [_head_sum(yc * yc, ones) * (1.0 / HEAD) for yc in ycs]
    yns = [yc * lax.rsqrt(v + EPS_GN) * gng_ref[...] + gnb_ref[...] for yc, v in zip(ycs, vs)]
    yas = [_dot((yn + bv_ref[r, :]) * sg_ref[r, :], wba_ref[...]) for r, yn in zip(groups, yns)]
    outs = [_dot(sa_ref[r, :] * ya + mb_ref[r, :], wout_ref[...]) for r, ya in zip(groups, yas)]
    for r, out in zip(groups, outs):
        o_ref[r, :] = x_ref[r, :] + mod_ref[MOD_BACK:MOD_BACK + 1, :] * _rms(out, g_ref[3:4, :])


def _back(x, yf, yb, bv, sg, sa, mb, mod3, norm_g, gng, gnb, wba, wout, ones):
    tok = lambda t: (t, 0)
    spec_r = pl.BlockSpec((BACK_TM, D_RWKV), tok)
    spec_d = pl.BlockSpec((BACK_TM, D_MODEL), tok)
    consts = [norm_g, gng, gnb, wba, wout, ones]
    return pl.pallas_call(
        _back_kernel,
        grid=(N_TOK // BACK_TM,),
        in_specs=[spec_d, spec_r, spec_r, spec_r, spec_r, spec_d, spec_d,
                  pl.BlockSpec((None, MOD_TAIL, D_MODEL), lambda t: (_mod_row(t * (BACK_TM // TM)), 0, 0))]
                 + [_const_spec(c.shape) for c in consts],
        out_specs=spec_d,
        out_shape=jax.ShapeDtypeStruct((N_TOK, D_MODEL), F32),
        scratch_shapes=[pltpu.VMEM(wba.shape, BF16), pltpu.VMEM(wout.shape, BF16)],
        compiler_params=_params(),
        name="mixer_back",
    )(x, yf, yb, bv, sg, sa, mb, mod3, *consts)


def kernel(x_prompt, x_sample, c, state_rwkv, c_ctx, w_mod, b_mod, norm_g, ffn1_w13, ffn1_w2,
           ffn2_w13, ffn2_w2, w_in, mu_shift, decay_w0, decay_w1, decay_w2, iclr_a0, iclr_a1,
           iclr_a2, k_k, k_a, r_k, gn_gain, gn_bias, conv_w, conv_b, w_branch_a, w_branch_b, w_out):
    assert x_prompt.shape == (N_CTX_SEQ, CTX_LEN, D_MODEL) and x_sample.shape == (N_LAT_SEQ, LAT_LEN, D_MODEL)
    assert w_mod.shape[0] == 1, "single trunk layer"

    cvec = jnp.concatenate([c_ctx[None, :], c, jnp.zeros((MOD_ROWS - 1 - N_LAT_SEQ, D_MODEL), F32)], axis=0)
    mod_head = _modulation(cvec, w_mod[0], b_mod).reshape(MOD_ROWS, MOD_HEAD, D_MODEL)
    g = norm_g[0]

    x, w_in_bf, mod_tail = _ffn_first(x_prompt.reshape(-1, D_MODEL), x_sample.reshape(-1, D_MODEL),
                                      mod_head, g, ffn1_w13[0], ffn1_w2[0], w_in[0], cvec, w_mod[0], b_mod)
    mod3 = mod_tail.reshape(MOD_ROWS, MOD_TAIL, D_MODEL)

    row = lambda p: p.reshape(1, -1)
    mu4 = mu_shift[0].reshape(4, D_MODEL)
    w1_cat = jnp.concatenate([decay_w1[0, 0], iclr_a1[0, 0], decay_w1[0, 1], iclr_a1[0, 1]], axis=1)
    w_aug = _lora_prep(jnp.repeat(mu4.T, LORA, axis=1), w1_cat)
    w2_blocks = [decay_w2[0, 0], iclr_a2[0, 0], decay_w2[0, 1], iclr_a2[0, 1]]
    w2_zero = jnp.zeros((LORA, D_RWKV), F32)
    w2_aug = jnp.concatenate(
        [jnp.concatenate([blk if j == i else w2_zero for j in range(4)], axis=1)
         for i, blk in enumerate(w2_blocks)], axis=0).astype(BF16)
    b2_aug = jnp.concatenate([decay_w0[0, 0], iclr_a0[0, 0], decay_w0[0, 1], iclr_a0[0, 1]]).reshape(1, -1)
    ones = _head_ones()
    front = _front(x, mod3, g, w_in_bf, w_aug,
                   [w2_aug, b2_aug, row(k_k[0]), row(k_a[0]), row(r_k[0]), conv_w[0], row(conv_b[0]),
                    w_branch_b[0], ones])
    ops_f, ops_b = front[0:N_DIR_OPS], front[N_DIR_OPS:2 * N_DIR_OPS]
    v, ends_f, ends_b, bv, sg, sa, mb = front[2 * N_DIR_OPS:]

    yf, yb, s_fin, w13_bf, w2_bf = _scan(ops_f, ops_b, v, ends_f, ends_b, state_rwkv,
                                         ffn2_w13[0], ffn2_w2[0])

    x = _back(x, yf, yb, bv, sg, sa, mb, mod3, g, row(gn_gain[0]), row(gn_bias[0]),
              w_branch_a[0], w_out[0], ones)
    y_ctx, y_lat = _ffn_last(x, mod3, g, w13_bf, w2_bf)

    y_prompt = y_ctx.reshape(N_CTX_SEQ, CTX_LEN, D_MODEL)
    y_sample = y_lat.reshape(N_LAT_SEQ, LAT_LEN, D_MODEL)
    return y_prompt, y_sample, s_fin
```

```python
import jax
import jax.numpy as jnp
from jax import lax
from jax.experimental import pallas as pl
from jax.experimental.pallas import tpu as pltpu

F32 = jnp.float32
BF16 = jnp.bfloat16

D_MODEL = 1024
D_FF = 2816
D_RWKV = 512
D_CONV = 512
HEAD = 64
HEAD_SHIFT = 6
D_IN = 4 * D_RWKV + 3 * D_CONV + 2 * D_MODEL
N_MOD = 9
EPS_RMS = 1e-6
EPS_GN = 64e-5
HALF_STEP = 0.5
EXP_M05 = 0.6065306597126334

N_CTX_SEQ = 16
CTX_LEN = 256
N_LAT_SEQ = 2
LAT_LEN = 2048
GRID_W = 64
N_TOK = N_CTX_SEQ * CTX_LEN + N_LAT_SEQ * LAT_LEN

TM = 256
N_CTX_TILES = N_CTX_SEQ * CTX_LEN // TM
LAT_TILES = LAT_LEN // TM
N_TILES = N_TOK // TM
FRONT_TM = 512
FRONT_GROUPS = FRONT_TM // TM
N_FRONT_CTX_TILES = N_CTX_SEQ * CTX_LEN // FRONT_TM
FRONT_LAT_TILES = LAT_LEN // FRONT_TM
N_FRONT_CONSTS = 9
FRONT_WBB = 7
LORA = 64
SCAN_TM = 512
SCAN_TILES = SCAN_TM // TM
N_SCAN_CTX_STEPS = N_CTX_SEQ * CTX_LEN // SCAN_TM
SCAN_LAT_STEPS = LAT_LEN // SCAN_TM
BACK_TM = 1024
FFN_TM = 512
N_FFN_TILES = N_TOK // FFN_TM
N_FFN_CTX_TILES = N_CTX_SEQ * CTX_LEN // FFN_TM
FFN2_TM = 1024
N_FFN2_CTX_TILES = N_CTX_SEQ * CTX_LEN // FFN2_TM
N_WCHUNK = 11
CHUNK = 64
N_CHUNK = TM // CHUNK
LANES = 128
N_PAIR = D_RWKV // LANES
MOD_ROWS = 8
MOD_HEAD = 3
MOD_TAIL = N_MOD - MOD_HEAD
MOD_SIDE = 512
N_MOD_SIDE = MOD_TAIL * D_MODEL // MOD_SIDE
MOD_FRONT, MOD_BACK, MOD_FFN2 = 0, 2, 3
VMEM_LIMIT = 56 * 1024 * 1024

OP_A, OP_R, OP_B, OP_K, OP_BH, OP_KH, OP_V = range(7)
N_DIR_OPS = 6

NN = (((1,), (0,)), ((), ()))
NT = (((1,), (1,)), ((), ()))
TN = (((0,), (0,)), ((), ()))


def _dot(a, b, dims=NN):
    return lax.dot_general(a.astype(BF16), b.astype(BF16), dims, preferred_element_type=F32)


def _bdot(a, b, dims=NN):
    return lax.dot_general(a, b, dims, preferred_element_type=F32)


def _rms(x, g):
    ms = jnp.mean(x * x, axis=-1, keepdims=True)
    return x * lax.rsqrt(ms + EPS_RMS) * g


def _head_ones():
    i = lax.broadcasted_iota(jnp.int32, (D_RWKV, D_RWKV), 0) >> HEAD_SHIFT
    j = lax.broadcasted_iota(jnp.int32, (D_RWKV, D_RWKV), 1) >> HEAD_SHIFT
    return jnp.where(i == j, 1.0, 0.0).astype(BF16)


def _head_sum(x, ones):
    return jnp.dot(x.astype(BF16), ones, preferred_element_type=F32)


def _sigmoid(x):
    return 0.5 * jnp.tanh(0.5 * x) + 0.5


def _mod_row(t):
    return jnp.where(t < N_CTX_TILES, 0, 1 + (t - N_CTX_TILES) // LAT_TILES)


def _const_spec(shape):
    nd = len(shape)
    return pl.BlockSpec(shape, lambda *_: (0,) * nd, pipeline_mode=pl.Buffered(1))


def _params(n_axes=1):
    return pltpu.CompilerParams(dimension_semantics=("arbitrary",) * n_axes,
                                vmem_limit_bytes=VMEM_LIMIT)


def _mod_kernel(c_ref, w_ref, b_ref, o_ref):
    c = c_ref[...]
    s = c * jax.nn.sigmoid(c)
    o_ref[...] = _dot(s, w_ref[...]) + b_ref[...]


def _modulation(cvec, w_mod, b_mod):
    width = MOD_HEAD * D_MODEL
    return pl.pallas_call(
        _mod_kernel,
        grid=(1,),
        in_specs=[pl.BlockSpec((MOD_ROWS, D_MODEL), lambda j: (0, 0)),
                  pl.BlockSpec((D_MODEL, width), lambda j: (0, 0)),
                  pl.BlockSpec((1, width), lambda j: (0, 0))],
        out_specs=pl.BlockSpec((MOD_ROWS, width), lambda j: (0, 0)),
        out_shape=jax.ShapeDtypeStruct((MOD_ROWS, width), F32),
        compiler_params=_params(),
        name="modulation",
    )(cvec, w_mod, b_mod)


def _ffn_body(x, mod_ref, g_ref, w13_ref, w2_ref, im, ig):
    shift = mod_ref[im:im + 1, :]
    scale = mod_ref[im + 1:im + 2, :]
    gate = mod_ref[im + 2:im + 3, :]
    xs = [x[i * TM:(i + 1) * TM, :] for i in range(x.shape[0] // TM)]
    hs = [(_rms(xi, g_ref[ig:ig + 1, :]) * (1.0 + scale) + shift).astype(BF16) for xi in xs]
    gus = [_dot(hi, w13_ref[...]) for hi in hs]
    acts = []
    for gu in gus:
        gt = gu[:, :D_FF]
        up = gu[:, D_FF:]
        acts.append((gt * jax.nn.sigmoid(gt) * up).astype(BF16))
    os_ = [_dot(ai, w2_ref[...]) for ai in acts]
    outs = [xi + HALF_STEP * gate * _rms(oi, g_ref[ig + 1:ig + 2, :]) for xi, oi in zip(xs, os_)]
    return jnp.concatenate(outs, axis=0)


def _cast_chunk(step, src_ref, dst_ref, axis):
    size = src_ref.shape[axis]

    @pl.when(step < N_WCHUNK)
    def _():
        start = pl.multiple_of(step * size, size)
        if axis == 0:
            dst_ref[pl.ds(start, size), :] = src_ref[...].astype(BF16)
        else:
            dst_ref[:, pl.ds(start, size)] = src_ref[...].astype(BF16)


def _chunk_spec(w, axis):
    block = list(w.shape)
    block[axis] = w.shape[axis] // N_WCHUNK
    clamp = lambda s: jnp.minimum(s, N_WCHUNK - 1)
    index = (lambda s: (clamp(s), 0)) if axis == 0 else (lambda s: (0, clamp(s)))
    return pl.BlockSpec(tuple(block), index)


def _tile_step(s):
    return jnp.maximum(s - N_WCHUNK, 0)


def _slab_spec(w, n_steps, step=lambda s: s):
    rows = w.shape[0] // n_steps
    return pl.BlockSpec((rows, w.shape[1]), lambda s: (step(s), 0))


def _ffn_first_kernel(xc_ref, xl_ref, mod_ref, g_ref, w13_ref, w2_ref, win_ref, c_ref, wmod_ref, bmod_ref,
                      o_ref, win_bf_ref, mod_tail_ref, w13_bf, w2_bf):
    s = pl.program_id(0)
    _cast_chunk(s, w13_ref, w13_bf, 1)
    _cast_chunk(s, w2_ref, w2_bf, 0)

    @pl.when(s >= N_WCHUNK)
    def _():
        x = jnp.where(s - N_WCHUNK < N_FFN_CTX_TILES, xc_ref[...], xl_ref[...])
        o_ref[...] = _ffn_body(x, mod_ref, g_ref, w13_bf, w2_bf, 0, 0)
        win_bf_ref[...] = win_ref[...].astype(BF16)

        @pl.when(s - N_WCHUNK < N_MOD_SIDE)
        def _():
            _mod_kernel(c_ref, wmod_ref, bmod_ref, mod_tail_ref)


def _ffn_last_kernel(x_ref, mod_ref, g_ref, w13_ref, w2_ref, oc_ref, ol_ref):
    out = _ffn_body(x_ref[...], mod_ref, g_ref, w13_ref, w2_ref, MOD_FFN2, 4)
    t = pl.program_id(0)

    @pl.when(t < N_FFN2_CTX_TILES)
    def _():
        oc_ref[...] = out

    @pl.when(t >= N_FFN2_CTX_TILES)
    def _():
        ol_ref[...] = out


def _ctx_tile(t):
    return jnp.minimum(t, N_FFN_CTX_TILES - 1), 0


def _lat_tile(t):
    return jnp.maximum(t - N_FFN_CTX_TILES, 0), 0


def _ffn_mod(t):
    return _mod_row(t * (FFN_TM // TM)), 0, 0


def _ffn_first(x_ctx, x_lat, mod_head, norm_g, w13, w2, w_in, cvec, w_mod, b_mod):
    side = lambda s: jnp.minimum(_tile_step(s), N_MOD_SIDE - 1)
    tail_col = MOD_HEAD * D_MODEL // MOD_SIDE
    return pl.pallas_call(
        _ffn_first_kernel,
        grid=(N_WCHUNK + N_FFN_TILES,),
        in_specs=[pl.BlockSpec((FFN_TM, D_MODEL), lambda s: _ctx_tile(_tile_step(s))),
                  pl.BlockSpec((FFN_TM, D_MODEL), lambda s: _lat_tile(_tile_step(s))),
                  pl.BlockSpec((None, MOD_HEAD, D_MODEL), lambda s: _ffn_mod(_tile_step(s))),
                  _const_spec(norm_g.shape),
                  _chunk_spec(w13, 1),
                  _chunk_spec(w2, 0),
                  _slab_spec(w_in, N_FFN_TILES, step=_tile_step),
                  _const_spec(cvec.shape),
                  pl.BlockSpec((D_MODEL, MOD_SIDE), lambda s: (0, tail_col + side(s))),
                  pl.BlockSpec((1, MOD_SIDE), lambda s: (0, tail_col + side(s)))],
        out_specs=[pl.BlockSpec((FFN_TM, D_MODEL), lambda s: (_tile_step(s), 0)),
                   _slab_spec(w_in, N_FFN_TILES, step=_tile_step),
                   pl.BlockSpec((MOD_ROWS, MOD_SIDE), lambda s: (0, side(s)))],
        out_shape=[jax.ShapeDtypeStruct((N_TOK, D_MODEL), F32),
                   jax.ShapeDtypeStruct(w_in.shape, BF16),
                   jax.ShapeDtypeStruct((MOD_ROWS, MOD_TAIL * D_MODEL), F32)],
        scratch_shapes=[pltpu.VMEM((D_MODEL, 2 * D_FF), BF16), pltpu.VMEM((D_FF, D_MODEL), BF16)],
        compiler_params=_params(),
        name="ffn1",
    )(x_ctx, x_lat, mod_head, norm_g, w13, w2, w_in, cvec, w_mod, b_mod)


def _ffn_last(x, mod3, norm_g, w13_bf, w2_bf):
    n_ctx = N_CTX_SEQ * CTX_LEN
    return pl.pallas_call(
        _ffn_last_kernel,
        grid=(N_TOK // FFN2_TM,),
        in_specs=[pl.BlockSpec((FFN2_TM, D_MODEL), lambda t: (t, 0)),
                  pl.BlockSpec((None, MOD_TAIL, D_MODEL), lambda t: (_mod_row(t * (FFN2_TM // TM)), 0, 0)),
                  _const_spec(norm_g.shape),
                  _const_spec(w13_bf.shape),
                  _const_spec(w2_bf.shape)],
        out_specs=[pl.BlockSpec((FFN2_TM, D_MODEL), lambda t: (jnp.minimum(t, N_FFN2_CTX_TILES - 1), 0)),
                   pl.BlockSpec((FFN2_TM, D_MODEL), lambda t: (jnp.maximum(t - N_FFN2_CTX_TILES, 0), 0))],
        out_shape=[jax.ShapeDtypeStruct((n_ctx, D_MODEL), F32),
                   jax.ShapeDtypeStruct((N_TOK - n_ctx, D_MODEL), F32)],
        compiler_params=_params(),
        name="ffn2",
    )(x, mod3, norm_g, w13_bf, w2_bf)


def _lora_prep_kernel(mu_ref, w1_ref, o_ref):
    mu = mu_ref[...]
    w1 = w1_ref[...]
    n = w1.shape[1]
    o_ref[:, :n] = ((1.0 - mu) * w1).astype(BF16)
    o_ref[:, n:] = (mu * w1).astype(BF16)


def _lora_prep(mu_cols, w1_cat):
    n = w1_cat.shape[1]
    return pl.pallas_call(
        _lora_prep_kernel,
        out_shape=jax.ShapeDtypeStruct((D_MODEL, 2 * n), BF16),
        compiler_params=pltpu.CompilerParams(vmem_limit_bytes=VMEM_LIMIT),
        name="lora_prep",
    )(mu_cols, w1_cat)


def _front_group(gi, h, halo_prev, halo_next, is_lat, win_ref, waug_ref, refs, outs):
    w2aug_ref, b2aug_ref, kkw_ref, ka_ref, rk_ref, cw_ref, cbias_ref, wbb_ref, ones_ref = refs
    ops_o, v_o, ends_o, bv_o, sg_o, sa_o, mb_o = outs
    rows = slice(gi * TM, (gi + 1) * TM)
    row = lax.broadcasted_iota(jnp.int32, (TM, 1), 0)
    rin = row & (CHUNK - 1)
    ones = ones_ref[...]

    def proj(lo, hi):
        return _bdot(h, win_ref[:, lo:hi])

    pab = _bdot(h, waug_ref[...])
    rk = proj(0, 2 * D_RWKV)
    r = rk[:, :D_RWKV]
    k = rk[:, D_RWKV:]
    pb = pab[:, 2 * LANES:]
    sh_f = jnp.where(row == 0, halo_prev, pltpu.roll(pb[:, :LANES], 1, 0))
    sh_b = jnp.where(row == TM - 1, halo_next, pltpu.roll(pb[:, LANES:], TM - 1, 0))
    t_in = pab[:, :2 * LANES] + jnp.concatenate([sh_f, sh_b], axis=1)
    lane = lax.broadcasted_iota(jnp.int32, (1, 2 * LANES), 1)
    t_in = jnp.where((lane & HEAD) == 0, jnp.tanh(t_in), t_in)
    za = _dot(t_in, w2aug_ref[...]) + b2aug_ref[...]

    kk = k * kkw_ref[...]
    kkn = kk * lax.rsqrt(_head_sum(kk * kk, ones) + 1e-12)
    ka = ka_ref[...]

    vg = proj(2 * D_RWKV, 4 * D_RWKV)
    v = vg[:, :D_RWKV]
    v_o[rows, :] = v.astype(BF16)
    sg_o[rows, :] = _sigmoid(vg[:, D_RWKV:]).astype(BF16)

    def scan_operands(d):
        lw = -EXP_M05 * _sigmoid(za[:, 2 * d * D_RWKV:(2 * d + 1) * D_RWKV])
        a = _sigmoid(za[:, (2 * d + 1) * D_RWKV:(2 * d + 2) * D_RWKV])
        k_d = k * (1.0 + (a - 1.0) * ka)
        b = kkn * a
        cs = lw
        for s in (1, 2, 4, 8, 16, 32):
            if d == 0:
                cs = cs + jnp.where(rin >= s, pltpu.roll(cs, s, 0), 0.0)
            else:
                cs = cs + jnp.where(rin < CHUNK - s, pltpu.roll(cs, TM - s, 0), 0.0)
        end_row = CHUNK - 1 if d == 0 else 0
        ends = [cs[c * CHUNK + end_row:c * CHUNK + end_row + 1, :] for c in range(N_CHUNK)]
        for c in range(N_CHUNK):
            ends_o[d][gi, c:c + 1, :] = ends[c]
        cs_end = jnp.concatenate([jnp.broadcast_to(e, (CHUNK, D_RWKV)) for e in ends], axis=0)
        dec_inv = jnp.exp(-cs)
        dec_rest = jnp.exp(cs_end - cs)
        o_a, o_r, o_b, o_k, o_bh, o_kh = ops_o[d]
        o_a[rows, :] = (-kkn * jnp.exp(cs - lw)).astype(BF16)
        o_r[rows, :] = (r * jnp.exp(cs)).astype(BF16)
        o_b[rows, :] = (b * dec_inv).astype(BF16)
        o_k[rows, :] = (k_d * dec_inv).astype(BF16)
        o_bh[rows, :] = (b * dec_rest).astype(BF16)
        o_kh[rows, :] = (k_d * dec_rest).astype(BF16)
        return k_d

    conv_in = proj(4 * D_RWKV, 4 * D_RWKV + 3 * D_CONV)
    k_0 = scan_operands(0)
    gate_a = proj(4 * D_RWKV + 3 * D_CONV, 4 * D_RWKV + 3 * D_CONV + D_MODEL)
    sa_o[rows, :] = _sigmoid(gate_a).astype(BF16)
    k_1 = scan_operands(1)
    gate_b = proj(4 * D_RWKV + 3 * D_CONV + D_MODEL, D_IN)

    cgate = conv_in[:, :D_CONV]
    u = conv_in[:, D_CONV:2 * D_CONV] * conv_in[:, 2 * D_CONV:]
    col = row & (GRID_W - 1)
    zl = jnp.logical_or(row == 0, jnp.logical_and(is_lat, col == 0))
    zr = jnp.logical_or(row == TM - 1, jnp.logical_and(is_lat, col == GRID_W - 1))
    left = jnp.where(zl, 0.0, pltpu.roll(u, 1, 0))
    right = jnp.where(zr, 0.0, pltpu.roll(u, TM - 1, 0))
    conv = left * cw_ref[0:1, :] + u * cw_ref[1:2, :] + right * cw_ref[2:3, :] + cbias_ref[...]
    y_b = _dot(cgate * conv, wbb_ref[...])
    mb_o[rows, :] = (_sigmoid(gate_b) * y_b).astype(BF16)

    bv_o[rows, :] = (_head_sum(r * (k_0 + k_1) * rk_ref[...], ones) * v).astype(BF16)


def _front_kernel(x_ref, xp_ref, xn_ref, mod_ref, g_ref, win_ref, waug_ref, *rest):
    refs = list(rest[:N_FRONT_CONSTS])
    outs = rest[N_FRONT_CONSTS:-1]
    wbb_bf = rest[-1]
    ops_o = (outs[0:N_DIR_OPS], outs[N_DIR_OPS:2 * N_DIR_OPS])
    v_o, ends0_o, ends1_o, bv_o, sg_o, sa_o, mb_o = outs[2 * N_DIR_OPS:]
    outs = (ops_o, v_o, (ends0_o, ends1_o), bv_o, sg_o, sa_o, mb_o)

    t = pl.program_id(0)

    @pl.when(t == 0)
    def _():
        wbb_bf[...] = refs[FRONT_WBB][...].astype(BF16)

    refs[FRONT_WBB] = wbb_bf

    is_lat = t >= N_FRONT_CTX_TILES
    i_in = (t - N_FRONT_CTX_TILES) % FRONT_LAT_TILES
    lat_first = jnp.logical_and(is_lat, i_in == 0)
    lat_last = jnp.logical_and(is_lat, i_in == FRONT_LAT_TILES - 1)
    shift = mod_ref[MOD_FRONT:MOD_FRONT + 1, :]
    scale = mod_ref[MOD_FRONT + 1:MOD_FRONT + 2, :]
    g2 = g_ref[2:3, :]

    def pre(x):
        return (_rms(x, g2) * (1.0 + scale) + shift).astype(BF16)

    x = x_ref[...]
    edge = jnp.concatenate([xp_ref[...], x[TM - 8:TM + 8, :], xn_ref[...]], axis=0)
    edge_b = _bdot(pre(edge), waug_ref[:, 2 * LANES:])
    zero = jnp.zeros((1, LANES), F32)
    halo_prev = (jnp.where(jnp.logical_and(is_lat, jnp.logical_not(lat_first)), edge_b[7:8, :LANES], zero),
                 jnp.where(is_lat, edge_b[15:16, :LANES], zero))
    halo_next = (jnp.where(is_lat, edge_b[16:17, LANES:], zero),
                 jnp.where(jnp.logical_and(is_lat, jnp.logical_not(lat_last)), edge_b[24:25, LANES:], zero))

    for gi in range(FRONT_GROUPS):
        h = pre(x[gi * TM:(gi + 1) * TM, :])
        _front_group(gi, h, halo_prev[gi], halo_next[gi], is_lat, win_ref, waug_ref, refs, outs)


def _front(x, mod3, norm_g, w_in_bf, w_aug, consts):
    tok = lambda t: (t, 0)
    rows8 = FRONT_TM // 8
    last8 = N_TOK // 8 - 1
    assert len(consts) == N_FRONT_CONSTS
    out_rb = jax.ShapeDtypeStruct((N_TOK, D_RWKV), BF16)
    out_db = jax.ShapeDtypeStruct((N_TOK, D_MODEL), BF16)
    out_e = jax.ShapeDtypeStruct((N_TILES, N_CHUNK, D_RWKV), F32)
    spec_r = pl.BlockSpec((FRONT_TM, D_RWKV), tok)
    spec_d = pl.BlockSpec((FRONT_TM, D_MODEL), tok)
    spec_e = pl.BlockSpec((FRONT_GROUPS, N_CHUNK, D_RWKV), lambda t: (t, 0, 0))
    n_b = 2 * N_DIR_OPS + 1
    return pl.pallas_call(
        _front_kernel,
        grid=(N_TOK // FRONT_TM,),
        in_specs=[pl.BlockSpec((FRONT_TM, D_MODEL), tok),
                  pl.BlockSpec((8, D_MODEL), lambda t: (jnp.maximum(t * rows8 - 1, 0), 0)),
                  pl.BlockSpec((8, D_MODEL), lambda t: (jnp.minimum((t + 1) * rows8, last8), 0)),
                  pl.BlockSpec((None, MOD_TAIL, D_MODEL), lambda t: (_mod_row(t * FRONT_GROUPS), 0, 0))]
                 + [_const_spec(c.shape) for c in [norm_g, w_in_bf, w_aug] + list(consts)],
        out_specs=[spec_r] * n_b + [spec_e] * 2 + [spec_r] * 2 + [spec_d] * 2,
        out_shape=[out_rb] * n_b + [out_e] * 2 + [out_rb] * 2 + [out_db] * 2,
        scratch_shapes=[pltpu.VMEM((D_CONV, D_MODEL), BF16)],
        compiler_params=_params(),
        name="mixer_front",
    )(x, x, x, mod3, norm_g, w_in_bf, w_aug, *consts)


def _stack(x):
    lane_lo = lax.broadcasted_iota(jnp.int32, x.shape, 1) < HEAD
    z = jnp.zeros_like(x)
    return jnp.concatenate([jnp.where(lane_lo, x, z), jnp.where(lane_lo, z, x)], axis=0)


def _scan_kernel(*refs):
    ops = (refs[0:N_DIR_OPS + 1], refs[N_DIR_OPS + 1:2 * N_DIR_OPS + 2])
    (ends0_ref, ends1_ref, s0_ref, w13_ref, w2_ref,
     yf_ref, yb_ref, sout_ref, w13_bf_ref, w2_bf_ref, st_ref) = refs[2 * N_DIR_OPS + 2:]
    ends_ref = (ends0_ref, ends1_ref)
    w13_bf_ref[...] = w13_ref[...].astype(BF16)
    w2_bf_ref[...] = w2_ref[...].astype(BF16)
    step_id = pl.program_id(0)
    is_lat = step_id >= N_SCAN_CTX_STEPS
    lat_first = (step_id - N_SCAN_CTX_STEPS) % SCAN_LAT_STEPS == 0

    @pl.when(jnp.logical_and(is_lat, lat_first))
    def _():
        zero = jnp.zeros((HEAD, HEAD), F32)
        for d in (0, 1):
            for p in range(N_PAIR):
                top = jnp.concatenate([s0_ref[0, d, 2 * p], zero], axis=1)
                bottom = jnp.concatenate([zero, s0_ref[0, d, 2 * p + 1]], axis=1)
                st_ref[d, p] = jnp.concatenate([top, bottom], axis=0)

    ti = lax.broadcasted_iota(jnp.int32, (CHUNK, LANES), 0)
    sj = lax.broadcasted_iota(jnp.int32, (CHUNK, LANES), 1) & (CHUNK - 1)
    m_strict = (sj < ti, sj > ti)
    m_incl = (sj <= ti, sj >= ti)
    eye_cat = jnp.where(sj == ti, 1.0, 0.0)
    bi = lax.broadcasted_iota(jnp.int32, (LANES, LANES), 0)
    bj = lax.broadcasted_iota(jnp.int32, (LANES, LANES), 1)
    blk = (bi >> HEAD_SHIFT) == (bj >> HEAD_SHIFT)
    eye_bd = bi == bj
    zero_bd = jnp.zeros((LANES, LANES), BF16)

    def op(i, u):
        d, p, c = u
        return ops[d][i][c * CHUNK:(c + 1) * CHUNK, p * LANES:(p + 1) * LANES]

    g_mat, w_mat, p_t, q_t = {}, {}, {}, {}

    def local_stages(units):
        low, nak, mrbk = {}, {}, {}
        for u in units:
            lhs = jnp.concatenate([op(OP_A, u), op(OP_R, u)], axis=0)
            rhs = jnp.concatenate([_stack(op(OP_B, u)), _stack(op(OP_K, u))], axis=0)
            gram = _bdot(lhs, rhs, NT)
            d = u[0]
            low[u] = jnp.where(m_strict[d], gram[:CHUNK, :LANES], 0.0)
            nak[u] = jnp.where(m_strict[d], gram[:CHUNK, LANES:], 0.0).astype(BF16)
            mrbk[u] = jnp.concatenate([jnp.where(m_incl[d], gram[CHUNK:, :LANES], 0.0),
                                       jnp.where(m_incl[d], gram[CHUNK:, LANES:], 0.0)], axis=1).astype(BF16)

        inv = {u: eye_cat + low[u] for u in units}
        pwb = {u: low[u].astype(BF16) for u in units}
        for u in units:
            pwb[u] = _bdot(pwb[u], _stack(pwb[u])).astype(BF16)
        for _ in range(4):
            for u in units:
                both = _bdot(pwb[u], jnp.concatenate([_stack(pwb[u]), _stack(inv[u].astype(BF16))], axis=1))
                pwb[u] = both[:, :LANES].astype(BF16)
                inv[u] = inv[u] + both[:, LANES:]
        for u in units:
            inv[u] = inv[u] + _bdot(pwb[u], _stack(inv[u].astype(BF16)))

        sv = {u: _stack(op(OP_V, u)) for u in units}
        nv = {}
        for u in units:
            nv[u] = _bdot(nak[u], sv[u]).astype(BF16)
        x1, x2 = {}, {}
        for u in units:
            x12 = _bdot(inv[u].astype(BF16), jnp.concatenate([_stack(op(OP_A, u)), _stack(nv[u])], axis=1))
            x1[u] = x12[:, :LANES].astype(BF16)
            x2[u] = x12[:, LANES:].astype(BF16)
        for u in units:
            rhs = jnp.concatenate([jnp.concatenate([_stack(x1[u]), _stack(x2[u])], axis=1),
                                   jnp.concatenate([zero_bd, sv[u]], axis=1)], axis=0)
            gw = _bdot(mrbk[u], rhs)
            g_mat[u] = (op(OP_R, u).astype(F32) + gw[:, :LANES]).astype(BF16)
            w_mat[u] = gw[:, LANES:]
        for u in units:
            d, p, c = u
            ct = c % N_CHUNK
            gam = jnp.exp(ends_ref[d][c // N_CHUNK, ct:ct + 1, p * LANES:(p + 1) * LANES])
            p_t[u] = (jnp.where(blk, _bdot(x1[u], op(OP_BH, u), TN), 0.0)
                      + jnp.where(eye_bd, gam, 0.0)).astype(BF16)
            q_t[u] = jnp.where(blk, _bdot(jnp.concatenate([x2[u], op(OP_V, u)], axis=0),
                                          jnp.concatenate([op(OP_BH, u), op(OP_KH, u)], axis=0), TN), 0.0)

    n_c = SCAN_TILES * N_CHUNK
    state = {(d, p): jnp.where(is_lat, st_ref[d, p], 0.0) for d in (0, 1) for p in range(N_PAIR)}
    finals = {}
    y_refs = (yf_ref, yb_ref)

    def chunk_of(d, step):
        return step if d == 0 else n_c - 1 - step

    def chain(step, d, p):
        c = chunk_of(d, step)
        u = (d, p, c)
        sb = state[d, p].astype(BF16)
        y_refs[d][c * CHUNK:(c + 1) * CHUNK, p * LANES:(p + 1) * LANES] = _bdot(g_mat[u], sb, NT) + w_mat[u]
        state[d, p] = _bdot(sb, p_t[u]) + q_t[u]
        if (step + 1) % N_CHUNK == 0:
            finals[d, c // N_CHUNK, p] = state[d, p]
            if step + 1 < n_c:
                state[d, p] = jnp.where(is_lat, state[d, p], 0.0)

    local_stages([(d, p, c) for d in (0, 1) for p in range(N_PAIR) for c in range(n_c)])
    for step in range(n_c):
        for d in (0, 1):
            for p in range(N_PAIR):
                chain(step, d, p)
    for (d, p), s in state.items():
        st_ref[d, p] = s

    @pl.when(jnp.logical_not(is_lat))
    def _():
        for (d, tile, p), s in finals.items():
            sout_ref[tile, 0, d, 2 * p] = s[:HEAD, :HEAD]
            sout_ref[tile, 0, d, 2 * p + 1] = s[HEAD:, HEAD:]


def _scan(ops_f, ops_b, v, ends_f, ends_b, s0, w13, w2):
    def mirror(s):
        u = s - N_SCAN_CTX_STEPS
        return jnp.where(s < N_SCAN_CTX_STEPS, s,
                         N_SCAN_CTX_STEPS + (u // SCAN_LAT_STEPS) * SCAN_LAT_STEPS
                         + (SCAN_LAT_STEPS - 1 - u % SCAN_LAT_STEPS))

    def lat_seq(s):
        return jnp.maximum(s - N_SCAN_CTX_STEPS, 0) // SCAN_LAT_STEPS

    fwd = pl.BlockSpec((SCAN_TM, D_RWKV), lambda s: (s, 0))
    bwd = pl.BlockSpec((SCAN_TM, D_RWKV), lambda s: (mirror(s), 0))
    e_block = (SCAN_TILES, N_CHUNK, D_RWKV)
    st_block = (None, 1, 2, 2 * N_PAIR, HEAD, HEAD)
    out_y = jax.ShapeDtypeStruct((N_TOK, D_RWKV), F32)
    n_steps = N_TOK // SCAN_TM
    return pl.pallas_call(
        _scan_kernel,
        grid=(n_steps,),
        in_specs=[fwd] * (N_DIR_OPS + 1) + [bwd] * (N_DIR_OPS + 1)
                 + [pl.BlockSpec(e_block, lambda s: (s, 0, 0)),
                    pl.BlockSpec(e_block, lambda s: (mirror(s), 0, 0)),
                    pl.BlockSpec(st_block, lambda s: (lat_seq(s), 0, 0, 0, 0, 0)),
                    _slab_spec(w13, n_steps),
                    _slab_spec(w2, n_steps)],
        out_specs=[fwd, bwd,
                   pl.BlockSpec((SCAN_TILES, 1, 2, 2 * N_PAIR, HEAD, HEAD),
                                lambda s: (jnp.minimum(s, N_SCAN_CTX_STEPS - 1), 0, 0, 0, 0, 0)),
                   _slab_spec(w13, n_steps),
                   _slab_spec(w2, n_steps)],
        out_shape=[out_y, out_y,
                   jax.ShapeDtypeStruct((N_CTX_SEQ, 1, 2, 2 * N_PAIR, HEAD, HEAD), F32),
                   jax.ShapeDtypeStruct(w13.shape, BF16),
                   jax.ShapeDtypeStruct(w2.shape, BF16)],
        scratch_shapes=[pltpu.VMEM((2, N_PAIR, LANES, LANES), F32)],
        compiler_params=_params(),
        name="rwkv7_scan",
    )(*ops_f, v, *ops_b, v, ends_f, ends_b, s0, w13, w2)


def _back_kernel(x_ref, yf_ref, yb_ref, bv_ref, sg_ref, sa_ref, mb_ref, mod_ref, g_ref,
                 gng_ref, gnb_ref, wba_f32_ref, wout_f32_ref, ones_ref, o_ref, wba_ref, wout_ref):
    @pl.when(pl.program_id(0) == 0)
    def _():
        wba_ref[...] = wba_f32_ref[...].astype(BF16)
        wout_ref[...] = wout_f32_ref[...].astype(BF16)

    ones = ones_ref[...]
    groups = [slice(i * TM, (i + 1) * TM) for i in range(BACK_TM // TM)]
    ys = [yf_ref[r, :] + yb_ref[r, :] for r in groups]
    ycs = [y - _head_sum(y, ones) * (1.0 / HEAD) for y in ys]
    vs = [_head_sum(yc * yc, ones) * (1.0 / HEAD) for yc in ycs]
    yns = [yc * lax.rsqrt(v + EPS_GN) * gng_ref[...] + gnb_ref[...] for yc, v in zip(ycs, vs)]
    yas = [_dot((yn + bv_ref[r, :]) * sg_ref[r, :], wba_ref[...]) for r, yn in zip(groups, yns)]
    outs = [_dot(sa_ref[r, :] * ya + mb_ref[r, :], wout_ref[...]) for r, ya in zip(groups, yas)]
    for r, out in zip(groups, outs):
        o_ref[r, :] = x_ref[r, :] + mod_ref[MOD_BACK:MOD_BACK + 1, :] * _rms(out, g_ref[3:4, :])


def _back(x, yf, yb, bv, sg, sa, mb, mod3, norm_g, gng, gnb, wba, wout, ones):
    tok = lambda t: (t, 0)
    spec_r = pl.BlockSpec((BACK_TM, D_RWKV), tok)
    spec_d = pl.BlockSpec((BACK_TM, D_MODEL), tok)
    consts = [norm_g, gng, gnb, wba, wout, ones]
    return pl.pallas_call(
        _back_kernel,
        grid=(N_TOK // BACK_TM,),
        in_specs=[spec_d, spec_r, spec_r, spec_r, spec_r, spec_d, spec_d,
                  pl.BlockSpec((None, MOD_TAIL, D_MODEL), lambda t: (_mod_row(t * (BACK_TM // TM)), 0, 0))]
                 + [_const_spec(c.shape) for c in consts],
        out_specs=spec_d,
        out_shape=jax.ShapeDtypeStruct((N_TOK, D_MODEL), F32),
        scratch_shapes=[pltpu.VMEM(wba.shape, BF16), pltpu.VMEM(wout.shape, BF16)],
        compiler_params=_params(),
        name="mixer_back",
    )(x, yf, yb, bv, sg, sa, mb, mod3, *consts)


def kernel(x_prompt, x_sample, c, state_rwkv, c_ctx, w_mod, b_mod, norm_g, ffn1_w13, ffn1_w2,
           ffn2_w13, ffn2_w2, w_in, mu_shift, decay_w0, decay_w1, decay_w2, iclr_a0, iclr_a1,
           iclr_a2, k_k, k_a, r_k, gn_gain, gn_bias, conv_w, conv_b, w_branch_a, w_branch_b, w_out):
    assert x_prompt.shape == (N_CTX_SEQ, CTX_LEN, D_MODEL) and x_sample.shape == (N_LAT_SEQ, LAT_LEN, D_MODEL)
    assert w_mod.shape[0] == 1, "single trunk layer"

    cvec = jnp.concatenate([c_ctx[None, :], c, jnp.zeros((MOD_ROWS - 1 - N_LAT_SEQ, D_MODEL), F32)], axis=0)
    mod_head = _modulation(cvec, w_mod[0], b_mod).reshape(MOD_ROWS, MOD_HEAD, D_MODEL)
    g = norm_g[0]

    x, w_in_bf, mod_tail = _ffn_first(x_prompt.reshape(-1, D_MODEL), x_sample.reshape(-1, D_MODEL),
                                      mod_head, g, ffn1_w13[0], ffn1_w2[0], w_in[0], cvec, w_mod[0], b_mod)
    mod3 = mod_tail.reshape(MOD_ROWS, MOD_TAIL, D_MODEL)

    row = lambda p: p.reshape(1, -1)
    mu4 = mu_shift[0].reshape(4, D_MODEL)
    w1_cat = jnp.concatenate([decay_w1[0, 0], iclr_a1[0, 0], decay_w1[0, 1], iclr_a1[0, 1]], axis=1)
    w_aug = _lora_prep(jnp.repeat(mu4.T, LORA, axis=1), w1_cat)
    w2_blocks = [decay_w2[0, 0], iclr_a2[0, 0], decay_w2[0, 1], iclr_a2[0, 1]]
    w2_zero = jnp.zeros((LORA, D_RWKV), F32)
    w2_aug = jnp.concatenate(
        [jnp.concatenate([blk if j == i else w2_zero for j in range(4)], axis=1)
         for i, blk in enumerate(w2_blocks)], axis=0).astype(BF16)
    b2_aug = jnp.concatenate([decay_w0[0, 0], iclr_a0[0, 0], decay_w0[0, 1], iclr_a0[0, 1]]).reshape(1, -1)
    ones = _head_ones()
    front = _front(x, mod3, g, w_in_bf, w_aug,
                   [w2_aug, b2_aug, row(k_k[0]), row(k_a[0]), row(r_k[0]), conv_w[0], row(conv_b[0]),
                    w_branch_b[0], ones])
    ops_f, ops_b = front[0:N_DIR_OPS], front[N_DIR_OPS:2 * N_DIR_OPS]
    v, ends_f, ends_b, bv, sg, sa, mb = front[2 * N_DIR_OPS:]

    yf, yb, s_fin, w13_bf, w2_bf = _scan(ops_f, ops_b, v, ends_f, ends_b, state_rwkv,
                                         ffn2_w13[0], ffn2_w2[0])

    x = _back(x, yf, yb, bv, sg, sa, mb, mod3, g, row(gn_gain[0]), row(gn_bias[0]),
              w_branch_a[0], w_out[0], ones)
    y_ctx, y_lat = _ffn_last(x, mod3, g, w13_bf, w2_bf)

    y_prompt = y_ctx.reshape(N_CTX_SEQ, CTX_LEN, D_MODEL)
    y_sample = y_lat.reshape(N_LAT_SEQ, LAT_LEN, D_MODEL)
    return y_prompt, y_sample, s_fin
```

```python
import jax
import jax.numpy as jnp
from jax import lax
from jax.experimental import pallas as pl
from jax.experimental.pallas import tpu as pltpu

F32 = jnp.float32
BF16 = jnp.bfloat16

D_MODEL = 1024
D_FF = 2816
D_RWKV = 512
D_CONV = 512
HEAD = 64
HEAD_SHIFT = 6
D_IN = 4 * D_RWKV + 3 * D_CONV + 2 * D_MODEL
N_MOD = 9
EPS_RMS = 1e-6
EPS_GN = 64e-5
HALF_STEP = 0.5
EXP_M05 = 0.6065306597126334

N_CTX_SEQ = 16
CTX_LEN = 256
N_LAT_SEQ = 2
LAT_LEN = 2048
GRID_W = 64
N_TOK = N_CTX_SEQ * CTX_LEN + N_LAT_SEQ * LAT_LEN

TM = 256
N_CTX_TILES = N_CTX_SEQ * CTX_LEN // TM
LAT_TILES = LAT_LEN // TM
N_TILES = N_TOK // TM
FRONT_TM = 512
FRONT_GROUPS = FRONT_TM // TM
N_FRONT_CTX_TILES = N_CTX_SEQ * CTX_LEN // FRONT_TM
FRONT_LAT_TILES = LAT_LEN // FRONT_TM
N_FRONT_CONSTS = 9
FRONT_WBB = 7
LORA = 64
SCAN_TM = 512
SCAN_TILES = SCAN_TM // TM
N_SCAN_CTX_STEPS = N_CTX_SEQ * CTX_LEN // SCAN_TM
SCAN_LAT_STEPS = LAT_LEN // SCAN_TM
BACK_TM = 512
FFN_TM = 512
N_FFN_TILES = N_TOK // FFN_TM
N_FFN_CTX_TILES = N_CTX_SEQ * CTX_LEN // FFN_TM
FFN2_TM = 1024
N_FFN2_CTX_TILES = N_CTX_SEQ * CTX_LEN // FFN2_TM
N_WCHUNK = 11
CHUNK = 64
N_CHUNK = TM // CHUNK
LANES = 128
N_PAIR = D_RWKV // LANES
MOD_ROWS = 8
MOD_HEAD = 3
MOD_TAIL = N_MOD - MOD_HEAD
MOD_SIDE = 512
N_MOD_SIDE = MOD_TAIL * D_MODEL // MOD_SIDE
MOD_FRONT, MOD_BACK, MOD_FFN2 = 0, 2, 3
VMEM_LIMIT = 56 * 1024 * 1024

OP_A, OP_R, OP_B, OP_K, OP_BH, OP_KH, OP_V = range(7)
N_DIR_OPS = 6

NN = (((1,), (0,)), ((), ()))
NT = (((1,), (1,)), ((), ()))
TN = (((0,), (0,)), ((), ()))


def _dot(a, b, dims=NN):
    return lax.dot_general(a.astype(BF16), b.astype(BF16), dims, preferred_element_type=F32)


def _bdot(a, b, dims=NN):
    return lax.dot_general(a, b, dims, preferred_element_type=F32)


def _rms(x, g):
    ms = jnp.mean(x * x, axis=-1, keepdims=True)
    return x * lax.rsqrt(ms + EPS_RMS) * g


def _head_ones():
    i = lax.broadcasted_iota(jnp.int32, (D_RWKV, D_RWKV), 0) >> HEAD_SHIFT
    j = lax.broadcasted_iota(jnp.int32, (D_RWKV, D_RWKV), 1) >> HEAD_SHIFT
    return jnp.where(i == j, 1.0, 0.0).astype(BF16)


def _head_sum(x, ones):
    return jnp.dot(x.astype(BF16), ones, preferred_element_type=F32)


def _sigmoid(x):
    return 0.5 * jnp.tanh(0.5 * x) + 0.5


def _mod_row(t):
    return jnp.where(t < N_CTX_TILES, 0, 1 + (t - N_CTX_TILES) // LAT_TILES)


def _const_spec(shape):
    nd = len(shape)
    return pl.BlockSpec(shape, lambda *_: (0,) * nd, pipeline_mode=pl.Buffered(1))


def _params(n_axes=1):
    return pltpu.CompilerParams(dimension_semantics=("arbitrary",) * n_axes,
                                vmem_limit_bytes=VMEM_LIMIT)


def _mod_kernel(c_ref, w_ref, b_ref, o_ref):
    c = c_ref[...]
    s = c * jax.nn.sigmoid(c)
    o_ref[...] = _dot(s, w_ref[...]) + b_ref[...]


def _modulation(cvec, w_mod, b_mod):
    width = MOD_HEAD * D_MODEL
    return pl.pallas_call(
        _mod_kernel,
        grid=(1,),
        in_specs=[pl.BlockSpec((MOD_ROWS, D_MODEL), lambda j: (0, 0)),
                  pl.BlockSpec((D_MODEL, width), lambda j: (0, 0)),
                  pl.BlockSpec((1, width), lambda j: (0, 0))],
        out_specs=pl.BlockSpec((MOD_ROWS, width), lambda j: (0, 0)),
        out_shape=jax.ShapeDtypeStruct((MOD_ROWS, width), F32),
        compiler_params=_params(),
        name="modulation",
    )(cvec, w_mod, b_mod)


def _ffn_body(x, mod_ref, g_ref, w13_ref, w2_ref, im, ig):
    shift = mod_ref[im:im + 1, :]
    scale = mod_ref[im + 1:im + 2, :]
    gate = mod_ref[im + 2:im + 3, :]
    xs = [x[i * TM:(i + 1) * TM, :] for i in range(x.shape[0] // TM)]
    hs = [(_rms(xi, g_ref[ig:ig + 1, :]) * (1.0 + scale) + shift).astype(BF16) for xi in xs]
    gus = [_dot(hi, w13_ref[...]) for hi in hs]
    acts = []
    for gu in gus:
        gt = gu[:, :D_FF]
        up = gu[:, D_FF:]
        acts.append((gt * jax.nn.sigmoid(gt) * up).astype(BF16))
    os_ = [_dot(ai, w2_ref[...]) for ai in acts]
    outs = [xi + HALF_STEP * gate * _rms(oi, g_ref[ig + 1:ig + 2, :]) for xi, oi in zip(xs, os_)]
    return jnp.concatenate(outs, axis=0)


def _cast_chunk(step, src_ref, dst_ref, axis):
    size = src_ref.shape[axis]

    @pl.when(step < N_WCHUNK)
    def _():
        start = pl.multiple_of(step * size, size)
        if axis == 0:
            dst_ref[pl.ds(start, size), :] = src_ref[...].astype(BF16)
        else:
            dst_ref[:, pl.ds(start, size)] = src_ref[...].astype(BF16)


def _chunk_spec(w, axis):
    block = list(w.shape)
    block[axis] = w.shape[axis] // N_WCHUNK
    clamp = lambda s: jnp.minimum(s, N_WCHUNK - 1)
    index = (lambda s: (clamp(s), 0)) if axis == 0 else (lambda s: (0, clamp(s)))
    return pl.BlockSpec(tuple(block), index)


def _tile_step(s):
    return jnp.maximum(s - N_WCHUNK, 0)


def _slab_spec(w, n_steps, step=lambda s: s):
    rows = w.shape[0] // n_steps
    return pl.BlockSpec((rows, w.shape[1]), lambda s: (step(s), 0))


def _ffn_first_kernel(xc_ref, xl_ref, mod_ref, g_ref, w13_ref, w2_ref, win_ref, c_ref, wmod_ref, bmod_ref,
                      o_ref, win_bf_ref, mod_tail_ref, w13_bf, w2_bf):
    s = pl.program_id(0)
    _cast_chunk(s, w13_ref, w13_bf, 1)
    _cast_chunk(s, w2_ref, w2_bf, 0)

    @pl.when(s >= N_WCHUNK)
    def _():
        x = jnp.where(s - N_WCHUNK < N_FFN_CTX_TILES, xc_ref[...], xl_ref[...])
        o_ref[...] = _ffn_body(x, mod_ref, g_ref, w13_bf, w2_bf, 0, 0)
        win_bf_ref[...] = win_ref[...].astype(BF16)

        @pl.when(s - N_WCHUNK < N_MOD_SIDE)
        def _():
            _mod_kernel(c_ref, wmod_ref, bmod_ref, mod_tail_ref)


def _ffn_last_kernel(x_ref, mod_ref, g_ref, w13_ref, w2_ref, oc_ref, ol_ref):
    out = _ffn_body(x_ref[...], mod_ref, g_ref, w13_ref, w2_ref, MOD_FFN2, 4)
    t = pl.program_id(0)

    @pl.when(t < N_FFN2_CTX_TILES)
    def _():
        oc_ref[...] = out

    @pl.when(t >= N_FFN2_CTX_TILES)
    def _():
        ol_ref[...] = out


def _ctx_tile(t):
    return jnp.minimum(t, N_FFN_CTX_TILES - 1), 0


def _lat_tile(t):
    return jnp.maximum(t - N_FFN_CTX_TILES, 0), 0


def _ffn_mod(t):
    return _mod_row(t * (FFN_TM // TM)), 0, 0


def _ffn_first(x_ctx, x_lat, mod_head, norm_g, w13, w2, w_in, cvec, w_mod, b_mod):
    side = lambda s: jnp.minimum(_tile_step(s), N_MOD_SIDE - 1)
    tail_col = MOD_HEAD * D_MODEL // MOD_SIDE
    return pl.pallas_call(
        _ffn_first_kernel,
        grid=(N_WCHUNK + N_FFN_TILES,),
        in_specs=[pl.BlockSpec((FFN_TM, D_MODEL), lambda s: _ctx_tile(_tile_step(s))),
                  pl.BlockSpec((FFN_TM, D_MODEL), lambda s: _lat_tile(_tile_step(s))),
                  pl.BlockSpec((None, MOD_HEAD, D_MODEL), lambda s: _ffn_mod(_tile_step(s))),
                  _const_spec(norm_g.shape),
                  _chunk_spec(w13, 1),
                  _chunk_spec(w2, 0),
                  _slab_spec(w_in, N_FFN_TILES, step=_tile_step),
                  _const_spec(cvec.shape),
                  pl.BlockSpec((D_MODEL, MOD_SIDE), lambda s: (0, tail_col + side(s))),
                  pl.BlockSpec((1, MOD_SIDE), lambda s: (0, tail_col + side(s)))],
        out_specs=[pl.BlockSpec((FFN_TM, D_MODEL), lambda s: (_tile_step(s), 0)),
                   _slab_spec(w_in, N_FFN_TILES, step=_tile_step),
                   pl.BlockSpec((MOD_ROWS, MOD_SIDE), lambda s: (0, side(s)))],
        out_shape=[jax.ShapeDtypeStruct((N_TOK, D_MODEL), F32),
                   jax.ShapeDtypeStruct(w_in.shape, BF16),
                   jax.ShapeDtypeStruct((MOD_ROWS, MOD_TAIL * D_MODEL), F32)],
        scratch_shapes=[pltpu.VMEM((D_MODEL, 2 * D_FF), BF16), pltpu.VMEM((D_FF, D_MODEL), BF16)],
        compiler_params=_params(),
        name="ffn1",
    )(x_ctx, x_lat, mod_head, norm_g, w13, w2, w_in, cvec, w_mod, b_mod)


def _ffn_last(x, mod3, norm_g, w13_bf, w2_bf):
    n_ctx = N_CTX_SEQ * CTX_LEN
    return pl.pallas_call(
        _ffn_last_kernel,
        grid=(N_TOK // FFN2_TM,),
        in_specs=[pl.BlockSpec((FFN2_TM, D_MODEL), lambda t: (t, 0)),
                  pl.BlockSpec((None, MOD_TAIL, D_MODEL), lambda t: (_mod_row(t * (FFN2_TM // TM)), 0, 0)),
                  _const_spec(norm_g.shape),
                  _const_spec(w13_bf.shape),
                  _const_spec(w2_bf.shape)],
        out_specs=[pl.BlockSpec((FFN2_TM, D_MODEL), lambda t: (jnp.minimum(t, N_FFN2_CTX_TILES - 1), 0)),
                   pl.BlockSpec((FFN2_TM, D_MODEL), lambda t: (jnp.maximum(t - N_FFN2_CTX_TILES, 0), 0))],
        out_shape=[jax.ShapeDtypeStruct((n_ctx, D_MODEL), F32),
                   jax.ShapeDtypeStruct((N_TOK - n_ctx, D_MODEL), F32)],
        compiler_params=_params(),
        name="ffn2",
    )(x, mod3, norm_g, w13_bf, w2_bf)


def _lora_prep_kernel(mu_ref, w1_ref, o_ref):
    mu = mu_ref[...]
    w1 = w1_ref[...]
    n = w1.shape[1]
    o_ref[:, :n] = ((1.0 - mu) * w1).astype(BF16)
    o_ref[:, n:] = (mu * w1).astype(BF16)


def _lora_prep(mu_cols, w1_cat):
    n = w1_cat.shape[1]
    return pl.pallas_call(
        _lora_prep_kernel,
        out_shape=jax.ShapeDtypeStruct((D_MODEL, 2 * n), BF16),
        compiler_params=pltpu.CompilerParams(vmem_limit_bytes=VMEM_LIMIT),
        name="lora_prep",
    )(mu_cols, w1_cat)


def _front_group(gi, h, halo_prev, halo_next, is_lat, win_ref, waug_ref, refs, outs):
    w2aug_ref, b2aug_ref, kkw_ref, ka_ref, rk_ref, cw_ref, cbias_ref, wbb_ref, ones_ref = refs
    ops_o, v_o, ends_o, bv_o, sg_o, sa_o, mb_o = outs
    rows = slice(gi * TM, (gi + 1) * TM)
    row = lax.broadcasted_iota(jnp.int32, (TM, 1), 0)
    rin = row & (CHUNK - 1)
    ones = ones_ref[...]

    def proj(lo, hi):
        return _bdot(h, win_ref[:, lo:hi])

    pab = _bdot(h, waug_ref[...])
    rk = proj(0, 2 * D_RWKV)
    r = rk[:, :D_RWKV]
    k = rk[:, D_RWKV:]
    pb = pab[:, 2 * LANES:]
    sh_f = jnp.where(row == 0, halo_prev, pltpu.roll(pb[:, :LANES], 1, 0))
    sh_b = jnp.where(row == TM - 1, halo_next, pltpu.roll(pb[:, LANES:], TM - 1, 0))
    t_in = pab[:, :2 * LANES] + jnp.concatenate([sh_f, sh_b], axis=1)
    lane = lax.broadcasted_iota(jnp.int32, (1, 2 * LANES), 1)
    t_in = jnp.where((lane & HEAD) == 0, jnp.tanh(t_in), t_in)
    za = _dot(t_in, w2aug_ref[...]) + b2aug_ref[...]

    kk = k * kkw_ref[...]
    kkn = kk * lax.rsqrt(_head_sum(kk * kk, ones) + 1e-12)
    ka = ka_ref[...]

    vg = proj(2 * D_RWKV, 4 * D_RWKV)
    v = vg[:, :D_RWKV]
    v_o[rows, :] = v.astype(BF16)
    sg_o[rows, :] = _sigmoid(vg[:, D_RWKV:]).astype(BF16)

    def scan_operands(d):
        lw = -EXP_M05 * _sigmoid(za[:, 2 * d * D_RWKV:(2 * d + 1) * D_RWKV])
        a = _sigmoid(za[:, (2 * d + 1) * D_RWKV:(2 * d + 2) * D_RWKV])
        k_d = k * (1.0 + (a - 1.0) * ka)
        b = kkn * a
        cs = lw
        for s in (1, 2, 4, 8, 16, 32):
            if d == 0:
                cs = cs + jnp.where(rin >= s, pltpu.roll(cs, s, 0), 0.0)
            else:
                cs = cs + jnp.where(rin < CHUNK - s, pltpu.roll(cs, TM - s, 0), 0.0)
        end_row = CHUNK - 1 if d == 0 else 0
        ends = [cs[c * CHUNK + end_row:c * CHUNK + end_row + 1, :] for c in range(N_CHUNK)]
        for c in range(N_CHUNK):
            ends_o[d][gi, c:c + 1, :] = ends[c]
        cs_end = jnp.concatenate([jnp.broadcast_to(e, (CHUNK, D_RWKV)) for e in ends], axis=0)
        dec_inv = jnp.exp(-cs)
        dec_rest = jnp.exp(cs_end - cs)
        o_a, o_r, o_b, o_k, o_bh, o_kh = ops_o[d]
        o_a[rows, :] = (-kkn * jnp.exp(cs - lw)).astype(BF16)
        o_r[rows, :] = (r * jnp.exp(cs)).astype(BF16)
        o_b[rows, :] = (b * dec_inv).astype(BF16)
        o_k[rows, :] = (k_d * dec_inv).astype(BF16)
        o_bh[rows, :] = (b * dec_rest).astype(BF16)
        o_kh[rows, :] = (k_d * dec_rest).astype(BF16)
        return k_d

    conv_in = proj(4 * D_RWKV, 4 * D_RWKV + 3 * D_CONV)
    k_0 = scan_operands(0)
    gate_a = proj(4 * D_RWKV + 3 * D_CONV, 4 * D_RWKV + 3 * D_CONV + D_MODEL)
    sa_o[rows, :] = _sigmoid(gate_a).astype(BF16)
    k_1 = scan_operands(1)
    gate_b = proj(4 * D_RWKV + 3 * D_CONV + D_MODEL, D_IN)

    cgate = conv_in[:, :D_CONV]
    u = conv_in[:, D_CONV:2 * D_CONV] * conv_in[:, 2 * D_CONV:]
    col = row & (GRID_W - 1)
    zl = jnp.logical_or(row == 0, jnp.logical_and(is_lat, col == 0))
    zr = jnp.logical_or(row == TM - 1, jnp.logical_and(is_lat, col == GRID_W - 1))
    left = jnp.where(zl, 0.0, pltpu.roll(u, 1, 0))
    right = jnp.where(zr, 0.0, pltpu.roll(u, TM - 1, 0))
    conv = left * cw_ref[0:1, :] + u * cw_ref[1:2, :] + right * cw_ref[2:3, :] + cbias_ref[...]
    y_b = _dot(cgate * conv, wbb_ref[...])
    mb_o[rows, :] = (_sigmoid(gate_b) * y_b).astype(BF16)

    bv_o[rows, :] = (_head_sum(r * (k_0 + k_1) * rk_ref[...], ones) * v).astype(BF16)


def _front_kernel(x_ref, xp_ref, xn_ref, mod_ref, g_ref, win_ref, waug_ref, *rest):
    refs = list(rest[:N_FRONT_CONSTS])
    outs = rest[N_FRONT_CONSTS:-1]
    wbb_bf = rest[-1]
    ops_o = (outs[0:N_DIR_OPS], outs[N_DIR_OPS:2 * N_DIR_OPS])
    v_o, ends0_o, ends1_o, bv_o, sg_o, sa_o, mb_o = outs[2 * N_DIR_OPS:]
    outs = (ops_o, v_o, (ends0_o, ends1_o), bv_o, sg_o, sa_o, mb_o)

    t = pl.program_id(0)

    @pl.when(t == 0)
    def _():
        wbb_bf[...] = refs[FRONT_WBB][...].astype(BF16)

    refs[FRONT_WBB] = wbb_bf

    is_lat = t >= N_FRONT_CTX_TILES
    i_in = (t - N_FRONT_CTX_TILES) % FRONT_LAT_TILES
    lat_first = jnp.logical_and(is_lat, i_in == 0)
    lat_last = jnp.logical_and(is_lat, i_in == FRONT_LAT_TILES - 1)
    shift = mod_ref[MOD_FRONT:MOD_FRONT + 1, :]
    scale = mod_ref[MOD_FRONT + 1:MOD_FRONT + 2, :]
    g2 = g_ref[2:3, :]

    def pre(x):
        return (_rms(x, g2) * (1.0 + scale) + shift).astype(BF16)

    x = x_ref[...]
    edge = jnp.concatenate([xp_ref[...], x[TM - 8:TM + 8, :], xn_ref[...]], axis=0)
    edge_b = _bdot(pre(edge), waug_ref[:, 2 * LANES:])
    zero = jnp.zeros((1, LANES), F32)
    halo_prev = (jnp.where(jnp.logical_and(is_lat, jnp.logical_not(lat_first)), edge_b[7:8, :LANES], zero),
                 jnp.where(is_lat, edge_b[15:16, :LANES], zero))
    halo_next = (jnp.where(is_lat, edge_b[16:17, LANES:], zero),
                 jnp.where(jnp.logical_and(is_lat, jnp.logical_not(lat_last)), edge_b[24:25, LANES:], zero))

    for gi in range(FRONT_GROUPS):
        h = pre(x[gi * TM:(gi + 1) * TM, :])
        _front_group(gi, h, halo_prev[gi], halo_next[gi], is_lat, win_ref, waug_ref, refs, outs)


def _front(x, mod3, norm_g, w_in_bf, w_aug, consts):
    tok = lambda t: (t, 0)
    rows8 = FRONT_TM // 8
    last8 = N_TOK // 8 - 1
    assert len(consts) == N_FRONT_CONSTS
    out_rb = jax.ShapeDtypeStruct((N_TOK, D_RWKV), BF16)
    out_db = jax.ShapeDtypeStruct((N_TOK, D_MODEL), BF16)
    out_e = jax.ShapeDtypeStruct((N_TILES, N_CHUNK, D_RWKV), F32)
    spec_r = pl.BlockSpec((FRONT_TM, D_RWKV), tok)
    spec_d = pl.BlockSpec((FRONT_TM, D_MODEL), tok)
    spec_e = pl.BlockSpec((FRONT_GROUPS, N_CHUNK, D_RWKV), lambda t: (t, 0, 0))
    n_b = 2 * N_DIR_OPS + 1
    return pl.pallas_call(
        _front_kernel,
        grid=(N_TOK // FRONT_TM,),
        in_specs=[pl.BlockSpec((FRONT_TM, D_MODEL), tok),
                  pl.BlockSpec((8, D_MODEL), lambda t: (jnp.maximum(t * rows8 - 1, 0), 0)),
                  pl.BlockSpec((8, D_MODEL), lambda t: (jnp.minimum((t + 1) * rows8, last8), 0)),
                  pl.BlockSpec((None, MOD_TAIL, D_MODEL), lambda t: (_mod_row(t * FRONT_GROUPS), 0, 0))]
                 + [_const_spec(c.shape) for c in [norm_g, w_in_bf, w_aug] + list(consts)],
        out_specs=[spec_r] * n_b + [spec_e] * 2 + [spec_r] * 2 + [spec_d] * 2,
        out_shape=[out_rb] * n_b + [out_e] * 2 + [out_rb] * 2 + [out_db] * 2,
        scratch_shapes=[pltpu.VMEM((D_CONV, D_MODEL), BF16)],
        compiler_params=_params(),
        name="mixer_front",
    )(x, x, x, mod3, norm_g, w_in_bf, w_aug, *consts)


def _stack(x):
    lane_lo = lax.broadcasted_iota(jnp.int32, x.shape, 1) < HEAD
    z = jnp.zeros_like(x)
    return jnp.concatenate([jnp.where(lane_lo, x, z), jnp.where(lane_lo, z, x)], axis=0)


def _scan_kernel(*refs):
    ops = (refs[0:N_DIR_OPS + 1], refs[N_DIR_OPS + 1:2 * N_DIR_OPS + 2])
    (ends0_ref, ends1_ref, s0_ref, w13_ref, w2_ref,
     yf_ref, yb_ref, sout_ref, w13_bf_ref, w2_bf_ref, st_ref) = refs[2 * N_DIR_OPS + 2:]
    ends_ref = (ends0_ref, ends1_ref)
    w13_bf_ref[...] = w13_ref[...].astype(BF16)
    w2_bf_ref[...] = w2_ref[...].astype(BF16)
    step_id = pl.program_id(0)
    is_lat = step_id >= N_SCAN_CTX_STEPS
    lat_first = (step_id - N_SCAN_CTX_STEPS) % SCAN_LAT_STEPS == 0

    @pl.when(jnp.logical_and(is_lat, lat_first))
    def _():
        zero = jnp.zeros((HEAD, HEAD), F32)
        for d in (0, 1):
            for p in range(N_PAIR):
                top = jnp.concatenate([s0_ref[0, d, 2 * p], zero], axis=1)
                bottom = jnp.concatenate([zero, s0_ref[0, d, 2 * p + 1]], axis=1)
                st_ref[d, p] = jnp.concatenate([top, bottom], axis=0)

    ti = lax.broadcasted_iota(jnp.int32, (CHUNK, LANES), 0)
    sj = lax.broadcasted_iota(jnp.int32, (CHUNK, LANES), 1) & (CHUNK - 1)
    m_strict = (sj < ti, sj > ti)
    m_incl = (sj <= ti, sj >= ti)
    eye_cat = jnp.where(sj == ti, 1.0, 0.0)
    bi = lax.broadcasted_iota(jnp.int32, (LANES, LANES), 0)
    bj = lax.broadcasted_iota(jnp.int32, (LANES, LANES), 1)
    blk = (bi >> HEAD_SHIFT) == (bj >> HEAD_SHIFT)
    eye_bd = bi == bj
    zero_bd = jnp.zeros((LANES, LANES), BF16)

    def op(i, u):
        d, p, c = u
        return ops[d][i][c * CHUNK:(c + 1) * CHUNK, p * LANES:(p + 1) * LANES]

    g_mat, w_mat, p_t, q_t = {}, {}, {}, {}

    def local_stages(units):
        low, nak, mrbk = {}, {}, {}
        for u in units:
            lhs = jnp.concatenate([op(OP_A, u), op(OP_R, u)], axis=0)
            rhs = jnp.concatenate([_stack(op(OP_B, u)), _stack(op(OP_K, u))], axis=0)
            gram = _bdot(lhs, rhs, NT)
            d = u[0]
            low[u] = jnp.where(m_strict[d], gram[:CHUNK, :LANES], 0.0)
            nak[u] = jnp.where(m_strict[d], gram[:CHUNK, LANES:], 0.0).astype(BF16)
            mrbk[u] = jnp.concatenate([jnp.where(m_incl[d], gram[CHUNK:, :LANES], 0.0),
                                       jnp.where(m_incl[d], gram[CHUNK:, LANES:], 0.0)], axis=1).astype(BF16)

        inv = {u: eye_cat + low[u] for u in units}
        pwb = {u: low[u].astype(BF16) for u in units}
        for u in units:
            pwb[u] = _bdot(pwb[u], _stack(pwb[u])).astype(BF16)
        for _ in range(4):
            for u in units:
                both = _bdot(pwb[u], jnp.concatenate([_stack(pwb[u]), _stack(inv[u].astype(BF16))], axis=1))
                pwb[u] = both[:, :LANES].astype(BF16)
                inv[u] = inv[u] + both[:, LANES:]
        for u in units:
            inv[u] = inv[u] + _bdot(pwb[u], _stack(inv[u].astype(BF16)))

        sv = {u: _stack(op(OP_V, u)) for u in units}
        nv = {}
        for u in units:
            nv[u] = _bdot(nak[u], sv[u]).astype(BF16)
        x1, x2 = {}, {}
        for u in units:
            x12 = _bdot(inv[u].astype(BF16), jnp.concatenate([_stack(op(OP_A, u)), _stack(nv[u])], axis=1))
            x1[u] = x12[:, :LANES].astype(BF16)
            x2[u] = x12[:, LANES:].astype(BF16)
        for u in units:
            rhs = jnp.concatenate([jnp.concatenate([_stack(x1[u]), _stack(x2[u])], axis=1),
                                   jnp.concatenate([zero_bd, sv[u]], axis=1)], axis=0)
            gw = _bdot(mrbk[u], rhs)
            g_mat[u] = (op(OP_R, u).astype(F32) + gw[:, :LANES]).astype(BF16)
            w_mat[u] = gw[:, LANES:]
        for u in units:
            d, p, c = u
            ct = c % N_CHUNK
            gam = jnp.exp(ends_ref[d][c // N_CHUNK, ct:ct + 1, p * LANES:(p + 1) * LANES])
            p_t[u] = (jnp.where(blk, _bdot(x1[u], op(OP_BH, u), TN), 0.0)
                      + jnp.where(eye_bd, gam, 0.0)).astype(BF16)
            q_t[u] = jnp.where(blk, _bdot(jnp.concatenate([x2[u], op(OP_V, u)], axis=0),
                                          jnp.concatenate([op(OP_BH, u), op(OP_KH, u)], axis=0), TN), 0.0)

    n_c = SCAN_TILES * N_CHUNK
    state = {(d, p): jnp.where(is_lat, st_ref[d, p], 0.0) for d in (0, 1) for p in range(N_PAIR)}
    finals = {}
    y_refs = (yf_ref, yb_ref)

    def chunk_of(d, step):
        return step if d == 0 else n_c - 1 - step

    def chain(step, d, p):
        c = chunk_of(d, step)
        u = (d, p, c)
        sb = state[d, p].astype(BF16)
        y_refs[d][c * CHUNK:(c + 1) * CHUNK, p * LANES:(p + 1) * LANES] = _bdot(g_mat[u], sb, NT) + w_mat[u]
        state[d, p] = _bdot(sb, p_t[u]) + q_t[u]
        if (step + 1) % N_CHUNK == 0:
            finals[d, c // N_CHUNK, p] = state[d, p]
            if step + 1 < n_c:
                state[d, p] = jnp.where(is_lat, state[d, p], 0.0)

    local_stages([(d, p, c) for d in (0, 1) for p in range(N_PAIR) for c in range(n_c)])
    for step in range(n_c):
        for d in (0, 1):
            for p in range(N_PAIR):
                chain(step, d, p)
    for (d, p), s in state.items():
        st_ref[d, p] = s

    @pl.when(jnp.logical_not(is_lat))
    def _():
        for (d, tile, p), s in finals.items():
            sout_ref[tile, 0, d, 2 * p] = s[:HEAD, :HEAD]
            sout_ref[tile, 0, d, 2 * p + 1] = s[HEAD:, HEAD:]


def _scan(ops_f, ops_b, v, ends_f, ends_b, s0, w13, w2):
    def mirror(s):
        u = s - N_SCAN_CTX_STEPS
        return jnp.where(s < N_SCAN_CTX_STEPS, s,
                         N_SCAN_CTX_STEPS + (u // SCAN_LAT_STEPS) * SCAN_LAT_STEPS
                         + (SCAN_LAT_STEPS - 1 - u % SCAN_LAT_STEPS))

    def lat_seq(s):
        return jnp.maximum(s - N_SCAN_CTX_STEPS, 0) // SCAN_LAT_STEPS

    fwd = pl.BlockSpec((SCAN_TM, D_RWKV), lambda s: (s, 0))
    bwd = pl.BlockSpec((SCAN_TM, D_RWKV), lambda s: (mirror(s), 0))
    e_block = (SCAN_TILES, N_CHUNK, D_RWKV)
    st_block = (None, 1, 2, 2 * N_PAIR, HEAD, HEAD)
    out_y = jax.ShapeDtypeStruct((N_TOK, D_RWKV), F32)
    n_steps = N_TOK // SCAN_TM
    return pl.pallas_call(
        _scan_kernel,
        grid=(n_steps,),
        in_specs=[fwd] * (N_DIR_OPS + 1) + [bwd] * (N_DIR_OPS + 1)
                 + [pl.BlockSpec(e_block, lambda s: (s, 0, 0)),
                    pl.BlockSpec(e_block, lambda s: (mirror(s), 0, 0)),
                    pl.BlockSpec(st_block, lambda s: (lat_seq(s), 0, 0, 0, 0, 0)),
                    _slab_spec(w13, n_steps),
                    _slab_spec(w2, n_steps)],
        out_specs=[fwd, bwd,
                   pl.BlockSpec((SCAN_TILES, 1, 2, 2 * N_PAIR, HEAD, HEAD),
                                lambda s: (jnp.minimum(s, N_SCAN_CTX_STEPS - 1), 0, 0, 0, 0, 0)),
                   _slab_spec(w13, n_steps),
                   _slab_spec(w2, n_steps)],
        out_shape=[out_y, out_y,
                   jax.ShapeDtypeStruct((N_CTX_SEQ, 1, 2, 2 * N_PAIR, HEAD, HEAD), F32),
                   jax.ShapeDtypeStruct(w13.shape, BF16),
                   jax.ShapeDtypeStruct(w2.shape, BF16)],
        scratch_shapes=[pltpu.VMEM((2, N_PAIR, LANES, LANES), F32)],
        compiler_params=_params(),
        name="rwkv7_scan",
    )(*ops_f, v, *ops_b, v, ends_f, ends_b, s0, w13, w2)


def _back_kernel(x_ref, yf_ref, yb_ref, bv_ref, sg_ref, sa_ref, mb_ref, mod_ref, g_ref,
                 gng_ref, gnb_ref, wba_f32_ref, wout_f32_ref, ones_ref, o_ref, wba_ref, wout_ref):
    @pl.when(pl.program_id(0) == 0)
    def _():
        wba_ref[...] = wba_f32_ref[...].astype(BF16)
        wout_ref[...] = wout_f32_ref[...].astype(BF16)

    ones = ones_ref[...]
    groups = [slice(i * TM, (i + 1) * TM) for i in range(BACK_TM // TM)]
    ys = [yf_ref[r, :] + yb_ref[r, :] for r in groups]
    ycs = [y - _head_sum(y, ones) * (1.0 / HEAD) for y in ys]
    vs = [_head_sum(yc * yc, ones) * (1.0 / HEAD) for yc in ycs]
    yns = [yc * lax.rsqrt(v + EPS_GN) * gng_ref[...] + gnb_ref[...] for yc, v in zip(ycs, vs)]
    yas = [_dot((yn + bv_ref[r, :]) * sg_ref[r, :], wba_ref[...]) for r, yn in zip(groups, yns)]
    outs = [_dot(sa_ref[r, :] * ya + mb_ref[r, :], wout_ref[...]) for r, ya in zip(groups, yas)]
    for r, out in zip(groups, outs):
        o_ref[r, :] = x_ref[r, :] + mod_ref[MOD_BACK:MOD_BACK + 1, :] * _rms(out, g_ref[3:4, :])


def _back(x, yf, yb, bv, sg, sa, mb, mod3, norm_g, gng, gnb, wba, wout, ones):
    tok = lambda t: (t, 0)
    spec_r = pl.BlockSpec((BACK_TM, D_RWKV), tok)
    spec_d = pl.BlockSpec((BACK_TM, D_MODEL), tok)
    consts = [norm_g, gng, gnb, wba, wout, ones]
    return pl.pallas_call(
        _back_kernel,
        grid=(N_TOK // BACK_TM,),
        in_specs=[spec_d, spec_r, spec_r, spec_r, spec_r, spec_d, spec_d,
                  pl.BlockSpec((None, MOD_TAIL, D_MODEL), lambda t: (_mod_row(t * (BACK_TM // TM)), 0, 0))]
                 + [_const_spec(c.shape) for c in consts],
        out_specs=spec_d,
        out_shape=jax.ShapeDtypeStruct((N_TOK, D_MODEL), F32),
        scratch_shapes=[pltpu.VMEM(wba.shape, BF16), pltpu.VMEM(wout.shape, BF16)],
        compiler_params=_params(),
        name="mixer_back",
    )(x, yf, yb, bv, sg, sa, mb, mod3, *consts)


def kernel(x_prompt, x_sample, c, state_rwkv, c_ctx, w_mod, b_mod, norm_g, ffn1_w13, ffn1_w2,
           ffn2_w13, ffn2_w2, w_in, mu_shift, decay_w0, decay_w1, decay_w2, iclr_a0, iclr_a1,
           iclr_a2, k_k, k_a, r_k, gn_gain, gn_bias, conv_w, conv_b, w_branch_a, w_branch_b, w_out):
    assert x_prompt.shape == (N_CTX_SEQ, CTX_LEN, D_MODEL) and x_sample.shape == (N_LAT_SEQ, LAT_LEN, D_MODEL)
    assert w_mod.shape[0] == 1, "single trunk layer"

    cvec = jnp.concatenate([c_ctx[None, :], c, jnp.zeros((MOD_ROWS - 1 - N_LAT_SEQ, D_MODEL), F32)], axis=0)
    mod_head = _modulation(cvec, w_mod[0], b_mod).reshape(MOD_ROWS, MOD_HEAD, D_MODEL)
    g = norm_g[0]

    x, w_in_bf, mod_tail = _ffn_first(x_prompt.reshape(-1, D_MODEL), x_sample.reshape(-1, D_MODEL),
                                      mod_head, g, ffn1_w13[0], ffn1_w2[0], w_in[0], cvec, w_mod[0], b_mod)
    mod3 = mod_tail.reshape(MOD_ROWS, MOD_TAIL, D_MODEL)

    row = lambda p: p.reshape(1, -1)
    mu4 = mu_shift[0].reshape(4, D_MODEL)
    w1_cat = jnp.concatenate([decay_w1[0, 0], iclr_a1[0, 0], decay_w1[0, 1], iclr_a1[0, 1]], axis=1)
    w_aug = _lora_prep(jnp.repeat(mu4.T, LORA, axis=1), w1_cat)
    w2_blocks = [decay_w2[0, 0], iclr_a2[0, 0], decay_w2[0, 1], iclr_a2[0, 1]]
    w2_zero = jnp.zeros((LORA, D_RWKV), F32)
    w2_aug = jnp.concatenate(
        [jnp.concatenate([blk if j == i else w2_zero for j in range(4)], axis=1)
         for i, blk in enumerate(w2_blocks)], axis=0).astype(BF16)
    b2_aug = jnp.concatenate([decay_w0[0, 0], iclr_a0[0, 0], decay_w0[0, 1], iclr_a0[0, 1]]).reshape(1, -1)
    ones = _head_ones()
    front = _front(x, mod3, g, w_in_bf, w_aug,
                   [w2_aug, b2_aug, row(k_k[0]), row(k_a[0]), row(r_k[0]), conv_w[0], row(conv_b[0]),
                    w_branch_b[0], ones])
    ops_f, ops_b = front[0:N_DIR_OPS], front[N_DIR_OPS:2 * N_DIR_OPS]
    v, ends_f, ends_b, bv, sg, sa, mb = front[2 * N_DIR_OPS:]

    yf, yb, s_fin, w13_bf, w2_bf = _scan(ops_f, ops_b, v, ends_f, ends_b, state_rwkv,
                                         ffn2_w13[0], ffn2_w2[0])

    x = _back(x, yf, yb, bv, sg, sa, mb, mod3, g, row(gn_gain[0]), row(gn_bias[0]),
              w_branch_a[0], w_out[0], ones)
    y_ctx, y_lat = _ffn_last(x, mod3, g, w13_bf, w2_bf)

    y_prompt = y_ctx.reshape(N_CTX_SEQ, CTX_LEN, D_MODEL)
    y_sample = y_lat.reshape(N_LAT_SEQ, LAT_LEN, D_MODEL)
    return y_prompt, y_sample, s_fin
```

```python
import jax
import jax.numpy as jnp
from jax import lax
from jax.experimental import pallas as pl
from jax.experimental.pallas import tpu as pltpu

F32 = jnp.float32
BF16 = jnp.bfloat16

D_MODEL = 1024
D_FF = 2816
D_RWKV = 512
D_CONV = 512
HEAD = 64
HEAD_SHIFT = 6
D_IN = 4 * D_RWKV + 3 * D_CONV + 2 * D_MODEL
N_MOD = 9
EPS_RMS = 1e-6
EPS_GN = 64e-5
HALF_STEP = 0.5
EXP_M05 = 0.6065306597126334

N_CTX_SEQ = 16
CTX_LEN = 256
N_LAT_SEQ = 2
LAT_LEN = 2048
GRID_W = 64
N_TOK = N_CTX_SEQ * CTX_LEN + N_LAT_SEQ * LAT_LEN

TM = 256
N_CTX_TILES = N_CTX_SEQ * CTX_LEN // TM
LAT_TILES = LAT_LEN // TM
N_TILES = N_TOK // TM
FRONT_TM = 512
FRONT_GROUPS = FRONT_TM // TM
N_FRONT_CTX_TILES = N_CTX_SEQ * CTX_LEN // FRONT_TM
FRONT_LAT_TILES = LAT_LEN // FRONT_TM
N_FRONT_CONSTS = 9
FRONT_WBB = 7
LORA = 64
SCAN_TM = 512
SCAN_TILES = SCAN_TM // TM
N_SCAN_CTX_STEPS = N_CTX_SEQ * CTX_LEN // SCAN_TM
SCAN_LAT_STEPS = LAT_LEN // SCAN_TM
BACK_TM = 512
FFN_TM = 512
N_FFN_TILES = N_TOK // FFN_TM
N_FFN_CTX_TILES = N_CTX_SEQ * CTX_LEN // FFN_TM
FFN2_TM = 1024
N_FFN2_CTX_TILES = N_CTX_SEQ * CTX_LEN // FFN2_TM
N_WCHUNK = 11
CHUNK = 64
N_CHUNK = TM // CHUNK
LANES = 128
N_PAIR = D_RWKV // LANES
MOD_ROWS = 8
MOD_HEAD = 3
MOD_TAIL = N_MOD - MOD_HEAD
MOD_SIDE = 512
N_MOD_SIDE = MOD_TAIL * D_MODEL // MOD_SIDE
MOD_FRONT, MOD_BACK, MOD_FFN2 = 0, 2, 3
VMEM_LIMIT = 56 * 1024 * 1024

OP_A, OP_R, OP_B, OP_K, OP_BH, OP_KH, OP_V = range(7)
N_DIR_OPS = 6

NN = (((1,), (0,)), ((), ()))
NT = (((1,), (1,)), ((), ()))
TN = (((0,), (0,)), ((), ()))


def _dot(a, b, dims=NN):
    return lax.dot_general(a.astype(BF16), b.astype(BF16), dims, preferred_element_type=F32)


def _bdot(a, b, dims=NN):
    return lax.dot_general(a, b, dims, preferred_element_type=F32)


def _rms(x, g):
    ms = jnp.mean(x * x, axis=-1, keepdims=True)
    return x * lax.rsqrt(ms + EPS_RMS) * g


def _head_ones():
    i = lax.broadcasted_iota(jnp.int32, (D_RWKV, D_RWKV), 0) >> HEAD_SHIFT
    j = lax.broadcasted_iota(jnp.int32, (D_RWKV, D_RWKV), 1) >> HEAD_SHIFT
    return jnp.where(i == j, 1.0, 0.0).astype(BF16)


def _head_sum(x, ones):
    return jnp.dot(x.astype(BF16), ones, preferred_element_type=F32)


def _sigmoid(x):
    return 0.5 * jnp.tanh(0.5 * x) + 0.5


def _mod_row(t):
    return jnp.where(t < N_CTX_TILES, 0, 1 + (t - N_CTX_TILES) // LAT_TILES)


def _mod_chunks(mod_ref, tile, first, count):
    row = pl.ds(_mod_row(tile), 1)
    return [mod_ref[row, (first + i) * D_MODEL:(first + i + 1) * D_MODEL] for i in range(count)]


def _const_spec(shape):
    nd = len(shape)
    return pl.BlockSpec(shape, lambda *_: (0,) * nd, pipeline_mode=pl.Buffered(1))


def _params(n_axes=1):
    return pltpu.CompilerParams(dimension_semantics=("arbitrary",) * n_axes,
                                vmem_limit_bytes=VMEM_LIMIT)


def _mod_kernel(cctx_ref, c_ref, w_ref, b_ref, o_ref):
    w = w_ref[...].astype(BF16)

    def rows(c):
        return jnp.dot((c * jax.nn.sigmoid(c)).astype(BF16), w, preferred_element_type=F32) + b_ref[...]

    o_ref[0:1, :] = rows(cctx_ref[...])
    o_ref[1:1 + N_LAT_SEQ, :] = rows(c_ref[...])
    o_ref[1 + N_LAT_SEQ:, :] = jnp.zeros((MOD_ROWS - 1 - N_LAT_SEQ, o_ref.shape[1]), F32)


def _modulation(c_ctx, c, w_mod, b_mod):
    width = MOD_HEAD * D_MODEL
    return pl.pallas_call(
        _mod_kernel,
        grid=(1,),
        in_specs=[pl.BlockSpec(c_ctx.shape, lambda j: (0, 0)),
                  pl.BlockSpec(c.shape, lambda j: (0, 0)),
                  pl.BlockSpec((D_MODEL, width), lambda j: (0, 0)),
                  pl.BlockSpec((1, width), lambda j: (0, 0))],
        out_specs=pl.BlockSpec((MOD_ROWS, width), lambda j: (0, 0)),
        out_shape=jax.ShapeDtypeStruct((MOD_ROWS, width), F32),
        compiler_params=_params(),
        name="modulation",
    )(c_ctx, c, w_mod, b_mod)


def _ffn_body(x, mod, g_ref, w13_ref, w2_ref, ig):
    shift, scale, gate = mod
    xs = [x[i * TM:(i + 1) * TM, :] for i in range(x.shape[0] // TM)]
    hs = [(_rms(xi, g_ref[ig:ig + 1, :]) * (1.0 + scale) + shift).astype(BF16) for xi in xs]
    gus = [_dot(hi, w13_ref[...]) for hi in hs]
    acts = []
    for gu in gus:
        gt = gu[:, :D_FF]
        up = gu[:, D_FF:]
        acts.append((gt * jax.nn.sigmoid(gt) * up).astype(BF16))
    os_ = [_dot(ai, w2_ref[...]) for ai in acts]
    outs = [xi + HALF_STEP * gate * _rms(oi, g_ref[ig + 1:ig + 2, :]) for xi, oi in zip(xs, os_)]
    return jnp.concatenate(outs, axis=0)


def _cast_chunk(step, src_ref, dst_ref, axis):
    size = src_ref.shape[axis]

    @pl.when(step < N_WCHUNK)
    def _():
        start = pl.multiple_of(step * size, size)
        if axis == 0:
            dst_ref[pl.ds(start, size), :] = src_ref[...].astype(BF16)
        else:
            dst_ref[:, pl.ds(start, size)] = src_ref[...].astype(BF16)


def _chunk_spec(w, axis):
    block = list(w.shape)
    block[axis] = w.shape[axis] // N_WCHUNK
    clamp = lambda s: jnp.minimum(s, N_WCHUNK - 1)
    index = (lambda s: (clamp(s), 0)) if axis == 0 else (lambda s: (0, clamp(s)))
    return pl.BlockSpec(tuple(block), index)


def _tile_step(s):
    return jnp.maximum(s - N_WCHUNK, 0)


def _slab_spec(w, n_steps, step=lambda s: s):
    rows = w.shape[0] // n_steps
    return pl.BlockSpec((rows, w.shape[1]), lambda s: (step(s), 0))


def _ffn_first_kernel(xc_ref, xl_ref, mod_ref, g_ref, w13_ref, w2_ref, win_ref, cctx_ref, c_ref, wmod_ref, bmod_ref,
                      o_ref, win_bf_ref, mod_tail_ref, w13_bf, w2_bf):
    s = pl.program_id(0)
    _cast_chunk(s, w13_ref, w13_bf, 1)
    _cast_chunk(s, w2_ref, w2_bf, 0)

    @pl.when(s >= N_WCHUNK)
    def _():
        x = jnp.where(s - N_WCHUNK < N_FFN_CTX_TILES, xc_ref[...], xl_ref[...])
        mod = _mod_chunks(mod_ref, (s - N_WCHUNK) * (FFN_TM // TM), 0, 3)
        o_ref[...] = _ffn_body(x, mod, g_ref, w13_bf, w2_bf, 0)
        win_bf_ref[...] = win_ref[...].astype(BF16)

        @pl.when(s - N_WCHUNK < N_MOD_SIDE)
        def _():
            _mod_kernel(cctx_ref, c_ref, wmod_ref, bmod_ref, mod_tail_ref)


def _ffn_last_kernel(x_ref, mod_ref, g_ref, w13_ref, w2_ref, oc_ref, ol_ref):
    t = pl.program_id(0)
    mod = _mod_chunks(mod_ref, t * (FFN2_TM // TM), MOD_FFN2, 3)
    out = _ffn_body(x_ref[...], mod, g_ref, w13_ref, w2_ref, 4)

    @pl.when(t < N_FFN2_CTX_TILES)
    def _():
        oc_ref[...] = out

    @pl.when(t >= N_FFN2_CTX_TILES)
    def _():
        ol_ref[...] = out


def _ctx_tile(t):
    return jnp.minimum(t, N_FFN_CTX_TILES - 1), 0


def _lat_tile(t):
    return jnp.maximum(t - N_FFN_CTX_TILES, 0), 0


def _ffn_first(x_ctx, x_lat, mod_head, norm_g, w13, w2, w_in, c_ctx, c, w_mod, b_mod):
    side = lambda s: jnp.minimum(_tile_step(s), N_MOD_SIDE - 1)
    tail_col = MOD_HEAD * D_MODEL // MOD_SIDE
    return pl.pallas_call(
        _ffn_first_kernel,
        grid=(N_WCHUNK + N_FFN_TILES,),
        in_specs=[pl.BlockSpec((FFN_TM, D_MODEL), lambda s: _ctx_tile(_tile_step(s))),
                  pl.BlockSpec((FFN_TM, D_MODEL), lambda s: _lat_tile(_tile_step(s))),
                  _const_spec(mod_head.shape),
                  _const_spec(norm_g.shape),
                  _chunk_spec(w13, 1),
                  _chunk_spec(w2, 0),
                  _slab_spec(w_in, N_FFN_TILES, step=_tile_step),
                  _const_spec(c_ctx.shape),
                  _const_spec(c.shape),
                  pl.BlockSpec((D_MODEL, MOD_SIDE), lambda s: (0, tail_col + side(s))),
                  pl.BlockSpec((1, MOD_SIDE), lambda s: (0, tail_col + side(s)))],
        out_specs=[pl.BlockSpec((FFN_TM, D_MODEL), lambda s: (_tile_step(s), 0)),
                   _slab_spec(w_in, N_FFN_TILES, step=_tile_step),
                   pl.BlockSpec((MOD_ROWS, MOD_SIDE), lambda s: (0, side(s)))],
        out_shape=[jax.ShapeDtypeStruct((N_TOK, D_MODEL), F32),
                   jax.ShapeDtypeStruct(w_in.shape, BF16),
                   jax.ShapeDtypeStruct((MOD_ROWS, MOD_TAIL * D_MODEL), F32)],
        scratch_shapes=[pltpu.VMEM((D_MODEL, 2 * D_FF), BF16), pltpu.VMEM((D_FF, D_MODEL), BF16)],
        compiler_params=_params(),
        name="ffn1",
    )(x_ctx, x_lat, mod_head, norm_g, w13, w2, w_in, c_ctx, c, w_mod, b_mod)


def _ffn_last(x, mod3, norm_g, w13_bf, w2_bf):
    n_ctx = N_CTX_SEQ * CTX_LEN
    return pl.pallas_call(
        _ffn_last_kernel,
        grid=(N_TOK // FFN2_TM,),
        in_specs=[pl.BlockSpec((FFN2_TM, D_MODEL), lambda t: (t, 0)),
                  _const_spec(mod3.shape),
                  _const_spec(norm_g.shape),
                  _const_spec(w13_bf.shape),
                  _const_spec(w2_bf.shape)],
        out_specs=[pl.BlockSpec((FFN2_TM, D_MODEL), lambda t: (jnp.minimum(t, N_FFN2_CTX_TILES - 1), 0)),
                   pl.BlockSpec((FFN2_TM, D_MODEL), lambda t: (jnp.maximum(t - N_FFN2_CTX_TILES, 0), 0))],
        out_shape=[jax.ShapeDtypeStruct((n_ctx, D_MODEL), F32),
                   jax.ShapeDtypeStruct((N_TOK - n_ctx, D_MODEL), F32)],
        compiler_params=_params(),
        name="ffn2",
    )(x, mod3, norm_g, w13_bf, w2_bf)


def _lora_prep_kernel(mu_ref, dw1_ref, ia1_ref, dw2_ref, ia2_ref, dw0_ref, ia0_ref,
                      w1_o, w2_o, b2_o):
    firsts = (dw1_ref[0], ia1_ref[0], dw1_ref[1], ia1_ref[1])
    seconds = (dw2_ref[0], ia2_ref[0], dw2_ref[1], ia2_ref[1])
    biases = (dw0_ref[0:1, :], ia0_ref[0:1, :], dw0_ref[1:2, :], ia0_ref[1:2, :])
    n = len(firsts) * LORA
    w2_o[...] = jnp.zeros(w2_o.shape, BF16)
    for j in range(len(firsts)):
        mu = mu_ref[:, j:j + 1]
        w1_o[:, j * LORA:(j + 1) * LORA] = ((1.0 - mu) * firsts[j]).astype(BF16)
        w1_o[:, n + j * LORA:n + (j + 1) * LORA] = (mu * firsts[j]).astype(BF16)
        w2_o[j * LORA:(j + 1) * LORA, j * D_RWKV:(j + 1) * D_RWKV] = seconds[j].astype(BF16)
        b2_o[:, j * D_RWKV:(j + 1) * D_RWKV] = biases[j]


def _lora_prep(mu_t, dw1, ia1, dw2, ia2, dw0, ia0):
    n = 4 * LORA
    return pl.pallas_call(
        _lora_prep_kernel,
        out_shape=[jax.ShapeDtypeStruct((D_MODEL, 2 * n), BF16),
                   jax.ShapeDtypeStruct((n, 4 * D_RWKV), BF16),
                   jax.ShapeDtypeStruct((1, 4 * D_RWKV), F32)],
        compiler_params=pltpu.CompilerParams(vmem_limit_bytes=VMEM_LIMIT),
        name="lora_prep",
    )(mu_t, dw1, ia1, dw2, ia2, dw0, ia0)


def _front_group(gi, h, halo_prev, halo_next, is_lat, win_ref, waug_ref, refs, outs):
    w2aug_ref, b2aug_ref, kkw_ref, ka_ref, rk_ref, cw_ref, cbias_ref, wbb_ref, ones_ref = refs
    ops_o, v_o, ends_o, bv_o, sg_o, sa_o, mb_o = outs
    rows = slice(gi * TM, (gi + 1) * TM)
    row = lax.broadcasted_iota(jnp.int32, (TM, 1), 0)
    rin = row & (CHUNK - 1)
    ones = ones_ref[...]

    def proj(lo, hi):
        return _bdot(h, win_ref[:, lo:hi])

    pab = _bdot(h, waug_ref[...])
    rk = proj(0, 2 * D_RWKV)
    r = rk[:, :D_RWKV]
    k = rk[:, D_RWKV:]
    pb = pab[:, 2 * LANES:]
    sh_f = jnp.where(row == 0, halo_prev, pltpu.roll(pb[:, :LANES], 1, 0))
    sh_b = jnp.where(row == TM - 1, halo_next, pltpu.roll(pb[:, LANES:], TM - 1, 0))
    t_in = pab[:, :2 * LANES] + jnp.concatenate([sh_f, sh_b], axis=1)
    lane = lax.broadcasted_iota(jnp.int32, (1, 2 * LANES), 1)
    t_in = jnp.where((lane & HEAD) == 0, jnp.tanh(t_in), t_in)
    za = _dot(t_in, w2aug_ref[...]) + b2aug_ref[...]

    kk = k * kkw_ref[...]
    kkn = kk * lax.rsqrt(_head_sum(kk * kk, ones) + 1e-12)
    ka = ka_ref[...]

    vg = proj(2 * D_RWKV, 4 * D_RWKV)
    v = vg[:, :D_RWKV]
    v_o[rows, :] = v.astype(BF16)
    sg_o[rows, :] = _sigmoid(vg[:, D_RWKV:]).astype(BF16)

    def scan_operands(d):
        lw = -EXP_M05 * _sigmoid(za[:, 2 * d * D_RWKV:(2 * d + 1) * D_RWKV])
        a = _sigmoid(za[:, (2 * d + 1) * D_RWKV:(2 * d + 2) * D_RWKV])
        k_d = k * (1.0 + (a - 1.0) * ka)
        b = kkn * a
        cs = lw
        for s in (1, 2, 4, 8, 16, 32):
            if d == 0:
                cs = cs + jnp.where(rin >= s, pltpu.roll(cs, s, 0), 0.0)
            else:
                cs = cs + jnp.where(rin < CHUNK - s, pltpu.roll(cs, TM - s, 0), 0.0)
        end_row = CHUNK - 1 if d == 0 else 0
        ends = [cs[c * CHUNK + end_row:c * CHUNK + end_row + 1, :] for c in range(N_CHUNK)]
        for c in range(N_CHUNK):
            ends_o[d][gi, c:c + 1, :] = ends[c]
        cs_end = jnp.concatenate([jnp.broadcast_to(e, (CHUNK, D_RWKV)) for e in ends], axis=0)
        dec_inv = jnp.exp(-cs)
        dec_rest = jnp.exp(cs_end - cs)
        o_a, o_r, o_b, o_k, o_bh, o_kh = ops_o[d]
        o_a[rows, :] = (-kkn * jnp.exp(cs - lw)).astype(BF16)
        o_r[rows, :] = (r * jnp.exp(cs)).astype(BF16)
        o_b[rows, :] = (b * dec_inv).astype(BF16)
        o_k[rows, :] = (k_d * dec_inv).astype(BF16)
        o_bh[rows, :] = (b * dec_rest).astype(BF16)
        o_kh[rows, :] = (k_d * dec_rest).astype(BF16)
        return k_d

    conv_in = proj(4 * D_RWKV, 4 * D_RWKV + 3 * D_CONV)
    k_0 = scan_operands(0)
    gate_a = proj(4 * D_RWKV + 3 * D_CONV, 4 * D_RWKV + 3 * D_CONV + D_MODEL)
    sa_o[rows, :] = _sigmoid(gate_a).astype(BF16)
    k_1 = scan_operands(1)
    gate_b = proj(4 * D_RWKV + 3 * D_CONV + D_MODEL, D_IN)

    cgate = conv_in[:, :D_CONV]
    u = conv_in[:, D_CONV:2 * D_CONV] * conv_in[:, 2 * D_CONV:]
    col = row & (GRID_W - 1)
    zl = jnp.logical_or(row == 0, jnp.logical_and(is_lat, col == 0))
    zr = jnp.logical_or(row == TM - 1, jnp.logical_and(is_lat, col == GRID_W - 1))
    left = jnp.where(zl, 0.0, pltpu.roll(u, 1, 0))
    right = jnp.where(zr, 0.0, pltpu.roll(u, TM - 1, 0))
    conv = left * cw_ref[0:1, :] + u * cw_ref[1:2, :] + right * cw_ref[2:3, :] + cbias_ref[...]
    y_b = _dot(cgate * conv, wbb_ref[...])
    mb_o[rows, :] = (_sigmoid(gate_b) * y_b).astype(BF16)

    bv_o[rows, :] = (_head_sum(r * (k_0 + k_1) * rk_ref[...], ones) * v).astype(BF16)


def _front_kernel(x_ref, xp_ref, xn_ref, mod_ref, g_ref, win_ref, waug_ref, *rest):
    refs = list(rest[:N_FRONT_CONSTS])
    outs = rest[N_FRONT_CONSTS:-1]
    wbb_bf = rest[-1]
    ops_o = (outs[0:N_DIR_OPS], outs[N_DIR_OPS:2 * N_DIR_OPS])
    v_o, ends0_o, ends1_o, bv_o, sg_o, sa_o, mb_o = outs[2 * N_DIR_OPS:]
    outs = (ops_o, v_o, (ends0_o, ends1_o), bv_o, sg_o, sa_o, mb_o)

    t = pl.program_id(0)

    @pl.when(t == 0)
    def _():
        wbb_bf[...] = refs[FRONT_WBB][...].astype(BF16)

    refs[FRONT_WBB] = wbb_bf

    is_lat = t >= N_FRONT_CTX_TILES
    i_in = (t - N_FRONT_CTX_TILES) % FRONT_LAT_TILES
    lat_first = jnp.logical_and(is_lat, i_in == 0)
    lat_last = jnp.logical_and(is_lat, i_in == FRONT_LAT_TILES - 1)
    shift, scale = _mod_chunks(mod_ref, t * FRONT_GROUPS, MOD_FRONT, 2)
    g2 = g_ref[2:3, :]

    def pre(x):
        return (_rms(x, g2) * (1.0 + scale) + shift).astype(BF16)

    x = x_ref[...]
    edge = jnp.concatenate([xp_ref[...], x[TM - 8:TM + 8, :], xn_ref[...]], axis=0)
    edge_b = _bdot(pre(edge), waug_ref[:, 2 * LANES:])
    zero = jnp.zeros((1, LANES), F32)
    halo_prev = (jnp.where(jnp.logical_and(is_lat, jnp.logical_not(lat_first)), edge_b[7:8, :LANES], zero),
                 jnp.where(is_lat, edge_b[15:16, :LANES], zero))
    halo_next = (jnp.where(is_lat, edge_b[16:17, LANES:], zero),
                 jnp.where(jnp.logical_and(is_lat, jnp.logical_not(lat_last)), edge_b[24:25, LANES:], zero))

    for gi in range(FRONT_GROUPS):
        h = pre(x[gi * TM:(gi + 1) * TM, :])
        _front_group(gi, h, halo_prev[gi], halo_next[gi], is_lat, win_ref, waug_ref, refs, outs)


def _front(x, mod3, norm_g, w_in_bf, w_aug, consts):
    tok = lambda t: (t, 0)
    rows8 = FRONT_TM // 8
    last8 = N_TOK // 8 - 1
    assert len(consts) == N_FRONT_CONSTS
    out_rb = jax.ShapeDtypeStruct((N_TOK, D_RWKV), BF16)
    out_db = jax.ShapeDtypeStruct((N_TOK, D_MODEL), BF16)
    out_e = jax.ShapeDtypeStruct((N_TILES, N_CHUNK, D_RWKV), F32)
    spec_r = pl.BlockSpec((FRONT_TM, D_RWKV), tok)
    spec_d = pl.BlockSpec((FRONT_TM, D_MODEL), tok)
    spec_e = pl.BlockSpec((FRONT_GROUPS, N_CHUNK, D_RWKV), lambda t: (t, 0, 0))
    n_b = 2 * N_DIR_OPS + 1
    return pl.pallas_call(
        _front_kernel,
        grid=(N_TOK // FRONT_TM,),
        in_specs=[pl.BlockSpec((FRONT_TM, D_MODEL), tok),
                  pl.BlockSpec((8, D_MODEL), lambda t: (jnp.maximum(t * rows8 - 1, 0), 0)),
                  pl.BlockSpec((8, D_MODEL), lambda t: (jnp.minimum((t + 1) * rows8, last8), 0)),
                  _const_spec(mod3.shape)]
                 + [_const_spec(c.shape) for c in [norm_g, w_in_bf, w_aug] + list(consts)],
        out_specs=[spec_r] * n_b + [spec_e] * 2 + [spec_r] * 2 + [spec_d] * 2,
        out_shape=[out_rb] * n_b + [out_e] * 2 + [out_rb] * 2 + [out_db] * 2,
        scratch_shapes=[pltpu.VMEM((D_CONV, D_MODEL), BF16)],
        compiler_params=_params(),
        name="mixer_front",
    )(x, x, x, mod3, norm_g, w_in_bf, w_aug, *consts)


def _stack(x):
    lane_lo = lax.broadcasted_iota(jnp.int32, x.shape, 1) < HEAD
    z = jnp.zeros_like(x)
    return jnp.concatenate([jnp.where(lane_lo, x, z), jnp.where(lane_lo, z, x)], axis=0)


def _scan_kernel(*refs):
    ops = (refs[0:N_DIR_OPS + 1], refs[N_DIR_OPS + 1:2 * N_DIR_OPS + 2])
    (ends0_ref, ends1_ref, s0_ref, w13_ref, w2_ref,
     yf_ref, yb_ref, sout_ref, w13_bf_ref, w2_bf_ref, st_ref) = refs[2 * N_DIR_OPS + 2:]
    ends_ref = (ends0_ref, ends1_ref)
    w13_bf_ref[...] = w13_ref[...].astype(BF16)
    w2_bf_ref[...] = w2_ref[...].astype(BF16)
    step_id = pl.program_id(0)
    is_lat = step_id >= N_SCAN_CTX_STEPS
    lat_first = (step_id - N_SCAN_CTX_STEPS) % SCAN_LAT_STEPS == 0

    @pl.when(jnp.logical_and(is_lat, lat_first))
    def _():
        zero = jnp.zeros((HEAD, HEAD), F32)
        for d in (0, 1):
            for p in range(N_PAIR):
                top = jnp.concatenate([s0_ref[0, d, 2 * p], zero], axis=1)
                bottom = jnp.concatenate([zero, s0_ref[0, d, 2 * p + 1]], axis=1)
                st_ref[d, p] = jnp.concatenate([top, bottom], axis=0)

    ti = lax.broadcasted_iota(jnp.int32, (CHUNK, LANES), 0)
    sj = lax.broadcasted_iota(jnp.int32, (CHUNK, LANES), 1) & (CHUNK - 1)
    m_strict = (sj < ti, sj > ti)
    m_incl = (sj <= ti, sj >= ti)
    eye_cat = jnp.where(sj == ti, 1.0, 0.0)
    bi = lax.broadcasted_iota(jnp.int32, (LANES, LANES), 0)
    bj = lax.broadcasted_iota(jnp.int32, (LANES, LANES), 1)
    blk = (bi >> HEAD_SHIFT) == (bj >> HEAD_SHIFT)
    eye_bd = bi == bj
    zero_bd = jnp.zeros((LANES, LANES), BF16)

    def op(i, u):
        d, p, c = u
        return ops[d][i][c * CHUNK:(c + 1) * CHUNK, p * LANES:(p + 1) * LANES]

    g_mat, w_mat, p_t, q_t = {}, {}, {}, {}

    def local_stages(units):
        low, nak, mrbk = {}, {}, {}
        for u in units:
            lhs = jnp.concatenate([op(OP_A, u), op(OP_R, u)], axis=0)
            rhs = jnp.concatenate([_stack(op(OP_B, u)), _stack(op(OP_K, u))], axis=0)
            gram = _bdot(lhs, rhs, NT)
            d = u[0]
            low[u] = jnp.where(m_strict[d], gram[:CHUNK, :LANES], 0.0)
            nak[u] = jnp.where(m_strict[d], gram[:CHUNK, LANES:], 0.0).astype(BF16)
            mrbk[u] = jnp.concatenate([jnp.where(m_incl[d], gram[CHUNK:, :LANES], 0.0),
                                       jnp.where(m_incl[d], gram[CHUNK:, LANES:], 0.0)], axis=1).astype(BF16)

        inv = {u: eye_cat + low[u] for u in units}
        pwb = {u: low[u].astype(BF16) for u in units}
        for u in units:
            pwb[u] = _bdot(pwb[u], _stack(pwb[u])).astype(BF16)
        for _ in range(4):
            for u in units:
                both = _bdot(pwb[u], jnp.concatenate([_stack(pwb[u]), _stack(inv[u].astype(BF16))], axis=1))
                pwb[u] = both[:, :LANES].astype(BF16)
                inv[u] = inv[u] + both[:, LANES:]
        for u in units:
            inv[u] = inv[u] + _bdot(pwb[u], _stack(inv[u].astype(BF16)))

        sv = {u: _stack(op(OP_V, u)) for u in units}
        nv = {}
        for u in units:
            nv[u] = _bdot(nak[u], sv[u]).astype(BF16)
        x1, x2 = {}, {}
        for u in units:
            x12 = _bdot(inv[u].astype(BF16), jnp.concatenate([_stack(op(OP_A, u)), _stack(nv[u])], axis=1))
            x1[u] = x12[:, :LANES].astype(BF16)
            x2[u] = x12[:, LANES:].astype(BF16)
        for u in units:
            rhs = jnp.concatenate([jnp.concatenate([_stack(x1[u]), _stack(x2[u])], axis=1),
                                   jnp.concatenate([zero_bd, sv[u]], axis=1)], axis=0)
            gw = _bdot(mrbk[u], rhs)
            g_mat[u] = (op(OP_R, u).astype(F32) + gw[:, :LANES]).astype(BF16)
            w_mat[u] = gw[:, LANES:]
        for u in units:
            d, p, c = u
            ct = c % N_CHUNK
            gam = jnp.exp(ends_ref[d][c // N_CHUNK, ct:ct + 1, p * LANES:(p + 1) * LANES])
            p_t[u] = (jnp.where(blk, _bdot(x1[u], op(OP_BH, u), TN), 0.0)
                      + jnp.where(eye_bd, gam, 0.0)).astype(BF16)
            q_t[u] = jnp.where(blk, _bdot(jnp.concatenate([x2[u], op(OP_V, u)], axis=0),
                                          jnp.concatenate([op(OP_BH, u), op(OP_KH, u)], axis=0), TN), 0.0)

    n_c = SCAN_TILES * N_CHUNK
    state = {(d, p): jnp.where(is_lat, st_ref[d, p], 0.0) for d in (0, 1) for p in range(N_PAIR)}
    finals = {}
    y_refs = (yf_ref, yb_ref)

    def chunk_of(d, step):
        return step if d == 0 else n_c - 1 - step

    def chain(step, d, p):
        c = chunk_of(d, step)
        u = (d, p, c)
        sb = state[d, p].astype(BF16)
        y_refs[d][c * CHUNK:(c + 1) * CHUNK, p * LANES:(p + 1) * LANES] = _bdot(g_mat[u], sb, NT) + w_mat[u]
        state[d, p] = _bdot(sb, p_t[u]) + q_t[u]
        if (step + 1) % N_CHUNK == 0:
            finals[d, c // N_CHUNK, p] = state[d, p]
            if step + 1 < n_c:
                state[d, p] = jnp.where(is_lat, state[d, p], 0.0)

    local_stages([(d, p, c) for d in (0, 1) for p in range(N_PAIR) for c in range(n_c)])
    for step in range(n_c):
        for d in (0, 1):
            for p in range(N_PAIR):
                chain(step, d, p)
    for (d, p), s in state.items():
        st_ref[d, p] = s

    @pl.when(jnp.logical_not(is_lat))
    def _():
        for (d, tile, p), s in finals.items():
            sout_ref[tile, 0, d, 2 * p] = s[:HEAD, :HEAD]
            sout_ref[tile, 0, d, 2 * p + 1] = s[HEAD:, HEAD:]


def _scan(ops_f, ops_b, v, ends_f, ends_b, s0, w13, w2):
    def mirror(s):
        u = s - N_SCAN_CTX_STEPS
        return jnp.where(s < N_SCAN_CTX_STEPS, s,
                         N_SCAN_CTX_STEPS + (u // SCAN_LAT_STEPS) * SCAN_LAT_STEPS
                         + (SCAN_LAT_STEPS - 1 - u % SCAN_LAT_STEPS))

    def lat_seq(s):
        return jnp.maximum(s - N_SCAN_CTX_STEPS, 0) // SCAN_LAT_STEPS

    fwd = pl.BlockSpec((SCAN_TM, D_RWKV), lambda s: (s, 0))
    bwd = pl.BlockSpec((SCAN_TM, D_RWKV), lambda s: (mirror(s), 0))
    e_block = (SCAN_TILES, N_CHUNK, D_RWKV)
    st_block = (None, 1, 2, 2 * N_PAIR, HEAD, HEAD)
    out_y = jax.ShapeDtypeStruct((N_TOK, D_RWKV), F32)
    n_steps = N_TOK // SCAN_TM
    return pl.pallas_call(
        _scan_kernel,
        grid=(n_steps,),
        in_specs=[fwd] * (N_DIR_OPS + 1) + [bwd] * (N_DIR_OPS + 1)
                 + [pl.BlockSpec(e_block, lambda s: (s, 0, 0)),
                    pl.BlockSpec(e_block, lambda s: (mirror(s), 0, 0)),
                    pl.BlockSpec(st_block, lambda s: (lat_seq(s), 0, 0, 0, 0, 0)),
                    _slab_spec(w13, n_steps),
                    _slab_spec(w2, n_steps)],
        out_specs=[fwd, bwd,
                   pl.BlockSpec((SCAN_TILES, 1, 2, 2 * N_PAIR, HEAD, HEAD),
                                lambda s: (jnp.minimum(s, N_SCAN_CTX_STEPS - 1), 0, 0, 0, 0, 0)),
                   _slab_spec(w13, n_steps),
                   _slab_spec(w2, n_steps)],
        out_shape=[out_y, out_y,
                   jax.ShapeDtypeStruct((N_CTX_SEQ, 1, 2, 2 * N_PAIR, HEAD, HEAD), F32),
                   jax.ShapeDtypeStruct(w13.shape, BF16),
                   jax.ShapeDtypeStruct(w2.shape, BF16)],
        scratch_shapes=[pltpu.VMEM((2, N_PAIR, LANES, LANES), F32)],
        compiler_params=_params(),
        name="rwkv7_scan",
    )(*ops_f, v, *ops_b, v, ends_f, ends_b, s0, w13, w2)


def _back_kernel(x_ref, yf_ref, yb_ref, bv_ref, sg_ref, sa_ref, mb_ref, mod_ref, g_ref,
                 gng_ref, gnb_ref, wba_f32_ref, wout_f32_ref, ones_ref, o_ref, wba_ref, wout_ref):
    @pl.when(pl.program_id(0) == 0)
    def _():
        wba_ref[...] = wba_f32_ref[...].astype(BF16)
        wout_ref[...] = wout_f32_ref[...].astype(BF16)

    gate, = _mod_chunks(mod_ref, pl.program_id(0) * (BACK_TM // TM), MOD_BACK, 1)
    ones = ones_ref[...]
    groups = [slice(i * TM, (i + 1) * TM) for i in range(BACK_TM // TM)]
    ys = [yf_ref[r, :] + yb_ref[r, :] for r in groups]
    ycs = [y - _head_sum(y, ones) * (1.0 / HEAD) for y in ys]
    vs = [_head_sum(yc * yc, ones) * (1.0 / HEAD) for yc in ycs]
    yns = [yc * lax.rsqrt(v + EPS_GN) * gng_ref[...] + gnb_ref[...] for yc, v in zip(ycs, vs)]
    yas = [_dot((yn + bv_ref[r, :]) * sg_ref[r, :], wba_ref[...]) for r, yn in zip(groups, yns)]
    outs = [_dot(sa_ref[r, :] * ya + mb_ref[r, :], wout_ref[...]) for r, ya in zip(groups, yas)]
    for r, out in zip(groups, outs):
        o_ref[r, :] = x_ref[r, :] + gate * _rms(out, g_ref[3:4, :])


def _back(x, yf, yb, bv, sg, sa, mb, mod3, norm_g, gng, gnb, wba, wout, ones):
    tok = lambda t: (t, 0)
    spec_r = pl.BlockSpec((BACK_TM, D_RWKV), tok)
    spec_d = pl.BlockSpec((BACK_TM, D_MODEL), tok)
    consts = [norm_g, gng, gnb, wba, wout, ones]
    return pl.pallas_call(
        _back_kernel,
        grid=(N_TOK // BACK_TM,),
        in_specs=[spec_d, spec_r, spec_r, spec_r, spec_r, spec_d, spec_d,
                  _const_spec(mod3.shape)]
                 + [_const_spec(c.shape) for c in consts],
        out_specs=spec_d,
        out_shape=jax.ShapeDtypeStruct((N_TOK, D_MODEL), F32),
        scratch_shapes=[pltpu.VMEM(wba.shape, BF16), pltpu.VMEM(wout.shape, BF16)],
        compiler_params=_params(),
        name="mixer_back",
    )(x, yf, yb, bv, sg, sa, mb, mod3, *consts)


def kernel(x_prompt, x_sample, c, state_rwkv, c_ctx, w_mod, b_mod, norm_g, ffn1_w13, ffn1_w2,
           ffn2_w13, ffn2_w2, w_in, mu_shift, decay_w0, decay_w1, decay_w2, iclr_a0, iclr_a1,
           iclr_a2, k_k, k_a, r_k, gn_gain, gn_bias, conv_w, conv_b, w_branch_a, w_branch_b, w_out):
    assert x_prompt.shape == (N_CTX_SEQ, CTX_LEN, D_MODEL) and x_sample.shape == (N_LAT_SEQ, LAT_LEN, D_MODEL)
    assert w_mod.shape[0] == 1, "single trunk layer"

    c_ctx = c_ctx.reshape(1, D_MODEL)
    mod_head = _modulation(c_ctx, c, w_mod[0], b_mod)
    g = norm_g[0]

    x, w_in_bf, mod3 = _ffn_first(x_prompt.reshape(-1, D_MODEL), x_sample.reshape(-1, D_MODEL),
                                  mod_head, g, ffn1_w13[0], ffn1_w2[0], w_in[0], c_ctx, c, w_mod[0], b_mod)

    row = lambda p: p.reshape(1, -1)
    w_aug, w2_aug, b2_aug = _lora_prep(mu_shift[0].reshape(4, D_MODEL).T, decay_w1[0], iclr_a1[0],
                                       decay_w2[0], iclr_a2[0], decay_w0[0], iclr_a0[0])
    ones = _head_ones()
    front = _front(x, mod3, g, w_in_bf, w_aug,
                   [w2_aug, b2_aug, row(k_k[0]), row(k_a[0]), row(r_k[0]), conv_w[0], row(conv_b[0]),
                    w_branch_b[0], ones])
    ops_f, ops_b = front[0:N_DIR_OPS], front[N_DIR_OPS:2 * N_DIR_OPS]
    v, ends_f, ends_b, bv, sg, sa, mb = front[2 * N_DIR_OPS:]

    yf, yb, s_fin, w13_bf, w2_bf = _scan(ops_f, ops_b, v, ends_f, ends_b, state_rwkv,
                                         ffn2_w13[0], ffn2_w2[0])

    x = _back(x, yf, yb, bv, sg, sa, mb, mod3, g, row(gn_gain[0]), row(gn_bias[0]),
              w_branch_a[0], w_out[0], ones)
    y_ctx, y_lat = _ffn_last(x, mod3, g, w13_bf, w2_bf)

    y_prompt = y_ctx.reshape(N_CTX_SEQ, CTX_LEN, D_MODEL)
    y_sample = y_lat.reshape(N_LAT_SEQ, LAT_LEN, D_MODEL)
    return y_prompt, y_sample, s_fin
```

```python
import jax
import jax.numpy as jnp
from jax import lax
from jax.experimental import pallas as pl
from jax.experimental.pallas import tpu as pltpu

F32 = jnp.float32
BF16 = jnp.bfloat16

D_MODEL = 1024
D_FF = 2816
D_RWKV = 512
D_CONV = 512
HEAD = 64
HEAD_SHIFT = 6
D_IN = 4 * D_RWKV + 3 * D_CONV + 2 * D_MODEL
N_MOD = 9
EPS_RMS = 1e-6
EPS_GN = 64e-5
HALF_STEP = 0.5
EXP_M05 = 0.6065306597126334

N_CTX_SEQ = 16
CTX_LEN = 256
N_LAT_SEQ = 2
LAT_LEN = 2048
GRID_W = 64
N_TOK = N_CTX_SEQ * CTX_LEN + N_LAT_SEQ * LAT_LEN

TM = 256
N_CTX_TILES = N_CTX_SEQ * CTX_LEN // TM
LAT_TILES = LAT_LEN // TM
N_TILES = N_TOK // TM
FRONT_TM = 512
FRONT_GROUPS = FRONT_TM // TM
N_FRONT_CTX_TILES = N_CTX_SEQ * CTX_LEN // FRONT_TM
FRONT_LAT_TILES = LAT_LEN // FRONT_TM
N_FRONT_CONSTS = 9
FRONT_WBB = 7
LORA = 64
SCAN_TM = 512
SCAN_TILES = SCAN_TM // TM
N_SCAN_CTX_STEPS = N_CTX_SEQ * CTX_LEN // SCAN_TM
SCAN_LAT_STEPS = LAT_LEN // SCAN_TM
BACK_TM = 512
FFN_TM = 512
N_FFN_TILES = N_TOK // FFN_TM
N_FFN_CTX_TILES = N_CTX_SEQ * CTX_LEN // FFN_TM
FFN2_TM = 1024
N_FFN2_CTX_TILES = N_CTX_SEQ * CTX_LEN // FFN2_TM
N_WCHUNK = 11
CHUNK = 64
N_CHUNK = TM // CHUNK
LANES = 128
N_PAIR = D_RWKV // LANES
MOD_ROWS = 8
MOD_HEAD = 3
MOD_TAIL = N_MOD - MOD_HEAD
MOD_SIDE = 512
N_MOD_SIDE = MOD_TAIL * D_MODEL // MOD_SIDE
N_MOD_HEAD_BLOCKS = MOD_HEAD * D_MODEL // MOD_SIDE
MOD_FRONT, MOD_BACK, MOD_FFN2 = 0, 2, 3
VMEM_LIMIT = 56 * 1024 * 1024

OP_A, OP_R, OP_B, OP_K, OP_BH, OP_KH, OP_V = range(7)
N_DIR_OPS = 6

NN = (((1,), (0,)), ((), ()))
NT = (((1,), (1,)), ((), ()))
TN = (((0,), (0,)), ((), ()))


def _dot(a, b, dims=NN):
    return lax.dot_general(a.astype(BF16), b.astype(BF16), dims, preferred_element_type=F32)


def _bdot(a, b, dims=NN):
    return lax.dot_general(a, b, dims, preferred_element_type=F32)


def _rms(x, g):
    ms = jnp.mean(x * x, axis=-1, keepdims=True)
    return x * lax.rsqrt(ms + EPS_RMS) * g


def _head_ones():
    i = lax.broadcasted_iota(jnp.int32, (D_RWKV, D_RWKV), 0) >> HEAD_SHIFT
    j = lax.broadcasted_iota(jnp.int32, (D_RWKV, D_RWKV), 1) >> HEAD_SHIFT
    return jnp.where(i == j, 1.0, 0.0).astype(BF16)


def _head_sum(x, ones):
    return jnp.dot(x.astype(BF16), ones, preferred_element_type=F32)


def _sigmoid(x):
    return 0.5 * jnp.tanh(0.5 * x) + 0.5


def _mod_row(t):
    return jnp.where(t < N_CTX_TILES, 0, 1 + (t - N_CTX_TILES) // LAT_TILES)


def _mod_chunks(mod_ref, tile, first, count):
    row = pl.ds(_mod_row(tile), 1)
    return [mod_ref[row, (first + i) * D_MODEL:(first + i + 1) * D_MODEL] for i in range(count)]


def _const_spec(shape):
    nd = len(shape)
    return pl.BlockSpec(shape, lambda *_: (0,) * nd, pipeline_mode=pl.Buffered(1))


def _params(n_axes=1):
    return pltpu.CompilerParams(dimension_semantics=("arbitrary",) * n_axes,
                                vmem_limit_bytes=VMEM_LIMIT)


def _mod_kernel(cctx_ref, c_ref, w_ref, b_ref, o_ref):
    w = w_ref[...].astype(BF16)

    def rows(c):
        return jnp.dot((c * jax.nn.sigmoid(c)).astype(BF16), w, preferred_element_type=F32) + b_ref[...]

    o_ref[0:1, :] = rows(cctx_ref[...])
    o_ref[1:1 + N_LAT_SEQ, :] = rows(c_ref[...])
    o_ref[1 + N_LAT_SEQ:, :] = jnp.zeros((MOD_ROWS - 1 - N_LAT_SEQ, o_ref.shape[1]), F32)


def _ffn_body(x, mod, g_ref, w13_ref, w2_ref, ig):
    shift, scale, gate = mod
    xs = [x[i * TM:(i + 1) * TM, :] for i in range(x.shape[0] // TM)]
    hs = [(_rms(xi, g_ref[ig:ig + 1, :]) * (1.0 + scale) + shift).astype(BF16) for xi in xs]
    gus = [_dot(hi, w13_ref[...]) for hi in hs]
    acts = []
    for gu in gus:
        gt = gu[:, :D_FF]
        up = gu[:, D_FF:]
        acts.append((gt * jax.nn.sigmoid(gt) * up).astype(BF16))
    os_ = [_dot(ai, w2_ref[...]) for ai in acts]
    outs = [xi + HALF_STEP * gate * _rms(oi, g_ref[ig + 1:ig + 2, :]) for xi, oi in zip(xs, os_)]
    return jnp.concatenate(outs, axis=0)


def _cast_chunk(step, src_ref, dst_ref, axis):
    size = src_ref.shape[axis]

    @pl.when(step < N_WCHUNK)
    def _():
        start = pl.multiple_of(step * size, size)
        if axis == 0:
            dst_ref[pl.ds(start, size), :] = src_ref[...].astype(BF16)
        else:
            dst_ref[:, pl.ds(start, size)] = src_ref[...].astype(BF16)


def _chunk_spec(w, axis):
    block = list(w.shape)
    block[axis] = w.shape[axis] // N_WCHUNK
    clamp = lambda s: jnp.minimum(s, N_WCHUNK - 1)
    index = (lambda s: (clamp(s), 0)) if axis == 0 else (lambda s: (0, clamp(s)))
    return pl.BlockSpec(tuple(block), index)


def _tile_step(s):
    return jnp.maximum(s - N_WCHUNK, 0)


def _slab_spec(w, n_steps, step=lambda s: s):
    rows = w.shape[0] // n_steps
    return pl.BlockSpec((rows, w.shape[1]), lambda s: (step(s), 0))


def _ffn_first_kernel(xc_ref, xl_ref, g_ref, w13_ref, w2_ref, win_ref, cctx_ref, c_ref, wmod_ref, bmod_ref,
                      o_ref, win_bf_ref, mod_tail_ref, w13_bf, w2_bf, mod_ref):
    s = pl.program_id(0)
    _cast_chunk(s, w13_ref, w13_bf, 1)
    _cast_chunk(s, w2_ref, w2_bf, 0)

    @pl.when(s < N_MOD_HEAD_BLOCKS)
    def _():
        col = pl.multiple_of(s * MOD_SIDE, MOD_SIDE)
        _mod_kernel(cctx_ref, c_ref, wmod_ref, bmod_ref, mod_ref.at[:, pl.ds(col, MOD_SIDE)])

    @pl.when(s >= N_WCHUNK)
    def _():
        x = jnp.where(s - N_WCHUNK < N_FFN_CTX_TILES, xc_ref[...], xl_ref[...])
        mod = _mod_chunks(mod_ref, (s - N_WCHUNK) * (FFN_TM // TM), 0, 3)
        o_ref[...] = _ffn_body(x, mod, g_ref, w13_bf, w2_bf, 0)
        win_bf_ref[...] = win_ref[...].astype(BF16)

        @pl.when(s - N_WCHUNK < N_MOD_SIDE)
        def _():
            _mod_kernel(cctx_ref, c_ref, wmod_ref, bmod_ref, mod_tail_ref)


def _ffn_last_kernel(x_ref, mod_ref, g_ref, w13_ref, w2_ref, oc_ref, ol_ref):
    t = pl.program_id(0)
    mod = _mod_chunks(mod_ref, t * (FFN2_TM // TM), MOD_FFN2, 3)
    out = _ffn_body(x_ref[...], mod, g_ref, w13_ref, w2_ref, 4)

    @pl.when(t < N_FFN2_CTX_TILES)
    def _():
        oc_ref[...] = out

    @pl.when(t >= N_FFN2_CTX_TILES)
    def _():
        ol_ref[...] = out


def _ctx_tile(t):
    return jnp.minimum(t, N_FFN_CTX_TILES - 1), 0


def _lat_tile(t):
    return jnp.maximum(t - N_FFN_CTX_TILES, 0), 0


def _ffn_first(x_ctx, x_lat, norm_g, w13, w2, w_in, c_ctx, c, w_mod, b_mod):
    side = lambda s: jnp.minimum(_tile_step(s), N_MOD_SIDE - 1)

    def mod_block(s):
        return 0, jnp.where(s < N_WCHUNK, jnp.minimum(s, N_MOD_HEAD_BLOCKS - 1), N_MOD_HEAD_BLOCKS + side(s))

    return pl.pallas_call(
        _ffn_first_kernel,
        grid=(N_WCHUNK + N_FFN_TILES,),
        in_specs=[pl.BlockSpec((FFN_TM, D_MODEL), lambda s: _ctx_tile(_tile_step(s))),
                  pl.BlockSpec((FFN_TM, D_MODEL), lambda s: _lat_tile(_tile_step(s))),
                  _const_spec(norm_g.shape),
                  _chunk_spec(w13, 1),
                  _chunk_spec(w2, 0),
                  _slab_spec(w_in, N_FFN_TILES, step=_tile_step),
                  _const_spec(c_ctx.shape),
                  _const_spec(c.shape),
                  pl.BlockSpec((D_MODEL, MOD_SIDE), mod_block),
                  pl.BlockSpec((1, MOD_SIDE), mod_block)],
        out_specs=[pl.BlockSpec((FFN_TM, D_MODEL), lambda s: (_tile_step(s), 0)),
                   _slab_spec(w_in, N_FFN_TILES, step=_tile_step),
                   pl.BlockSpec((MOD_ROWS, MOD_SIDE), lambda s: (0, side(s)))],
        out_shape=[jax.ShapeDtypeStruct((N_TOK, D_MODEL), F32),
                   jax.ShapeDtypeStruct(w_in.shape, BF16),
                   jax.ShapeDtypeStruct((MOD_ROWS, MOD_TAIL * D_MODEL), F32)],
        scratch_shapes=[pltpu.VMEM((D_MODEL, 2 * D_FF), BF16), pltpu.VMEM((D_FF, D_MODEL), BF16),
                        pltpu.VMEM((MOD_ROWS, MOD_HEAD * D_MODEL), F32)],
        compiler_params=_params(),
        name="ffn1",
    )(x_ctx, x_lat, norm_g, w13, w2, w_in, c_ctx, c, w_mod, b_mod)


def _ffn_last(x, mod3, norm_g, w13_bf, w2_bf):
    n_ctx = N_CTX_SEQ * CTX_LEN
    return pl.pallas_call(
        _ffn_last_kernel,
        grid=(N_TOK // FFN2_TM,),
        in_specs=[pl.BlockSpec((FFN2_TM, D_MODEL), lambda t: (t, 0)),
                  _const_spec(mod3.shape),
                  _const_spec(norm_g.shape),
                  _const_spec(w13_bf.shape),
                  _const_spec(w2_bf.shape)],
        out_specs=[pl.BlockSpec((FFN2_TM, D_MODEL), lambda t: (jnp.minimum(t, N_FFN2_CTX_TILES - 1), 0)),
                   pl.BlockSpec((FFN2_TM, D_MODEL), lambda t: (jnp.maximum(t - N_FFN2_CTX_TILES, 0), 0))],
        out_shape=[jax.ShapeDtypeStruct((n_ctx, D_MODEL), F32),
                   jax.ShapeDtypeStruct((N_TOK - n_ctx, D_MODEL), F32)],
        compiler_params=_params(),
        name="ffn2",
    )(x, mod3, norm_g, w13_bf, w2_bf)


def _lora_prep_kernel(mu_ref, dw1_ref, ia1_ref, dw2_ref, ia2_ref, dw0_ref, ia0_ref,
                      w1_o, w2_o, b2_o):
    firsts = (dw1_ref[0], ia1_ref[0], dw1_ref[1], ia1_ref[1])
    seconds = (dw2_ref[0], ia2_ref[0], dw2_ref[1], ia2_ref[1])
    biases = (dw0_ref[0:1, :], ia0_ref[0:1, :], dw0_ref[1:2, :], ia0_ref[1:2, :])
    n = len(firsts) * LORA
    w2_o[...] = jnp.zeros(w2_o.shape, BF16)
    for j in range(len(firsts)):
        mu = mu_ref[:, j:j + 1]
        w1_o[:, j * LORA:(j + 1) * LORA] = ((1.0 - mu) * firsts[j]).astype(BF16)
        w1_o[:, n + j * LORA:n + (j + 1) * LORA] = (mu * firsts[j]).astype(BF16)
        w2_o[j * LORA:(j + 1) * LORA, j * D_RWKV:(j + 1) * D_RWKV] = seconds[j].astype(BF16)
        b2_o[:, j * D_RWKV:(j + 1) * D_RWKV] = biases[j]


def _lora_prep(mu_t, dw1, ia1, dw2, ia2, dw0, ia0):
    n = 4 * LORA
    return pl.pallas_call(
        _lora_prep_kernel,
        out_shape=[jax.ShapeDtypeStruct((D_MODEL, 2 * n), BF16),
                   jax.ShapeDtypeStruct((n, 4 * D_RWKV), BF16),
                   jax.ShapeDtypeStruct((1, 4 * D_RWKV), F32)],
        compiler_params=pltpu.CompilerParams(vmem_limit_bytes=VMEM_LIMIT),
        name="lora_prep",
    )(mu_t, dw1, ia1, dw2, ia2, dw0, ia0)


def _front_group(gi, h, halo_prev, halo_next, is_lat, win_ref, waug_ref, refs, outs):
    w2aug_ref, b2aug_ref, kkw_ref, ka_ref, rk_ref, cw_ref, cbias_ref, wbb_ref, ones_ref = refs
    ops_o, v_o, ends_o, bv_o, sg_o, sa_o, mb_o = outs
    rows = slice(gi * TM, (gi + 1) * TM)
    row = lax.broadcasted_iota(jnp.int32, (TM, 1), 0)
    rin = row & (CHUNK - 1)
    ones = ones_ref[...]

    def proj(lo, hi):
        return _bdot(h, win_ref[:, lo:hi])

    pab = _bdot(h, waug_ref[...])
    rk = proj(0, 2 * D_RWKV)
    r = rk[:, :D_RWKV]
    k = rk[:, D_RWKV:]
    pb = pab[:, 2 * LANES:]
    sh_f = jnp.where(row == 0, halo_prev, pltpu.roll(pb[:, :LANES], 1, 0))
    sh_b = jnp.where(row == TM - 1, halo_next, pltpu.roll(pb[:, LANES:], TM - 1, 0))
    t_in = pab[:, :2 * LANES] + jnp.concatenate([sh_f, sh_b], axis=1)
    lane = lax.broadcasted_iota(jnp.int32, (1, 2 * LANES), 1)
    t_in = jnp.where((lane & HEAD) == 0, jnp.tanh(t_in), t_in)
    za = _dot(t_in, w2aug_ref[...]) + b2aug_ref[...]

    kk = k * kkw_ref[...]
    kkn = kk * lax.rsqrt(_head_sum(kk * kk, ones) + 1e-12)
    ka = ka_ref[...]

    vg = proj(2 * D_RWKV, 4 * D_RWKV)
    v = vg[:, :D_RWKV]
    v_o[rows, :] = v.astype(BF16)
    sg_o[rows, :] = _sigmoid(vg[:, D_RWKV:]).astype(BF16)

    def scan_operands(d):
        lw = -EXP_M05 * _sigmoid(za[:, 2 * d * D_RWKV:(2 * d + 1) * D_RWKV])
        a = _sigmoid(za[:, (2 * d + 1) * D_RWKV:(2 * d + 2) * D_RWKV])
        k_d = k * (1.0 + (a - 1.0) * ka)
        b = kkn * a
        cs = lw
        for s in (1, 2, 4, 8, 16, 32):
            if d == 0:
                cs = cs + jnp.where(rin >= s, pltpu.roll(cs, s, 0), 0.0)
            else:
                cs = cs + jnp.where(rin < CHUNK - s, pltpu.roll(cs, TM - s, 0), 0.0)
        end_row = CHUNK - 1 if d == 0 else 0
        ends = [cs[c * CHUNK + end_row:c * CHUNK + end_row + 1, :] for c in range(N_CHUNK)]
        for c in range(N_CHUNK):
            ends_o[d][gi, c:c + 1, :] = ends[c]
        cs_end = jnp.concatenate([jnp.broadcast_to(e, (CHUNK, D_RWKV)) for e in ends], axis=0)
        dec_inv = jnp.exp(-cs)
        dec_rest = jnp.exp(cs_end - cs)
        o_a, o_r, o_b, o_k, o_bh, o_kh = ops_o[d]
        o_a[rows, :] = (-kkn * jnp.exp(cs - lw)).astype(BF16)
        o_r[rows, :] = (r * jnp.exp(cs)).astype(BF16)
        o_b[rows, :] = (b * dec_inv).astype(BF16)
        o_k[rows, :] = (k_d * dec_inv).astype(BF16)
        o_bh[rows, :] = (b * dec_rest).astype(BF16)
        o_kh[rows, :] = (k_d * dec_rest).astype(BF16)
        return k_d

    conv_in = proj(4 * D_RWKV, 4 * D_RWKV + 3 * D_CONV)
    k_0 = scan_operands(0)
    gate_a = proj(4 * D_RWKV + 3 * D_CONV, 4 * D_RWKV + 3 * D_CONV + D_MODEL)
    sa_o[rows, :] = _sigmoid(gate_a).astype(BF16)
    k_1 = scan_operands(1)
    gate_b = proj(4 * D_RWKV + 3 * D_CONV + D_MODEL, D_IN)

    cgate = conv_in[:, :D_CONV]
    u = conv_in[:, D_CONV:2 * D_CONV] * conv_in[:, 2 * D_CONV:]
    col = row & (GRID_W - 1)
    zl = jnp.logical_or(row == 0, jnp.logical_and(is_lat, col == 0))
    zr = jnp.logical_or(row == TM - 1, jnp.logical_and(is_lat, col == GRID_W - 1))
    left = jnp.where(zl, 0.0, pltpu.roll(u, 1, 0))
    right = jnp.where(zr, 0.0, pltpu.roll(u, TM - 1, 0))
    conv = left * cw_ref[0:1, :] + u * cw_ref[1:2, :] + right * cw_ref[2:3, :] + cbias_ref[...]
    y_b = _dot(cgate * conv, wbb_ref[...])
    mb_o[rows, :] = (_sigmoid(gate_b) * y_b).astype(BF16)

    bv_o[rows, :] = (_head_sum(r * (k_0 + k_1) * rk_ref[...], ones) * v).astype(BF16)


def _front_kernel(x_ref, xp_ref, xn_ref, mod_ref, g_ref, win_ref, waug_ref, *rest):
    refs = list(rest[:N_FRONT_CONSTS])
    outs = rest[N_FRONT_CONSTS:-1]
    wbb_bf = rest[-1]
    ops_o = (outs[0:N_DIR_OPS], outs[N_DIR_OPS:2 * N_DIR_OPS])
    v_o, ends0_o, ends1_o, bv_o, sg_o, sa_o, mb_o = outs[2 * N_DIR_OPS:]
    outs = (ops_o, v_o, (ends0_o, ends1_o), bv_o, sg_o, sa_o, mb_o)

    t = pl.program_id(0)

    @pl.when(t == 0)
    def _():
        wbb_bf[...] = refs[FRONT_WBB][...].astype(BF16)

    refs[FRONT_WBB] = wbb_bf

    is_lat = t >= N_FRONT_CTX_TILES
    i_in = (t - N_FRONT_CTX_TILES) % FRONT_LAT_TILES
    lat_first = jnp.logical_and(is_lat, i_in == 0)
    lat_last = jnp.logical_and(is_lat, i_in == FRONT_LAT_TILES - 1)
    shift, scale = _mod_chunks(mod_ref, t * FRONT_GROUPS, MOD_FRONT, 2)
    g2 = g_ref[2:3, :]

    def pre(x):
        return (_rms(x, g2) * (1.0 + scale) + shift).astype(BF16)

    x = x_ref[...]
    edge = jnp.concatenate([xp_ref[...], x[TM - 8:TM + 8, :], xn_ref[...]], axis=0)
    edge_b = _bdot(pre(edge), waug_ref[:, 2 * LANES:])
    zero = jnp.zeros((1, LANES), F32)
    halo_prev = (jnp.where(jnp.logical_and(is_lat, jnp.logical_not(lat_first)), edge_b[7:8, :LANES], zero),
                 jnp.where(is_lat, edge_b[15:16, :LANES], zero))
    halo_next = (jnp.where(is_lat, edge_b[16:17, LANES:], zero),
                 jnp.where(jnp.logical_and(is_lat, jnp.logical_not(lat_last)), edge_b[24:25, LANES:], zero))

    for gi in range(FRONT_GROUPS):
        h = pre(x[gi * TM:(gi + 1) * TM, :])
        _front_group(gi, h, halo_prev[gi], halo_next[gi], is_lat, win_ref, waug_ref, refs, outs)


def _front(x, mod3, norm_g, w_in_bf, w_aug, consts):
    tok = lambda t: (t, 0)
    rows8 = FRONT_TM // 8
    last8 = N_TOK // 8 - 1
    assert len(consts) == N_FRONT_CONSTS
    out_rb = jax.ShapeDtypeStruct((N_TOK, D_RWKV), BF16)
    out_db = jax.ShapeDtypeStruct((N_TOK, D_MODEL), BF16)
    out_e = jax.ShapeDtypeStruct((N_TILES, N_CHUNK, D_RWKV), F32)
    spec_r = pl.BlockSpec((FRONT_TM, D_RWKV), tok)
    spec_d = pl.BlockSpec((FRONT_TM, D_MODEL), tok)
    spec_e = pl.BlockSpec((FRONT_GROUPS, N_CHUNK, D_RWKV), lambda t: (t, 0, 0))
    n_b = 2 * N_DIR_OPS + 1
    return pl.pallas_call(
        _front_kernel,
        grid=(N_TOK // FRONT_TM,),
        in_specs=[pl.BlockSpec((FRONT_TM, D_MODEL), tok),
                  pl.BlockSpec((8, D_MODEL), lambda t: (jnp.maximum(t * rows8 - 1, 0), 0)),
                  pl.BlockSpec((8, D_MODEL), lambda t: (jnp.minimum((t + 1) * rows8, last8), 0)),
                  _const_spec(mod3.shape)]
                 + [_const_spec(c.shape) for c in [norm_g, w_in_bf, w_aug] + list(consts)],
        out_specs=[spec_r] * n_b + [spec_e] * 2 + [spec_r] * 2 + [spec_d] * 2,
        out_shape=[out_rb] * n_b + [out_e] * 2 + [out_rb] * 2 + [out_db] * 2,
        scratch_shapes=[pltpu.VMEM((D_CONV, D_MODEL), BF16)],
        compiler_params=_params(),
        name="mixer_front",
    )(x, x, x, mod3, norm_g, w_in_bf, w_aug, *consts)


def _stack(x):
    lane_lo = lax.broadcasted_iota(jnp.int32, x.shape, 1) < HEAD
    z = jnp.zeros_like(x)
    return jnp.concatenate([jnp.where(lane_lo, x, z), jnp.where(lane_lo, z, x)], axis=0)


def _scan_kernel(*refs):
    ops = (refs[0:N_DIR_OPS + 1], refs[N_DIR_OPS + 1:2 * N_DIR_OPS + 2])
    (ends0_ref, ends1_ref, s0_ref, w13_ref, w2_ref,
     yf_ref, yb_ref, sout_ref, w13_bf_ref, w2_bf_ref, st_ref) = refs[2 * N_DIR_OPS + 2:]
    ends_ref = (ends0_ref, ends1_ref)
    w13_bf_ref[...] = w13_ref[...].astype(BF16)
    w2_bf_ref[...] = w2_ref[...].astype(BF16)
    step_id = pl.program_id(0)
    is_lat = step_id >= N_SCAN_CTX_STEPS
    lat_first = (step_id - N_SCAN_CTX_STEPS) % SCAN_LAT_STEPS == 0

    @pl.when(jnp.logical_and(is_lat, lat_first))
    def _():
        zero = jnp.zeros((HEAD, HEAD), F32)
        for d in (0, 1):
            for p in range(N_PAIR):
                top = jnp.concatenate([s0_ref[0, d, 2 * p], zero], axis=1)
                bottom = jnp.concatenate([zero, s0_ref[0, d, 2 * p + 1]], axis=1)
                st_ref[d, p] = jnp.concatenate([top, bottom], axis=0)

    ti = lax.broadcasted_iota(jnp.int32, (CHUNK, LANES), 0)
    sj = lax.broadcasted_iota(jnp.int32, (CHUNK, LANES), 1) & (CHUNK - 1)
    m_strict = (sj < ti, sj > ti)
    m_incl = (sj <= ti, sj >= ti)
    eye_cat = jnp.where(sj == ti, 1.0, 0.0)
    bi = lax.broadcasted_iota(jnp.int32, (LANES, LANES), 0)
    bj = lax.broadcasted_iota(jnp.int32, (LANES, LANES), 1)
    blk = (bi >> HEAD_SHIFT) == (bj >> HEAD_SHIFT)
    eye_bd = bi == bj
    zero_bd = jnp.zeros((LANES, LANES), BF16)

    def op(i, u):
        d, p, c = u
        return ops[d][i][c * CHUNK:(c + 1) * CHUNK, p * LANES:(p + 1) * LANES]

    g_mat, w_mat, p_t, q_t = {}, {}, {}, {}

    def local_stages(units):
        low, nak, mrbk = {}, {}, {}
        for u in units:
            lhs = jnp.concatenate([op(OP_A, u), op(OP_R, u)], axis=0)
            rhs = jnp.concatenate([_stack(op(OP_B, u)), _stack(op(OP_K, u))], axis=0)
            gram = _bdot(lhs, rhs, NT)
            d = u[0]
            low[u] = jnp.where(m_strict[d], gram[:CHUNK, :LANES], 0.0)
            nak[u] = jnp.where(m_strict[d], gram[:CHUNK, LANES:], 0.0).astype(BF16)
            mrbk[u] = jnp.concatenate([jnp.where(m_incl[d], gram[CHUNK:, :LANES], 0.0),
                                       jnp.where(m_incl[d], gram[CHUNK:, LANES:], 0.0)], axis=1).astype(BF16)

        inv = {u: eye_cat + low[u] for u in units}
        pwb = {u: low[u].astype(BF16) for u in units}
        for u in units:
            pwb[u] = _bdot(pwb[u], _stack(pwb[u])).astype(BF16)
        for _ in range(4):
            for u in units:
                both = _bdot(pwb[u], jnp.concatenate([_stack(pwb[u]), _stack(inv[u].astype(BF16))], axis=1))
                pwb[u] = both[:, :LANES].astype(BF16)
                inv[u] = inv[u] + both[:, LANES:]
        for u in units:
            inv[u] = inv[u] + _bdot(pwb[u], _stack(inv[u].astype(BF16)))

        sv = {u: _stack(op(OP_V, u)) for u in units}
        nv = {}
        for u in units:
            nv[u] = _bdot(nak[u], sv[u]).astype(BF16)
        x1, x2 = {}, {}
        for u in units:
            x12 = _bdot(inv[u].astype(BF16), jnp.concatenate([_stack(op(OP_A, u)), _stack(nv[u])], axis=1))
            x1[u] = x12[:, :LANES].astype(BF16)
            x2[u] = x12[:, LANES:].astype(BF16)
        for u in units:
            rhs = jnp.concatenate([jnp.concatenate([_stack(x1[u]), _stack(x2[u])], axis=1),
                                   jnp.concatenate([zero_bd, sv[u]], axis=1)], axis=0)
            gw = _bdot(mrbk[u], rhs)
            g_mat[u] = (op(OP_R, u).astype(F32) + gw[:, :LANES]).astype(BF16)
            w_mat[u] = gw[:, LANES:]
        for u in units:
            d, p, c = u
            ct = c % N_CHUNK
            gam = jnp.exp(ends_ref[d][c // N_CHUNK, ct:ct + 1, p * LANES:(p + 1) * LANES])
            p_t[u] = (jnp.where(blk, _bdot(x1[u], op(OP_BH, u), TN), 0.0)
                      + jnp.where(eye_bd, gam, 0.0)).astype(BF16)
            q_t[u] = jnp.where(blk, _bdot(jnp.concatenate([x2[u], op(OP_V, u)], axis=0),
                                          jnp.concatenate([op(OP_BH, u), op(OP_KH, u)], axis=0), TN), 0.0)

    n_c = SCAN_TILES * N_CHUNK
    state = {(d, p): jnp.where(is_lat, st_ref[d, p], 0.0) for d in (0, 1) for p in range(N_PAIR)}
    finals = {}
    y_refs = (yf_ref, yb_ref)

    def chunk_of(d, step):
        return step if d == 0 else n_c - 1 - step

    def chain(step, d, p):
        c = chunk_of(d, step)
        u = (d, p, c)
        sb = state[d, p].astype(BF16)
        y_refs[d][c * CHUNK:(c + 1) * CHUNK, p * LANES:(p + 1) * LANES] = _bdot(g_mat[u], sb, NT) + w_mat[u]
        state[d, p] = _bdot(sb, p_t[u]) + q_t[u]
        if (step + 1) % N_CHUNK == 0:
            finals[d, c // N_CHUNK, p] = state[d, p]
            if step + 1 < n_c:
                state[d, p] = jnp.where(is_lat, state[d, p], 0.0)

    local_stages([(d, p, c) for d in (0, 1) for p in range(N_PAIR) for c in range(n_c)])
    for step in range(n_c):
        for d in (0, 1):
            for p in range(N_PAIR):
                chain(step, d, p)
    for (d, p), s in state.items():
        st_ref[d, p] = s

    @pl.when(jnp.logical_not(is_lat))
    def _():
        for (d, tile, p), s in finals.items():
            sout_ref[tile, 0, d, 2 * p] = s[:HEAD, :HEAD]
            sout_ref[tile, 0, d, 2 * p + 1] = s[HEAD:, HEAD:]


def _scan(ops_f, ops_b, v, ends_f, ends_b, s0, w13, w2):
    def mirror(s):
        u = s - N_SCAN_CTX_STEPS
        return jnp.where(s < N_SCAN_CTX_STEPS, s,
                         N_SCAN_CTX_STEPS + (u // SCAN_LAT_STEPS) * SCAN_LAT_STEPS
                         + (SCAN_LAT_STEPS - 1 - u % SCAN_LAT_STEPS))

    def lat_seq(s):
        return jnp.maximum(s - N_SCAN_CTX_STEPS, 0) // SCAN_LAT_STEPS

    fwd = pl.BlockSpec((SCAN_TM, D_RWKV), lambda s: (s, 0))
    bwd = pl.BlockSpec((SCAN_TM, D_RWKV), lambda s: (mirror(s), 0))
    e_block = (SCAN_TILES, N_CHUNK, D_RWKV)
    st_block = (None, 1, 2, 2 * N_PAIR, HEAD, HEAD)
    out_y = jax.ShapeDtypeStruct((N_TOK, D_RWKV), F32)
    n_steps = N_TOK // SCAN_TM
    return pl.pallas_call(
        _scan_kernel,
        grid=(n_steps,),
        in_specs=[fwd] * (N_DIR_OPS + 1) + [bwd] * (N_DIR_OPS + 1)
                 + [pl.BlockSpec(e_block, lambda s: (s, 0, 0)),
                    pl.BlockSpec(e_block, lambda s: (mirror(s), 0, 0)),
                    pl.BlockSpec(st_block, lambda s: (lat_seq(s), 0, 0, 0, 0, 0)),
                    _slab_spec(w13, n_steps),
                    _slab_spec(w2, n_steps)],
        out_specs=[fwd, bwd,
                   pl.BlockSpec((SCAN_TILES, 1, 2, 2 * N_PAIR, HEAD, HEAD),
                                lambda s: (jnp.minimum(s, N_SCAN_CTX_STEPS - 1), 0, 0, 0, 0, 0)),
                   _slab_spec(w13, n_steps),
                   _slab_spec(w2, n_steps)],
        out_shape=[out_y, out_y,
                   jax.ShapeDtypeStruct((N_CTX_SEQ, 1, 2, 2 * N_PAIR, HEAD, HEAD), F32),
                   jax.ShapeDtypeStruct(w13.shape, BF16),
                   jax.ShapeDtypeStruct(w2.shape, BF16)],
        scratch_shapes=[pltpu.VMEM((2, N_PAIR, LANES, LANES), F32)],
        compiler_params=_params(),
        name="rwkv7_scan",
    )(*ops_f, v, *ops_b, v, ends_f, ends_b, s0, w13, w2)


def _back_kernel(x_ref, yf_ref, yb_ref, bv_ref, sg_ref, sa_ref, mb_ref, mod_ref, g_ref,
                 gng_ref, gnb_ref, wba_f32_ref, wout_f32_ref, ones_ref, o_ref, wba_ref, wout_ref):
    @pl.when(pl.program_id(0) == 0)
    def _():
        wba_ref[...] = wba_f32_ref[...].astype(BF16)
        wout_ref[...] = wout_f32_ref[...].astype(BF16)

    gate, = _mod_chunks(mod_ref, pl.program_id(0) * (BACK_TM // TM), MOD_BACK, 1)
    ones = ones_ref[...]
    groups = [slice(i * TM, (i + 1) * TM) for i in range(BACK_TM // TM)]
    ys = [yf_ref[r, :] + yb_ref[r, :] for r in groups]
    ycs = [y - _head_sum(y, ones) * (1.0 / HEAD) for y in ys]
    vs = [_head_sum(yc * yc, ones) * (1.0 / HEAD) for yc in ycs]
    yns = [yc * lax.rsqrt(v + EPS_GN) * gng_ref[...] + gnb_ref[...] for yc, v in zip(ycs, vs)]
    yas = [_dot((yn + bv_ref[r, :]) * sg_ref[r, :], wba_ref[...]) for r, yn in zip(groups, yns)]
    outs = [_dot(sa_ref[r, :] * ya + mb_ref[r, :], wout_ref[...]) for r, ya in zip(groups, yas)]
    for r, out in zip(groups, outs):
        o_ref[r, :] = x_ref[r, :] + gate * _rms(out, g_ref[3:4, :])


def _back(x, yf, yb, bv, sg, sa, mb, mod3, norm_g, gng, gnb, wba, wout, ones):
    tok = lambda t: (t, 0)
    spec_r = pl.BlockSpec((BACK_TM, D_RWKV), tok)
    spec_d = pl.BlockSpec((BACK_TM, D_MODEL), tok)
    consts = [norm_g, gng, gnb, wba, wout, ones]
    return pl.pallas_call(
        _back_kernel,
        grid=(N_TOK // BACK_TM,),
        in_specs=[spec_d, spec_r, spec_r, spec_r, spec_r, spec_d, spec_d,
                  _const_spec(mod3.shape)]
                 + [_const_spec(c.shape) for c in consts],
        out_specs=spec_d,
        out_shape=jax.ShapeDtypeStruct((N_TOK, D_MODEL), F32),
        scratch_shapes=[pltpu.VMEM(wba.shape, BF16), pltpu.VMEM(wout.shape, BF16)],
        compiler_params=_params(),
        name="mixer_back",
    )(x, yf, yb, bv, sg, sa, mb, mod3, *consts)


def kernel(x_prompt, x_sample, c, state_rwkv, c_ctx, w_mod, b_mod, norm_g, ffn1_w13, ffn1_w2,
           ffn2_w13, ffn2_w2, w_in, mu_shift, decay_w0, decay_w1, decay_w2, iclr_a0, iclr_a1,
           iclr_a2, k_k, k_a, r_k, gn_gain, gn_bias, conv_w, conv_b, w_branch_a, w_branch_b, w_out):
    assert x_prompt.shape == (N_CTX_SEQ, CTX_LEN, D_MODEL) and x_sample.shape == (N_LAT_SEQ, LAT_LEN, D_MODEL)
    assert w_mod.shape[0] == 1, "single trunk layer"

    g = norm_g[0]
    x, w_in_bf, mod3 = _ffn_first(x_prompt.reshape(-1, D_MODEL), x_sample.reshape(-1, D_MODEL), g,
                                  ffn1_w13[0], ffn1_w2[0], w_in[0], c_ctx.reshape(1, D_MODEL), c,
                                  w_mod[0], b_mod)

    row = lambda p: p.reshape(1, -1)
    w_aug, w2_aug, b2_aug = _lora_prep(mu_shift[0].reshape(4, D_MODEL).T, decay_w1[0], iclr_a1[0],
                                       decay_w2[0], iclr_a2[0], decay_w0[0], iclr_a0[0])
    ones = _head_ones()
    front = _front(x, mod3, g, w_in_bf, w_aug,
                   [w2_aug, b2_aug, row(k_k[0]), row(k_a[0]), row(r_k[0]), conv_w[0], row(conv_b[0]),
                    w_branch_b[0], ones])
    ops_f, ops_b = front[0:N_DIR_OPS], front[N_DIR_OPS:2 * N_DIR_OPS]
    v, ends_f, ends_b, bv, sg, sa, mb = front[2 * N_DIR_OPS:]

    yf, yb, s_fin, w13_bf, w2_bf = _scan(ops_f, ops_b, v, ends_f, ends_b, state_rwkv,
                                         ffn2_w13[0], ffn2_w2[0])

    x = _back(x, yf, yb, bv, sg, sa, mb, mod3, g, row(gn_gain[0]), row(gn_bias[0]),
              w_branch_a[0], w_out[0], ones)
    y_ctx, y_lat = _ffn_last(x, mod3, g, w13_bf, w2_bf)

    y_prompt = y_ctx.reshape(N_CTX_SEQ, CTX_LEN, D_MODEL)
    y_sample = y_lat.reshape(N_LAT_SEQ, LAT_LEN, D_MODEL)
    return y_prompt, y_sample, s_fin
```

```python
import jax
import jax.numpy as jnp
from jax import lax
from jax.experimental import pallas as pl
from jax.experimental.pallas import tpu as pltpu

F32 = jnp.float32
BF16 = jnp.bfloat16

D_MODEL = 1024
D_FF = 2816
D_RWKV = 512
D_CONV = 512
HEAD = 64
HEAD_SHIFT = 6
D_IN = 4 * D_RWKV + 3 * D_CONV + 2 * D_MODEL
N_MOD = 9
EPS_RMS = 1e-6
EPS_GN = 64e-5
HALF_STEP = 0.5
EXP_M05 = 0.6065306597126334

N_CTX_SEQ = 16
CTX_LEN = 256
N_LAT_SEQ = 2
LAT_LEN = 2048
GRID_W = 64
N_TOK = N_CTX_SEQ * CTX_LEN + N_LAT_SEQ * LAT_LEN

TM = 256
N_CTX_TILES = N_CTX_SEQ * CTX_LEN // TM
LAT_TILES = LAT_LEN // TM
N_TILES = N_TOK // TM
FRONT_TM = 512
FRONT_GROUPS = FRONT_TM // TM
N_FRONT_CTX_TILES = N_CTX_SEQ * CTX_LEN // FRONT_TM
FRONT_LAT_TILES = LAT_LEN // FRONT_TM
N_FRONT_CONSTS = 9
FRONT_WBB = 7
LORA = 64
SCAN_TM = 512
SCAN_TILES = SCAN_TM // TM
N_SCAN_CTX_STEPS = N_CTX_SEQ * CTX_LEN // SCAN_TM
SCAN_LAT_STEPS = LAT_LEN // SCAN_TM
BACK_TM = 512
FFN_TM = 512
N_FFN_TILES = N_TOK // FFN_TM
N_FFN_CTX_TILES = N_CTX_SEQ * CTX_LEN // FFN_TM
FFN2_TM = 1024
N_FFN2_CTX_TILES = N_CTX_SEQ * CTX_LEN // FFN2_TM
N_WCHUNK = 11
CHUNK = 64
N_CHUNK = TM // CHUNK
LANES = 128
N_PAIR = D_RWKV // LANES
MOD_ROWS = 8
MOD_HEAD = 3
MOD_TAIL = N_MOD - MOD_HEAD
MOD_SIDE = 512
N_MOD_SIDE = MOD_TAIL * D_MODEL // MOD_SIDE
N_MOD_HEAD_BLOCKS = MOD_HEAD * D_MODEL // MOD_SIDE
MOD_FRONT, MOD_BACK, MOD_FFN2 = 0, 2, 3
VMEM_LIMIT = 56 * 1024 * 1024

OP_A, OP_R, OP_B, OP_K, OP_BH, OP_KH, OP_V = range(7)
N_DIR_OPS = 6

NN = (((1,), (0,)), ((), ()))
NT = (((1,), (1,)), ((), ()))
TN = (((0,), (0,)), ((), ()))


def _dot(a, b, dims=NN):
    return lax.dot_general(a.astype(BF16), b.astype(BF16), dims, preferred_element_type=F32)


def _bdot(a, b, dims=NN):
    return lax.dot_general(a, b, dims, preferred_element_type=F32)


def _rms(x, g):
    ms = jnp.mean(x * x, axis=-1, keepdims=True)
    return x * lax.rsqrt(ms + EPS_RMS) * g


def _head_ones():
    i = lax.broadcasted_iota(jnp.int32, (D_RWKV, D_RWKV), 0) >> HEAD_SHIFT
    j = lax.broadcasted_iota(jnp.int32, (D_RWKV, D_RWKV), 1) >> HEAD_SHIFT
    return jnp.where(i == j, 1.0, 0.0).astype(BF16)


def _head_sum(x, ones):
    return jnp.dot(x.astype(BF16), ones, preferred_element_type=F32)


def _sigmoid(x):
    return 0.5 * jnp.tanh(0.5 * x) + 0.5


def _mod_row(t):
    return jnp.where(t < N_CTX_TILES, 0, 1 + (t - N_CTX_TILES) // LAT_TILES)


def _mod_chunks(mod_ref, tile, first, count):
    row = pl.ds(_mod_row(tile), 1)
    return [mod_ref[row, (first + i) * D_MODEL:(first + i + 1) * D_MODEL] for i in range(count)]


def _const_spec(shape):
    nd = len(shape)
    return pl.BlockSpec(shape, lambda *_: (0,) * nd, pipeline_mode=pl.Buffered(1))


def _params(n_axes=1):
    return pltpu.CompilerParams(dimension_semantics=("arbitrary",) * n_axes,
                                vmem_limit_bytes=VMEM_LIMIT)


def _mod_kernel(cctx_ref, c_ref, w_ref, b_ref, o_ref):
    w = w_ref[...].astype(BF16)

    def rows(c):
        return jnp.dot((c * jax.nn.sigmoid(c)).astype(BF16), w, preferred_element_type=F32) + b_ref[...]

    o_ref[0:1, :] = rows(cctx_ref[...])
    o_ref[1:1 + N_LAT_SEQ, :] = rows(c_ref[...])
    o_ref[1 + N_LAT_SEQ:, :] = jnp.zeros((MOD_ROWS - 1 - N_LAT_SEQ, o_ref.shape[1]), F32)


def _ffn_body(x, mod, g_ref, w13_ref, w2_ref, ig):
    shift, scale, gate = mod
    xs = [x[i * TM:(i + 1) * TM, :] for i in range(x.shape[0] // TM)]
    hs = [(_rms(xi, g_ref[ig:ig + 1, :]) * (1.0 + scale) + shift).astype(BF16) for xi in xs]
    gus = [_dot(hi, w13_ref[...]) for hi in hs]
    acts = []
    for gu in gus:
        gt = gu[:, :D_FF]
        up = gu[:, D_FF:]
        acts.append((gt * jax.nn.sigmoid(gt) * up).astype(BF16))
    os_ = [_dot(ai, w2_ref[...]) for ai in acts]
    outs = [xi + HALF_STEP * gate * _rms(oi, g_ref[ig + 1:ig + 2, :]) for xi, oi in zip(xs, os_)]
    return jnp.concatenate(outs, axis=0)


def _cast_chunk(step, src_ref, dst_ref, axis):
    size = src_ref.shape[axis]

    @pl.when(step < N_WCHUNK)
    def _():
        start = pl.multiple_of(step * size, size)
        if axis == 0:
            dst_ref[pl.ds(start, size), :] = src_ref[...].astype(BF16)
        else:
            dst_ref[:, pl.ds(start, size)] = src_ref[...].astype(BF16)


def _chunk_spec(w, axis):
    block = list(w.shape)
    block[axis] = w.shape[axis] // N_WCHUNK
    clamp = lambda s: jnp.minimum(s, N_WCHUNK - 1)
    index = (lambda s: (clamp(s), 0)) if axis == 0 else (lambda s: (0, clamp(s)))
    return pl.BlockSpec(tuple(block), index)


def _tile_step(s):
    return jnp.maximum(s - N_WCHUNK, 0)


def _slab_spec(w, n_steps, step=lambda s: s):
    rows = w.shape[0] // n_steps
    return pl.BlockSpec((rows, w.shape[1]), lambda s: (step(s), 0))


def _ffn_first_kernel(xc_ref, xl_ref, g_ref, w13_ref, w2_ref, win_ref, cctx_ref, c_ref, wmod_ref, bmod_ref,
                      o_ref, win_bf_ref, mod_tail_ref, w13_bf, w2_bf, mod_ref):
    s = pl.program_id(0)
    _cast_chunk(s, w13_ref, w13_bf, 1)
    _cast_chunk(s, w2_ref, w2_bf, 0)

    @pl.when(s < N_MOD_HEAD_BLOCKS)
    def _():
        col = pl.multiple_of(s * MOD_SIDE, MOD_SIDE)
        _mod_kernel(cctx_ref, c_ref, wmod_ref, bmod_ref, mod_ref.at[:, pl.ds(col, MOD_SIDE)])

    @pl.when(s >= N_WCHUNK)
    def _():
        x = jnp.where(s - N_WCHUNK < N_FFN_CTX_TILES, xc_ref[...], xl_ref[...])
        mod = _mod_chunks(mod_ref, (s - N_WCHUNK) * (FFN_TM // TM), 0, 3)
        o_ref[...] = _ffn_body(x, mod, g_ref, w13_bf, w2_bf, 0)
        win_bf_ref[...] = win_ref[...].astype(BF16)

        @pl.when(s - N_WCHUNK < N_MOD_SIDE)
        def _():
            _mod_kernel(cctx_ref, c_ref, wmod_ref, bmod_ref, mod_tail_ref)


def _ffn_last_kernel(x_ref, mod_ref, g_ref, w13_ref, w2_ref, oc_ref, ol_ref):
    t = pl.program_id(0)
    mod = _mod_chunks(mod_ref, t * (FFN2_TM // TM), MOD_FFN2, 3)
    out = _ffn_body(x_ref[...], mod, g_ref, w13_ref, w2_ref, 4)

    @pl.when(t < N_FFN2_CTX_TILES)
    def _():
        oc_ref[...] = out

    @pl.when(t >= N_FFN2_CTX_TILES)
    def _():
        ol_ref[...] = out


def _ctx_tile(t):
    return jnp.minimum(t, N_FFN_CTX_TILES - 1), 0


def _lat_tile(t):
    return jnp.maximum(t - N_FFN_CTX_TILES, 0), 0


def _ffn_first(x_ctx, x_lat, norm_g, w13, w2, w_in, c_ctx, c, w_mod, b_mod):
    side = lambda s: jnp.minimum(_tile_step(s), N_MOD_SIDE - 1)

    def mod_block(s):
        return 0, jnp.where(s < N_WCHUNK, jnp.minimum(s, N_MOD_HEAD_BLOCKS - 1), N_MOD_HEAD_BLOCKS + side(s))

    return pl.pallas_call(
        _ffn_first_kernel,
        grid=(N_WCHUNK + N_FFN_TILES,),
        in_specs=[pl.BlockSpec((FFN_TM, D_MODEL), lambda s: _ctx_tile(_tile_step(s))),
                  pl.BlockSpec((FFN_TM, D_MODEL), lambda s: _lat_tile(_tile_step(s))),
                  _const_spec(norm_g.shape),
                  _chunk_spec(w13, 1),
                  _chunk_spec(w2, 0),
                  _slab_spec(w_in, N_FFN_TILES, step=_tile_step),
                  _const_spec(c_ctx.shape),
                  _const_spec(c.shape),
                  pl.BlockSpec((D_MODEL, MOD_SIDE), mod_block),
                  pl.BlockSpec((1, MOD_SIDE), mod_block)],
        out_specs=[pl.BlockSpec((FFN_TM, D_MODEL), lambda s: (_tile_step(s), 0)),
                   _slab_spec(w_in, N_FFN_TILES, step=_tile_step),
                   pl.BlockSpec((MOD_ROWS, MOD_SIDE), lambda s: (0, side(s)))],
        out_shape=[jax.ShapeDtypeStruct((N_TOK, D_MODEL), F32),
                   jax.ShapeDtypeStruct(w_in.shape, BF16),
                   jax.ShapeDtypeStruct((MOD_ROWS, MOD_TAIL * D_MODEL), F32)],
        scratch_shapes=[pltpu.VMEM((D_MODEL, 2 * D_FF), BF16), pltpu.VMEM((D_FF, D_MODEL), BF16),
                        pltpu.VMEM((MOD_ROWS, MOD_HEAD * D_MODEL), F32)],
        compiler_params=_params(),
        name="ffn1",
    )(x_ctx, x_lat, norm_g, w13, w2, w_in, c_ctx, c, w_mod, b_mod)


def _ffn_last(x, mod3, norm_g, w13_bf, w2_bf):
    n_ctx = N_CTX_SEQ * CTX_LEN
    return pl.pallas_call(
        _ffn_last_kernel,
        grid=(N_TOK // FFN2_TM,),
        in_specs=[pl.BlockSpec((FFN2_TM, D_MODEL), lambda t: (t, 0)),
                  _const_spec(mod3.shape),
                  _const_spec(norm_g.shape),
                  _const_spec(w13_bf.shape),
                  _const_spec(w2_bf.shape)],
        out_specs=[pl.BlockSpec((FFN2_TM, D_MODEL), lambda t: (jnp.minimum(t, N_FFN2_CTX_TILES - 1), 0)),
                   pl.BlockSpec((FFN2_TM, D_MODEL), lambda t: (jnp.maximum(t - N_FFN2_CTX_TILES, 0), 0))],
        out_shape=[jax.ShapeDtypeStruct((n_ctx, D_MODEL), F32),
                   jax.ShapeDtypeStruct((N_TOK - n_ctx, D_MODEL), F32)],
        compiler_params=_params(),
        name="ffn2",
    )(x, mod3, norm_g, w13_bf, w2_bf)


def _lora_prep_kernel(mu_ref, dw1_ref, ia1_ref, dw2_ref, ia2_ref, dw0_ref, ia0_ref,
                      w1_o, w2_o, b2_o):
    firsts = (dw1_ref[0], ia1_ref[0], dw1_ref[1], ia1_ref[1])
    seconds = (dw2_ref[0], ia2_ref[0], dw2_ref[1], ia2_ref[1])
    biases = (dw0_ref[0:1, :], ia0_ref[0:1, :], dw0_ref[1:2, :], ia0_ref[1:2, :])
    n = len(firsts) * LORA
    w2_o[...] = jnp.zeros(w2_o.shape, BF16)
    for j in range(len(firsts)):
        mu = mu_ref[:, j:j + 1]
        w1_o[:, j * LORA:(j + 1) * LORA] = ((1.0 - mu) * firsts[j]).astype(BF16)
        w1_o[:, n + j * LORA:n + (j + 1) * LORA] = (mu * firsts[j]).astype(BF16)
        w2_o[j * LORA:(j + 1) * LORA, j * D_RWKV:(j + 1) * D_RWKV] = seconds[j].astype(BF16)
        b2_o[:, j * D_RWKV:(j + 1) * D_RWKV] = biases[j]


def _lora_prep(mu_t, dw1, ia1, dw2, ia2, dw0, ia0):
    n = 4 * LORA
    return pl.pallas_call(
        _lora_prep_kernel,
        out_shape=[jax.ShapeDtypeStruct((D_MODEL, 2 * n), BF16),
                   jax.ShapeDtypeStruct((n, 4 * D_RWKV), BF16),
                   jax.ShapeDtypeStruct((1, 4 * D_RWKV), F32)],
        compiler_params=pltpu.CompilerParams(vmem_limit_bytes=VMEM_LIMIT),
        name="lora_prep",
    )(mu_t, dw1, ia1, dw2, ia2, dw0, ia0)


def _front_group(gi, h, halo_prev, halo_next, is_lat, win_ref, waug_ref, refs, outs):
    w2aug_ref, b2aug_ref, kkw_ref, ka_ref, rk_ref, cw_ref, cbias_ref, wbb_ref, ones_ref = refs
    ops_o, v_o, ends_o, bv_o, sg_o, sa_o, mb_o = outs
    rows = slice(gi * TM, (gi + 1) * TM)
    row = lax.broadcasted_iota(jnp.int32, (TM, 1), 0)
    rin = row & (CHUNK - 1)
    ones = ones_ref[...]

    def proj(lo, hi):
        return _bdot(h, win_ref[:, lo:hi])

    pab = _bdot(h, waug_ref[...])
    rk = proj(0, 2 * D_RWKV)
    r = rk[:, :D_RWKV]
    k = rk[:, D_RWKV:]
    pb = pab[:, 2 * LANES:]
    sh_f = jnp.where(row == 0, halo_prev, pltpu.roll(pb[:, :LANES], 1, 0))
    sh_b = jnp.where(row == TM - 1, halo_next, pltpu.roll(pb[:, LANES:], TM - 1, 0))
    t_in = pab[:, :2 * LANES] + jnp.concatenate([sh_f, sh_b], axis=1)
    lane = lax.broadcasted_iota(jnp.int32, (1, 2 * LANES), 1)
    t_in = jnp.where((lane & HEAD) == 0, jnp.tanh(t_in), t_in)
    za = _dot(t_in, w2aug_ref[...]) + b2aug_ref[...]

    kk = k * kkw_ref[...]
    kkn = kk * lax.rsqrt(_head_sum(kk * kk, ones) + 1e-12)
    ka = ka_ref[...]

    vg = proj(2 * D_RWKV, 4 * D_RWKV)
    v = vg[:, :D_RWKV]
    v_o[rows, :] = v.astype(BF16)
    sg_o[rows, :] = _sigmoid(vg[:, D_RWKV:]).astype(BF16)

    def scan_operands(d):
        lw = -EXP_M05 * _sigmoid(za[:, 2 * d * D_RWKV:(2 * d + 1) * D_RWKV])
        a = _sigmoid(za[:, (2 * d + 1) * D_RWKV:(2 * d + 2) * D_RWKV])
        k_d = k * (1.0 + (a - 1.0) * ka)
        b = kkn * a
        cs = lw
        for s in (1, 2, 4, 8, 16, 32):
            if d == 0:
                cs = cs + jnp.where(rin >= s, pltpu.roll(cs, s, 0), 0.0)
            else:
                cs = cs + jnp.where(rin < CHUNK - s, pltpu.roll(cs, TM - s, 0), 0.0)
        end_row = CHUNK - 1 if d == 0 else 0
        ends = [cs[c * CHUNK + end_row:c * CHUNK + end_row + 1, :] for c in range(N_CHUNK)]
        for c in range(N_CHUNK):
            ends_o[d][gi, c:c + 1, :] = ends[c]
        cs_end = jnp.concatenate([jnp.broadcast_to(e, (CHUNK, D_RWKV)) for e in ends], axis=0)
        dec_inv = jnp.exp(-cs)
        dec_rest = jnp.exp(cs_end - cs)
        o_a, o_r, o_b, o_k, o_bh, o_kh = ops_o[d]
        o_a[rows, :] = (-kkn * jnp.exp(cs - lw)).astype(BF16)
        o_r[rows, :] = (r * jnp.exp(cs)).astype(BF16)
        o_b[rows, :] = (b * dec_inv).astype(BF16)
        o_k[rows, :] = (k_d * dec_inv).astype(BF16)
        o_bh[rows, :] = (b * dec_rest).astype(BF16)
        o_kh[rows, :] = (k_d * dec_rest).astype(BF16)
        return k_d

    conv_in = proj(4 * D_RWKV, 4 * D_RWKV + 3 * D_CONV)
    k_0 = scan_operands(0)
    gate_a = proj(4 * D_RWKV + 3 * D_CONV, 4 * D_RWKV + 3 * D_CONV + D_MODEL)
    sa_o[rows, :] = _sigmoid(gate_a).astype(BF16)
    k_1 = scan_operands(1)
    gate_b = proj(4 * D_RWKV + 3 * D_CONV + D_MODEL, D_IN)

    cgate = conv_in[:, :D_CONV]
    u = conv_in[:, D_CONV:2 * D_CONV] * conv_in[:, 2 * D_CONV:]
    col = row & (GRID_W - 1)
    zl = jnp.logical_or(row == 0, jnp.logical_and(is_lat, col == 0))
    zr = jnp.logical_or(row == TM - 1, jnp.logical_and(is_lat, col == GRID_W - 1))
    left = jnp.where(zl, 0.0, pltpu.roll(u, 1, 0))
    right = jnp.where(zr, 0.0, pltpu.roll(u, TM - 1, 0))
    conv = left * cw_ref[0:1, :] + u * cw_ref[1:2, :] + right * cw_ref[2:3, :] + cbias_ref[...]
    y_b = _dot(cgate * conv, wbb_ref[...])
    mb_o[rows, :] = (_sigmoid(gate_b) * y_b).astype(BF16)

    bv_o[rows, :] = (_head_sum(r * (k_0 + k_1) * rk_ref[...], ones) * v).astype(BF16)


def _front_kernel(x_ref, xp_ref, xn_ref, mod_ref, g_ref, win_ref, waug_ref, *rest):
    refs = list(rest[:N_FRONT_CONSTS])
    outs = rest[N_FRONT_CONSTS:-1]
    wbb_bf = rest[-1]
    ops_o = (outs[0:N_DIR_OPS], outs[N_DIR_OPS:2 * N_DIR_OPS])
    v_o, ends0_o, ends1_o, bv_o, sg_o, sa_o, mb_o = outs[2 * N_DIR_OPS:]
    outs = (ops_o, v_o, (ends0_o, ends1_o), bv_o, sg_o, sa_o, mb_o)

    t = pl.program_id(0)

    @pl.when(t == 0)
    def _():
        wbb_bf[...] = refs[FRONT_WBB][...].astype(BF16)

    refs[FRONT_WBB] = wbb_bf

    is_lat = t >= N_FRONT_CTX_TILES
    i_in = (t - N_FRONT_CTX_TILES) % FRONT_LAT_TILES
    lat_first = jnp.logical_and(is_lat, i_in == 0)
    lat_last = jnp.logical_and(is_lat, i_in == FRONT_LAT_TILES - 1)
    shift, scale = _mod_chunks(mod_ref, t * FRONT_GROUPS, MOD_FRONT, 2)
    g2 = g_ref[2:3, :]

    def pre(x):
        return (_rms(x, g2) * (1.0 + scale) + shift).astype(BF16)

    x = x_ref[...]
    edge = jnp.concatenate([xp_ref[...], x[TM - 8:TM + 8, :], xn_ref[...]], axis=0)
    edge_b = _bdot(pre(edge), waug_ref[:, 2 * LANES:])
    zero = jnp.zeros((1, LANES), F32)
    halo_prev = (jnp.where(jnp.logical_and(is_lat, jnp.logical_not(lat_first)), edge_b[7:8, :LANES], zero),
                 jnp.where(is_lat, edge_b[15:16, :LANES], zero))
    halo_next = (jnp.where(is_lat, edge_b[16:17, LANES:], zero),
                 jnp.where(jnp.logical_and(is_lat, jnp.logical_not(lat_last)), edge_b[24:25, LANES:], zero))

    for gi in range(FRONT_GROUPS):
        h = pre(x[gi * TM:(gi + 1) * TM, :])
        _front_group(gi, h, halo_prev[gi], halo_next[gi], is_lat, win_ref, waug_ref, refs, outs)


def _front(x, mod3, norm_g, w_in_bf, w_aug, consts):
    tok = lambda t: (t, 0)
    rows8 = FRONT_TM // 8
    last8 = N_TOK // 8 - 1
    assert len(consts) == N_FRONT_CONSTS
    out_rb = jax.ShapeDtypeStruct((N_TOK, D_RWKV), BF16)
    out_db = jax.ShapeDtypeStruct((N_TOK, D_MODEL), BF16)
    out_e = jax.ShapeDtypeStruct((N_TILES, N_CHUNK, D_RWKV), F32)
    spec_r = pl.BlockSpec((FRONT_TM, D_RWKV), tok)
    spec_d = pl.BlockSpec((FRONT_TM, D_MODEL), tok)
    spec_e = pl.BlockSpec((FRONT_GROUPS, N_CHUNK, D_RWKV), lambda t: (t, 0, 0))
    n_b = 2 * N_DIR_OPS + 1
    return pl.pallas_call(
        _front_kernel,
        grid=(N_TOK // FRONT_TM,),
        in_specs=[pl.BlockSpec((FRONT_TM, D_MODEL), tok),
                  pl.BlockSpec((8, D_MODEL), lambda t: (jnp.maximum(t * rows8 - 1, 0), 0)),
                  pl.BlockSpec((8, D_MODEL), lambda t: (jnp.minimum((t + 1) * rows8, last8), 0)),
                  _const_spec(mod3.shape)]
                 + [_const_spec(c.shape) for c in [norm_g, w_in_bf, w_aug] + list(consts)],
        out_specs=[spec_r] * n_b + [spec_e] * 2 + [spec_r] * 2 + [spec_d] * 2,
        out_shape=[out_rb] * n_b + [out_e] * 2 + [out_rb] * 2 + [out_db] * 2,
        scratch_shapes=[pltpu.VMEM((D_CONV, D_MODEL), BF16)],
        compiler_params=_params(),
        name="mixer_front",
    )(x, x, x, mod3, norm_g, w_in_bf, w_aug, *consts)


def _stack(x):
    lane_lo = lax.broadcasted_iota(jnp.int32, x.shape, 1) < HEAD
    z = jnp.zeros_like(x)
    return jnp.concatenate([jnp.where(lane_lo, x, z), jnp.where(lane_lo, z, x)], axis=0)


def _scan_kernel(*refs):
    ops = (refs[0:N_DIR_OPS + 1], refs[N_DIR_OPS + 1:2 * N_DIR_OPS + 2])
    (ends0_ref, ends1_ref, s0_ref, w13_ref, w2_ref,
     yf_ref, yb_ref, sout_ref, w13_bf_ref, w2_bf_ref, st_ref) = refs[2 * N_DIR_OPS + 2:]
    ends_ref = (ends0_ref, ends1_ref)
    w13_bf_ref[...] = w13_ref[...].astype(BF16)
    w2_bf_ref[...] = w2_ref[...].astype(BF16)
    step_id = pl.program_id(0)
    is_lat = step_id >= N_SCAN_CTX_STEPS
    lat_first = (step_id - N_SCAN_CTX_STEPS) % SCAN_LAT_STEPS == 0

    @pl.when(jnp.logical_and(is_lat, lat_first))
    def _():
        zero = jnp.zeros((HEAD, HEAD), F32)
        for d in (0, 1):
            for p in range(N_PAIR):
                top = jnp.concatenate([s0_ref[0, d, 2 * p], zero], axis=1)
                bottom = jnp.concatenate([zero, s0_ref[0, d, 2 * p + 1]], axis=1)
                st_ref[d, p] = jnp.concatenate([top, bottom], axis=0)

    ti = lax.broadcasted_iota(jnp.int32, (CHUNK, LANES), 0)
    sj = lax.broadcasted_iota(jnp.int32, (CHUNK, LANES), 1) & (CHUNK - 1)
    m_strict = (sj < ti, sj > ti)
    m_incl = (sj <= ti, sj >= ti)
    eye_cat = jnp.where(sj == ti, 1.0, 0.0)
    bi = lax.broadcasted_iota(jnp.int32, (LANES, LANES), 0)
    bj = lax.broadcasted_iota(jnp.int32, (LANES, LANES), 1)
    blk = (bi >> HEAD_SHIFT) == (bj >> HEAD_SHIFT)
    eye_bd = bi == bj
    zero_bd = jnp.zeros((LANES, LANES), BF16)

    def op(i, u):
        d, p, c = u
        return ops[d][i][c * CHUNK:(c + 1) * CHUNK, p * LANES:(p + 1) * LANES]

    g_mat, w_mat, p_t, q_t = {}, {}, {}, {}

    def local_stages(units):
        low, nak, mrbk = {}, {}, {}
        for u in units:
            lhs = jnp.concatenate([op(OP_A, u), op(OP_R, u)], axis=0)
            rhs = jnp.concatenate([_stack(op(OP_B, u)), _stack(op(OP_K, u))], axis=0)
            gram = _bdot(lhs, rhs, NT)
            d = u[0]
            low[u] = jnp.where(m_strict[d], gram[:CHUNK, :LANES], 0.0)
            nak[u] = jnp.where(m_strict[d], gram[:CHUNK, LANES:], 0.0).astype(BF16)
            mrbk[u] = jnp.concatenate([jnp.where(m_incl[d], gram[CHUNK:, :LANES], 0.0),
                                       jnp.where(m_incl[d], gram[CHUNK:, LANES:], 0.0)], axis=1).astype(BF16)

        inv = {u: eye_cat + low[u] for u in units}
        pwb = {u: low[u].astype(BF16) for u in units}
        for u in units:
            pwb[u] = _bdot(pwb[u], _stack(pwb[u])).astype(BF16)
        for _ in range(4):
            for u in units:
                both = _bdot(pwb[u], jnp.concatenate([_stack(pwb[u]), _stack(inv[u].astype(BF16))], axis=1))
                pwb[u] = both[:, :LANES].astype(BF16)
                inv[u] = inv[u] + both[:, LANES:]
        for u in units:
            inv[u] = inv[u] + _bdot(pwb[u], _stack(inv[u].astype(BF16)))

        sv = {u: _stack(op(OP_V, u)) for u in units}
        nv = {}
        for u in units:
            nv[u] = _bdot(nak[u], sv[u]).astype(BF16)
        x1, x2 = {}, {}
        for u in units:
            x12 = _bdot(inv[u].astype(BF16), jnp.concatenate([_stack(op(OP_A, u)), _stack(nv[u])], axis=1))
            x1[u] = x12[:, :LANES].astype(BF16)
            x2[u] = x12[:, LANES:].astype(BF16)
        for u in units:
            rhs = jnp.concatenate([jnp.concatenate([_stack(x1[u]), _stack(x2[u])], axis=1),
                                   jnp.concatenate([zero_bd, sv[u]], axis=1)], axis=0)
            gw = _bdot(mrbk[u], rhs)
            g_mat[u] = (op(OP_R, u).astype(F32) + gw[:, :LANES]).astype(BF16)
            w_mat[u] = gw[:, LANES:]
        for u in units:
            d, p, c = u
            ct = c % N_CHUNK
            gam = jnp.exp(ends_ref[d][c // N_CHUNK, ct:ct + 1, p * LANES:(p + 1) * LANES])
            lhs = jnp.concatenate([jnp.concatenate([x1[u], x2[u]], axis=1),
                                   jnp.concatenate([jnp.zeros_like(x1[u]), op(OP_V, u)], axis=1)], axis=0)
            pq = _bdot(lhs, jnp.concatenate([op(OP_BH, u), op(OP_KH, u)], axis=0), TN)
            p_t[u] = (jnp.where(blk, pq[:LANES, :], 0.0) + jnp.where(eye_bd, gam, 0.0)).astype(BF16)
            q_t[u] = jnp.where(blk, pq[LANES:, :], 0.0)

    n_c = SCAN_TILES * N_CHUNK
    state = {(d, p): jnp.where(is_lat, st_ref[d, p], 0.0) for d in (0, 1) for p in range(N_PAIR)}
    finals = {}
    y_refs = (yf_ref, yb_ref)

    def chunk_of(d, step):
        return step if d == 0 else n_c - 1 - step

    def chain(step, d, p):
        c = chunk_of(d, step)
        u = (d, p, c)
        sb = state[d, p].astype(BF16)
        y_refs[d][c * CHUNK:(c + 1) * CHUNK, p * LANES:(p + 1) * LANES] = _bdot(g_mat[u], sb, NT) + w_mat[u]
        state[d, p] = _bdot(sb, p_t[u]) + q_t[u]
        if (step + 1) % N_CHUNK == 0:
            finals[d, c // N_CHUNK, p] = state[d, p]
            if step + 1 < n_c:
                state[d, p] = jnp.where(is_lat, state[d, p], 0.0)

    local_stages([(d, p, c) for d in (0, 1) for p in range(N_PAIR) for c in range(n_c)])
    for step in range(n_c):
        for d in (0, 1):
            for p in range(N_PAIR):
                chain(step, d, p)
    for (d, p), s in state.items():
        st_ref[d, p] = s

    @pl.when(jnp.logical_not(is_lat))
    def _():
        for (d, tile, p), s in finals.items():
            sout_ref[tile, 0, d, 2 * p] = s[:HEAD, :HEAD]
            sout_ref[tile, 0, d, 2 * p + 1] = s[HEAD:, HEAD:]


def _scan(ops_f, ops_b, v, ends_f, ends_b, s0, w13, w2):
    def mirror(s):
        u = s - N_SCAN_CTX_STEPS
        return jnp.where(s < N_SCAN_CTX_STEPS, s,
                         N_SCAN_CTX_STEPS + (u // SCAN_LAT_STEPS) * SCAN_LAT_STEPS
                         + (SCAN_LAT_STEPS - 1 - u % SCAN_LAT_STEPS))

    def lat_seq(s):
        return jnp.maximum(s - N_SCAN_CTX_STEPS, 0) // SCAN_LAT_STEPS

    fwd = pl.BlockSpec((SCAN_TM, D_RWKV), lambda s: (s, 0))
    bwd = pl.BlockSpec((SCAN_TM, D_RWKV), lambda s: (mirror(s), 0))
    e_block = (SCAN_TILES, N_CHUNK, D_RWKV)
    st_block = (None, 1, 2, 2 * N_PAIR, HEAD, HEAD)
    out_y = jax.ShapeDtypeStruct((N_TOK, D_RWKV), F32)
    n_steps = N_TOK // SCAN_TM
    return pl.pallas_call(
        _scan_kernel,
        grid=(n_steps,),
        in_specs=[fwd] * (N_DIR_OPS + 1) + [bwd] * (N_DIR_OPS + 1)
                 + [pl.BlockSpec(e_block, lambda s: (s, 0, 0)),
                    pl.BlockSpec(e_block, lambda s: (mirror(s), 0, 0)),
                    pl.BlockSpec(st_block, lambda s: (lat_seq(s), 0, 0, 0, 0, 0)),
                    _slab_spec(w13, n_steps),
                    _slab_spec(w2, n_steps)],
        out_specs=[fwd, bwd,
                   pl.BlockSpec((SCAN_TILES, 1, 2, 2 * N_PAIR, HEAD, HEAD),
                                lambda s: (jnp.minimum(s, N_SCAN_CTX_STEPS - 1), 0, 0, 0, 0, 0)),
                   _slab_spec(w13, n_steps),
                   _slab_spec(w2, n_steps)],
        out_shape=[out_y, out_y,
                   jax.ShapeDtypeStruct((N_CTX_SEQ, 1, 2, 2 * N_PAIR, HEAD, HEAD), F32),
                   jax.ShapeDtypeStruct(w13.shape, BF16),
                   jax.ShapeDtypeStruct(w2.shape, BF16)],
        scratch_shapes=[pltpu.VMEM((2, N_PAIR, LANES, LANES), F32)],
        compiler_params=_params(),
        name="rwkv7_scan",
    )(*ops_f, v, *ops_b, v, ends_f, ends_b, s0, w13, w2)


def _back_kernel(x_ref, yf_ref, yb_ref, bv_ref, sg_ref, sa_ref, mb_ref, mod_ref, g_ref,
                 gng_ref, gnb_ref, wba_f32_ref, wout_f32_ref, ones_ref, o_ref, wba_ref, wout_ref):
    @pl.when(pl.program_id(0) == 0)
    def _():
        wba_ref[...] = wba_f32_ref[...].astype(BF16)
        wout_ref[...] = wout_f32_ref[...].astype(BF16)

    gate, = _mod_chunks(mod_ref, pl.program_id(0) * (BACK_TM // TM), MOD_BACK, 1)
    ones = ones_ref[...]
    groups = [slice(i * TM, (i + 1) * TM) for i in range(BACK_TM // TM)]
    ys = [yf_ref[r, :] + yb_ref[r, :] for r in groups]
    ycs = [y - _head_sum(y, ones) * (1.0 / HEAD) for y in ys]
    vs = [_head_sum(yc * yc, ones) * (1.0 / HEAD) for yc in ycs]
    yns = [yc * lax.rsqrt(v + EPS_GN) * gng_ref[...] + gnb_ref[...] for yc, v in zip(ycs, vs)]
    yas = [_dot((yn + bv_ref[r, :]) * sg_ref[r, :], wba_ref[...]) for r, yn in zip(groups, yns)]
    outs = [_dot(sa_ref[r, :] * ya + mb_ref[r, :], wout_ref[...]) for r, ya in zip(groups, yas)]
    for r, out in zip(groups, outs):
        o_ref[r, :] = x_ref[r, :] + gate * _rms(out, g_ref[3:4, :])


def _back(x, yf, yb, bv, sg, sa, mb, mod3, norm_g, gng, gnb, wba, wout, ones):
    tok = lambda t: (t, 0)
    spec_r = pl.BlockSpec((BACK_TM, D_RWKV), tok)
    spec_d = pl.BlockSpec((BACK_TM, D_MODEL), tok)
    consts = [norm_g, gng, gnb, wba, wout, ones]
    return pl.pallas_call(
        _back_kernel,
        grid=(N_TOK // BACK_TM,),
        in_specs=[spec_d, spec_r, spec_r, spec_r, spec_r, spec_d, spec_d,
                  _const_spec(mod3.shape)]
                 + [_const_spec(c.shape) for c in consts],
        out_specs=spec_d,
        out_shape=jax.ShapeDtypeStruct((N_TOK, D_MODEL), F32),
        scratch_shapes=[pltpu.VMEM(wba.shape, BF16), pltpu.VMEM(wout.shape, BF16)],
        compiler_params=_params(),
        name="mixer_back",
    )(x, yf, yb, bv, sg, sa, mb, mod3, *consts)


def kernel(x_prompt, x_sample, c, state_rwkv, c_ctx, w_mod, b_mod, norm_g, ffn1_w13, ffn1_w2,
           ffn2_w13, ffn2_w2, w_in, mu_shift, decay_w0, decay_w1, decay_w2, iclr_a0, iclr_a1,
           iclr_a2, k_k, k_a, r_k, gn_gain, gn_bias, conv_w, conv_b, w_branch_a, w_branch_b, w_out):
    assert x_prompt.shape == (N_CTX_SEQ, CTX_LEN, D_MODEL) and x_sample.shape == (N_LAT_SEQ, LAT_LEN, D_MODEL)
    assert w_mod.shape[0] == 1, "single trunk layer"

    g = norm_g[0]
    x, w_in_bf, mod3 = _ffn_first(x_prompt.reshape(-1, D_MODEL), x_sample.reshape(-1, D_MODEL), g,
                                  ffn1_w13[0], ffn1_w2[0], w_in[0], c_ctx.reshape(1, D_MODEL), c,
                                  w_mod[0], b_mod)

    row = lambda p: p.reshape(1, -1)
    w_aug, w2_aug, b2_aug = _lora_prep(mu_shift[0].reshape(4, D_MODEL).T, decay_w1[0], iclr_a1[0],
                                       decay_w2[0], iclr_a2[0], decay_w0[0], iclr_a0[0])
    ones = _head_ones()
    front = _front(x, mod3, g, w_in_bf, w_aug,
                   [w2_aug, b2_aug, row(k_k[0]), row(k_a[0]), row(r_k[0]), conv_w[0], row(conv_b[0]),
                    w_branch_b[0], ones])
    ops_f, ops_b = front[0:N_DIR_OPS], front[N_DIR_OPS:2 * N_DIR_OPS]
    v, ends_f, ends_b, bv, sg, sa, mb = front[2 * N_DIR_OPS:]

    yf, yb, s_fin, w13_bf, w2_bf = _scan(ops_f, ops_b, v, ends_f, ends_b, state_rwkv,
                                         ffn2_w13[0], ffn2_w2[0])

    x = _back(x, yf, yb, bv, sg, sa, mb, mod3, g, row(gn_gain[0]), row(gn_bias[0]),
              w_branch_a[0], w_out[0], ones)
    y_ctx, y_lat = _ffn_last(x, mod3, g, w13_bf, w2_bf)

    y_prompt = y_ctx.reshape(N_CTX_SEQ, CTX_LEN, D_MODEL)
    y_sample = y_lat.reshape(N_LAT_SEQ, LAT_LEN, D_MODEL)
    return y_prompt, y_sample, s_fin
```

```python
import jax
import jax.numpy as jnp
from jax import lax
from jax.experimental import pallas as pl
from jax.experimental.pallas import tpu as pltpu

F32 = jnp.float32
BF16 = jnp.bfloat16

D_MODEL = 1024
D_FF = 2816
D_RWKV = 512
D_CONV = 512
HEAD = 64
HEAD_SHIFT = 6
D_IN = 4 * D_RWKV + 3 * D_CONV + 2 * D_MODEL
N_MOD = 9
EPS_RMS = 1e-6
EPS_GN = 64e-5
HALF_STEP = 0.5
EXP_M05 = 0.6065306597126334

N_CTX_SEQ = 16
CTX_LEN = 256
N_LAT_SEQ = 2
LAT_LEN = 2048
GRID_W = 64
N_TOK = N_CTX_SEQ * CTX_LEN + N_LAT_SEQ * LAT_LEN

TM = 256
N_CTX_TILES = N_CTX_SEQ * CTX_LEN // TM
LAT_TILES = LAT_LEN // TM
N_TILES = N_TOK // TM
FRONT_TM = 512
FRONT_GROUPS = FRONT_TM // TM
N_FRONT_CTX_TILES = N_CTX_SEQ * CTX_LEN // FRONT_TM
FRONT_LAT_TILES = LAT_LEN // FRONT_TM
N_FRONT_CONSTS = 9
FRONT_WBB = 7
LORA = 64
SCAN_TM = 512
SCAN_TILES = SCAN_TM // TM
N_SCAN_CTX_STEPS = N_CTX_SEQ * CTX_LEN // SCAN_TM
SCAN_LAT_STEPS = LAT_LEN // SCAN_TM
BACK_TM = 512
FFN_TM = 512
N_FFN_TILES = N_TOK // FFN_TM
N_FFN_CTX_TILES = N_CTX_SEQ * CTX_LEN // FFN_TM
FFN2_TM = 1024
N_FFN2_CTX_TILES = N_CTX_SEQ * CTX_LEN // FFN2_TM
N_WCHUNK = 11
CHUNK = 64
N_CHUNK = TM // CHUNK
LANES = 128
N_PAIR = D_RWKV // LANES
MOD_ROWS = 8
MOD_HEAD = 3
MOD_TAIL = N_MOD - MOD_HEAD
MOD_SIDE = 512
N_MOD_SIDE = MOD_TAIL * D_MODEL // MOD_SIDE
N_MOD_HEAD_BLOCKS = MOD_HEAD * D_MODEL // MOD_SIDE
MOD_FRONT, MOD_BACK, MOD_FFN2 = 0, 2, 3
VMEM_LIMIT = 56 * 1024 * 1024

OP_A, OP_R, OP_B, OP_K, OP_BH, OP_KH, OP_V = range(7)
N_DIR_OPS = 6

NN = (((1,), (0,)), ((), ()))
NT = (((1,), (1,)), ((), ()))
TN = (((0,), (0,)), ((), ()))


def _dot(a, b, dims=NN):
    return lax.dot_general(a.astype(BF16), b.astype(BF16), dims, preferred_element_type=F32)


def _bdot(a, b, dims=NN):
    return lax.dot_general(a, b, dims, preferred_element_type=F32)


def _rms(x, g):
    ms = jnp.mean(x * x, axis=-1, keepdims=True)
    return x * lax.rsqrt(ms + EPS_RMS) * g


def _head_ones():
    i = lax.broadcasted_iota(jnp.int32, (D_RWKV, D_RWKV), 0) >> HEAD_SHIFT
    j = lax.broadcasted_iota(jnp.int32, (D_RWKV, D_RWKV), 1) >> HEAD_SHIFT
    return jnp.where(i == j, 1.0, 0.0).astype(BF16)


def _head_sum(x, ones):
    return jnp.dot(x.astype(BF16), ones, preferred_element_type=F32)


def _sigmoid(x):
    return 0.5 * jnp.tanh(0.5 * x) + 0.5


def _mod_row(t):
    return jnp.where(t < N_CTX_TILES, 0, 1 + (t - N_CTX_TILES) // LAT_TILES)


def _mod_chunks(mod_ref, tile, first, count):
    row = pl.ds(_mod_row(tile), 1)
    return [mod_ref[row, (first + i) * D_MODEL:(first + i + 1) * D_MODEL] for i in range(count)]


def _const_spec(shape):
    nd = len(shape)
    return pl.BlockSpec(shape, lambda *_: (0,) * nd, pipeline_mode=pl.Buffered(1))


def _params(n_axes=1):
    return pltpu.CompilerParams(dimension_semantics=("arbitrary",) * n_axes,
                                vmem_limit_bytes=VMEM_LIMIT)


def _mod_kernel(cctx_ref, c_ref, w_ref, b_ref, o_ref):
    w = w_ref[...].astype(BF16)

    def rows(c):
        return jnp.dot((c * jax.nn.sigmoid(c)).astype(BF16), w, preferred_element_type=F32) + b_ref[...]

    o_ref[0:1, :] = rows(cctx_ref[...])
    o_ref[1:1 + N_LAT_SEQ, :] = rows(c_ref[...])
    o_ref[1 + N_LAT_SEQ:, :] = jnp.zeros((MOD_ROWS - 1 - N_LAT_SEQ, o_ref.shape[1]), F32)


def _ffn_body(x, mod, g_ref, w13_ref, w2_ref, ig):
    shift, scale, gate = mod
    xs = [x[i * TM:(i + 1) * TM, :] for i in range(x.shape[0] // TM)]
    hs = [(_rms(xi, g_ref[ig:ig + 1, :]) * (1.0 + scale) + shift).astype(BF16) for xi in xs]
    gus = [_dot(hi, w13_ref[...]) for hi in hs]
    acts = []
    for gu in gus:
        gt = gu[:, :D_FF]
        up = gu[:, D_FF:]
        acts.append((gt * jax.nn.sigmoid(gt) * up).astype(BF16))
    os_ = [_dot(ai, w2_ref[...]) for ai in acts]
    outs = [xi + HALF_STEP * gate * _rms(oi, g_ref[ig + 1:ig + 2, :]) for xi, oi in zip(xs, os_)]
    return jnp.concatenate(outs, axis=0)


def _cast_chunk(step, src_ref, dst_ref, axis):
    size = src_ref.shape[axis]

    @pl.when(step < N_WCHUNK)
    def _():
        start = pl.multiple_of(step * size, size)
        if axis == 0:
            dst_ref[pl.ds(start, size), :] = src_ref[...].astype(BF16)
        else:
            dst_ref[:, pl.ds(start, size)] = src_ref[...].astype(BF16)


def _chunk_spec(w, axis):
    block = list(w.shape)
    block[axis] = w.shape[axis] // N_WCHUNK
    clamp = lambda s: jnp.minimum(s, N_WCHUNK - 1)
    index = (lambda s: (clamp(s), 0)) if axis == 0 else (lambda s: (0, clamp(s)))
    return pl.BlockSpec(tuple(block), index)


def _tile_step(s):
    return jnp.maximum(s - N_WCHUNK, 0)


def _slab_spec(w, n_steps, step=lambda s: s):
    rows = w.shape[0] // n_steps
    return pl.BlockSpec((rows, w.shape[1]), lambda s: (step(s), 0))


def _ffn_first_kernel(xc_ref, xl_ref, g_ref, w13_ref, w2_ref, win_ref, cctx_ref, c_ref, wmod_ref, bmod_ref,
                      o_ref, win_bf_ref, mod_tail_ref, w13_bf, w2_bf, mod_ref):
    s = pl.program_id(0)
    _cast_chunk(s, w13_ref, w13_bf, 1)
    _cast_chunk(s, w2_ref, w2_bf, 0)

    @pl.when(s < N_MOD_HEAD_BLOCKS)
    def _():
        col = pl.multiple_of(s * MOD_SIDE, MOD_SIDE)
        _mod_kernel(cctx_ref, c_ref, wmod_ref, bmod_ref, mod_ref.at[:, pl.ds(col, MOD_SIDE)])

    @pl.when(s >= N_WCHUNK)
    def _():
        x = jnp.where(s - N_WCHUNK < N_FFN_CTX_TILES, xc_ref[...], xl_ref[...])
        mod = _mod_chunks(mod_ref, (s - N_WCHUNK) * (FFN_TM // TM), 0, 3)
        o_ref[...] = _ffn_body(x, mod, g_ref, w13_bf, w2_bf, 0)
        win_bf_ref[...] = win_ref[...].astype(BF16)

        @pl.when(s - N_WCHUNK < N_MOD_SIDE)
        def _():
            _mod_kernel(cctx_ref, c_ref, wmod_ref, bmod_ref, mod_tail_ref)


def _ffn_last_kernel(x_ref, mod_ref, g_ref, w13_ref, w2_ref, oc_ref, ol_ref):
    t = pl.program_id(0)
    mod = _mod_chunks(mod_ref, t * (FFN2_TM // TM), MOD_FFN2, 3)
    out = _ffn_body(x_ref[...], mod, g_ref, w13_ref, w2_ref, 4)

    @pl.when(t < N_FFN2_CTX_TILES)
    def _():
        oc_ref[...] = out

    @pl.when(t >= N_FFN2_CTX_TILES)
    def _():
        ol_ref[...] = out


def _ctx_tile(t):
    return jnp.minimum(t, N_FFN_CTX_TILES - 1), 0


def _lat_tile(t):
    return jnp.maximum(t - N_FFN_CTX_TILES, 0), 0


def _ffn_first(x_ctx, x_lat, norm_g, w13, w2, w_in, c_ctx, c, w_mod, b_mod):
    side = lambda s: jnp.minimum(_tile_step(s), N_MOD_SIDE - 1)

    def mod_block(s):
        return 0, jnp.where(s < N_WCHUNK, jnp.minimum(s, N_MOD_HEAD_BLOCKS - 1), N_MOD_HEAD_BLOCKS + side(s))

    return pl.pallas_call(
        _ffn_first_kernel,
        grid=(N_WCHUNK + N_FFN_TILES,),
        in_specs=[pl.BlockSpec((FFN_TM, D_MODEL), lambda s: _ctx_tile(_tile_step(s))),
                  pl.BlockSpec((FFN_TM, D_MODEL), lambda s: _lat_tile(_tile_step(s))),
                  _const_spec(norm_g.shape),
                  _chunk_spec(w13, 1),
                  _chunk_spec(w2, 0),
                  _slab_spec(w_in, N_FFN_TILES, step=_tile_step),
                  _const_spec(c_ctx.shape),
                  _const_spec(c.shape),
                  pl.BlockSpec((D_MODEL, MOD_SIDE), mod_block),
                  pl.BlockSpec((1, MOD_SIDE), mod_block)],
        out_specs=[pl.BlockSpec((FFN_TM, D_MODEL), lambda s: (_tile_step(s), 0)),
                   _slab_spec(w_in, N_FFN_TILES, step=_tile_step),
                   pl.BlockSpec((MOD_ROWS, MOD_SIDE), lambda s: (0, side(s)))],
        out_shape=[jax.ShapeDtypeStruct((N_TOK, D_MODEL), F32),
                   jax.ShapeDtypeStruct(w_in.shape, BF16),
                   jax.ShapeDtypeStruct((MOD_ROWS, MOD_TAIL * D_MODEL), F32)],
        scratch_shapes=[pltpu.VMEM((D_MODEL, 2 * D_FF), BF16), pltpu.VMEM((D_FF, D_MODEL), BF16),
                        pltpu.VMEM((MOD_ROWS, MOD_HEAD * D_MODEL), F32)],
        compiler_params=_params(),
        name="ffn1",
    )(x_ctx, x_lat, norm_g, w13, w2, w_in, c_ctx, c, w_mod, b_mod)


def _ffn_last(x, mod3, norm_g, w13_bf, w2_bf):
    n_ctx = N_CTX_SEQ * CTX_LEN
    return pl.pallas_call(
        _ffn_last_kernel,
        grid=(N_TOK // FFN2_TM,),
        in_specs=[pl.BlockSpec((FFN2_TM, D_MODEL), lambda t: (t, 0)),
                  _const_spec(mod3.shape),
                  _const_spec(norm_g.shape),
                  _const_spec(w13_bf.shape),
                  _const_spec(w2_bf.shape)],
        out_specs=[pl.BlockSpec((FFN2_TM, D_MODEL), lambda t: (jnp.minimum(t, N_FFN2_CTX_TILES - 1), 0)),
                   pl.BlockSpec((FFN2_TM, D_MODEL), lambda t: (jnp.maximum(t - N_FFN2_CTX_TILES, 0), 0))],
        out_shape=[jax.ShapeDtypeStruct((n_ctx, D_MODEL), F32),
                   jax.ShapeDtypeStruct((N_TOK - n_ctx, D_MODEL), F32)],
        compiler_params=_params(),
        name="ffn2",
    )(x, mod3, norm_g, w13_bf, w2_bf)


def _lora_prep_kernel(mu_ref, dw1_ref, ia1_ref, dw2_ref, ia2_ref, dw0_ref, ia0_ref,
                      w1_o, w2_o, b2_o):
    firsts = (dw1_ref[0], ia1_ref[0], dw1_ref[1], ia1_ref[1])
    seconds = (dw2_ref[0], ia2_ref[0], dw2_ref[1], ia2_ref[1])
    biases = (dw0_ref[0:1, :], ia0_ref[0:1, :], dw0_ref[1:2, :], ia0_ref[1:2, :])
    n = len(firsts) * LORA
    w2_o[...] = jnp.zeros(w2_o.shape, BF16)
    for j in range(len(firsts)):
        mu = mu_ref[:, j:j + 1]
        w1_o[:, j * LORA:(j + 1) * LORA] = ((1.0 - mu) * firsts[j]).astype(BF16)
        w1_o[:, n + j * LORA:n + (j + 1) * LORA] = (mu * firsts[j]).astype(BF16)
        w2_o[j * LORA:(j + 1) * LORA, j * D_RWKV:(j + 1) * D_RWKV] = seconds[j].astype(BF16)
        b2_o[:, j * D_RWKV:(j + 1) * D_RWKV] = biases[j]


def _lora_prep(mu_t, dw1, ia1, dw2, ia2, dw0, ia0):
    n = 4 * LORA
    return pl.pallas_call(
        _lora_prep_kernel,
        out_shape=[jax.ShapeDtypeStruct((D_MODEL, 2 * n), BF16),
                   jax.ShapeDtypeStruct((n, 4 * D_RWKV), BF16),
                   jax.ShapeDtypeStruct((1, 4 * D_RWKV), F32)],
        compiler_params=pltpu.CompilerParams(vmem_limit_bytes=VMEM_LIMIT),
        name="lora_prep",
    )(mu_t, dw1, ia1, dw2, ia2, dw0, ia0)


def _front_group(gi, h, halo_prev, halo_next, is_lat, win_ref, waug_ref, refs, outs):
    w2aug_ref, b2aug_ref, kkw_ref, ka_ref, rk_ref, cw_ref, cbias_ref, wbb_ref, ones_ref = refs
    ops_o, v_o, ends_o, bv_o, sg_o, sa_o, mb_o = outs
    rows = slice(gi * TM, (gi + 1) * TM)
    row = lax.broadcasted_iota(jnp.int32, (TM, 1), 0)
    rin = row & (CHUNK - 1)
    ones = ones_ref[...]

    def proj(lo, hi):
        return _bdot(h, win_ref[:, lo:hi])

    pab = _bdot(h, waug_ref[...])
    rk = proj(0, 2 * D_RWKV)
    r = rk[:, :D_RWKV]
    k = rk[:, D_RWKV:]
    pb = pab[:, 2 * LANES:]
    sh_f = jnp.where(row == 0, halo_prev, pltpu.roll(pb[:, :LANES], 1, 0))
    sh_b = jnp.where(row == TM - 1, halo_next, pltpu.roll(pb[:, LANES:], TM - 1, 0))
    t_in = pab[:, :2 * LANES] + jnp.concatenate([sh_f, sh_b], axis=1)
    lane = lax.broadcasted_iota(jnp.int32, (1, 2 * LANES), 1)
    t_in = jnp.where((lane & HEAD) == 0, jnp.tanh(t_in), t_in)
    za = _dot(t_in, w2aug_ref[...]) + b2aug_ref[...]

    kk = k * kkw_ref[...]
    kkn = kk * lax.rsqrt(_head_sum(kk * kk, ones) + 1e-12)
    ka = ka_ref[...]

    vg = proj(2 * D_RWKV, 4 * D_RWKV)
    v = vg[:, :D_RWKV]
    v_o[rows, :] = v.astype(BF16)
    sg_o[rows, :] = _sigmoid(vg[:, D_RWKV:]).astype(BF16)

    def scan_operands(d):
        lw = -EXP_M05 * _sigmoid(za[:, 2 * d * D_RWKV:(2 * d + 1) * D_RWKV])
        a = _sigmoid(za[:, (2 * d + 1) * D_RWKV:(2 * d + 2) * D_RWKV])
        k_d = k * (1.0 + (a - 1.0) * ka)
        b = kkn * a
        cs = lw
        for s in (1, 2, 4, 8, 16, 32):
            if d == 0:
                cs = cs + jnp.where(rin >= s, pltpu.roll(cs, s, 0), 0.0)
            else:
                cs = cs + jnp.where(rin < CHUNK - s, pltpu.roll(cs, TM - s, 0), 0.0)
        end_row = CHUNK - 1 if d == 0 else 0
        ends = [cs[c * CHUNK + end_row:c * CHUNK + end_row + 1, :] for c in range(N_CHUNK)]
        for c in range(N_CHUNK):
            ends_o[d][gi, c:c + 1, :] = ends[c]
        cs_end = jnp.concatenate([jnp.broadcast_to(e, (CHUNK, D_RWKV)) for e in ends], axis=0)
        dec_inv = jnp.exp(-cs)
        dec_rest = jnp.exp(cs_end - cs)
        o_a, o_r, o_b, o_k, o_bh, o_kh = ops_o[d]
        o_a[rows, :] = (-kkn * jnp.exp(cs - lw)).astype(BF16)
        o_r[rows, :] = (r * jnp.exp(cs)).astype(BF16)
        o_b[rows, :] = (b * dec_inv).astype(BF16)
        o_k[rows, :] = (k_d * dec_inv).astype(BF16)
        o_bh[rows, :] = (b * dec_rest).astype(BF16)
        o_kh[rows, :] = (k_d * dec_rest).astype(BF16)
        return k_d

    conv_in = proj(4 * D_RWKV, 4 * D_RWKV + 3 * D_CONV)
    k_0 = scan_operands(0)
    gate_a = proj(4 * D_RWKV + 3 * D_CONV, 4 * D_RWKV + 3 * D_CONV + D_MODEL)
    sa_o[rows, :] = _sigmoid(gate_a).astype(BF16)
    k_1 = scan_operands(1)
    gate_b = proj(4 * D_RWKV + 3 * D_CONV + D_MODEL, D_IN)

    cgate = conv_in[:, :D_CONV]
    u = conv_in[:, D_CONV:2 * D_CONV] * conv_in[:, 2 * D_CONV:]
    col = row & (GRID_W - 1)
    zl = jnp.logical_or(row == 0, jnp.logical_and(is_lat, col == 0))
    zr = jnp.logical_or(row == TM - 1, jnp.logical_and(is_lat, col == GRID_W - 1))
    left = jnp.where(zl, 0.0, pltpu.roll(u, 1, 0))
    right = jnp.where(zr, 0.0, pltpu.roll(u, TM - 1, 0))
    conv = left * cw_ref[0:1, :] + u * cw_ref[1:2, :] + right * cw_ref[2:3, :] + cbias_ref[...]
    y_b = _dot(cgate * conv, wbb_ref[...])
    mb_o[rows, :] = (_sigmoid(gate_b) * y_b).astype(BF16)

    bv_o[rows, :] = (_head_sum(r * (k_0 + k_1) * rk_ref[...], ones) * v).astype(BF16)


def _front_kernel(x_ref, xp_ref, xn_ref, mod_ref, g_ref, win_ref, waug_ref, *rest):
    refs = list(rest[:N_FRONT_CONSTS])
    outs = rest[N_FRONT_CONSTS:-1]
    wbb_bf = rest[-1]
    ops_o = (outs[0:N_DIR_OPS], outs[N_DIR_OPS:2 * N_DIR_OPS])
    v_o, ends0_o, ends1_o, bv_o, sg_o, sa_o, mb_o = outs[2 * N_DIR_OPS:]
    outs = (ops_o, v_o, (ends0_o, ends1_o), bv_o, sg_o, sa_o, mb_o)

    t = pl.program_id(0)

    @pl.when(t == 0)
    def _():
        wbb_bf[...] = refs[FRONT_WBB][...].astype(BF16)

    refs[FRONT_WBB] = wbb_bf

    is_lat = t >= N_FRONT_CTX_TILES
    i_in = (t - N_FRONT_CTX_TILES) % FRONT_LAT_TILES
    lat_first = jnp.logical_and(is_lat, i_in == 0)
    lat_last = jnp.logical_and(is_lat, i_in == FRONT_LAT_TILES - 1)
    shift, scale = _mod_chunks(mod_ref, t * FRONT_GROUPS, MOD_FRONT, 2)
    g2 = g_ref[2:3, :]

    def pre(x):
        return (_rms(x, g2) * (1.0 + scale) + shift).astype(BF16)

    x = x_ref[...]
    edge = jnp.concatenate([xp_ref[...], x[TM - 8:TM + 8, :], xn_ref[...]], axis=0)
    edge_b = _bdot(pre(edge), waug_ref[:, 2 * LANES:])
    zero = jnp.zeros((1, LANES), F32)
    halo_prev = (jnp.where(jnp.logical_and(is_lat, jnp.logical_not(lat_first)), edge_b[7:8, :LANES], zero),
                 jnp.where(is_lat, edge_b[15:16, :LANES], zero))
    halo_next = (jnp.where(is_lat, edge_b[16:17, LANES:], zero),
                 jnp.where(jnp.logical_and(is_lat, jnp.logical_not(lat_last)), edge_b[24:25, LANES:], zero))

    for gi in range(FRONT_GROUPS):
        h = pre(x[gi * TM:(gi + 1) * TM, :])
        _front_group(gi, h, halo_prev[gi], halo_next[gi], is_lat, win_ref, waug_ref, refs, outs)


def _front(x, mod3, norm_g, w_in_bf, w_aug, consts):
    tok = lambda t: (t, 0)
    rows8 = FRONT_TM // 8
    last8 = N_TOK // 8 - 1
    assert len(consts) == N_FRONT_CONSTS
    out_rb = jax.ShapeDtypeStruct((N_TOK, D_RWKV), BF16)
    out_db = jax.ShapeDtypeStruct((N_TOK, D_MODEL), BF16)
    out_e = jax.ShapeDtypeStruct((N_TILES, N_CHUNK, D_RWKV), F32)
    spec_r = pl.BlockSpec((FRONT_TM, D_RWKV), tok)
    spec_d = pl.BlockSpec((FRONT_TM, D_MODEL), tok)
    spec_e = pl.BlockSpec((FRONT_GROUPS, N_CHUNK, D_RWKV), lambda t: (t, 0, 0))
    n_b = 2 * N_DIR_OPS + 1
    return pl.pallas_call(
        _front_kernel,
        grid=(N_TOK // FRONT_TM,),
        in_specs=[pl.BlockSpec((FRONT_TM, D_MODEL), tok),
                  pl.BlockSpec((8, D_MODEL), lambda t: (jnp.maximum(t * rows8 - 1, 0), 0)),
                  pl.BlockSpec((8, D_MODEL), lambda t: (jnp.minimum((t + 1) * rows8, last8), 0)),
                  _const_spec(mod3.shape)]
                 + [_const_spec(c.shape) for c in [norm_g, w_in_bf, w_aug] + list(consts)],
        out_specs=[spec_r] * n_b + [spec_e] * 2 + [spec_r] * 2 + [spec_d] * 2,
        out_shape=[out_rb] * n_b + [out_e] * 2 + [out_rb] * 2 + [out_db] * 2,
        scratch_shapes=[pltpu.VMEM((D_CONV, D_MODEL), BF16)],
        compiler_params=_params(),
        name="mixer_front",
    )(x, x, x, mod3, norm_g, w_in_bf, w_aug, *consts)


def _stack(x):
    lane_lo = lax.broadcasted_iota(jnp.int32, x.shape, 1) < HEAD
    z = jnp.zeros_like(x)
    return jnp.concatenate([jnp.where(lane_lo, x, z), jnp.where(lane_lo, z, x)], axis=0)


def _scan_kernel(*refs):
    ops = (refs[0:N_DIR_OPS + 1], refs[N_DIR_OPS + 1:2 * N_DIR_OPS + 2])
    (ends0_ref, ends1_ref, s0_ref, w13_ref, w2_ref,
     yf_ref, yb_ref, sout_ref, w13_bf_ref, w2_bf_ref, st_ref) = refs[2 * N_DIR_OPS + 2:]
    ends_ref = (ends0_ref, ends1_ref)
    w13_bf_ref[...] = w13_ref[...].astype(BF16)
    w2_bf_ref[...] = w2_ref[...].astype(BF16)
    step_id = pl.program_id(0)
    is_lat = step_id >= N_SCAN_CTX_STEPS
    lat_first = (step_id - N_SCAN_CTX_STEPS) % SCAN_LAT_STEPS == 0

    @pl.when(jnp.logical_and(is_lat, lat_first))
    def _():
        zero = jnp.zeros((HEAD, HEAD), F32)
        for d in (0, 1):
            for p in range(N_PAIR):
                top = jnp.concatenate([s0_ref[0, d, 2 * p], zero], axis=1)
                bottom = jnp.concatenate([zero, s0_ref[0, d, 2 * p + 1]], axis=1)
                st_ref[d, p] = jnp.concatenate([top, bottom], axis=0).T

    ti = lax.broadcasted_iota(jnp.int32, (CHUNK, LANES), 0)
    sj = lax.broadcasted_iota(jnp.int32, (CHUNK, LANES), 1) & (CHUNK - 1)
    m_strict = (sj < ti, sj > ti)
    m_incl = (sj <= ti, sj >= ti)
    eye_cat = jnp.where(sj == ti, 1.0, 0.0)
    bi = lax.broadcasted_iota(jnp.int32, (LANES, LANES), 0)
    bj = lax.broadcasted_iota(jnp.int32, (LANES, LANES), 1)
    blk = (bi >> HEAD_SHIFT) == (bj >> HEAD_SHIFT)
    eye_bd = bi == bj
    zero_bd = jnp.zeros((LANES, LANES), BF16)

    def op(i, u):
        d, p, c = u
        return ops[d][i][c * CHUNK:(c + 1) * CHUNK, p * LANES:(p + 1) * LANES]

    g_mat, w_mat, pg, q_mat = {}, {}, {}, {}

    def local_stages(units):
        low, nak, mrbk = {}, {}, {}
        for u in units:
            lhs = jnp.concatenate([op(OP_A, u), op(OP_R, u)], axis=0)
            rhs = jnp.concatenate([_stack(op(OP_B, u)), _stack(op(OP_K, u))], axis=0)
            gram = _bdot(lhs, rhs, NT)
            d = u[0]
            low[u] = jnp.where(m_strict[d], gram[:CHUNK, :LANES], 0.0)
            nak[u] = jnp.where(m_strict[d], gram[:CHUNK, LANES:], 0.0).astype(BF16)
            mrbk[u] = jnp.concatenate([jnp.where(m_incl[d], gram[CHUNK:, :LANES], 0.0),
                                       jnp.where(m_incl[d], gram[CHUNK:, LANES:], 0.0)], axis=1).astype(BF16)

        inv = {u: eye_cat + low[u] for u in units}
        pwb = {u: low[u].astype(BF16) for u in units}
        for u in units:
            pwb[u] = _bdot(pwb[u], _stack(pwb[u])).astype(BF16)
        for _ in range(4):
            for u in units:
                both = _bdot(pwb[u], jnp.concatenate([_stack(pwb[u]), _stack(inv[u].astype(BF16))], axis=1))
                pwb[u] = both[:, :LANES].astype(BF16)
                inv[u] = inv[u] + both[:, LANES:]
        for u in units:
            inv[u] = inv[u] + _bdot(pwb[u], _stack(inv[u].astype(BF16)))

        sv = {u: _stack(op(OP_V, u)) for u in units}
        nv = {}
        for u in units:
            nv[u] = _bdot(nak[u], sv[u]).astype(BF16)
        x1, x2 = {}, {}
        for u in units:
            x12 = _bdot(inv[u].astype(BF16), jnp.concatenate([_stack(op(OP_A, u)), _stack(nv[u])], axis=1))
            x1[u] = x12[:, :LANES].astype(BF16)
            x2[u] = x12[:, LANES:].astype(BF16)
        for u in units:
            rhs = jnp.concatenate([jnp.concatenate([_stack(x1[u]), _stack(x2[u])], axis=1),
                                   jnp.concatenate([zero_bd, sv[u]], axis=1)], axis=0)
            gw = _bdot(mrbk[u], rhs)
            g_mat[u] = (op(OP_R, u).astype(F32) + gw[:, :LANES]).astype(BF16)
            w_mat[u] = gw[:, LANES:]
        for u in units:
            d, p, c = u
            ct = c % N_CHUNK
            gam = jnp.exp(ends_ref[d][c // N_CHUNK, ct:ct + 1, p * LANES:(p + 1) * LANES])
            rhs = jnp.concatenate([jnp.concatenate([x1[u], x2[u]], axis=1),
                                   jnp.concatenate([jnp.zeros_like(x1[u]), op(OP_V, u)], axis=1)], axis=0)
            pq = _bdot(jnp.concatenate([op(OP_BH, u), op(OP_KH, u)], axis=0), rhs, TN)
            p_mat = (jnp.where(blk, pq[:, :LANES], 0.0) + jnp.where(eye_bd, gam, 0.0)).astype(BF16)
            pg[u] = jnp.concatenate([p_mat, g_mat[u]], axis=0)
            q_mat[u] = jnp.where(blk, pq[:, LANES:], 0.0)

    n_c = SCAN_TILES * N_CHUNK
    state = {(d, p): jnp.where(is_lat, st_ref[d, p], 0.0) for d in (0, 1) for p in range(N_PAIR)}
    finals = {}
    y_refs = (yf_ref, yb_ref)

    def chunk_of(d, step):
        return step if d == 0 else n_c - 1 - step

    def chain(step, d, p):
        c = chunk_of(d, step)
        u = (d, p, c)
        both = _bdot(pg[u], state[d, p].astype(BF16))
        y_refs[d][c * CHUNK:(c + 1) * CHUNK, p * LANES:(p + 1) * LANES] = both[LANES:, :] + w_mat[u]
        state[d, p] = both[:LANES, :] + q_mat[u]
        if (step + 1) % N_CHUNK == 0:
            finals[d, c // N_CHUNK, p] = state[d, p]
            if step + 1 < n_c:
                state[d, p] = jnp.where(is_lat, state[d, p], 0.0)

    local_stages([(d, p, c) for d in (0, 1) for p in range(N_PAIR) for c in range(n_c)])
    for step in range(n_c):
        for d in (0, 1):
            for p in range(N_PAIR):
                chain(step, d, p)
    for (d, p), s in state.items():
        st_ref[d, p] = s

    @pl.when(jnp.logical_not(is_lat))
    def _():
        for (d, tile, p), h in finals.items():
            s = h.T
            sout_ref[tile, 0, d, 2 * p] = s[:HEAD, :HEAD]
            sout_ref[tile, 0, d, 2 * p + 1] = s[HEAD:, HEAD:]


def _scan(ops_f, ops_b, v, ends_f, ends_b, s0, w13, w2):
    def mirror(s):
        u = s - N_SCAN_CTX_STEPS
        return jnp.where(s < N_SCAN_CTX_STEPS, s,
                         N_SCAN_CTX_STEPS + (u // SCAN_LAT_STEPS) * SCAN_LAT_STEPS
                         + (SCAN_LAT_STEPS - 1 - u % SCAN_LAT_STEPS))

    def lat_seq(s):
        return jnp.maximum(s - N_SCAN_CTX_STEPS, 0) // SCAN_LAT_STEPS

    fwd = pl.BlockSpec((SCAN_TM, D_RWKV), lambda s: (s, 0))
    bwd = pl.BlockSpec((SCAN_TM, D_RWKV), lambda s: (mirror(s), 0))
    e_block = (SCAN_TILES, N_CHUNK, D_RWKV)
    st_block = (None, 1, 2, 2 * N_PAIR, HEAD, HEAD)
    out_y = jax.ShapeDtypeStruct((N_TOK, D_RWKV), F32)
    n_steps = N_TOK // SCAN_TM
    return pl.pallas_call(
        _scan_kernel,
        grid=(n_steps,),
        in_specs=[fwd] * (N_DIR_OPS + 1) + [bwd] * (N_DIR_OPS + 1)
                 + [pl.BlockSpec(e_block, lambda s: (s, 0, 0)),
                    pl.BlockSpec(e_block, lambda s: (mirror(s), 0, 0)),
                    pl.BlockSpec(st_block, lambda s: (lat_seq(s), 0, 0, 0, 0, 0)),
                    _slab_spec(w13, n_steps),
                    _slab_spec(w2, n_steps)],
        out_specs=[fwd, bwd,
                   pl.BlockSpec((SCAN_TILES, 1, 2, 2 * N_PAIR, HEAD, HEAD),
                                lambda s: (jnp.minimum(s, N_SCAN_CTX_STEPS - 1), 0, 0, 0, 0, 0)),
                   _slab_spec(w13, n_steps),
                   _slab_spec(w2, n_steps)],
        out_shape=[out_y, out_y,
                   jax.ShapeDtypeStruct((N_CTX_SEQ, 1, 2, 2 * N_PAIR, HEAD, HEAD), F32),
                   jax.ShapeDtypeStruct(w13.shape, BF16),
                   jax.ShapeDtypeStruct(w2.shape, BF16)],
        scratch_shapes=[pltpu.VMEM((2, N_PAIR, LANES, LANES), F32)],
        compiler_params=_params(),
        name="rwkv7_scan",
    )(*ops_f, v, *ops_b, v, ends_f, ends_b, s0, w13, w2)


def _back_kernel(x_ref, yf_ref, yb_ref, bv_ref, sg_ref, sa_ref, mb_ref, mod_ref, g_ref,
                 gng_ref, gnb_ref, wba_f32_ref, wout_f32_ref, ones_ref, o_ref, wba_ref, wout_ref):
    @pl.when(pl.program_id(0) == 0)
    def _():
        wba_ref[...] = wba_f32_ref[...].astype(BF16)
        wout_ref[...] = wout_f32_ref[...].astype(BF16)

    gate, = _mod_chunks(mod_ref, pl.program_id(0) * (BACK_TM // TM), MOD_BACK, 1)
    ones = ones_ref[...]
    groups = [slice(i * TM, (i + 1) * TM) for i in range(BACK_TM // TM)]
    ys = [yf_ref[r, :] + yb_ref[r, :] for r in groups]
    ycs = [y - _head_sum(y, ones) * (1.0 / HEAD) for y in ys]
    vs = [_head_sum(yc * yc, ones) * (1.0 / HEAD) for yc in ycs]
    yns = [yc * lax.rsqrt(v + EPS_GN) * gng_ref[...] + gnb_ref[...] for yc, v in zip(ycs, vs)]
    yas = [_dot((yn + bv_ref[r, :]) * sg_ref[r, :], wba_ref[...]) for r, yn in zip(groups, yns)]
    outs = [_dot(sa_ref[r, :] * ya + mb_ref[r, :], wout_ref[...]) for r, ya in zip(groups, yas)]
    for r, out in zip(groups, outs):
        o_ref[r, :] = x_ref[r, :] + gate * _rms(out, g_ref[3:4, :])


def _back(x, yf, yb, bv, sg, sa, mb, mod3, norm_g, gng, gnb, wba, wout, ones):
    tok = lambda t: (t, 0)
    spec_r = pl.BlockSpec((BACK_TM, D_RWKV), tok)
    spec_d = pl.BlockSpec((BACK_TM, D_MODEL), tok)
    consts = [norm_g, gng, gnb, wba, wout, ones]
    return pl.pallas_call(
        _back_kernel,
        grid=(N_TOK // BACK_TM,),
        in_specs=[spec_d, spec_r, spec_r, spec_r, spec_r, spec_d, spec_d,
                  _const_spec(mod3.shape)]
                 + [_const_spec(c.shape) for c in consts],
        out_specs=spec_d,
        out_shape=jax.ShapeDtypeStruct((N_TOK, D_MODEL), F32),
        scratch_shapes=[pltpu.VMEM(wba.shape, BF16), pltpu.VMEM(wout.shape, BF16)],
        compiler_params=_params(),
        name="mixer_back",
    )(x, yf, yb, bv, sg, sa, mb, mod3, *consts)


def kernel(x_prompt, x_sample, c, state_rwkv, c_ctx, w_mod, b_mod, norm_g, ffn1_w13, ffn1_w2,
           ffn2_w13, ffn2_w2, w_in, mu_shift, decay_w0, decay_w1, decay_w2, iclr_a0, iclr_a1,
           iclr_a2, k_k, k_a, r_k, gn_gain, gn_bias, conv_w, conv_b, w_branch_a, w_branch_b, w_out):
    assert x_prompt.shape == (N_CTX_SEQ, CTX_LEN, D_MODEL) and x_sample.shape == (N_LAT_SEQ, LAT_LEN, D_MODEL)
    assert w_mod.shape[0] == 1, "single trunk layer"

    g = norm_g[0]
    x, w_in_bf, mod3 = _ffn_first(x_prompt.reshape(-1, D_MODEL), x_sample.reshape(-1, D_MODEL), g,
                                  ffn1_w13[0], ffn1_w2[0], w_in[0], c_ctx.reshape(1, D_MODEL), c,
                                  w_mod[0], b_mod)

    row = lambda p: p.reshape(1, -1)
    w_aug, w2_aug, b2_aug = _lora_prep(mu_shift[0].reshape(4, D_MODEL).T, decay_w1[0], iclr_a1[0],
                                       decay_w2[0], iclr_a2[0], decay_w0[0], iclr_a0[0])
    ones = _head_ones()
    front = _front(x, mod3, g, w_in_bf, w_aug,
                   [w2_aug, b2_aug, row(k_k[0]), row(k_a[0]), row(r_k[0]), conv_w[0], row(conv_b[0]),
                    w_branch_b[0], ones])
    ops_f, ops_b = front[0:N_DIR_OPS], front[N_DIR_OPS:2 * N_DIR_OPS]
    v, ends_f, ends_b, bv, sg, sa, mb = front[2 * N_DIR_OPS:]

    yf, yb, s_fin, w13_bf, w2_bf = _scan(ops_f, ops_b, v, ends_f, ends_b, state_rwkv,
                                         ffn2_w13[0], ffn2_w2[0])

    x = _back(x, yf, yb, bv, sg, sa, mb, mod3, g, row(gn_gain[0]), row(gn_bias[0]),
              w_branch_a[0], w_out[0], ones)
    y_ctx, y_lat = _ffn_last(x, mod3, g, w13_bf, w2_bf)

    y_prompt = y_ctx.reshape(N_CTX_SEQ, CTX_LEN, D_MODEL)
    y_sample = y_lat.reshape(N_LAT_SEQ, LAT_LEN, D_MODEL)
    return y_prompt, y_sample, s_fin
```

```python
import jax
import jax.numpy as jnp
from jax import lax
from jax.experimental import pallas as pl
from jax.experimental.pallas import tpu as pltpu

F32 = jnp.float32
BF16 = jnp.bfloat16

D_MODEL = 1024
D_FF = 2816
D_RWKV = 512
D_CONV = 512
HEAD = 64
HEAD_SHIFT = 6
D_IN = 4 * D_RWKV + 3 * D_CONV + 2 * D_MODEL
N_MOD = 9
EPS_RMS = 1e-6
EPS_GN = 64e-5
HALF_STEP = 0.5
EXP_M05 = 0.6065306597126334

N_CTX_SEQ = 16
CTX_LEN = 256
N_LAT_SEQ = 2
LAT_LEN = 2048
GRID_W = 64
N_TOK = N_CTX_SEQ * CTX_LEN + N_LAT_SEQ * LAT_LEN

TM = 256
N_CTX_TILES = N_CTX_SEQ * CTX_LEN // TM
LAT_TILES = LAT_LEN // TM
N_TILES = N_TOK // TM
FRONT_TM = 512
FRONT_GROUPS = FRONT_TM // TM
N_FRONT_CTX_TILES = N_CTX_SEQ * CTX_LEN // FRONT_TM
FRONT_LAT_TILES = LAT_LEN // FRONT_TM
N_FRONT_CONSTS = 9
FRONT_WBB = 7
LORA = 64
SCAN_TM = 512
SCAN_TILES = SCAN_TM // TM
N_SCAN_CTX_STEPS = N_CTX_SEQ * CTX_LEN // SCAN_TM
SCAN_LAT_STEPS = LAT_LEN // SCAN_TM
BACK_TM = 512
FFN_TM = 512
N_FFN_TILES = N_TOK // FFN_TM
N_FFN_CTX_TILES = N_CTX_SEQ * CTX_LEN // FFN_TM
FFN2_TM = 1024
N_FFN2_CTX_TILES = N_CTX_SEQ * CTX_LEN // FFN2_TM
N_WCHUNK = 11
CHUNK = 64
N_CHUNK = TM // CHUNK
LANES = 128
N_PAIR = D_RWKV // LANES
MOD_ROWS = 8
MOD_HEAD = 3
MOD_TAIL = N_MOD - MOD_HEAD
MOD_SIDE = 512
N_MOD_SIDE = MOD_TAIL * D_MODEL // MOD_SIDE
N_MOD_HEAD_BLOCKS = MOD_HEAD * D_MODEL // MOD_SIDE
MOD_FRONT, MOD_BACK, MOD_FFN2 = 0, 2, 3
VMEM_LIMIT = 56 * 1024 * 1024

OP_A, OP_R, OP_B, OP_K, OP_BH, OP_KH, OP_V = range(7)
N_DIR_OPS = 6

NN = (((1,), (0,)), ((), ()))
NT = (((1,), (1,)), ((), ()))
TN = (((0,), (0,)), ((), ()))


def _dot(a, b, dims=NN):
    return lax.dot_general(a.astype(BF16), b.astype(BF16), dims, preferred_element_type=F32)


def _bdot(a, b, dims=NN):
    return lax.dot_general(a, b, dims, preferred_element_type=F32)


def _rms(x, g):
    ms = jnp.mean(x * x, axis=-1, keepdims=True)
    return x * lax.rsqrt(ms + EPS_RMS) * g


def _head_ones():
    i = lax.broadcasted_iota(jnp.int32, (D_RWKV, D_RWKV), 0) >> HEAD_SHIFT
    j = lax.broadcasted_iota(jnp.int32, (D_RWKV, D_RWKV), 1) >> HEAD_SHIFT
    return jnp.where(i == j, 1.0, 0.0).astype(BF16)


def _head_sum(x, ones):
    return jnp.dot(x.astype(BF16), ones, preferred_element_type=F32)


def _sigmoid(x):
    return 0.5 * jnp.tanh(0.5 * x) + 0.5


def _mod_row(t):
    return jnp.where(t < N_CTX_TILES, 0, 1 + (t - N_CTX_TILES) // LAT_TILES)


def _mod_chunks(mod_ref, tile, first, count):
    row = pl.ds(_mod_row(tile), 1)
    return [mod_ref[row, (first + i) * D_MODEL:(first + i + 1) * D_MODEL] for i in range(count)]


def _const_spec(shape):
    nd = len(shape)
    return pl.BlockSpec(shape, lambda *_: (0,) * nd, pipeline_mode=pl.Buffered(1))


def _params(n_axes=1):
    return pltpu.CompilerParams(dimension_semantics=("arbitrary",) * n_axes,
                                vmem_limit_bytes=VMEM_LIMIT)


def _mod_kernel(cctx_ref, c_ref, w_ref, b_ref, o_ref):
    w = w_ref[...].astype(BF16)

    def rows(c):
        return jnp.dot((c * jax.nn.sigmoid(c)).astype(BF16), w, preferred_element_type=F32) + b_ref[...]

    o_ref[0:1, :] = rows(cctx_ref[...])
    o_ref[1:1 + N_LAT_SEQ, :] = rows(c_ref[...])
    o_ref[1 + N_LAT_SEQ:, :] = jnp.zeros((MOD_ROWS - 1 - N_LAT_SEQ, o_ref.shape[1]), F32)


def _ffn_body(x, mod, g_ref, w13_ref, w2_ref, ig):
    shift, scale, gate = mod
    xs = [x[i * TM:(i + 1) * TM, :] for i in range(x.shape[0] // TM)]
    hs = [(_rms(xi, g_ref[ig:ig + 1, :]) * (1.0 + scale) + shift).astype(BF16) for xi in xs]
    gus = [_dot(hi, w13_ref[...]) for hi in hs]
    acts = []
    for gu in gus:
        gt = gu[:, :D_FF]
        up = gu[:, D_FF:]
        acts.append((gt * jax.nn.sigmoid(gt) * up).astype(BF16))
    os_ = [_dot(ai, w2_ref[...]) for ai in acts]
    outs = [xi + HALF_STEP * gate * _rms(oi, g_ref[ig + 1:ig + 2, :]) for xi, oi in zip(xs, os_)]
    return jnp.concatenate(outs, axis=0)


def _cast_chunk(step, src_ref, dst_ref, axis):
    size = src_ref.shape[axis]

    @pl.when(step < N_WCHUNK)
    def _():
        start = pl.multiple_of(step * size, size)
        if axis == 0:
            dst_ref[pl.ds(start, size), :] = src_ref[...].astype(BF16)
        else:
            dst_ref[:, pl.ds(start, size)] = src_ref[...].astype(BF16)


def _chunk_spec(w, axis):
    block = list(w.shape)
    block[axis] = w.shape[axis] // N_WCHUNK
    clamp = lambda s: jnp.minimum(s, N_WCHUNK - 1)
    index = (lambda s: (clamp(s), 0)) if axis == 0 else (lambda s: (0, clamp(s)))
    return pl.BlockSpec(tuple(block), index)


def _tile_step(s):
    return jnp.maximum(s - N_WCHUNK, 0)


def _slab_spec(w, n_steps, step=lambda s: s):
    rows = w.shape[0] // n_steps
    return pl.BlockSpec((rows, w.shape[1]), lambda s: (step(s), 0))


def _ffn_first_kernel(xc_ref, xl_ref, g_ref, w13_ref, w2_ref, win_ref, cctx_ref, c_ref, wmod_ref, bmod_ref,
                      o_ref, win_bf_ref, mod_tail_ref, w13_bf, w2_bf, mod_ref):
    s = pl.program_id(0)
    _cast_chunk(s, w13_ref, w13_bf, 1)
    _cast_chunk(s, w2_ref, w2_bf, 0)

    @pl.when(s < N_MOD_HEAD_BLOCKS)
    def _():
        col = pl.multiple_of(s * MOD_SIDE, MOD_SIDE)
        _mod_kernel(cctx_ref, c_ref, wmod_ref, bmod_ref, mod_ref.at[:, pl.ds(col, MOD_SIDE)])

    @pl.when(s >= N_WCHUNK)
    def _():
        x = jnp.where(s - N_WCHUNK < N_FFN_CTX_TILES, xc_ref[...], xl_ref[...])
        mod = _mod_chunks(mod_ref, (s - N_WCHUNK) * (FFN_TM // TM), 0, 3)
        o_ref[...] = _ffn_body(x, mod, g_ref, w13_bf, w2_bf, 0)
        win_bf_ref[...] = win_ref[...].astype(BF16)

        @pl.when(s - N_WCHUNK < N_MOD_SIDE)
        def _():
            _mod_kernel(cctx_ref, c_ref, wmod_ref, bmod_ref, mod_tail_ref)


def _ffn_last_kernel(x_ref, mod_ref, g_ref, w13_ref, w2_ref, oc_ref, ol_ref):
    t = pl.program_id(0)
    mod = _mod_chunks(mod_ref, t * (FFN2_TM // TM), MOD_FFN2, 3)
    out = _ffn_body(x_ref[...], mod, g_ref, w13_ref, w2_ref, 4)

    @pl.when(t < N_FFN2_CTX_TILES)
    def _():
        oc_ref[...] = out

    @pl.when(t >= N_FFN2_CTX_TILES)
    def _():
        ol_ref[...] = out


def _ctx_tile(t):
    return jnp.minimum(t, N_FFN_CTX_TILES - 1), 0


def _lat_tile(t):
    return jnp.maximum(t - N_FFN_CTX_TILES, 0), 0


def _ffn_first(x_ctx, x_lat, norm_g, w13, w2, w_in, c_ctx, c, w_mod, b_mod):
    side = lambda s: jnp.minimum(_tile_step(s), N_MOD_SIDE - 1)

    def mod_block(s):
        return 0, jnp.where(s < N_WCHUNK, jnp.minimum(s, N_MOD_HEAD_BLOCKS - 1), N_MOD_HEAD_BLOCKS + side(s))

    return pl.pallas_call(
        _ffn_first_kernel,
        grid=(N_WCHUNK + N_FFN_TILES,),
        in_specs=[pl.BlockSpec((FFN_TM, D_MODEL), lambda s: _ctx_tile(_tile_step(s))),
                  pl.BlockSpec((FFN_TM, D_MODEL), lambda s: _lat_tile(_tile_step(s))),
                  _const_spec(norm_g.shape),
                  _chunk_spec(w13, 1),
                  _chunk_spec(w2, 0),
                  _slab_spec(w_in, N_FFN_TILES, step=_tile_step),
                  _const_spec(c_ctx.shape),
                  _const_spec(c.shape),
                  pl.BlockSpec((D_MODEL, MOD_SIDE), mod_block),
                  pl.BlockSpec((1, MOD_SIDE), mod_block)],
        out_specs=[pl.BlockSpec((FFN_TM, D_MODEL), lambda s: (_tile_step(s), 0)),
                   _slab_spec(w_in, N_FFN_TILES, step=_tile_step),
                   pl.BlockSpec((MOD_ROWS, MOD_SIDE), lambda s: (0, side(s)))],
        out_shape=[jax.ShapeDtypeStruct((N_TOK, D_MODEL), F32),
                   jax.ShapeDtypeStruct(w_in.shape, BF16),
                   jax.ShapeDtypeStruct((MOD_ROWS, MOD_TAIL * D_MODEL), F32)],
        scratch_shapes=[pltpu.VMEM((D_MODEL, 2 * D_FF), BF16), pltpu.VMEM((D_FF, D_MODEL), BF16),
                        pltpu.VMEM((MOD_ROWS, MOD_HEAD * D_MODEL), F32)],
        compiler_params=_params(),
        name="ffn1",
    )(x_ctx, x_lat, norm_g, w13, w2, w_in, c_ctx, c, w_mod, b_mod)


def _ffn_last(x, mod3, norm_g, w13_bf, w2_bf):
    n_ctx = N_CTX_SEQ * CTX_LEN
    return pl.pallas_call(
        _ffn_last_kernel,
        grid=(N_TOK // FFN2_TM,),
        in_specs=[pl.BlockSpec((FFN2_TM, D_MODEL), lambda t: (t, 0)),
                  _const_spec(mod3.shape),
                  _const_spec(norm_g.shape),
                  _const_spec(w13_bf.shape),
                  _const_spec(w2_bf.shape)],
        out_specs=[pl.BlockSpec((FFN2_TM, D_MODEL), lambda t: (jnp.minimum(t, N_FFN2_CTX_TILES - 1), 0)),
                   pl.BlockSpec((FFN2_TM, D_MODEL), lambda t: (jnp.maximum(t - N_FFN2_CTX_TILES, 0), 0))],
        out_shape=[jax.ShapeDtypeStruct((n_ctx, D_MODEL), F32),
                   jax.ShapeDtypeStruct((N_TOK - n_ctx, D_MODEL), F32)],
        compiler_params=_params(),
        name="ffn2",
    )(x, mod3, norm_g, w13_bf, w2_bf)


def _lora_prep_kernel(mu_ref, dw1_ref, ia1_ref, dw2_ref, ia2_ref, dw0_ref, ia0_ref,
                      w1_o, w2_o, b2_o):
    firsts = (dw1_ref[0], ia1_ref[0], dw1_ref[1], ia1_ref[1])
    seconds = (dw2_ref[0], ia2_ref[0], dw2_ref[1], ia2_ref[1])
    biases = (dw0_ref[0:1, :], ia0_ref[0:1, :], dw0_ref[1:2, :], ia0_ref[1:2, :])
    n = len(firsts) * LORA
    w2_o[...] = jnp.zeros(w2_o.shape, BF16)
    for j in range(len(firsts)):
        mu = mu_ref[:, j:j + 1]
        w1_o[:, j * LORA:(j + 1) * LORA] = ((1.0 - mu) * firsts[j]).astype(BF16)
        w1_o[:, n + j * LORA:n + (j + 1) * LORA] = (mu * firsts[j]).astype(BF16)
        w2_o[j * LORA:(j + 1) * LORA, j * D_RWKV:(j + 1) * D_RWKV] = seconds[j].astype(BF16)
        b2_o[:, j * D_RWKV:(j + 1) * D_RWKV] = biases[j]


def _lora_prep(mu_t, dw1, ia1, dw2, ia2, dw0, ia0):
    n = 4 * LORA
    return pl.pallas_call(
        _lora_prep_kernel,
        out_shape=[jax.ShapeDtypeStruct((D_MODEL, 2 * n), BF16),
                   jax.ShapeDtypeStruct((n, 4 * D_RWKV), BF16),
                   jax.ShapeDtypeStruct((1, 4 * D_RWKV), F32)],
        compiler_params=pltpu.CompilerParams(vmem_limit_bytes=VMEM_LIMIT),
        name="lora_prep",
    )(mu_t, dw1, ia1, dw2, ia2, dw0, ia0)


def _front_group(gi, h, halo_prev, halo_next, is_lat, win_ref, waug_ref, refs, outs):
    w2aug_ref, b2aug_ref, kkw_ref, ka_ref, rk_ref, cw_ref, cbias_ref, wbb_ref, ones_ref = refs
    ops_o, v_o, ends_o, bv_o, sg_o, sa_o, mb_o = outs
    rows = slice(gi * TM, (gi + 1) * TM)
    row = lax.broadcasted_iota(jnp.int32, (TM, 1), 0)
    rin = row & (CHUNK - 1)
    ones = ones_ref[...]

    def proj(lo, hi):
        return _bdot(h, win_ref[:, lo:hi])

    pab = _bdot(h, waug_ref[...])
    rk = proj(0, 2 * D_RWKV)
    r = rk[:, :D_RWKV]
    k = rk[:, D_RWKV:]
    pb = pab[:, 2 * LANES:]
    sh_f = jnp.where(row == 0, halo_prev, pltpu.roll(pb[:, :LANES], 1, 0))
    sh_b = jnp.where(row == TM - 1, halo_next, pltpu.roll(pb[:, LANES:], TM - 1, 0))
    t_in = pab[:, :2 * LANES] + jnp.concatenate([sh_f, sh_b], axis=1)
    lane = lax.broadcasted_iota(jnp.int32, (1, 2 * LANES), 1)
    t_in = jnp.where((lane & HEAD) == 0, jnp.tanh(t_in), t_in)
    za = _dot(t_in, w2aug_ref[...]) + b2aug_ref[...]

    kk = k * kkw_ref[...]
    kkn = kk * lax.rsqrt(_head_sum(kk * kk, ones) + 1e-12)
    ka = ka_ref[...]

    vg = proj(2 * D_RWKV, 4 * D_RWKV)
    v = vg[:, :D_RWKV]
    v_o[rows, :] = v.astype(BF16)
    sg_o[rows, :] = _sigmoid(vg[:, D_RWKV:]).astype(BF16)

    def scan_operands(d):
        lw = -EXP_M05 * _sigmoid(za[:, 2 * d * D_RWKV:(2 * d + 1) * D_RWKV])
        a = _sigmoid(za[:, (2 * d + 1) * D_RWKV:(2 * d + 2) * D_RWKV])
        k_d = k * (1.0 + (a - 1.0) * ka)
        b = kkn * a
        cs = lw
        for s in (1, 2, 4, 8, 16, 32):
            if d == 0:
                cs = cs + jnp.where(rin >= s, pltpu.roll(cs, s, 0), 0.0)
            else:
                cs = cs + jnp.where(rin < CHUNK - s, pltpu.roll(cs, TM - s, 0), 0.0)
        end_row = CHUNK - 1 if d == 0 else 0
        ends = [cs[c * CHUNK + end_row:c * CHUNK + end_row + 1, :] for c in range(N_CHUNK)]
        for c in range(N_CHUNK):
            ends_o[d][gi, c:c + 1, :] = ends[c]
        cs_end = jnp.concatenate([jnp.broadcast_to(e, (CHUNK, D_RWKV)) for e in ends], axis=0)
        dec_inv = jnp.exp(-cs)
        dec_rest = jnp.exp(cs_end - cs)
        o_a, o_r, o_b, o_k, o_bh, o_kh = ops_o[d]
        o_a[rows, :] = (-kkn * jnp.exp(cs - lw)).astype(BF16)
        o_r[rows, :] = (r * jnp.exp(cs)).astype(BF16)
        o_b[rows, :] = (b * dec_inv).astype(BF16)
        o_k[rows, :] = (k_d * dec_inv).astype(BF16)
        o_bh[rows, :] = (b * dec_rest).astype(BF16)
        o_kh[rows, :] = (k_d * dec_rest).astype(BF16)
        return k_d

    conv_in = proj(4 * D_RWKV, 4 * D_RWKV + 3 * D_CONV)
    k_0 = scan_operands(0)
    gate_a = proj(4 * D_RWKV + 3 * D_CONV, 4 * D_RWKV + 3 * D_CONV + D_MODEL)
    sa_o[rows, :] = _sigmoid(gate_a).astype(BF16)
    k_1 = scan_operands(1)
    gate_b = proj(4 * D_RWKV + 3 * D_CONV + D_MODEL, D_IN)

    cgate = conv_in[:, :D_CONV]
    u = conv_in[:, D_CONV:2 * D_CONV] * conv_in[:, 2 * D_CONV:]
    col = row & (GRID_W - 1)
    zl = jnp.logical_or(row == 0, jnp.logical_and(is_lat, col == 0))
    zr = jnp.logical_or(row == TM - 1, jnp.logical_and(is_lat, col == GRID_W - 1))
    left = jnp.where(zl, 0.0, pltpu.roll(u, 1, 0))
    right = jnp.where(zr, 0.0, pltpu.roll(u, TM - 1, 0))
    conv = left * cw_ref[0:1, :] + u * cw_ref[1:2, :] + right * cw_ref[2:3, :] + cbias_ref[...]
    y_b = _dot(cgate * conv, wbb_ref[...])
    mb_o[rows, :] = (_sigmoid(gate_b) * y_b).astype(BF16)

    bv_o[rows, :] = (_head_sum(r * (k_0 + k_1) * rk_ref[...], ones) * v).astype(BF16)


def _front_kernel(x_ref, xp_ref, xn_ref, mod_ref, g_ref, win_ref, waug_ref, *rest):
    refs = list(rest[:N_FRONT_CONSTS])
    outs = rest[N_FRONT_CONSTS:-1]
    wbb_bf = rest[-1]
    ops_o = (outs[0:N_DIR_OPS], outs[N_DIR_OPS:2 * N_DIR_OPS])
    v_o, ends0_o, ends1_o, bv_o, sg_o, sa_o, mb_o = outs[2 * N_DIR_OPS:]
    outs = (ops_o, v_o, (ends0_o, ends1_o), bv_o, sg_o, sa_o, mb_o)

    t = pl.program_id(0)

    @pl.when(t == 0)
    def _():
        wbb_bf[...] = refs[FRONT_WBB][...].astype(BF16)

    refs[FRONT_WBB] = wbb_bf

    is_lat = t >= N_FRONT_CTX_TILES
    i_in = (t - N_FRONT_CTX_TILES) % FRONT_LAT_TILES
    lat_first = jnp.logical_and(is_lat, i_in == 0)
    lat_last = jnp.logical_and(is_lat, i_in == FRONT_LAT_TILES - 1)
    shift, scale = _mod_chunks(mod_ref, t * FRONT_GROUPS, MOD_FRONT, 2)
    g2 = g_ref[2:3, :]

    def pre(x):
        return (_rms(x, g2) * (1.0 + scale) + shift).astype(BF16)

    x = x_ref[...]
    edge = jnp.concatenate([xp_ref[...], x[TM - 8:TM + 8, :], xn_ref[...]], axis=0)
    edge_b = _bdot(pre(edge), waug_ref[:, 2 * LANES:])
    zero = jnp.zeros((1, LANES), F32)
    halo_prev = (jnp.where(jnp.logical_and(is_lat, jnp.logical_not(lat_first)), edge_b[7:8, :LANES], zero),
                 jnp.where(is_lat, edge_b[15:16, :LANES], zero))
    halo_next = (jnp.where(is_lat, edge_b[16:17, LANES:], zero),
                 jnp.where(jnp.logical_and(is_lat, jnp.logical_not(lat_last)), edge_b[24:25, LANES:], zero))

    for gi in range(FRONT_GROUPS):
        h = pre(x[gi * TM:(gi + 1) * TM, :])
        _front_group(gi, h, halo_prev[gi], halo_next[gi], is_lat, win_ref, waug_ref, refs, outs)


def _front(x, mod3, norm_g, w_in_bf, w_aug, consts):
    tok = lambda t: (t, 0)
    rows8 = FRONT_TM // 8
    last8 = N_TOK // 8 - 1
    assert len(consts) == N_FRONT_CONSTS
    out_rb = jax.ShapeDtypeStruct((N_TOK, D_RWKV), BF16)
    out_db = jax.ShapeDtypeStruct((N_TOK, D_MODEL), BF16)
    out_e = jax.ShapeDtypeStruct((N_TILES, N_CHUNK, D_RWKV), F32)
    spec_r = pl.BlockSpec((FRONT_TM, D_RWKV), tok)
    spec_d = pl.BlockSpec((FRONT_TM, D_MODEL), tok)
    spec_e = pl.BlockSpec((FRONT_GROUPS, N_CHUNK, D_RWKV), lambda t: (t, 0, 0))
    n_b = 2 * N_DIR_OPS + 1
    return pl.pallas_call(
        _front_kernel,
        grid=(N_TOK // FRONT_TM,),
        in_specs=[pl.BlockSpec((FRONT_TM, D_MODEL), tok),
                  pl.BlockSpec((8, D_MODEL), lambda t: (jnp.maximum(t * rows8 - 1, 0), 0)),
                  pl.BlockSpec((8, D_MODEL), lambda t: (jnp.minimum((t + 1) * rows8, last8), 0)),
                  _const_spec(mod3.shape)]
                 + [_const_spec(c.shape) for c in [norm_g, w_in_bf, w_aug] + list(consts)],
        out_specs=[spec_r] * n_b + [spec_e] * 2 + [spec_r] * 2 + [spec_d] * 2,
        out_shape=[out_rb] * n_b + [out_e] * 2 + [out_rb] * 2 + [out_db] * 2,
        scratch_shapes=[pltpu.VMEM((D_CONV, D_MODEL), BF16)],
        compiler_params=_params(),
        name="mixer_front",
    )(x, x, x, mod3, norm_g, w_in_bf, w_aug, *consts)


def _stack(x):
    lane_lo = lax.broadcasted_iota(jnp.int32, x.shape, 1) < HEAD
    z = jnp.zeros_like(x)
    return jnp.concatenate([jnp.where(lane_lo, x, z), jnp.where(lane_lo, z, x)], axis=0)


def _scan_kernel(*refs):
    ops = (refs[0:N_DIR_OPS + 1], refs[N_DIR_OPS + 1:2 * N_DIR_OPS + 2])
    (ends0_ref, ends1_ref, s0_ref, w13_ref, w2_ref,
     yf_ref, yb_ref, sout_ref, w13_bf_ref, w2_bf_ref, st_ref) = refs[2 * N_DIR_OPS + 2:]
    ends_ref = (ends0_ref, ends1_ref)
    w13_bf_ref[...] = w13_ref[...].astype(BF16)
    w2_bf_ref[...] = w2_ref[...].astype(BF16)
    step_id = pl.program_id(0)
    is_lat = step_id >= N_SCAN_CTX_STEPS
    lat_first = (step_id - N_SCAN_CTX_STEPS) % SCAN_LAT_STEPS == 0

    @pl.when(jnp.logical_and(is_lat, lat_first))
    def _():
        zero = jnp.zeros((HEAD, HEAD), F32)
        for d in (0, 1):
            for p in range(N_PAIR):
                top = jnp.concatenate([s0_ref[0, d, 2 * p], zero], axis=1)
                bottom = jnp.concatenate([zero, s0_ref[0, d, 2 * p + 1]], axis=1)
                st_ref[d, p] = jnp.concatenate([top, bottom], axis=0).T

    ti = lax.broadcasted_iota(jnp.int32, (CHUNK, LANES), 0)
    sj = lax.broadcasted_iota(jnp.int32, (CHUNK, LANES), 1) & (CHUNK - 1)
    m_strict = (sj < ti, sj > ti)
    m_incl = (sj <= ti, sj >= ti)
    eye_cat = jnp.where(sj == ti, 1.0, 0.0)
    bi = lax.broadcasted_iota(jnp.int32, (LANES, LANES), 0)
    bj = lax.broadcasted_iota(jnp.int32, (LANES, LANES), 1)
    blk = (bi >> HEAD_SHIFT) == (bj >> HEAD_SHIFT)
    eye_bd = bi == bj
    zero_bd = jnp.zeros((LANES, LANES), BF16)

    def op(i, u):
        d, p, c = u
        return ops[d][i][c * CHUNK:(c + 1) * CHUNK, p * LANES:(p + 1) * LANES]

    g_mat, w_mat, pg, q_mat = {}, {}, {}, {}

    def local_stages(units):
        low, nak, mrbk = {}, {}, {}
        for u in units:
            lhs = jnp.concatenate([op(OP_A, u), op(OP_R, u)], axis=0)
            rhs = jnp.concatenate([_stack(op(OP_B, u)), _stack(op(OP_K, u))], axis=0)
            gram = _bdot(lhs, rhs, NT)
            d = u[0]
            low[u] = jnp.where(m_strict[d], gram[:CHUNK, :LANES], 0.0)
            nak[u] = jnp.where(m_strict[d], gram[:CHUNK, LANES:], 0.0).astype(BF16)
            mrbk[u] = jnp.concatenate([jnp.where(m_incl[d], gram[CHUNK:, :LANES], 0.0),
                                       jnp.where(m_incl[d], gram[CHUNK:, LANES:], 0.0)], axis=1).astype(BF16)

        inv = {u: eye_cat + low[u] for u in units}
        pwb = {u: low[u].astype(BF16) for u in units}
        for u in units:
            pwb[u] = _bdot(pwb[u], _stack(pwb[u])).astype(BF16)
        for _ in range(4):
            for u in units:
                both = _bdot(pwb[u], jnp.concatenate([_stack(pwb[u]), _stack(inv[u].astype(BF16))], axis=1))
                pwb[u] = both[:, :LANES].astype(BF16)
                inv[u] = inv[u] + both[:, LANES:]
        for u in units:
            inv[u] = inv[u] + _bdot(pwb[u], _stack(inv[u].astype(BF16)))

        sv = {u: _stack(op(OP_V, u)) for u in units}
        nv = {}
        for u in units:
            nv[u] = _bdot(nak[u], sv[u]).astype(BF16)
        x1, x2 = {}, {}
        for u in units:
            x12 = _bdot(inv[u].astype(BF16), jnp.concatenate([_stack(op(OP_A, u)), _stack(nv[u])], axis=1))
            x1[u] = x12[:, :LANES].astype(BF16)
            x2[u] = x12[:, LANES:].astype(BF16)
        for u in units:
            rhs = jnp.concatenate([jnp.concatenate([_stack(x1[u]), _stack(x2[u])], axis=1),
                                   jnp.concatenate([zero_bd, sv[u]], axis=1)], axis=0)
            gw = _bdot(mrbk[u], rhs)
            g_mat[u] = (op(OP_R, u).astype(F32) + gw[:, :LANES]).astype(BF16)
            w_mat[u] = gw[:, LANES:]
        for u in units:
            d, p, c = u
            ct = c % N_CHUNK
            gam = jnp.exp(ends_ref[d][c // N_CHUNK, ct:ct + 1, p * LANES:(p + 1) * LANES])
            rhs = jnp.concatenate([jnp.concatenate([x1[u], x2[u]], axis=1),
                                   jnp.concatenate([jnp.zeros_like(x1[u]), op(OP_V, u)], axis=1)], axis=0)
            pq = _bdot(jnp.concatenate([op(OP_BH, u), op(OP_KH, u)], axis=0), rhs, TN)
            p_mat = (jnp.where(blk, pq[:, :LANES], 0.0) + jnp.where(eye_bd, gam, 0.0)).astype(BF16)
            pg[u] = jnp.concatenate([p_mat, g_mat[u]], axis=0)
            q_mat[u] = jnp.where(blk, pq[:, LANES:], 0.0)

    n_c = SCAN_TILES * N_CHUNK
    state = {(d, p): jnp.where(is_lat, st_ref[d, p], 0.0) for d in (0, 1) for p in range(N_PAIR)}
    finals = {}
    y_refs = (yf_ref, yb_ref)

    def chunk_of(d, step):
        return step if d == 0 else n_c - 1 - step

    def chain(step, d, p):
        c = chunk_of(d, step)
        u = (d, p, c)
        both = _bdot(pg[u], state[d, p].astype(BF16))
        y_refs[d][c * CHUNK:(c + 1) * CHUNK, p * LANES:(p + 1) * LANES] = both[LANES:, :] + w_mat[u]
        state[d, p] = both[:LANES, :] + q_mat[u]
        if (step + 1) % N_CHUNK == 0:
            finals[d, c // N_CHUNK, p] = state[d, p]
            if step + 1 < n_c:
                state[d, p] = jnp.where(is_lat, state[d, p], 0.0)

    local_stages([(d, p, c) for d in (0, 1) for p in range(N_PAIR) for c in range(n_c)])
    for step in range(n_c):
        for d in (0, 1):
            for p in range(N_PAIR):
                chain(step, d, p)
    for (d, p), s in state.items():
        st_ref[d, p] = s

    @pl.when(jnp.logical_not(is_lat))
    def _():
        for (d, tile, p), h in finals.items():
            s = h.T
            sout_ref[tile, 0, d, 2 * p] = s[:HEAD, :HEAD]
            sout_ref[tile, 0, d, 2 * p + 1] = s[HEAD:, HEAD:]


def _scan(ops_f, ops_b, v, ends_f, ends_b, s0, w13, w2):
    def mirror(s):
        u = s - N_SCAN_CTX_STEPS
        return jnp.where(s < N_SCAN_CTX_STEPS, s,
                         N_SCAN_CTX_STEPS + (u // SCAN_LAT_STEPS) * SCAN_LAT_STEPS
                         + (SCAN_LAT_STEPS - 1 - u % SCAN_LAT_STEPS))

    def lat_seq(s):
        return jnp.maximum(s - N_SCAN_CTX_STEPS, 0) // SCAN_LAT_STEPS

    fwd = pl.BlockSpec((SCAN_TM, D_RWKV), lambda s: (s, 0))
    bwd = pl.BlockSpec((SCAN_TM, D_RWKV), lambda s: (mirror(s), 0))
    e_block = (SCAN_TILES, N_CHUNK, D_RWKV)
    st_block = (None, 1, 2, 2 * N_PAIR, HEAD, HEAD)
    out_y = jax.ShapeDtypeStruct((N_TOK, D_RWKV), F32)
    n_steps = N_TOK // SCAN_TM
    return pl.pallas_call(
        _scan_kernel,
        grid=(n_steps,),
        in_specs=[fwd] * (N_DIR_OPS + 1) + [bwd] * (N_DIR_OPS + 1)
                 + [pl.BlockSpec(e_block, lambda s: (s, 0, 0)),
                    pl.BlockSpec(e_block, lambda s: (mirror(s), 0, 0)),
                    pl.BlockSpec(st_block, lambda s: (lat_seq(s), 0, 0, 0, 0, 0)),
                    _slab_spec(w13, n_steps),
                    _slab_spec(w2, n_steps)],
        out_specs=[fwd, bwd,
                   pl.BlockSpec((SCAN_TILES, 1, 2, 2 * N_PAIR, HEAD, HEAD),
                                lambda s: (jnp.minimum(s, N_SCAN_CTX_STEPS - 1), 0, 0, 0, 0, 0)),
                   _slab_spec(w13, n_steps),
                   _slab_spec(w2, n_steps)],
        out_shape=[out_y, out_y,
                   jax.ShapeDtypeStruct((N_CTX_SEQ, 1, 2, 2 * N_PAIR, HEAD, HEAD), F32),
                   jax.ShapeDtypeStruct(w13.shape, BF16),
                   jax.ShapeDtypeStruct(w2.shape, BF16)],
        scratch_shapes=[pltpu.VMEM((2, N_PAIR, LANES, LANES), F32)],
        compiler_params=_params(),
        name="rwkv7_scan",
    )(*ops_f, v, *ops_b, v, ends_f, ends_b, s0, w13, w2)


def _back_kernel(x_ref, yf_ref, yb_ref, bv_ref, sg_ref, sa_ref, mb_ref, mod_ref, g_ref,
                 gng_ref, gnb_ref, wba_f32_ref, wout_f32_ref, ones_ref, o_ref, wba_ref, wout_ref):
    @pl.when(pl.program_id(0) == 0)
    def _():
        wba_ref[...] = wba_f32_ref[...].astype(BF16)
        wout_ref[...] = wout_f32_ref[...].astype(BF16)

    gate, = _mod_chunks(mod_ref, pl.program_id(0) * (BACK_TM // TM), MOD_BACK, 1)
    ones = ones_ref[...]
    groups = [slice(i * TM, (i + 1) * TM) for i in range(BACK_TM // TM)]
    ys = [yf_ref[r, :] + yb_ref[r, :] for r in groups]
    ycs = [y - _head_sum(y, ones) * (1.0 / HEAD) for y in ys]
    vs = [_head_sum(yc * yc, ones) * (1.0 / HEAD) for yc in ycs]
    yns = [yc * lax.rsqrt(v + EPS_GN) * gng_ref[...] + gnb_ref[...] for yc, v in zip(ycs, vs)]
    yas = [_dot((yn + bv_ref[r, :]) * sg_ref[r, :], wba_ref[...]) for r, yn in zip(groups, yns)]
    outs = [_dot(sa_ref[r, :] * ya + mb_ref[r, :], wout_ref[...]) for r, ya in zip(groups, yas)]
    for r, out in zip(groups, outs):
        o_ref[r, :] = x_ref[r, :] + gate * _rms(out, g_ref[3:4, :])


def _back(x, yf, yb, bv, sg, sa, mb, mod3, norm_g, gng, gnb, wba, wout, ones):
    tok = lambda t: (t, 0)
    spec_r = pl.BlockSpec((BACK_TM, D_RWKV), tok)
    spec_d = pl.BlockSpec((BACK_TM, D_MODEL), tok)
    consts = [norm_g, gng, gnb, wba, wout, ones]
    return pl.pallas_call(
        _back_kernel,
        grid=(N_TOK // BACK_TM,),
        in_specs=[spec_d, spec_r, spec_r, spec_r, spec_r, spec_d, spec_d,
                  _const_spec(mod3.shape)]
                 + [_const_spec(c.shape) for c in consts],
        out_specs=spec_d,
        out_shape=jax.ShapeDtypeStruct((N_TOK, D_MODEL), F32),
        scratch_shapes=[pltpu.VMEM(wba.shape, BF16), pltpu.VMEM(wout.shape, BF16)],
        compiler_params=_params(),
        name="mixer_back",
    )(x, yf, yb, bv, sg, sa, mb, mod3, *consts)


def _back_ffn_kernel(x_ref, yf_ref, yb_ref, bv_ref, sg_ref, sa_ref, mb_ref, mod_ref, g_ref,
                     gng_ref, gnb_ref, wba_ref, wout_ref, ones_ref, w13_ref, w2_ref, oc_ref, ol_ref):
    t = pl.program_id(0)
    tile = t * (FFN_TM // TM)
    gate, = _mod_chunks(mod_ref, tile, MOD_BACK, 1)
    ones = ones_ref[...]
    groups = [slice(i * TM, (i + 1) * TM) for i in range(FFN_TM // TM)]
    ys = [yf_ref[r, :] + yb_ref[r, :] for r in groups]
    ycs = [y - _head_sum(y, ones) * (1.0 / HEAD) for y in ys]
    vs = [_head_sum(yc * yc, ones) * (1.0 / HEAD) for yc in ycs]
    yns = [yc * lax.rsqrt(v + EPS_GN) * gng_ref[...] + gnb_ref[...] for yc, v in zip(ycs, vs)]
    yas = [_dot((yn + bv_ref[r, :]) * sg_ref[r, :], wba_ref[...]) for r, yn in zip(groups, yns)]
    outs = [_dot(sa_ref[r, :] * ya + mb_ref[r, :], wout_ref[...]) for r, ya in zip(groups, yas)]
    x2 = jnp.concatenate([x_ref[r, :] + gate * _rms(out, g_ref[3:4, :]) for r, out in zip(groups, outs)], axis=0)

    mod = _mod_chunks(mod_ref, tile, MOD_FFN2, 3)
    out = _ffn_body(x2, mod, g_ref, w13_ref, w2_ref, 4)

    @pl.when(t < N_FFN_CTX_TILES)
    def _():
        oc_ref[...] = out

    @pl.when(t >= N_FFN_CTX_TILES)
    def _():
        ol_ref[...] = out


def _back_ffn(x, yf, yb, bv, sg, sa, mb, mod3, norm_g, gng, gnb, wba_bf, wout_bf, ones, w13_bf, w2_bf):
    tok = lambda t: (t, 0)
    spec_r = pl.BlockSpec((FFN_TM, D_RWKV), tok)
    spec_d = pl.BlockSpec((FFN_TM, D_MODEL), tok)
    consts = [mod3, norm_g, gng, gnb, wba_bf, wout_bf, ones, w13_bf, w2_bf]
    n_ctx = N_CTX_SEQ * CTX_LEN
    return pl.pallas_call(
        _back_ffn_kernel,
        grid=(N_FFN_TILES,),
        in_specs=[spec_d, spec_r, spec_r, spec_r, spec_r, spec_d, spec_d]
                 + [_const_spec(c.shape) for c in consts],
        out_specs=[pl.BlockSpec((FFN_TM, D_MODEL), _ctx_tile),
                   pl.BlockSpec((FFN_TM, D_MODEL), _lat_tile)],
        out_shape=[jax.ShapeDtypeStruct((n_ctx, D_MODEL), F32),
                   jax.ShapeDtypeStruct((N_TOK - n_ctx, D_MODEL), F32)],
        compiler_params=_params(),
        name="back_ffn2",
    )(x, yf, yb, bv, sg, sa, mb, *consts)


def kernel(x_prompt, x_sample, c, state_rwkv, c_ctx, w_mod, b_mod, norm_g, ffn1_w13, ffn1_w2,
           ffn2_w13, ffn2_w2, w_in, mu_shift, decay_w0, decay_w1, decay_w2, iclr_a0, iclr_a1,
           iclr_a2, k_k, k_a, r_k, gn_gain, gn_bias, conv_w, conv_b, w_branch_a, w_branch_b, w_out):
    assert x_prompt.shape == (N_CTX_SEQ, CTX_LEN, D_MODEL) and x_sample.shape == (N_LAT_SEQ, LAT_LEN, D_MODEL)
    assert w_mod.shape[0] == 1, "single trunk layer"

    g = norm_g[0]
    x, w_in_bf, mod3 = _ffn_first(x_prompt.reshape(-1, D_MODEL), x_sample.reshape(-1, D_MODEL), g,
                                  ffn1_w13[0], ffn1_w2[0], w_in[0], c_ctx.reshape(1, D_MODEL), c,
                                  w_mod[0], b_mod)

    row = lambda p: p.reshape(1, -1)
    w_aug, w2_aug, b2_aug = _lora_prep(mu_shift[0].reshape(4, D_MODEL).T, decay_w1[0], iclr_a1[0],
                                       decay_w2[0], iclr_a2[0], decay_w0[0], iclr_a0[0])
    ones = _head_ones()
    front = _front(x, mod3, g, w_in_bf, w_aug,
                   [w2_aug, b2_aug, row(k_k[0]), row(k_a[0]), row(r_k[0]), conv_w[0], row(conv_b[0]),
                    w_branch_b[0], ones])
    ops_f, ops_b = front[0:N_DIR_OPS], front[N_DIR_OPS:2 * N_DIR_OPS]
    v, ends_f, ends_b, bv, sg, sa, mb = front[2 * N_DIR_OPS:]

    yf, yb, s_fin, w13_bf, w2_bf = _scan(ops_f, ops_b, v, ends_f, ends_b, state_rwkv,
                                         ffn2_w13[0], ffn2_w2[0])

    y_ctx, y_lat = _back_ffn(x, yf, yb, bv, sg, sa, mb, mod3, g, row(gn_gain[0]), row(gn_bias[0]),
                             w_branch_a[0].astype(BF16), w_out[0].astype(BF16), ones, w13_bf, w2_bf)

    y_prompt = y_ctx.reshape(N_CTX_SEQ, CTX_LEN, D_MODEL)
    y_sample = y_lat.reshape(N_LAT_SEQ, LAT_LEN, D_MODEL)
    return y_prompt, y_sample, s_fin
```

```python
import jax
import jax.numpy as jnp
from jax import lax
from jax.experimental import pallas as pl
from jax.experimental.pallas import tpu as pltpu

F32 = jnp.float32
BF16 = jnp.bfloat16

D_MODEL = 1024
D_FF = 2816
D_RWKV = 512
D_CONV = 512
HEAD = 64
HEAD_SHIFT = 6
D_IN = 4 * D_RWKV + 3 * D_CONV + 2 * D_MODEL
N_MOD = 9
EPS_RMS = 1e-6
EPS_GN = 64e-5
HALF_STEP = 0.5
EXP_M05 = 0.6065306597126334

N_CTX_SEQ = 16
CTX_LEN = 256
N_LAT_SEQ = 2
LAT_LEN = 2048
GRID_W = 64
N_TOK = N_CTX_SEQ * CTX_LEN + N_LAT_SEQ * LAT_LEN

TM = 256
N_CTX_TILES = N_CTX_SEQ * CTX_LEN // TM
LAT_TILES = LAT_LEN // TM
N_TILES = N_TOK // TM
FRONT_TM = 512
FRONT_GROUPS = FRONT_TM // TM
N_FRONT_CTX_TILES = N_CTX_SEQ * CTX_LEN // FRONT_TM
FRONT_LAT_TILES = LAT_LEN // FRONT_TM
N_FRONT_CONSTS = 9
FRONT_WBB = 7
LORA = 64
SCAN_TM = 512
SCAN_TILES = SCAN_TM // TM
N_SCAN_CTX_STEPS = N_CTX_SEQ * CTX_LEN // SCAN_TM
SCAN_LAT_STEPS = LAT_LEN // SCAN_TM
FFN_TM = 512
N_FFN_TILES = N_TOK // FFN_TM
N_FFN_CTX_TILES = N_CTX_SEQ * CTX_LEN // FFN_TM
N_WCHUNK = 11
CHUNK = 64
N_CHUNK = TM // CHUNK
LANES = 128
N_PAIR = D_RWKV // LANES
MOD_ROWS = 8
MOD_HEAD = 3
MOD_TAIL = N_MOD - MOD_HEAD
MOD_SIDE = 512
N_MOD_SIDE = MOD_TAIL * D_MODEL // MOD_SIDE
N_MOD_HEAD_BLOCKS = MOD_HEAD * D_MODEL // MOD_SIDE
MOD_FRONT, MOD_BACK, MOD_FFN2 = 0, 2, 3
VMEM_LIMIT = 56 * 1024 * 1024

OP_A, OP_R, OP_B, OP_K, OP_BH, OP_KH, OP_V = range(7)
N_DIR_OPS = 6
N_SCAN_SIDE = 4

NN = (((1,), (0,)), ((), ()))
NT = (((1,), (1,)), ((), ()))
TN = (((0,), (0,)), ((), ()))


def _dot(a, b, dims=NN):
    return lax.dot_general(a.astype(BF16), b.astype(BF16), dims, preferred_element_type=F32)


def _bdot(a, b, dims=NN):
    return lax.dot_general(a, b, dims, preferred_element_type=F32)


def _rms(x, g):
    ms = jnp.mean(x * x, axis=-1, keepdims=True)
    return x * lax.rsqrt(ms + EPS_RMS) * g


def _head_ones():
    i = lax.broadcasted_iota(jnp.int32, (D_RWKV, D_RWKV), 0) >> HEAD_SHIFT
    j = lax.broadcasted_iota(jnp.int32, (D_RWKV, D_RWKV), 1) >> HEAD_SHIFT
    return jnp.where(i == j, 1.0, 0.0).astype(BF16)


def _head_sum(x, ones):
    return jnp.dot(x.astype(BF16), ones, preferred_element_type=F32)


def _sigmoid(x):
    return 0.5 * jnp.tanh(0.5 * x) + 0.5


def _mod_row(t):
    return jnp.where(t < N_CTX_TILES, 0, 1 + (t - N_CTX_TILES) // LAT_TILES)


def _mod_chunks(mod_ref, tile, first, count):
    row = pl.ds(_mod_row(tile), 1)
    return [mod_ref[row, (first + i) * D_MODEL:(first + i + 1) * D_MODEL] for i in range(count)]


def _const_spec(shape):
    nd = len(shape)
    return pl.BlockSpec(shape, lambda *_: (0,) * nd, pipeline_mode=pl.Buffered(1))


def _params(n_axes=1):
    return pltpu.CompilerParams(dimension_semantics=("arbitrary",) * n_axes,
                                vmem_limit_bytes=VMEM_LIMIT)


def _mod_kernel(cctx_ref, c_ref, w_ref, b_ref, o_ref):
    w = w_ref[...].astype(BF16)

    def rows(c):
        return jnp.dot((c * jax.nn.sigmoid(c)).astype(BF16), w, preferred_element_type=F32) + b_ref[...]

    o_ref[0:1, :] = rows(cctx_ref[...])
    o_ref[1:1 + N_LAT_SEQ, :] = rows(c_ref[...])
    o_ref[1 + N_LAT_SEQ:, :] = jnp.zeros((MOD_ROWS - 1 - N_LAT_SEQ, o_ref.shape[1]), F32)


def _ffn_body(x, mod, g_ref, w13_ref, w2_ref, ig):
    shift, scale, gate = mod
    xs = [x[i * TM:(i + 1) * TM, :] for i in range(x.shape[0] // TM)]
    hs = [(_rms(xi, g_ref[ig:ig + 1, :]) * (1.0 + scale) + shift).astype(BF16) for xi in xs]
    gus = [_dot(hi, w13_ref[...]) for hi in hs]
    acts = []
    for gu in gus:
        gt = gu[:, :D_FF]
        up = gu[:, D_FF:]
        acts.append((gt * jax.nn.sigmoid(gt) * up).astype(BF16))
    os_ = [_dot(ai, w2_ref[...]) for ai in acts]
    outs = [xi + HALF_STEP * gate * _rms(oi, g_ref[ig + 1:ig + 2, :]) for xi, oi in zip(xs, os_)]
    return jnp.concatenate(outs, axis=0)


def _cast_chunk(step, src_ref, dst_ref, axis):
    size = src_ref.shape[axis]

    @pl.when(step < N_WCHUNK)
    def _():
        start = pl.multiple_of(step * size, size)
        if axis == 0:
            dst_ref[pl.ds(start, size), :] = src_ref[...].astype(BF16)
        else:
            dst_ref[:, pl.ds(start, size)] = src_ref[...].astype(BF16)


def _chunk_spec(w, axis):
    block = list(w.shape)
    block[axis] = w.shape[axis] // N_WCHUNK
    clamp = lambda s: jnp.minimum(s, N_WCHUNK - 1)
    index = (lambda s: (clamp(s), 0)) if axis == 0 else (lambda s: (0, clamp(s)))
    return pl.BlockSpec(tuple(block), index)


def _tile_step(s):
    return jnp.maximum(s - N_WCHUNK, 0)


def _slab_spec(w, n_steps, step=lambda s: s):
    rows = w.shape[0] // n_steps
    return pl.BlockSpec((rows, w.shape[1]), lambda s: (step(s), 0))


def _ffn_first_kernel(xc_ref, xl_ref, g_ref, w13_ref, w2_ref, win_ref, cctx_ref, c_ref, wmod_ref, bmod_ref,
                      o_ref, win_bf_ref, mod_tail_ref, w13_bf, w2_bf, mod_ref):
    s = pl.program_id(0)
    _cast_chunk(s, w13_ref, w13_bf, 1)
    _cast_chunk(s, w2_ref, w2_bf, 0)

    @pl.when(s < N_MOD_HEAD_BLOCKS)
    def _():
        col = pl.multiple_of(s * MOD_SIDE, MOD_SIDE)
        _mod_kernel(cctx_ref, c_ref, wmod_ref, bmod_ref, mod_ref.at[:, pl.ds(col, MOD_SIDE)])

    @pl.when(s >= N_WCHUNK)
    def _():
        x = jnp.where(s - N_WCHUNK < N_FFN_CTX_TILES, xc_ref[...], xl_ref[...])
        mod = _mod_chunks(mod_ref, (s - N_WCHUNK) * (FFN_TM // TM), 0, 3)
        o_ref[...] = _ffn_body(x, mod, g_ref, w13_bf, w2_bf, 0)
        win_bf_ref[...] = win_ref[...].astype(BF16)

        @pl.when(s - N_WCHUNK < N_MOD_SIDE)
        def _():
            _mod_kernel(cctx_ref, c_ref, wmod_ref, bmod_ref, mod_tail_ref)


def _ctx_tile(t):
    return jnp.minimum(t, N_FFN_CTX_TILES - 1), 0


def _lat_tile(t):
    return jnp.maximum(t - N_FFN_CTX_TILES, 0), 0


def _ffn_first(x_ctx, x_lat, norm_g, w13, w2, w_in, c_ctx, c, w_mod, b_mod):
    side = lambda s: jnp.minimum(_tile_step(s), N_MOD_SIDE - 1)

    def mod_block(s):
        return 0, jnp.where(s < N_WCHUNK, jnp.minimum(s, N_MOD_HEAD_BLOCKS - 1), N_MOD_HEAD_BLOCKS + side(s))

    return pl.pallas_call(
        _ffn_first_kernel,
        grid=(N_WCHUNK + N_FFN_TILES,),
        in_specs=[pl.BlockSpec((FFN_TM, D_MODEL), lambda s: _ctx_tile(_tile_step(s))),
                  pl.BlockSpec((FFN_TM, D_MODEL), lambda s: _lat_tile(_tile_step(s))),
                  _const_spec(norm_g.shape),
                  _chunk_spec(w13, 1),
                  _chunk_spec(w2, 0),
                  _slab_spec(w_in, N_FFN_TILES, step=_tile_step),
                  _const_spec(c_ctx.shape),
                  _const_spec(c.shape),
                  pl.BlockSpec((D_MODEL, MOD_SIDE), mod_block),
                  pl.BlockSpec((1, MOD_SIDE), mod_block)],
        out_specs=[pl.BlockSpec((FFN_TM, D_MODEL), lambda s: (_tile_step(s), 0)),
                   _slab_spec(w_in, N_FFN_TILES, step=_tile_step),
                   pl.BlockSpec((MOD_ROWS, MOD_SIDE), lambda s: (0, side(s)))],
        out_shape=[jax.ShapeDtypeStruct((N_TOK, D_MODEL), F32),
                   jax.ShapeDtypeStruct(w_in.shape, BF16),
                   jax.ShapeDtypeStruct((MOD_ROWS, MOD_TAIL * D_MODEL), F32)],
        scratch_shapes=[pltpu.VMEM((D_MODEL, 2 * D_FF), BF16), pltpu.VMEM((D_FF, D_MODEL), BF16),
                        pltpu.VMEM((MOD_ROWS, MOD_HEAD * D_MODEL), F32)],
        compiler_params=_params(),
        name="ffn1",
    )(x_ctx, x_lat, norm_g, w13, w2, w_in, c_ctx, c, w_mod, b_mod)


def _lora_prep_kernel(mu_ref, dw1_ref, ia1_ref, dw2_ref, ia2_ref, dw0_ref, ia0_ref,
                      w1_o, w2_o, b2_o):
    firsts = (dw1_ref[0], ia1_ref[0], dw1_ref[1], ia1_ref[1])
    seconds = (dw2_ref[0], ia2_ref[0], dw2_ref[1], ia2_ref[1])
    biases = (dw0_ref[0:1, :], ia0_ref[0:1, :], dw0_ref[1:2, :], ia0_ref[1:2, :])
    n = len(firsts) * LORA
    w2_o[...] = jnp.zeros(w2_o.shape, BF16)
    for j in range(len(firsts)):
        mu = mu_ref[:, j:j + 1]
        w1_o[:, j * LORA:(j + 1) * LORA] = ((1.0 - mu) * firsts[j]).astype(BF16)
        w1_o[:, n + j * LORA:n + (j + 1) * LORA] = (mu * firsts[j]).astype(BF16)
        w2_o[j * LORA:(j + 1) * LORA, j * D_RWKV:(j + 1) * D_RWKV] = seconds[j].astype(BF16)
        b2_o[:, j * D_RWKV:(j + 1) * D_RWKV] = biases[j]


def _lora_prep(mu_t, dw1, ia1, dw2, ia2, dw0, ia0):
    n = 4 * LORA
    return pl.pallas_call(
        _lora_prep_kernel,
        out_shape=[jax.ShapeDtypeStruct((D_MODEL, 2 * n), BF16),
                   jax.ShapeDtypeStruct((n, 4 * D_RWKV), BF16),
                   jax.ShapeDtypeStruct((1, 4 * D_RWKV), F32)],
        compiler_params=pltpu.CompilerParams(vmem_limit_bytes=VMEM_LIMIT),
        name="lora_prep",
    )(mu_t, dw1, ia1, dw2, ia2, dw0, ia0)


def _front_group(gi, h, halo_prev, halo_next, is_lat, win_ref, waug_ref, refs, outs):
    w2aug_ref, b2aug_ref, kkw_ref, ka_ref, rk_ref, cw_ref, cbias_ref, wbb_ref, ones_ref = refs
    ops_o, v_o, ends_o, bv_o, sg_o, sa_o, mb_o = outs
    rows = slice(gi * TM, (gi + 1) * TM)
    row = lax.broadcasted_iota(jnp.int32, (TM, 1), 0)
    rin = row & (CHUNK - 1)
    ones = ones_ref[...]

    def proj(lo, hi):
        return _bdot(h, win_ref[:, lo:hi])

    pab = _bdot(h, waug_ref[...])
    rk = proj(0, 2 * D_RWKV)
    r = rk[:, :D_RWKV]
    k = rk[:, D_RWKV:]
    pb = pab[:, 2 * LANES:]
    sh_f = jnp.where(row == 0, halo_prev, pltpu.roll(pb[:, :LANES], 1, 0))
    sh_b = jnp.where(row == TM - 1, halo_next, pltpu.roll(pb[:, LANES:], TM - 1, 0))
    t_in = pab[:, :2 * LANES] + jnp.concatenate([sh_f, sh_b], axis=1)
    lane = lax.broadcasted_iota(jnp.int32, (1, 2 * LANES), 1)
    t_in = jnp.where((lane & HEAD) == 0, jnp.tanh(t_in), t_in)
    za = _dot(t_in, w2aug_ref[...]) + b2aug_ref[...]

    kk = k * kkw_ref[...]
    kkn = kk * lax.rsqrt(_head_sum(kk * kk, ones) + 1e-12)
    ka = ka_ref[...]

    vg = proj(2 * D_RWKV, 4 * D_RWKV)
    v = vg[:, :D_RWKV]
    v_o[rows, :] = v.astype(BF16)
    sg_o[rows, :] = _sigmoid(vg[:, D_RWKV:]).astype(BF16)

    def scan_operands(d):
        lw = -EXP_M05 * _sigmoid(za[:, 2 * d * D_RWKV:(2 * d + 1) * D_RWKV])
        a = _sigmoid(za[:, (2 * d + 1) * D_RWKV:(2 * d + 2) * D_RWKV])
        k_d = k * (1.0 + (a - 1.0) * ka)
        b = kkn * a
        cs = lw
        for s in (1, 2, 4, 8, 16, 32):
            if d == 0:
                cs = cs + jnp.where(rin >= s, pltpu.roll(cs, s, 0), 0.0)
            else:
                cs = cs + jnp.where(rin < CHUNK - s, pltpu.roll(cs, TM - s, 0), 0.0)
        end_row = CHUNK - 1 if d == 0 else 0
        ends = [cs[c * CHUNK + end_row:c * CHUNK + end_row + 1, :] for c in range(N_CHUNK)]
        for c in range(N_CHUNK):
            ends_o[d][gi, c:c + 1, :] = ends[c]
        cs_end = jnp.concatenate([jnp.broadcast_to(e, (CHUNK, D_RWKV)) for e in ends], axis=0)
        dec_inv = jnp.exp(-cs)
        dec_rest = jnp.exp(cs_end - cs)
        o_a, o_r, o_b, o_k, o_bh, o_kh = ops_o[d]
        o_a[rows, :] = (-kkn * jnp.exp(cs - lw)).astype(BF16)
        o_r[rows, :] = (r * jnp.exp(cs)).astype(BF16)
        o_b[rows, :] = (b * dec_inv).astype(BF16)
        o_k[rows, :] = (k_d * dec_inv).astype(BF16)
        o_bh[rows, :] = (b * dec_rest).astype(BF16)
        o_kh[rows, :] = (k_d * dec_rest).astype(BF16)
        return k_d

    conv_in = proj(4 * D_RWKV, 4 * D_RWKV + 3 * D_CONV)
    k_0 = scan_operands(0)
    gate_a = proj(4 * D_RWKV + 3 * D_CONV, 4 * D_RWKV + 3 * D_CONV + D_MODEL)
    sa_o[rows, :] = _sigmoid(gate_a).astype(BF16)
    k_1 = scan_operands(1)
    gate_b = proj(4 * D_RWKV + 3 * D_CONV + D_MODEL, D_IN)

    cgate = conv_in[:, :D_CONV]
    u = conv_in[:, D_CONV:2 * D_CONV] * conv_in[:, 2 * D_CONV:]
    col = row & (GRID_W - 1)
    zl = jnp.logical_or(row == 0, jnp.logical_and(is_lat, col == 0))
    zr = jnp.logical_or(row == TM - 1, jnp.logical_and(is_lat, col == GRID_W - 1))
    left = jnp.where(zl, 0.0, pltpu.roll(u, 1, 0))
    right = jnp.where(zr, 0.0, pltpu.roll(u, TM - 1, 0))
    conv = left * cw_ref[0:1, :] + u * cw_ref[1:2, :] + right * cw_ref[2:3, :] + cbias_ref[...]
    y_b = _dot(cgate * conv, wbb_ref[...])
    mb_o[rows, :] = (_sigmoid(gate_b) * y_b).astype(BF16)

    bv_o[rows, :] = (_head_sum(r * (k_0 + k_1) * rk_ref[...], ones) * v).astype(BF16)


def _front_kernel(x_ref, xp_ref, xn_ref, mod_ref, g_ref, win_ref, waug_ref, *rest):
    refs = list(rest[:N_FRONT_CONSTS])
    outs = rest[N_FRONT_CONSTS:-1]
    wbb_bf = rest[-1]
    ops_o = (outs[0:N_DIR_OPS], outs[N_DIR_OPS:2 * N_DIR_OPS])
    v_o, ends0_o, ends1_o, bv_o, sg_o, sa_o, mb_o = outs[2 * N_DIR_OPS:]
    outs = (ops_o, v_o, (ends0_o, ends1_o), bv_o, sg_o, sa_o, mb_o)

    t = pl.program_id(0)

    @pl.when(t == 0)
    def _():
        wbb_bf[...] = refs[FRONT_WBB][...].astype(BF16)

    refs[FRONT_WBB] = wbb_bf

    is_lat = t >= N_FRONT_CTX_TILES
    i_in = (t - N_FRONT_CTX_TILES) % FRONT_LAT_TILES
    lat_first = jnp.logical_and(is_lat, i_in == 0)
    lat_last = jnp.logical_and(is_lat, i_in == FRONT_LAT_TILES - 1)
    shift, scale = _mod_chunks(mod_ref, t * FRONT_GROUPS, MOD_FRONT, 2)
    g2 = g_ref[2:3, :]

    def pre(x):
        return (_rms(x, g2) * (1.0 + scale) + shift).astype(BF16)

    x = x_ref[...]
    edge = jnp.concatenate([xp_ref[...], x[TM - 8:TM + 8, :], xn_ref[...]], axis=0)
    edge_b = _bdot(pre(edge), waug_ref[:, 2 * LANES:])
    zero = jnp.zeros((1, LANES), F32)
    halo_prev = (jnp.where(jnp.logical_and(is_lat, jnp.logical_not(lat_first)), edge_b[7:8, :LANES], zero),
                 jnp.where(is_lat, edge_b[15:16, :LANES], zero))
    halo_next = (jnp.where(is_lat, edge_b[16:17, LANES:], zero),
                 jnp.where(jnp.logical_and(is_lat, jnp.logical_not(lat_last)), edge_b[24:25, LANES:], zero))

    for gi in range(FRONT_GROUPS):
        h = pre(x[gi * TM:(gi + 1) * TM, :])
        _front_group(gi, h, halo_prev[gi], halo_next[gi], is_lat, win_ref, waug_ref, refs, outs)


def _front(x, mod3, norm_g, w_in_bf, w_aug, consts):
    tok = lambda t: (t, 0)
    rows8 = FRONT_TM // 8
    last8 = N_TOK // 8 - 1
    assert len(consts) == N_FRONT_CONSTS
    out_rb = jax.ShapeDtypeStruct((N_TOK, D_RWKV), BF16)
    out_db = jax.ShapeDtypeStruct((N_TOK, D_MODEL), BF16)
    out_e = jax.ShapeDtypeStruct((N_TILES, N_CHUNK, D_RWKV), F32)
    spec_r = pl.BlockSpec((FRONT_TM, D_RWKV), tok)
    spec_d = pl.BlockSpec((FRONT_TM, D_MODEL), tok)
    spec_e = pl.BlockSpec((FRONT_GROUPS, N_CHUNK, D_RWKV), lambda t: (t, 0, 0))
    n_b = 2 * N_DIR_OPS + 1
    return pl.pallas_call(
        _front_kernel,
        grid=(N_TOK // FRONT_TM,),
        in_specs=[pl.BlockSpec((FRONT_TM, D_MODEL), tok),
                  pl.BlockSpec((8, D_MODEL), lambda t: (jnp.maximum(t * rows8 - 1, 0), 0)),
                  pl.BlockSpec((8, D_MODEL), lambda t: (jnp.minimum((t + 1) * rows8, last8), 0)),
                  _const_spec(mod3.shape)]
                 + [_const_spec(c.shape) for c in [norm_g, w_in_bf, w_aug] + list(consts)],
        out_specs=[spec_r] * n_b + [spec_e] * 2 + [spec_r] * 2 + [spec_d] * 2,
        out_shape=[out_rb] * n_b + [out_e] * 2 + [out_rb] * 2 + [out_db] * 2,
        scratch_shapes=[pltpu.VMEM((D_CONV, D_MODEL), BF16)],
        compiler_params=_params(),
        name="mixer_front",
    )(x, x, x, mod3, norm_g, w_in_bf, w_aug, *consts)


def _stack(x):
    lane_lo = lax.broadcasted_iota(jnp.int32, x.shape, 1) < HEAD
    z = jnp.zeros_like(x)
    return jnp.concatenate([jnp.where(lane_lo, x, z), jnp.where(lane_lo, z, x)], axis=0)


def _scan_kernel(*refs):
    ops = (refs[0:N_DIR_OPS + 1], refs[N_DIR_OPS + 1:2 * N_DIR_OPS + 2])
    tail = refs[2 * N_DIR_OPS + 2:]
    ends0_ref, ends1_ref, s0_ref = tail[:3]
    side_in = tail[3:3 + N_SCAN_SIDE]
    yf_ref, yb_ref, sout_ref = tail[3 + N_SCAN_SIDE:6 + N_SCAN_SIDE]
    side_out = tail[6 + N_SCAN_SIDE:6 + 2 * N_SCAN_SIDE]
    st_ref = tail[-1]
    ends_ref = (ends0_ref, ends1_ref)
    for src, dst in zip(side_in, side_out):
        dst[...] = src[...].astype(BF16)
    step_id = pl.program_id(0)
    is_lat = step_id >= N_SCAN_CTX_STEPS
    lat_first = (step_id - N_SCAN_CTX_STEPS) % SCAN_LAT_STEPS == 0

    @pl.when(jnp.logical_and(is_lat, lat_first))
    def _():
        zero = jnp.zeros((HEAD, HEAD), F32)
        for d in (0, 1):
            for p in range(N_PAIR):
                top = jnp.concatenate([s0_ref[0, d, 2 * p], zero], axis=1)
                bottom = jnp.concatenate([zero, s0_ref[0, d, 2 * p + 1]], axis=1)
                st_ref[d, p] = jnp.concatenate([top, bottom], axis=0).T

    ti = lax.broadcasted_iota(jnp.int32, (CHUNK, LANES), 0)
    sj = lax.broadcasted_iota(jnp.int32, (CHUNK, LANES), 1) & (CHUNK - 1)
    m_strict = (sj < ti, sj > ti)
    m_incl = (sj <= ti, sj >= ti)
    eye_cat = jnp.where(sj == ti, 1.0, 0.0)
    bi = lax.broadcasted_iota(jnp.int32, (LANES, LANES), 0)
    bj = lax.broadcasted_iota(jnp.int32, (LANES, LANES), 1)
    blk = (bi >> HEAD_SHIFT) == (bj >> HEAD_SHIFT)
    eye_bd = bi == bj
    zero_bd = jnp.zeros((LANES, LANES), BF16)

    def op(i, u):
        d, p, c = u
        return ops[d][i][c * CHUNK:(c + 1) * CHUNK, p * LANES:(p + 1) * LANES]

    g_mat, w_mat, pg, q_mat = {}, {}, {}, {}

    def local_stages(units):
        low, nak, mrbk = {}, {}, {}
        for u in units:
            lhs = jnp.concatenate([op(OP_A, u), op(OP_R, u)], axis=0)
            rhs = jnp.concatenate([_stack(op(OP_B, u)), _stack(op(OP_K, u))], axis=0)
            gram = _bdot(lhs, rhs, NT)
            d = u[0]
            low[u] = jnp.where(m_strict[d], gram[:CHUNK, :LANES], 0.0)
            nak[u] = jnp.where(m_strict[d], gram[:CHUNK, LANES:], 0.0).astype(BF16)
            mrbk[u] = jnp.concatenate([jnp.where(m_incl[d], gram[CHUNK:, :LANES], 0.0),
                                       jnp.where(m_incl[d], gram[CHUNK:, LANES:], 0.0)], axis=1).astype(BF16)

        inv = {u: eye_cat + low[u] for u in units}
        pwb = {u: low[u].astype(BF16) for u in units}
        for u in units:
            pwb[u] = _bdot(pwb[u], _stack(pwb[u])).astype(BF16)
        for _ in range(4):
            for u in units:
                both = _bdot(pwb[u], jnp.concatenate([_stack(pwb[u]), _stack(inv[u].astype(BF16))], axis=1))
                pwb[u] = both[:, :LANES].astype(BF16)
                inv[u] = inv[u] + both[:, LANES:]
        for u in units:
            inv[u] = inv[u] + _bdot(pwb[u], _stack(inv[u].astype(BF16)))

        sv = {u: _stack(op(OP_V, u)) for u in units}
        nv = {}
        for u in units:
            nv[u] = _bdot(nak[u], sv[u]).astype(BF16)
        x1, x2 = {}, {}
        for u in units:
            x12 = _bdot(inv[u].astype(BF16), jnp.concatenate([_stack(op(OP_A, u)), _stack(nv[u])], axis=1))
            x1[u] = x12[:, :LANES].astype(BF16)
            x2[u] = x12[:, LANES:].astype(BF16)
        for u in units:
            rhs = jnp.concatenate([jnp.concatenate([_stack(x1[u]), _stack(x2[u])], axis=1),
                                   jnp.concatenate([zero_bd, sv[u]], axis=1)], axis=0)
            gw = _bdot(mrbk[u], rhs)
            g_mat[u] = (op(OP_R, u).astype(F32) + gw[:, :LANES]).astype(BF16)
            w_mat[u] = gw[:, LANES:]
        for u in units:
            d, p, c = u
            ct = c % N_CHUNK
            gam = jnp.exp(ends_ref[d][c // N_CHUNK, ct:ct + 1, p * LANES:(p + 1) * LANES])
            rhs = jnp.concatenate([jnp.concatenate([x1[u], x2[u]], axis=1),
                                   jnp.concatenate([jnp.zeros_like(x1[u]), op(OP_V, u)], axis=1)], axis=0)
            pq = _bdot(jnp.concatenate([op(OP_BH, u), op(OP_KH, u)], axis=0), rhs, TN)
            p_mat = (jnp.where(blk, pq[:, :LANES], 0.0) + jnp.where(eye_bd, gam, 0.0)).astype(BF16)
            pg[u] = jnp.concatenate([p_mat, g_mat[u]], axis=0)
            q_mat[u] = jnp.where(blk, pq[:, LANES:], 0.0)

    n_c = SCAN_TILES * N_CHUNK
    state = {(d, p): jnp.where(is_lat, st_ref[d, p], 0.0) for d in (0, 1) for p in range(N_PAIR)}
    finals = {}
    y_refs = (yf_ref, yb_ref)

    def chunk_of(d, step):
        return step if d == 0 else n_c - 1 - step

    def chain(step, d, p):
        c = chunk_of(d, step)
        u = (d, p, c)
        both = _bdot(pg[u], state[d, p].astype(BF16))
        y_refs[d][c * CHUNK:(c + 1) * CHUNK, p * LANES:(p + 1) * LANES] = both[LANES:, :] + w_mat[u]
        state[d, p] = both[:LANES, :] + q_mat[u]
        if (step + 1) % N_CHUNK == 0:
            finals[d, c // N_CHUNK, p] = state[d, p]
            if step + 1 < n_c:
                state[d, p] = jnp.where(is_lat, state[d, p], 0.0)

    local_stages([(d, p, c) for d in (0, 1) for p in range(N_PAIR) for c in range(n_c)])
    for step in range(n_c):
        for d in (0, 1):
            for p in range(N_PAIR):
                chain(step, d, p)
    for (d, p), s in state.items():
        st_ref[d, p] = s

    @pl.when(jnp.logical_not(is_lat))
    def _():
        for (d, tile, p), h in finals.items():
            s = h.T
            sout_ref[tile, 0, d, 2 * p] = s[:HEAD, :HEAD]
            sout_ref[tile, 0, d, 2 * p + 1] = s[HEAD:, HEAD:]


def _scan(ops_f, ops_b, v, ends_f, ends_b, s0, side):
    assert len(side) == N_SCAN_SIDE
    def mirror(s):
        u = s - N_SCAN_CTX_STEPS
        return jnp.where(s < N_SCAN_CTX_STEPS, s,
                         N_SCAN_CTX_STEPS + (u // SCAN_LAT_STEPS) * SCAN_LAT_STEPS
                         + (SCAN_LAT_STEPS - 1 - u % SCAN_LAT_STEPS))

    def lat_seq(s):
        return jnp.maximum(s - N_SCAN_CTX_STEPS, 0) // SCAN_LAT_STEPS

    fwd = pl.BlockSpec((SCAN_TM, D_RWKV), lambda s: (s, 0))
    bwd = pl.BlockSpec((SCAN_TM, D_RWKV), lambda s: (mirror(s), 0))
    e_block = (SCAN_TILES, N_CHUNK, D_RWKV)
    st_block = (None, 1, 2, 2 * N_PAIR, HEAD, HEAD)
    out_y = jax.ShapeDtypeStruct((N_TOK, D_RWKV), F32)
    n_steps = N_TOK // SCAN_TM
    return pl.pallas_call(
        _scan_kernel,
        grid=(n_steps,),
        in_specs=[fwd] * (N_DIR_OPS + 1) + [bwd] * (N_DIR_OPS + 1)
                 + [pl.BlockSpec(e_block, lambda s: (s, 0, 0)),
                    pl.BlockSpec(e_block, lambda s: (mirror(s), 0, 0)),
                    pl.BlockSpec(st_block, lambda s: (lat_seq(s), 0, 0, 0, 0, 0))]
                 + [_slab_spec(w, n_steps) for w in side],
        out_specs=[fwd, bwd,
                   pl.BlockSpec((SCAN_TILES, 1, 2, 2 * N_PAIR, HEAD, HEAD),
                                lambda s: (jnp.minimum(s, N_SCAN_CTX_STEPS - 1), 0, 0, 0, 0, 0))]
                  + [_slab_spec(w, n_steps) for w in side],
        out_shape=[out_y, out_y,
                   jax.ShapeDtypeStruct((N_CTX_SEQ, 1, 2, 2 * N_PAIR, HEAD, HEAD), F32)]
                  + [jax.ShapeDtypeStruct(w.shape, BF16) for w in side],
        scratch_shapes=[pltpu.VMEM((2, N_PAIR, LANES, LANES), F32)],
        compiler_params=_params(),
        name="rwkv7_scan",
    )(*ops_f, v, *ops_b, v, ends_f, ends_b, s0, *side)


def _back_ffn_kernel(x_ref, yf_ref, yb_ref, bv_ref, sg_ref, sa_ref, mb_ref, mod_ref, g_ref,
                     gng_ref, gnb_ref, wba_ref, wout_ref, ones_ref, w13_ref, w2_ref, oc_ref, ol_ref):
    t = pl.program_id(0)
    tile = t * (FFN_TM // TM)
    gate, = _mod_chunks(mod_ref, tile, MOD_BACK, 1)
    ones = ones_ref[...]
    groups = [slice(i * TM, (i + 1) * TM) for i in range(FFN_TM // TM)]
    ys = [yf_ref[r, :] + yb_ref[r, :] for r in groups]
    ycs = [y - _head_sum(y, ones) * (1.0 / HEAD) for y in ys]
    vs = [_head_sum(yc * yc, ones) * (1.0 / HEAD) for yc in ycs]
    yns = [yc * lax.rsqrt(v + EPS_GN) * gng_ref[...] + gnb_ref[...] for yc, v in zip(ycs, vs)]
    yas = [_dot((yn + bv_ref[r, :]) * sg_ref[r, :], wba_ref[...]) for r, yn in zip(groups, yns)]
    outs = [_dot(sa_ref[r, :] * ya + mb_ref[r, :], wout_ref[...]) for r, ya in zip(groups, yas)]
    x2 = jnp.concatenate([x_ref[r, :] + gate * _rms(out, g_ref[3:4, :]) for r, out in zip(groups, outs)], axis=0)

    mod = _mod_chunks(mod_ref, tile, MOD_FFN2, 3)
    out = _ffn_body(x2, mod, g_ref, w13_ref, w2_ref, 4)

    @pl.when(t < N_FFN_CTX_TILES)
    def _():
        oc_ref[...] = out

    @pl.when(t >= N_FFN_CTX_TILES)
    def _():
        ol_ref[...] = out


def _back_ffn(x, yf, yb, bv, sg, sa, mb, mod3, norm_g, gng, gnb, wba_bf, wout_bf, ones, w13_bf, w2_bf):
    tok = lambda t: (t, 0)
    spec_r = pl.BlockSpec((FFN_TM, D_RWKV), tok)
    spec_d = pl.BlockSpec((FFN_TM, D_MODEL), tok)
    consts = [mod3, norm_g, gng, gnb, wba_bf, wout_bf, ones, w13_bf, w2_bf]
    n_ctx = N_CTX_SEQ * CTX_LEN
    return pl.pallas_call(
        _back_ffn_kernel,
        grid=(N_FFN_TILES,),
        in_specs=[spec_d, spec_r, spec_r, spec_r, spec_r, spec_d, spec_d]
                 + [_const_spec(c.shape) for c in consts],
        out_specs=[pl.BlockSpec((FFN_TM, D_MODEL), _ctx_tile),
                   pl.BlockSpec((FFN_TM, D_MODEL), _lat_tile)],
        out_shape=[jax.ShapeDtypeStruct((n_ctx, D_MODEL), F32),
                   jax.ShapeDtypeStruct((N_TOK - n_ctx, D_MODEL), F32)],
        compiler_params=_params(),
        name="back_ffn2",
    )(x, yf, yb, bv, sg, sa, mb, *consts)


def kernel(x_prompt, x_sample, c, state_rwkv, c_ctx, w_mod, b_mod, norm_g, ffn1_w13, ffn1_w2,
           ffn2_w13, ffn2_w2, w_in, mu_shift, decay_w0, decay_w1, decay_w2, iclr_a0, iclr_a1,
           iclr_a2, k_k, k_a, r_k, gn_gain, gn_bias, conv_w, conv_b, w_branch_a, w_branch_b, w_out):
    assert x_prompt.shape == (N_CTX_SEQ, CTX_LEN, D_MODEL) and x_sample.shape == (N_LAT_SEQ, LAT_LEN, D_MODEL)
    assert w_mod.shape[0] == 1, "single trunk layer"

    g = norm_g[0]
    x, w_in_bf, mod3 = _ffn_first(x_prompt.reshape(-1, D_MODEL), x_sample.reshape(-1, D_MODEL), g,
                                  ffn1_w13[0], ffn1_w2[0], w_in[0], c_ctx.reshape(1, D_MODEL), c,
                                  w_mod[0], b_mod)

    row = lambda p: p.reshape(1, -1)
    w_aug, w2_aug, b2_aug = _lora_prep(mu_shift[0].reshape(4, D_MODEL).T, decay_w1[0], iclr_a1[0],
                                       decay_w2[0], iclr_a2[0], decay_w0[0], iclr_a0[0])
    ones = _head_ones()
    front = _front(x, mod3, g, w_in_bf, w_aug,
                   [w2_aug, b2_aug, row(k_k[0]), row(k_a[0]), row(r_k[0]), conv_w[0], row(conv_b[0]),
                    w_branch_b[0], ones])
    ops_f, ops_b = front[0:N_DIR_OPS], front[N_DIR_OPS:2 * N_DIR_OPS]
    v, ends_f, ends_b, bv, sg, sa, mb = front[2 * N_DIR_OPS:]

    yf, yb, s_fin, w13_bf, w2_bf, wba_bf, wout_bf = _scan(
        ops_f, ops_b, v, ends_f, ends_b, state_rwkv, [ffn2_w13[0], ffn2_w2[0], w_branch_a[0], w_out[0]])

    y_ctx, y_lat = _back_ffn(x, yf, yb, bv, sg, sa, mb, mod3, g, row(gn_gain[0]), row(gn_bias[0]),
                             wba_bf, wout_bf, ones, w13_bf, w2_bf)

    y_prompt = y_ctx.reshape(N_CTX_SEQ, CTX_LEN, D_MODEL)
    y_sample = y_lat.reshape(N_LAT_SEQ, LAT_LEN, D_MODEL)
    return y_prompt, y_sample, s_fin
```

```python
import jax
import jax.numpy as jnp
from jax import lax
from jax.experimental import pallas as pl
from jax.experimental.pallas import tpu as pltpu

F32 = jnp.float32
BF16 = jnp.bfloat16

D_MODEL = 1024
D_FF = 2816
D_RWKV = 512
D_CONV = 512
HEAD = 64
HEAD_SHIFT = 6
D_IN = 4 * D_RWKV + 3 * D_CONV + 2 * D_MODEL
N_MOD = 9
EPS_RMS = 1e-6
EPS_GN = 64e-5
HALF_STEP = 0.5
EXP_M05 = 0.6065306597126334

N_CTX_SEQ = 16
CTX_LEN = 256
N_LAT_SEQ = 2
LAT_LEN = 2048
GRID_W = 64
N_TOK = N_CTX_SEQ * CTX_LEN + N_LAT_SEQ * LAT_LEN

TM = 256
N_CTX_TILES = N_CTX_SEQ * CTX_LEN // TM
LAT_TILES = LAT_LEN // TM
N_TILES = N_TOK // TM
FRONT_TM = 512
FRONT_GROUPS = FRONT_TM // TM
N_FRONT_CTX_TILES = N_CTX_SEQ * CTX_LEN // FRONT_TM
FRONT_LAT_TILES = LAT_LEN // FRONT_TM
N_FRONT_CONSTS = 9
FRONT_WBB = 7
LORA = 64
SCAN_TM = 512
SCAN_TILES = SCAN_TM // TM
N_SCAN_CTX_STEPS = N_CTX_SEQ * CTX_LEN // SCAN_TM
SCAN_LAT_STEPS = LAT_LEN // SCAN_TM
FFN_TM = 512
N_FFN_TILES = N_TOK // FFN_TM
N_FFN_CTX_TILES = N_CTX_SEQ * CTX_LEN // FFN_TM
N_WCHUNK = 11
CHUNK = 64
N_CHUNK = TM // CHUNK
LANES = 128
N_PAIR = D_RWKV // LANES
MOD_ROWS = 8
MOD_HEAD = 3
MOD_TAIL = N_MOD - MOD_HEAD
MOD_SIDE = 512
N_MOD_SIDE = MOD_TAIL * D_MODEL // MOD_SIDE
N_MOD_HEAD_BLOCKS = MOD_HEAD * D_MODEL // MOD_SIDE
MOD_FRONT, MOD_BACK, MOD_FFN2 = 0, 2, 3
VMEM_LIMIT = 56 * 1024 * 1024

OP_A, OP_R, OP_B, OP_K, OP_BH, OP_KH, OP_V = range(7)
N_DIR_OPS = 6
N_SCAN_SIDE = 4

NN = (((1,), (0,)), ((), ()))
NT = (((1,), (1,)), ((), ()))
TN = (((0,), (0,)), ((), ()))


def _dot(a, b, dims=NN):
    return lax.dot_general(a.astype(BF16), b.astype(BF16), dims, preferred_element_type=F32)


def _bdot(a, b, dims=NN):
    return lax.dot_general(a, b, dims, preferred_element_type=F32)


def _rms(x, g):
    ms = jnp.mean(x * x, axis=-1, keepdims=True)
    return x * lax.rsqrt(ms + EPS_RMS) * g


def _head_ones():
    i = lax.broadcasted_iota(jnp.int32, (D_RWKV, D_RWKV), 0) >> HEAD_SHIFT
    j = lax.broadcasted_iota(jnp.int32, (D_RWKV, D_RWKV), 1) >> HEAD_SHIFT
    return jnp.where(i == j, 1.0, 0.0).astype(BF16)


def _head_sum(x, ones):
    return jnp.dot(x.astype(BF16), ones, preferred_element_type=F32)


def _sigmoid(x):
    return 0.5 * jnp.tanh(0.5 * x) + 0.5


def _mod_row(t):
    return jnp.where(t < N_CTX_TILES, 0, 1 + (t - N_CTX_TILES) // LAT_TILES)


def _mod_chunks(mod_ref, tile, first, count):
    row = pl.ds(_mod_row(tile), 1)
    return [mod_ref[row, (first + i) * D_MODEL:(first + i + 1) * D_MODEL] for i in range(count)]


def _const_spec(shape):
    nd = len(shape)
    return pl.BlockSpec(shape, lambda *_: (0,) * nd, pipeline_mode=pl.Buffered(1))


def _params(n_axes=1):
    return pltpu.CompilerParams(dimension_semantics=("arbitrary",) * n_axes,
                                vmem_limit_bytes=VMEM_LIMIT)


def _mod_inputs(cctx_ref, c_ref, act_ref):
    def silu(c):
        return c * jax.nn.sigmoid(c)

    act_ref[0:1, :] = silu(cctx_ref[...])
    act_ref[1:1 + N_LAT_SEQ, :] = silu(c_ref[...])
    act_ref[1 + N_LAT_SEQ:, :] = jnp.zeros((MOD_ROWS - 1 - N_LAT_SEQ, D_MODEL), F32)


def _mod_block(act_ref, w_ref, b_ref, o_ref):
    o_ref[...] = _dot(act_ref[...], w_ref[...]) + b_ref[...]


def _ffn_body(x, mod, g_ref, w13_ref, w2_ref, ig):
    shift, scale, gate = mod
    xs = [x[i * TM:(i + 1) * TM, :] for i in range(x.shape[0] // TM)]
    hs = [(_rms(xi, g_ref[ig:ig + 1, :]) * (1.0 + scale) + shift).astype(BF16) for xi in xs]
    gus = [_dot(hi, w13_ref[...]) for hi in hs]
    acts = []
    for gu in gus:
        gt = gu[:, :D_FF]
        up = gu[:, D_FF:]
        acts.append((gt * jax.nn.sigmoid(gt) * up).astype(BF16))
    os_ = [_dot(ai, w2_ref[...]) for ai in acts]
    outs = [xi + HALF_STEP * gate * _rms(oi, g_ref[ig + 1:ig + 2, :]) for xi, oi in zip(xs, os_)]
    return jnp.concatenate(outs, axis=0)


def _cast_chunk(step, src_ref, dst_ref, axis):
    size = src_ref.shape[axis]

    @pl.when(step < N_WCHUNK)
    def _():
        start = pl.multiple_of(step * size, size)
        if axis == 0:
            dst_ref[pl.ds(start, size), :] = src_ref[...].astype(BF16)
        else:
            dst_ref[:, pl.ds(start, size)] = src_ref[...].astype(BF16)


def _chunk_spec(w, axis):
    block = list(w.shape)
    block[axis] = w.shape[axis] // N_WCHUNK
    clamp = lambda s: jnp.minimum(s, N_WCHUNK - 1)
    index = (lambda s: (clamp(s), 0)) if axis == 0 else (lambda s: (0, clamp(s)))
    return pl.BlockSpec(tuple(block), index)


def _tile_step(s):
    return jnp.maximum(s - N_WCHUNK, 0)


def _slab_spec(w, n_steps, step=lambda s: s):
    rows = w.shape[0] // n_steps
    return pl.BlockSpec((rows, w.shape[1]), lambda s: (step(s), 0))


def _ffn_first_kernel(xc_ref, xl_ref, g_ref, w13_ref, w2_ref, win_ref, cctx_ref, c_ref, wmod_ref, bmod_ref,
                      o_ref, win_bf_ref, mod_tail_ref, w13_bf, w2_bf, mod_ref, act_ref):
    s = pl.program_id(0)
    _cast_chunk(s, w13_ref, w13_bf, 1)
    _cast_chunk(s, w2_ref, w2_bf, 0)

    @pl.when(s == 0)
    def _():
        _mod_inputs(cctx_ref, c_ref, act_ref)

    @pl.when(s < N_MOD_HEAD_BLOCKS)
    def _():
        col = pl.multiple_of(s * MOD_SIDE, MOD_SIDE)
        _mod_block(act_ref, wmod_ref, bmod_ref, mod_ref.at[:, pl.ds(col, MOD_SIDE)])

    @pl.when(s >= N_WCHUNK)
    def _():
        x = jnp.where(s - N_WCHUNK < N_FFN_CTX_TILES, xc_ref[...], xl_ref[...])
        mod = _mod_chunks(mod_ref, (s - N_WCHUNK) * (FFN_TM // TM), 0, 3)
        o_ref[...] = _ffn_body(x, mod, g_ref, w13_bf, w2_bf, 0)
        win_bf_ref[...] = win_ref[...].astype(BF16)

        @pl.when(s - N_WCHUNK < N_MOD_SIDE)
        def _():
            _mod_block(act_ref, wmod_ref, bmod_ref, mod_tail_ref)


def _ctx_tile(t):
    return jnp.minimum(t, N_FFN_CTX_TILES - 1), 0


def _lat_tile(t):
    return jnp.maximum(t - N_FFN_CTX_TILES, 0), 0


def _ffn_first(x_ctx, x_lat, norm_g, w13, w2, w_in, c_ctx, c, w_mod, b_mod):
    side = lambda s: jnp.minimum(_tile_step(s), N_MOD_SIDE - 1)

    def mod_block(s):
        return 0, jnp.where(s < N_WCHUNK, jnp.minimum(s, N_MOD_HEAD_BLOCKS - 1), N_MOD_HEAD_BLOCKS + side(s))

    return pl.pallas_call(
        _ffn_first_kernel,
        grid=(N_WCHUNK + N_FFN_TILES,),
        in_specs=[pl.BlockSpec((FFN_TM, D_MODEL), lambda s: _ctx_tile(_tile_step(s))),
                  pl.BlockSpec((FFN_TM, D_MODEL), lambda s: _lat_tile(_tile_step(s))),
                  _const_spec(norm_g.shape),
                  _chunk_spec(w13, 1),
                  _chunk_spec(w2, 0),
                  _slab_spec(w_in, N_FFN_TILES, step=_tile_step),
                  _const_spec(c_ctx.shape),
                  _const_spec(c.shape),
                  pl.BlockSpec((D_MODEL, MOD_SIDE), mod_block),
                  pl.BlockSpec((1, MOD_SIDE), mod_block)],
        out_specs=[pl.BlockSpec((FFN_TM, D_MODEL), lambda s: (_tile_step(s), 0)),
                   _slab_spec(w_in, N_FFN_TILES, step=_tile_step),
                   pl.BlockSpec((MOD_ROWS, MOD_SIDE), lambda s: (0, side(s)))],
        out_shape=[jax.ShapeDtypeStruct((N_TOK, D_MODEL), F32),
                   jax.ShapeDtypeStruct(w_in.shape, BF16),
                   jax.ShapeDtypeStruct((MOD_ROWS, MOD_TAIL * D_MODEL), F32)],
        scratch_shapes=[pltpu.VMEM((D_MODEL, 2 * D_FF), BF16), pltpu.VMEM((D_FF, D_MODEL), BF16),
                        pltpu.VMEM((MOD_ROWS, MOD_HEAD * D_MODEL), F32),
                        pltpu.VMEM((MOD_ROWS, D_MODEL), F32)],
        compiler_params=_params(),
        name="ffn1",
    )(x_ctx, x_lat, norm_g, w13, w2, w_in, c_ctx, c, w_mod, b_mod)


def _lora_prep_kernel(mu_ref, dw1_ref, ia1_ref, dw2_ref, ia2_ref, dw0_ref, ia0_ref,
                      w1_o, w2_o, b2_o):
    firsts = (dw1_ref[0], ia1_ref[0], dw1_ref[1], ia1_ref[1])
    seconds = (dw2_ref[0], ia2_ref[0], dw2_ref[1], ia2_ref[1])
    biases = (dw0_ref[0:1, :], ia0_ref[0:1, :], dw0_ref[1:2, :], ia0_ref[1:2, :])
    n = len(firsts) * LORA
    w2_o[...] = jnp.zeros(w2_o.shape, BF16)
    for j in range(len(firsts)):
        mu = mu_ref[:, j:j + 1]
        w1_o[:, j * LORA:(j + 1) * LORA] = ((1.0 - mu) * firsts[j]).astype(BF16)
        w1_o[:, n + j * LORA:n + (j + 1) * LORA] = (mu * firsts[j]).astype(BF16)
        w2_o[j * LORA:(j + 1) * LORA, j * D_RWKV:(j + 1) * D_RWKV] = seconds[j].astype(BF16)
        b2_o[:, j * D_RWKV:(j + 1) * D_RWKV] = biases[j]


def _lora_prep(mu_t, dw1, ia1, dw2, ia2, dw0, ia0):
    n = 4 * LORA
    return pl.pallas_call(
        _lora_prep_kernel,
        out_shape=[jax.ShapeDtypeStruct((D_MODEL, 2 * n), BF16),
                   jax.ShapeDtypeStruct((n, 4 * D_RWKV), BF16),
                   jax.ShapeDtypeStruct((1, 4 * D_RWKV), F32)],
        compiler_params=pltpu.CompilerParams(vmem_limit_bytes=VMEM_LIMIT),
        name="lora_prep",
    )(mu_t, dw1, ia1, dw2, ia2, dw0, ia0)


def _front_group(gi, h, halo_prev, halo_next, is_lat, win_ref, waug_ref, refs, outs):
    w2aug_ref, b2aug_ref, kkw_ref, ka_ref, rk_ref, cw_ref, cbias_ref, wbb_ref, ones_ref = refs
    ops_o, v_o, ends_o, bv_o, sg_o, sa_o, mb_o = outs
    rows = slice(gi * TM, (gi + 1) * TM)
    row = lax.broadcasted_iota(jnp.int32, (TM, 1), 0)
    rin = row & (CHUNK - 1)
    ones = ones_ref[...]

    def proj(lo, hi):
        return _bdot(h, win_ref[:, lo:hi])

    pab = _bdot(h, waug_ref[...])
    rk = proj(0, 2 * D_RWKV)
    r = rk[:, :D_RWKV]
    k = rk[:, D_RWKV:]
    pb = pab[:, 2 * LANES:]
    sh_f = jnp.where(row == 0, halo_prev, pltpu.roll(pb[:, :LANES], 1, 0))
    sh_b = jnp.where(row == TM - 1, halo_next, pltpu.roll(pb[:, LANES:], TM - 1, 0))
    t_in = pab[:, :2 * LANES] + jnp.concatenate([sh_f, sh_b], axis=1)
    lane = lax.broadcasted_iota(jnp.int32, (1, 2 * LANES), 1)
    t_in = jnp.where((lane & HEAD) == 0, jnp.tanh(t_in), t_in)
    za = _dot(t_in, w2aug_ref[...]) + b2aug_ref[...]

    kk = k * kkw_ref[...]
    kkn = kk * lax.rsqrt(_head_sum(kk * kk, ones) + 1e-12)
    ka = ka_ref[...]

    vg = proj(2 * D_RWKV, 4 * D_RWKV)
    v = vg[:, :D_RWKV]
    v_o[rows, :] = v.astype(BF16)
    sg_o[rows, :] = _sigmoid(vg[:, D_RWKV:]).astype(BF16)

    def scan_operands(d):
        lw = -EXP_M05 * _sigmoid(za[:, 2 * d * D_RWKV:(2 * d + 1) * D_RWKV])
        a = _sigmoid(za[:, (2 * d + 1) * D_RWKV:(2 * d + 2) * D_RWKV])
        k_d = k * (1.0 + (a - 1.0) * ka)
        b = kkn * a
        cs = lw
        for s in (1, 2, 4, 8, 16, 32):
            if d == 0:
                cs = cs + jnp.where(rin >= s, pltpu.roll(cs, s, 0), 0.0)
            else:
                cs = cs + jnp.where(rin < CHUNK - s, pltpu.roll(cs, TM - s, 0), 0.0)
        end_row = CHUNK - 1 if d == 0 else 0
        ends = [cs[c * CHUNK + end_row:c * CHUNK + end_row + 1, :] for c in range(N_CHUNK)]
        for c in range(N_CHUNK):
            ends_o[d][gi, c:c + 1, :] = ends[c]
        cs_end = jnp.concatenate([jnp.broadcast_to(e, (CHUNK, D_RWKV)) for e in ends], axis=0)
        dec_inv = jnp.exp(-cs)
        dec_rest = jnp.exp(cs_end - cs)
        o_a, o_r, o_b, o_k, o_bh, o_kh = ops_o[d]
        o_a[rows, :] = (-kkn * jnp.exp(cs - lw)).astype(BF16)
        o_r[rows, :] = (r * jnp.exp(cs)).astype(BF16)
        o_b[rows, :] = (b * dec_inv).astype(BF16)
        o_k[rows, :] = (k_d * dec_inv).astype(BF16)
        o_bh[rows, :] = (b * dec_rest).astype(BF16)
        o_kh[rows, :] = (k_d * dec_rest).astype(BF16)
        return k_d

    conv_in = proj(4 * D_RWKV, 4 * D_RWKV + 3 * D_CONV)
    k_0 = scan_operands(0)
    gate_a = proj(4 * D_RWKV + 3 * D_CONV, 4 * D_RWKV + 3 * D_CONV + D_MODEL)
    sa_o[rows, :] = _sigmoid(gate_a).astype(BF16)
    k_1 = scan_operands(1)
    gate_b = proj(4 * D_RWKV + 3 * D_CONV + D_MODEL, D_IN)

    cgate = conv_in[:, :D_CONV]
    u = conv_in[:, D_CONV:2 * D_CONV] * conv_in[:, 2 * D_CONV:]
    col = row & (GRID_W - 1)
    zl = jnp.logical_or(row == 0, jnp.logical_and(is_lat, col == 0))
    zr = jnp.logical_or(row == TM - 1, jnp.logical_and(is_lat, col == GRID_W - 1))
    left = jnp.where(zl, 0.0, pltpu.roll(u, 1, 0))
    right = jnp.where(zr, 0.0, pltpu.roll(u, TM - 1, 0))
    conv = left * cw_ref[0:1, :] + u * cw_ref[1:2, :] + right * cw_ref[2:3, :] + cbias_ref[...]
    y_b = _dot(cgate * conv, wbb_ref[...])
    mb_o[rows, :] = (_sigmoid(gate_b) * y_b).astype(BF16)

    bv_o[rows, :] = (_head_sum(r * (k_0 + k_1) * rk_ref[...], ones) * v).astype(BF16)


def _front_kernel(x_ref, xp_ref, xn_ref, mod_ref, g_ref, win_ref, waug_ref, *rest):
    refs = list(rest[:N_FRONT_CONSTS])
    outs = rest[N_FRONT_CONSTS:-1]
    wbb_bf = rest[-1]
    ops_o = (outs[0:N_DIR_OPS], outs[N_DIR_OPS:2 * N_DIR_OPS])
    v_o, ends0_o, ends1_o, bv_o, sg_o, sa_o, mb_o = outs[2 * N_DIR_OPS:]
    outs = (ops_o, v_o, (ends0_o, ends1_o), bv_o, sg_o, sa_o, mb_o)

    t = pl.program_id(0)

    @pl.when(t == 0)
    def _():
        wbb_bf[...] = refs[FRONT_WBB][...].astype(BF16)

    refs[FRONT_WBB] = wbb_bf

    is_lat = t >= N_FRONT_CTX_TILES
    i_in = (t - N_FRONT_CTX_TILES) % FRONT_LAT_TILES
    lat_first = jnp.logical_and(is_lat, i_in == 0)
    lat_last = jnp.logical_and(is_lat, i_in == FRONT_LAT_TILES - 1)
    shift, scale = _mod_chunks(mod_ref, t * FRONT_GROUPS, MOD_FRONT, 2)
    g2 = g_ref[2:3, :]

    def pre(x):
        return (_rms(x, g2) * (1.0 + scale) + shift).astype(BF16)

    x = x_ref[...]
    edge = jnp.concatenate([xp_ref[...], x[TM - 8:TM + 8, :], xn_ref[...]], axis=0)
    edge_b = _bdot(pre(edge), waug_ref[:, 2 * LANES:])
    zero = jnp.zeros((1, LANES), F32)
    halo_prev = (jnp.where(jnp.logical_and(is_lat, jnp.logical_not(lat_first)), edge_b[7:8, :LANES], zero),
                 jnp.where(is_lat, edge_b[15:16, :LANES], zero))
    halo_next = (jnp.where(is_lat, edge_b[16:17, LANES:], zero),
                 jnp.where(jnp.logical_and(is_lat, jnp.logical_not(lat_last)), edge_b[24:25, LANES:], zero))

    for gi in range(FRONT_GROUPS):
        h = pre(x[gi * TM:(gi + 1) * TM, :])
        _front_group(gi, h, halo_prev[gi], halo_next[gi], is_lat, win_ref, waug_ref, refs, outs)


def _front(x, mod3, norm_g, w_in_bf, w_aug, consts):
    tok = lambda t: (t, 0)
    rows8 = FRONT_TM // 8
    last8 = N_TOK // 8 - 1
    assert len(consts) == N_FRONT_CONSTS
    out_rb = jax.ShapeDtypeStruct((N_TOK, D_RWKV), BF16)
    out_db = jax.ShapeDtypeStruct((N_TOK, D_MODEL), BF16)
    out_e = jax.ShapeDtypeStruct((N_TILES, N_CHUNK, D_RWKV), F32)
    spec_r = pl.BlockSpec((FRONT_TM, D_RWKV), tok)
    spec_d = pl.BlockSpec((FRONT_TM, D_MODEL), tok)
    spec_e = pl.BlockSpec((FRONT_GROUPS, N_CHUNK, D_RWKV), lambda t: (t, 0, 0))
    n_b = 2 * N_DIR_OPS + 1
    return pl.pallas_call(
        _front_kernel,
        grid=(N_TOK // FRONT_TM,),
        in_specs=[pl.BlockSpec((FRONT_TM, D_MODEL), tok),
                  pl.BlockSpec((8, D_MODEL), lambda t: (jnp.maximum(t * rows8 - 1, 0), 0)),
                  pl.BlockSpec((8, D_MODEL), lambda t: (jnp.minimum((t + 1) * rows8, last8), 0)),
                  _const_spec(mod3.shape)]
                 + [_const_spec(c.shape) for c in [norm_g, w_in_bf, w_aug] + list(consts)],
        out_specs=[spec_r] * n_b + [spec_e] * 2 + [spec_r] * 2 + [spec_d] * 2,
        out_shape=[out_rb] * n_b + [out_e] * 2 + [out_rb] * 2 + [out_db] * 2,
        scratch_shapes=[pltpu.VMEM((D_CONV, D_MODEL), BF16)],
        compiler_params=_params(),
        name="mixer_front",
    )(x, x, x, mod3, norm_g, w_in_bf, w_aug, *consts)


def _stack(x):
    lane_lo = lax.broadcasted_iota(jnp.int32, x.shape, 1) < HEAD
    z = jnp.zeros_like(x)
    return jnp.concatenate([jnp.where(lane_lo, x, z), jnp.where(lane_lo, z, x)], axis=0)


def _scan_kernel(*refs):
    ops = (refs[0:N_DIR_OPS + 1], refs[N_DIR_OPS + 1:2 * N_DIR_OPS + 2])
    tail = refs[2 * N_DIR_OPS + 2:]
    ends0_ref, ends1_ref, s0_ref = tail[:3]
    side_in = tail[3:3 + N_SCAN_SIDE]
    yf_ref, yb_ref, sout_ref = tail[3 + N_SCAN_SIDE:6 + N_SCAN_SIDE]
    side_out = tail[6 + N_SCAN_SIDE:6 + 2 * N_SCAN_SIDE]
    st_ref = tail[-1]
    ends_ref = (ends0_ref, ends1_ref)
    for src, dst in zip(side_in, side_out):
        dst[...] = src[...].astype(BF16)
    step_id = pl.program_id(0)
    is_lat = step_id >= N_SCAN_CTX_STEPS
    lat_first = (step_id - N_SCAN_CTX_STEPS) % SCAN_LAT_STEPS == 0

    @pl.when(jnp.logical_and(is_lat, lat_first))
    def _():
        zero = jnp.zeros((HEAD, HEAD), F32)
        for d in (0, 1):
            for p in range(N_PAIR):
                top = jnp.concatenate([s0_ref[0, d, 2 * p], zero], axis=1)
                bottom = jnp.concatenate([zero, s0_ref[0, d, 2 * p + 1]], axis=1)
                st_ref[d, p] = jnp.concatenate([top, bottom], axis=0).T

    ti = lax.broadcasted_iota(jnp.int32, (CHUNK, LANES), 0)
    sj = lax.broadcasted_iota(jnp.int32, (CHUNK, LANES), 1) & (CHUNK - 1)
    m_strict = (sj < ti, sj > ti)
    m_incl = (sj <= ti, sj >= ti)
    eye_cat = jnp.where(sj == ti, 1.0, 0.0)
    bi = lax.broadcasted_iota(jnp.int32, (LANES, LANES), 0)
    bj = lax.broadcasted_iota(jnp.int32, (LANES, LANES), 1)
    blk = (bi >> HEAD_SHIFT) == (bj >> HEAD_SHIFT)
    eye_bd = bi == bj
    zero_bd = jnp.zeros((LANES, LANES), BF16)

    def op(i, u):
        d, p, c = u
        return ops[d][i][c * CHUNK:(c + 1) * CHUNK, p * LANES:(p + 1) * LANES]

    g_mat, w_mat, pg, q_mat = {}, {}, {}, {}

    def local_stages(units):
        low, nak, mrbk = {}, {}, {}
        for u in units:
            lhs = jnp.concatenate([op(OP_A, u), op(OP_R, u)], axis=0)
            rhs = jnp.concatenate([_stack(op(OP_B, u)), _stack(op(OP_K, u))], axis=0)
            gram = _bdot(lhs, rhs, NT)
            d = u[0]
            low[u] = jnp.where(m_strict[d], gram[:CHUNK, :LANES], 0.0)
            nak[u] = jnp.where(m_strict[d], gram[:CHUNK, LANES:], 0.0).astype(BF16)
            mrbk[u] = jnp.concatenate([jnp.where(m_incl[d], gram[CHUNK:, :LANES], 0.0),
                                       jnp.where(m_incl[d], gram[CHUNK:, LANES:], 0.0)], axis=1).astype(BF16)

        inv = {u: eye_cat + low[u] for u in units}
        pwb = {u: low[u].astype(BF16) for u in units}
        for u in units:
            pwb[u] = _bdot(pwb[u], _stack(pwb[u])).astype(BF16)
        for _ in range(4):
            for u in units:
                both = _bdot(pwb[u], jnp.concatenate([_stack(pwb[u]), _stack(inv[u].astype(BF16))], axis=1))
                pwb[u] = both[:, :LANES].astype(BF16)
                inv[u] = inv[u] + both[:, LANES:]
        for u in units:
            inv[u] = inv[u] + _bdot(pwb[u], _stack(inv[u].astype(BF16)))

        sv = {u: _stack(op(OP_V, u)) for u in units}
        nv = {}
        for u in units:
            nv[u] = _bdot(nak[u], sv[u]).astype(BF16)
        x1, x2 = {}, {}
        for u in units:
            x12 = _bdot(inv[u].astype(BF16), jnp.concatenate([_stack(op(OP_A, u)), _stack(nv[u])], axis=1))
            x1[u] = x12[:, :LANES].astype(BF16)
            x2[u] = x12[:, LANES:].astype(BF16)
        for u in units:
            rhs = jnp.concatenate([jnp.concatenate([_stack(x1[u]), _stack(x2[u])], axis=1),
                                   jnp.concatenate([zero_bd, sv[u]], axis=1)], axis=0)
            gw = _bdot(mrbk[u], rhs)
            g_mat[u] = (op(OP_R, u).astype(F32) + gw[:, :LANES]).astype(BF16)
            w_mat[u] = gw[:, LANES:]
        for u in units:
            d, p, c = u
            ct = c % N_CHUNK
            gam = jnp.exp(ends_ref[d][c // N_CHUNK, ct:ct + 1, p * LANES:(p + 1) * LANES])
            rhs = jnp.concatenate([jnp.concatenate([x1[u], x2[u]], axis=1),
                                   jnp.concatenate([jnp.zeros_like(x1[u]), op(OP_V, u)], axis=1)], axis=0)
            pq = _bdot(jnp.concatenate([op(OP_BH, u), op(OP_KH, u)], axis=0), rhs, TN)
            p_mat = (jnp.where(blk, pq[:, :LANES], 0.0) + jnp.where(eye_bd, gam, 0.0)).astype(BF16)
            pg[u] = jnp.concatenate([p_mat, g_mat[u]], axis=0)
            q_mat[u] = jnp.where(blk, pq[:, LANES:], 0.0)

    n_c = SCAN_TILES * N_CHUNK
    state = {(d, p): jnp.where(is_lat, st_ref[d, p], 0.0) for d in (0, 1) for p in range(N_PAIR)}
    finals = {}
    y_refs = (yf_ref, yb_ref)

    def chunk_of(d, step):
        return step if d == 0 else n_c - 1 - step

    def chain(step, d, p):
        c = chunk_of(d, step)
        u = (d, p, c)
        both = _bdot(pg[u], state[d, p].astype(BF16))
        y_refs[d][c * CHUNK:(c + 1) * CHUNK, p * LANES:(p + 1) * LANES] = both[LANES:, :] + w_mat[u]
        state[d, p] = both[:LANES, :] + q_mat[u]
        if (step + 1) % N_CHUNK == 0:
            finals[d, c // N_CHUNK, p] = state[d, p]
            if step + 1 < n_c:
                state[d, p] = jnp.where(is_lat, state[d, p], 0.0)

    local_stages([(d, p, c) for d in (0, 1) for p in range(N_PAIR) for c in range(n_c)])
    for step in range(n_c):
        for d in (0, 1):
            for p in range(N_PAIR):
                chain(step, d, p)
    for (d, p), s in state.items():
        st_ref[d, p] = s

    @pl.when(jnp.logical_not(is_lat))
    def _():
        for (d, tile, p), h in finals.items():
            s = h.T
            sout_ref[tile, 0, d, 2 * p] = s[:HEAD, :HEAD]
            sout_ref[tile, 0, d, 2 * p + 1] = s[HEAD:, HEAD:]


def _scan(ops_f, ops_b, v, ends_f, ends_b, s0, side):
    assert len(side) == N_SCAN_SIDE
    def mirror(s):
        u = s - N_SCAN_CTX_STEPS
        return jnp.where(s < N_SCAN_CTX_STEPS, s,
                         N_SCAN_CTX_STEPS + (u // SCAN_LAT_STEPS) * SCAN_LAT_STEPS
                         + (SCAN_LAT_STEPS - 1 - u % SCAN_LAT_STEPS))

    def lat_seq(s):
        return jnp.maximum(s - N_SCAN_CTX_STEPS, 0) // SCAN_LAT_STEPS

    fwd = pl.BlockSpec((SCAN_TM, D_RWKV), lambda s: (s, 0))
    bwd = pl.BlockSpec((SCAN_TM, D_RWKV), lambda s: (mirror(s), 0))
    e_block = (SCAN_TILES, N_CHUNK, D_RWKV)
    st_block = (None, 1, 2, 2 * N_PAIR, HEAD, HEAD)
    out_y = jax.ShapeDtypeStruct((N_TOK, D_RWKV), F32)
    n_steps = N_TOK // SCAN_TM
    return pl.pallas_call(
        _scan_kernel,
        grid=(n_steps,),
        in_specs=[fwd] * (N_DIR_OPS + 1) + [bwd] * (N_DIR_OPS + 1)
                 + [pl.BlockSpec(e_block, lambda s: (s, 0, 0)),
                    pl.BlockSpec(e_block, lambda s: (mirror(s), 0, 0)),
                    pl.BlockSpec(st_block, lambda s: (lat_seq(s), 0, 0, 0, 0, 0))]
                 + [_slab_spec(w, n_steps) for w in side],
        out_specs=[fwd, bwd,
                   pl.BlockSpec((SCAN_TILES, 1, 2, 2 * N_PAIR, HEAD, HEAD),
                                lambda s: (jnp.minimum(s, N_SCAN_CTX_STEPS - 1), 0, 0, 0, 0, 0))]
                  + [_slab_spec(w, n_steps) for w in side],
        out_shape=[out_y, out_y,
                   jax.ShapeDtypeStruct((N_CTX_SEQ, 1, 2, 2 * N_PAIR, HEAD, HEAD), F32)]
                  + [jax.ShapeDtypeStruct(w.shape, BF16) for w in side],
        scratch_shapes=[pltpu.VMEM((2, N_PAIR, LANES, LANES), F32)],
        compiler_params=_params(),
        name="rwkv7_scan",
    )(*ops_f, v, *ops_b, v, ends_f, ends_b, s0, *side)


def _back_ffn_kernel(x_ref, yf_ref, yb_ref, bv_ref, sg_ref, sa_ref, mb_ref, mod_ref, g_ref,
                     gng_ref, gnb_ref, wba_ref, wout_ref, ones_ref, w13_ref, w2_ref, oc_ref, ol_ref):
    t = pl.program_id(0)
    tile = t * (FFN_TM // TM)
    gate, = _mod_chunks(mod_ref, tile, MOD_BACK, 1)
    ones = ones_ref[...]
    groups = [slice(i * TM, (i + 1) * TM) for i in range(FFN_TM // TM)]
    ys = [yf_ref[r, :] + yb_ref[r, :] for r in groups]
    ycs = [y - _head_sum(y, ones) * (1.0 / HEAD) for y in ys]
    vs = [_head_sum(yc * yc, ones) * (1.0 / HEAD) for yc in ycs]
    yns = [yc * lax.rsqrt(v + EPS_GN) * gng_ref[...] + gnb_ref[...] for yc, v in zip(ycs, vs)]
    yas = [_dot((yn + bv_ref[r, :]) * sg_ref[r, :], wba_ref[...]) for r, yn in zip(groups, yns)]
    outs = [_dot(sa_ref[r, :] * ya + mb_ref[r, :], wout_ref[...]) for r, ya in zip(groups, yas)]
    x2 = jnp.concatenate([x_ref[r, :] + gate * _rms(out, g_ref[3:4, :]) for r, out in zip(groups, outs)], axis=0)

    mod = _mod_chunks(mod_ref, tile, MOD_FFN2, 3)
    out = _ffn_body(x2, mod, g_ref, w13_ref, w2_ref, 4)

    @pl.when(t < N_FFN_CTX_TILES)
    def _():
        oc_ref[...] = out

    @pl.when(t >= N_FFN_CTX_TILES)
    def _():
        ol_ref[...] = out


def _back_ffn(x, yf, yb, bv, sg, sa, mb, mod3, norm_g, gng, gnb, wba_bf, wout_bf, ones, w13_bf, w2_bf):
    tok = lambda t: (t, 0)
    spec_r = pl.BlockSpec((FFN_TM, D_RWKV), tok)
    spec_d = pl.BlockSpec((FFN_TM, D_MODEL), tok)
    consts = [mod3, norm_g, gng, gnb, wba_bf, wout_bf, ones, w13_bf, w2_bf]
    n_ctx = N_CTX_SEQ * CTX_LEN
    return pl.pallas_call(
        _back_ffn_kernel,
        grid=(N_FFN_TILES,),
        in_specs=[spec_d, spec_r, spec_r, spec_r, spec_r, spec_d, spec_d]
                 + [_const_spec(c.shape) for c in consts],
        out_specs=[pl.BlockSpec((FFN_TM, D_MODEL), _ctx_tile),
                   pl.BlockSpec((FFN_TM, D_MODEL), _lat_tile)],
        out_shape=[jax.ShapeDtypeStruct((n_ctx, D_MODEL), F32),
                   jax.ShapeDtypeStruct((N_TOK - n_ctx, D_MODEL), F32)],
        compiler_params=_params(),
        name="back_ffn2",
    )(x, yf, yb, bv, sg, sa, mb, *consts)


def kernel(x_prompt, x_sample, c, state_rwkv, c_ctx, w_mod, b_mod, norm_g, ffn1_w13, ffn1_w2,
           ffn2_w13, ffn2_w2, w_in, mu_shift, decay_w0, decay_w1, decay_w2, iclr_a0, iclr_a1,
           iclr_a2, k_k, k_a, r_k, gn_gain, gn_bias, conv_w, conv_b, w_branch_a, w_branch_b, w_out):
    assert x_prompt.shape == (N_CTX_SEQ, CTX_LEN, D_MODEL) and x_sample.shape == (N_LAT_SEQ, LAT_LEN, D_MODEL)
    assert w_mod.shape[0] == 1, "single trunk layer"

    g = norm_g[0]
    x, w_in_bf, mod3 = _ffn_first(x_prompt.reshape(-1, D_MODEL), x_sample.reshape(-1, D_MODEL), g,
                                  ffn1_w13[0], ffn1_w2[0], w_in[0], c_ctx.reshape(1, D_MODEL), c,
                                  w_mod[0], b_mod)

    row = lambda p: p.reshape(1, -1)
    w_aug, w2_aug, b2_aug = _lora_prep(mu_shift[0].reshape(4, D_MODEL).T, decay_w1[0], iclr_a1[0],
                                       decay_w2[0], iclr_a2[0], decay_w0[0], iclr_a0[0])
    ones = _head_ones()
    front = _front(x, mod3, g, w_in_bf, w_aug,
                   [w2_aug, b2_aug, row(k_k[0]), row(k_a[0]), row(r_k[0]), conv_w[0], row(conv_b[0]),
                    w_branch_b[0], ones])
    ops_f, ops_b = front[0:N_DIR_OPS], front[N_DIR_OPS:2 * N_DIR_OPS]
    v, ends_f, ends_b, bv, sg, sa, mb = front[2 * N_DIR_OPS:]

    yf, yb, s_fin, w13_bf, w2_bf, wba_bf, wout_bf = _scan(
        ops_f, ops_b, v, ends_f, ends_b, state_rwkv, [ffn2_w13[0], ffn2_w2[0], w_branch_a[0], w_out[0]])

    y_ctx, y_lat = _back_ffn(x, yf, yb, bv, sg, sa, mb, mod3, g, row(gn_gain[0]), row(gn_bias[0]),
                             wba_bf, wout_bf, ones, w13_bf, w2_bf)

    y_prompt = y_ctx.reshape(N_CTX_SEQ, CTX_LEN, D_MODEL)
    y_sample = y_lat.reshape(N_LAT_SEQ, LAT_LEN, D_MODEL)
    return y_prompt, y_sample, s_fin
```

```python
import jax
import jax.numpy as jnp
from jax import lax
from jax.experimental import pallas as pl
from jax.experimental.pallas import tpu as pltpu

F32 = jnp.float32
BF16 = jnp.bfloat16

D_MODEL = 1024
D_FF = 2816
D_RWKV = 512
D_CONV = 512
HEAD = 64
HEAD_SHIFT = 6
D_IN = 4 * D_RWKV + 3 * D_CONV + 2 * D_MODEL
N_MOD = 9
EPS_RMS = 1e-6
EPS_GN = 64e-5
HALF_STEP = 0.5
EXP_M05 = 0.6065306597126334

N_CTX_SEQ = 16
CTX_LEN = 256
N_LAT_SEQ = 2
LAT_LEN = 2048
GRID_W = 64
N_TOK = N_CTX_SEQ * CTX_LEN + N_LAT_SEQ * LAT_LEN

TM = 256
N_CTX_TILES = N_CTX_SEQ * CTX_LEN // TM
LAT_TILES = LAT_LEN // TM
N_TILES = N_TOK // TM
FRONT_TM = 512
FRONT_GROUPS = FRONT_TM // TM
N_FRONT_CTX_TILES = N_CTX_SEQ * CTX_LEN // FRONT_TM
FRONT_LAT_TILES = LAT_LEN // FRONT_TM
N_FRONT_CONSTS = 9
FRONT_WBB = 7
LORA = 64
SCAN_TM = 512
SCAN_TILES = SCAN_TM // TM
N_SCAN_CTX_STEPS = N_CTX_SEQ * CTX_LEN // SCAN_TM
SCAN_LAT_STEPS = LAT_LEN // SCAN_TM
FFN_TM = 512
N_FFN_TILES = N_TOK // FFN_TM
N_FFN_CTX_TILES = N_CTX_SEQ * CTX_LEN // FFN_TM
N_WCHUNK = 11
CHUNK = 64
N_CHUNK = TM // CHUNK
LANES = 128
N_PAIR = D_RWKV // LANES
MOD_ROWS = 8
MOD_HEAD = 3
MOD_TAIL = N_MOD - MOD_HEAD
MOD_SIDE = 512
N_MOD_SIDE = MOD_TAIL * D_MODEL // MOD_SIDE
N_MOD_HEAD_BLOCKS = MOD_HEAD * D_MODEL // MOD_SIDE
MOD_FRONT, MOD_BACK, MOD_FFN2 = 0, 2, 3
VMEM_LIMIT = 56 * 1024 * 1024

OP_A, OP_R, OP_B, OP_K, OP_BH, OP_KH, OP_V = range(7)
N_DIR_OPS = 6
N_SCAN_SIDE = 4

NN = (((1,), (0,)), ((), ()))
NT = (((1,), (1,)), ((), ()))
TN = (((0,), (0,)), ((), ()))


def _dot(a, b, dims=NN):
    return lax.dot_general(a.astype(BF16), b.astype(BF16), dims, preferred_element_type=F32)


def _bdot(a, b, dims=NN):
    return lax.dot_general(a, b, dims, preferred_element_type=F32)


def _rms(x, g):
    ms = jnp.mean(x * x, axis=-1, keepdims=True)
    return x * lax.rsqrt(ms + EPS_RMS) * g


def _head_ones():
    i = lax.broadcasted_iota(jnp.int32, (D_RWKV, D_RWKV), 0) >> HEAD_SHIFT
    j = lax.broadcasted_iota(jnp.int32, (D_RWKV, D_RWKV), 1) >> HEAD_SHIFT
    return jnp.where(i == j, 1.0, 0.0).astype(BF16)


def _head_sum(x, ones):
    return jnp.dot(x.astype(BF16), ones, preferred_element_type=F32)


def _sigmoid(x):
    return 0.5 * jnp.tanh(0.5 * x) + 0.5


def _mod_row(t):
    return jnp.where(t < N_CTX_TILES, 0, 1 + (t - N_CTX_TILES) // LAT_TILES)


def _mod_chunks(mod_ref, tile, first, count):
    row = pl.ds(_mod_row(tile), 1)
    return [mod_ref[row, (first + i) * D_MODEL:(first + i + 1) * D_MODEL] for i in range(count)]


def _const_spec(shape):
    nd = len(shape)
    return pl.BlockSpec(shape, lambda *_: (0,) * nd, pipeline_mode=pl.Buffered(1))


def _params(n_axes=1):
    return pltpu.CompilerParams(dimension_semantics=("arbitrary",) * n_axes,
                                vmem_limit_bytes=VMEM_LIMIT)


def _mod_inputs(cctx_ref, c_ref, act_ref):
    def silu(c):
        return c * jax.nn.sigmoid(c)

    act_ref[0:1, :] = silu(cctx_ref[...])
    act_ref[1:1 + N_LAT_SEQ, :] = silu(c_ref[...])
    act_ref[1 + N_LAT_SEQ:, :] = jnp.zeros((MOD_ROWS - 1 - N_LAT_SEQ, D_MODEL), F32)


def _mod_block(act_ref, w_ref, b_ref, o_ref):
    o_ref[...] = _dot(act_ref[...], w_ref[...]) + b_ref[...]


def _ffn_body(x, mod, g_ref, w13_ref, w2_ref, ig):
    shift, scale, gate = mod
    xs = [x[i * TM:(i + 1) * TM, :] for i in range(x.shape[0] // TM)]
    hs = [(_rms(xi, g_ref[ig:ig + 1, :]) * (1.0 + scale) + shift).astype(BF16) for xi in xs]
    gus = [_dot(hi, w13_ref[...]) for hi in hs]
    acts = []
    for gu in gus:
        gt = gu[:, :D_FF]
        up = gu[:, D_FF:]
        acts.append((gt * jax.nn.sigmoid(gt) * up).astype(BF16))
    os_ = [_dot(ai, w2_ref[...]) for ai in acts]
    outs = [xi + HALF_STEP * gate * _rms(oi, g_ref[ig + 1:ig + 2, :]) for xi, oi in zip(xs, os_)]
    return jnp.concatenate(outs, axis=0)


def _cast_chunk(step, src_ref, dst_ref, axis):
    size = src_ref.shape[axis]

    @pl.when(step < N_WCHUNK)
    def _():
        start = pl.multiple_of(step * size, size)
        if axis == 0:
            dst_ref[pl.ds(start, size), :] = src_ref[...].astype(BF16)
        else:
            dst_ref[:, pl.ds(start, size)] = src_ref[...].astype(BF16)


def _chunk_spec(w, axis):
    block = list(w.shape)
    block[axis] = w.shape[axis] // N_WCHUNK
    clamp = lambda s: jnp.minimum(s, N_WCHUNK - 1)
    index = (lambda s: (clamp(s), 0)) if axis == 0 else (lambda s: (0, clamp(s)))
    return pl.BlockSpec(tuple(block), index)


def _tile_step(s):
    return jnp.maximum(s - N_WCHUNK, 0)


def _slab_spec(w, n_steps, step=lambda s: s):
    rows = w.shape[0] // n_steps
    return pl.BlockSpec((rows, w.shape[1]), lambda s: (step(s), 0))


def _ffn_first_kernel(xc_ref, xl_ref, g_ref, w13_ref, w2_ref, win_ref, cctx_ref, c_ref, wmod_ref, bmod_ref,
                      o_ref, win_bf_ref, mod_tail_ref, w13_bf, w2_bf, mod_ref, act_ref):
    s = pl.program_id(0)
    _cast_chunk(s, w13_ref, w13_bf, 1)
    _cast_chunk(s, w2_ref, w2_bf, 0)

    @pl.when(s == 0)
    def _():
        _mod_inputs(cctx_ref, c_ref, act_ref)

    @pl.when(s < N_MOD_HEAD_BLOCKS)
    def _():
        col = pl.multiple_of(s * MOD_SIDE, MOD_SIDE)
        _mod_block(act_ref, wmod_ref, bmod_ref, mod_ref.at[:, pl.ds(col, MOD_SIDE)])

    @pl.when(s >= N_WCHUNK)
    def _():
        x = jnp.where(s - N_WCHUNK < N_FFN_CTX_TILES, xc_ref[...], xl_ref[...])
        mod = _mod_chunks(mod_ref, (s - N_WCHUNK) * (FFN_TM // TM), 0, 3)
        o_ref[...] = _ffn_body(x, mod, g_ref, w13_bf, w2_bf, 0)
        win_bf_ref[...] = win_ref[...].astype(BF16)

        @pl.when(s - N_WCHUNK < N_MOD_SIDE)
        def _():
            _mod_block(act_ref, wmod_ref, bmod_ref, mod_tail_ref)


def _ctx_tile(t):
    return jnp.minimum(t, N_FFN_CTX_TILES - 1), 0


def _lat_tile(t):
    return jnp.maximum(t - N_FFN_CTX_TILES, 0), 0


def _ffn_first(x_ctx, x_lat, norm_g, w13, w2, w_in, c_ctx, c, w_mod, b_mod):
    side = lambda s: jnp.minimum(_tile_step(s), N_MOD_SIDE - 1)

    def mod_block(s):
        return 0, jnp.where(s < N_WCHUNK, jnp.minimum(s, N_MOD_HEAD_BLOCKS - 1), N_MOD_HEAD_BLOCKS + side(s))

    return pl.pallas_call(
        _ffn_first_kernel,
        grid=(N_WCHUNK + N_FFN_TILES,),
        in_specs=[pl.BlockSpec((FFN_TM, D_MODEL), lambda s: _ctx_tile(_tile_step(s))),
                  pl.BlockSpec((FFN_TM, D_MODEL), lambda s: _lat_tile(_tile_step(s))),
                  _const_spec(norm_g.shape),
                  _chunk_spec(w13, 1),
                  _chunk_spec(w2, 0),
                  _slab_spec(w_in, N_FFN_TILES, step=_tile_step),
                  _const_spec(c_ctx.shape),
                  _const_spec(c.shape),
                  pl.BlockSpec((D_MODEL, MOD_SIDE), mod_block),
                  pl.BlockSpec((1, MOD_SIDE), mod_block)],
        out_specs=[pl.BlockSpec((FFN_TM, D_MODEL), lambda s: (_tile_step(s), 0)),
                   _slab_spec(w_in, N_FFN_TILES, step=_tile_step),
                   pl.BlockSpec((MOD_ROWS, MOD_SIDE), lambda s: (0, side(s)))],
        out_shape=[jax.ShapeDtypeStruct((N_TOK, D_MODEL), F32),
                   jax.ShapeDtypeStruct(w_in.shape, BF16),
                   jax.ShapeDtypeStruct((MOD_ROWS, MOD_TAIL * D_MODEL), F32)],
        scratch_shapes=[pltpu.VMEM((D_MODEL, 2 * D_FF), BF16), pltpu.VMEM((D_FF, D_MODEL), BF16),
                        pltpu.VMEM((MOD_ROWS, MOD_HEAD * D_MODEL), F32),
                        pltpu.VMEM((MOD_ROWS, D_MODEL), F32)],
        compiler_params=_params(),
        name="ffn1",
    )(x_ctx, x_lat, norm_g, w13, w2, w_in, c_ctx, c, w_mod, b_mod)


def _lora_prep_kernel(mu_ref, dw1_ref, ia1_ref, dw2_ref, ia2_ref, dw0_ref, ia0_ref,
                      w1_o, w2_o, b2_o):
    firsts = (dw1_ref[0], ia1_ref[0], dw1_ref[1], ia1_ref[1])
    seconds = (dw2_ref[0], ia2_ref[0], dw2_ref[1], ia2_ref[1])
    biases = (dw0_ref[0:1, :], ia0_ref[0:1, :], dw0_ref[1:2, :], ia0_ref[1:2, :])
    n = len(firsts) * LORA
    w2_o[...] = jnp.zeros(w2_o.shape, BF16)
    for j in range(len(firsts)):
        mu = mu_ref[:, j:j + 1]
        w1_o[:, j * LORA:(j + 1) * LORA] = ((1.0 - mu) * firsts[j]).astype(BF16)
        w1_o[:, n + j * LORA:n + (j + 1) * LORA] = (mu * firsts[j]).astype(BF16)
        w2_o[j * LORA:(j + 1) * LORA, j * D_RWKV:(j + 1) * D_RWKV] = seconds[j].astype(BF16)
        b2_o[:, j * D_RWKV:(j + 1) * D_RWKV] = biases[j]


def _lora_prep(mu_t, dw1, ia1, dw2, ia2, dw0, ia0):
    n = 4 * LORA
    return pl.pallas_call(
        _lora_prep_kernel,
        out_shape=[jax.ShapeDtypeStruct((D_MODEL, 2 * n), BF16),
                   jax.ShapeDtypeStruct((n, 4 * D_RWKV), BF16),
                   jax.ShapeDtypeStruct((1, 4 * D_RWKV), F32)],
        compiler_params=pltpu.CompilerParams(vmem_limit_bytes=VMEM_LIMIT),
        name="lora_prep",
    )(mu_t, dw1, ia1, dw2, ia2, dw0, ia0)


def _front_group(gi, h, halo_prev, halo_next, is_lat, win_ref, waug_ref, refs, outs):
    w2aug_ref, b2aug_ref, kkw_ref, ka_ref, rk_ref, cw_ref, cbias_ref, wbb_ref, ones_ref = refs
    ops_o, v_o, ends_o, bv_o, sg_o, sa_o, mb_o = outs
    rows = slice(gi * TM, (gi + 1) * TM)
    row = lax.broadcasted_iota(jnp.int32, (TM, 1), 0)
    rin = row & (CHUNK - 1)
    ones = ones_ref[...]

    def proj(lo, hi):
        return _bdot(h, win_ref[:, lo:hi])

    pab = _bdot(h, waug_ref[...])
    rk = proj(0, 2 * D_RWKV)
    r = rk[:, :D_RWKV]
    k = rk[:, D_RWKV:]
    pb = pab[:, 2 * LANES:]
    sh_f = jnp.where(row == 0, halo_prev, pltpu.roll(pb[:, :LANES], 1, 0))
    sh_b = jnp.where(row == TM - 1, halo_next, pltpu.roll(pb[:, LANES:], TM - 1, 0))
    t_in = pab[:, :2 * LANES] + jnp.concatenate([sh_f, sh_b], axis=1)
    lane = lax.broadcasted_iota(jnp.int32, (1, 2 * LANES), 1)
    t_in = jnp.where((lane & HEAD) == 0, jnp.tanh(t_in), t_in)
    za = _dot(t_in, w2aug_ref[...]) + b2aug_ref[...]

    kk = k * kkw_ref[...]
    kkn = kk * lax.rsqrt(_head_sum(kk * kk, ones) + 1e-12)
    ka = ka_ref[...]

    vg = proj(2 * D_RWKV, 4 * D_RWKV)
    v = vg[:, :D_RWKV]
    v_o[rows, :] = v.astype(BF16)
    sg_o[rows, :] = _sigmoid(vg[:, D_RWKV:]).astype(BF16)

    def scan_operands(d):
        lw = -EXP_M05 * _sigmoid(za[:, 2 * d * D_RWKV:(2 * d + 1) * D_RWKV])
        a = _sigmoid(za[:, (2 * d + 1) * D_RWKV:(2 * d + 2) * D_RWKV])
        k_d = k * (1.0 + (a - 1.0) * ka)
        b = kkn * a
        cs = lw
        for s in (1, 2, 4, 8, 16, 32):
            if d == 0:
                cs = cs + jnp.where(rin >= s, pltpu.roll(cs, s, 0), 0.0)
            else:
                cs = cs + jnp.where(rin < CHUNK - s, pltpu.roll(cs, TM - s, 0), 0.0)
        end_row = CHUNK - 1 if d == 0 else 0
        ends = [cs[c * CHUNK + end_row:c * CHUNK + end_row + 1, :] for c in range(N_CHUNK)]
        for c in range(N_CHUNK):
            ends_o[d][gi, c:c + 1, :] = ends[c]
        dec_end = jnp.concatenate([jnp.broadcast_to(jnp.exp(e), (CHUNK, D_RWKV)) for e in ends], axis=0)
        dec_inv = jnp.exp(-cs)
        b_inv = b * dec_inv
        k_inv = k_d * dec_inv
        o_a, o_r, o_b, o_k, o_bh, o_kh = ops_o[d]
        o_a[rows, :] = (-kkn * jnp.exp(cs - lw)).astype(BF16)
        o_r[rows, :] = (r * jnp.exp(cs)).astype(BF16)
        o_b[rows, :] = b_inv.astype(BF16)
        o_k[rows, :] = k_inv.astype(BF16)
        o_bh[rows, :] = (b_inv * dec_end).astype(BF16)
        o_kh[rows, :] = (k_inv * dec_end).astype(BF16)
        return k_d

    conv_in = proj(4 * D_RWKV, 4 * D_RWKV + 3 * D_CONV)
    k_0 = scan_operands(0)
    gate_a = proj(4 * D_RWKV + 3 * D_CONV, 4 * D_RWKV + 3 * D_CONV + D_MODEL)
    sa_o[rows, :] = _sigmoid(gate_a).astype(BF16)
    k_1 = scan_operands(1)
    gate_b = proj(4 * D_RWKV + 3 * D_CONV + D_MODEL, D_IN)

    cgate = conv_in[:, :D_CONV]
    u = conv_in[:, D_CONV:2 * D_CONV] * conv_in[:, 2 * D_CONV:]
    col = row & (GRID_W - 1)
    zl = jnp.logical_or(row == 0, jnp.logical_and(is_lat, col == 0))
    zr = jnp.logical_or(row == TM - 1, jnp.logical_and(is_lat, col == GRID_W - 1))
    left = jnp.where(zl, 0.0, pltpu.roll(u, 1, 0))
    right = jnp.where(zr, 0.0, pltpu.roll(u, TM - 1, 0))
    conv = left * cw_ref[0:1, :] + u * cw_ref[1:2, :] + right * cw_ref[2:3, :] + cbias_ref[...]
    y_b = _dot(cgate * conv, wbb_ref[...])
    mb_o[rows, :] = (_sigmoid(gate_b) * y_b).astype(BF16)

    bv_o[rows, :] = (_head_sum(r * (k_0 + k_1) * rk_ref[...], ones) * v).astype(BF16)


def _front_kernel(x_ref, xp_ref, xn_ref, mod_ref, g_ref, win_ref, waug_ref, *rest):
    refs = list(rest[:N_FRONT_CONSTS])
    outs = rest[N_FRONT_CONSTS:-1]
    wbb_bf = rest[-1]
    ops_o = (outs[0:N_DIR_OPS], outs[N_DIR_OPS:2 * N_DIR_OPS])
    v_o, ends0_o, ends1_o, bv_o, sg_o, sa_o, mb_o = outs[2 * N_DIR_OPS:]
    outs = (ops_o, v_o, (ends0_o, ends1_o), bv_o, sg_o, sa_o, mb_o)

    t = pl.program_id(0)

    @pl.when(t == 0)
    def _():
        wbb_bf[...] = refs[FRONT_WBB][...].astype(BF16)

    refs[FRONT_WBB] = wbb_bf

    is_lat = t >= N_FRONT_CTX_TILES
    i_in = (t - N_FRONT_CTX_TILES) % FRONT_LAT_TILES
    lat_first = jnp.logical_and(is_lat, i_in == 0)
    lat_last = jnp.logical_and(is_lat, i_in == FRONT_LAT_TILES - 1)
    shift, scale = _mod_chunks(mod_ref, t * FRONT_GROUPS, MOD_FRONT, 2)
    g2 = g_ref[2:3, :]

    def pre(x):
        return (_rms(x, g2) * (1.0 + scale) + shift).astype(BF16)

    x = x_ref[...]
    edge = jnp.concatenate([xp_ref[...], x[TM - 8:TM + 8, :], xn_ref[...]], axis=0)
    edge_b = _bdot(pre(edge), waug_ref[:, 2 * LANES:])
    zero = jnp.zeros((1, LANES), F32)
    halo_prev = (jnp.where(jnp.logical_and(is_lat, jnp.logical_not(lat_first)), edge_b[7:8, :LANES], zero),
                 jnp.where(is_lat, edge_b[15:16, :LANES], zero))
    halo_next = (jnp.where(is_lat, edge_b[16:17, LANES:], zero),
                 jnp.where(jnp.logical_and(is_lat, jnp.logical_not(lat_last)), edge_b[24:25, LANES:], zero))

    for gi in range(FRONT_GROUPS):
        h = pre(x[gi * TM:(gi + 1) * TM, :])
        _front_group(gi, h, halo_prev[gi], halo_next[gi], is_lat, win_ref, waug_ref, refs, outs)


def _front(x, mod3, norm_g, w_in_bf, w_aug, consts):
    tok = lambda t: (t, 0)
    rows8 = FRONT_TM // 8
    last8 = N_TOK // 8 - 1
    assert len(consts) == N_FRONT_CONSTS
    out_rb = jax.ShapeDtypeStruct((N_TOK, D_RWKV), BF16)
    out_db = jax.ShapeDtypeStruct((N_TOK, D_MODEL), BF16)
    out_e = jax.ShapeDtypeStruct((N_TILES, N_CHUNK, D_RWKV), F32)
    spec_r = pl.BlockSpec((FRONT_TM, D_RWKV), tok)
    spec_d = pl.BlockSpec((FRONT_TM, D_MODEL), tok)
    spec_e = pl.BlockSpec((FRONT_GROUPS, N_CHUNK, D_RWKV), lambda t: (t, 0, 0))
    n_b = 2 * N_DIR_OPS + 1
    return pl.pallas_call(
        _front_kernel,
        grid=(N_TOK // FRONT_TM,),
        in_specs=[pl.BlockSpec((FRONT_TM, D_MODEL), tok),
                  pl.BlockSpec((8, D_MODEL), lambda t: (jnp.maximum(t * rows8 - 1, 0), 0)),
                  pl.BlockSpec((8, D_MODEL), lambda t: (jnp.minimum((t + 1) * rows8, last8), 0)),
                  _const_spec(mod3.shape)]
                 + [_const_spec(c.shape) for c in [norm_g, w_in_bf, w_aug] + list(consts)],
        out_specs=[spec_r] * n_b + [spec_e] * 2 + [spec_r] * 2 + [spec_d] * 2,
        out_shape=[out_rb] * n_b + [out_e] * 2 + [out_rb] * 2 + [out_db] * 2,
        scratch_shapes=[pltpu.VMEM((D_CONV, D_MODEL), BF16)],
        compiler_params=_params(),
        name="mixer_front",
    )(x, x, x, mod3, norm_g, w_in_bf, w_aug, *consts)


def _stack(x):
    lane_lo = lax.broadcasted_iota(jnp.int32, x.shape, 1) < HEAD
    z = jnp.zeros_like(x)
    return jnp.concatenate([jnp.where(lane_lo, x, z), jnp.where(lane_lo, z, x)], axis=0)


def _scan_kernel(*refs):
    ops = (refs[0:N_DIR_OPS + 1], refs[N_DIR_OPS + 1:2 * N_DIR_OPS + 2])
    tail = refs[2 * N_DIR_OPS + 2:]
    ends0_ref, ends1_ref, s0_ref = tail[:3]
    side_in = tail[3:3 + N_SCAN_SIDE]
    yf_ref, yb_ref, sout_ref = tail[3 + N_SCAN_SIDE:6 + N_SCAN_SIDE]
    side_out = tail[6 + N_SCAN_SIDE:6 + 2 * N_SCAN_SIDE]
    st_ref = tail[-1]
    ends_ref = (ends0_ref, ends1_ref)
    for src, dst in zip(side_in, side_out):
        dst[...] = src[...].astype(BF16)
    step_id = pl.program_id(0)
    is_lat = step_id >= N_SCAN_CTX_STEPS
    lat_first = (step_id - N_SCAN_CTX_STEPS) % SCAN_LAT_STEPS == 0

    @pl.when(jnp.logical_and(is_lat, lat_first))
    def _():
        zero = jnp.zeros((HEAD, HEAD), F32)
        for d in (0, 1):
            for p in range(N_PAIR):
                top = jnp.concatenate([s0_ref[0, d, 2 * p], zero], axis=1)
                bottom = jnp.concatenate([zero, s0_ref[0, d, 2 * p + 1]], axis=1)
                st_ref[d, p] = jnp.concatenate([top, bottom], axis=0).T

    ti = lax.broadcasted_iota(jnp.int32, (CHUNK, LANES), 0)
    sj = lax.broadcasted_iota(jnp.int32, (CHUNK, LANES), 1) & (CHUNK - 1)
    m_strict = (sj < ti, sj > ti)
    m_incl = (sj <= ti, sj >= ti)
    eye_cat = jnp.where(sj == ti, 1.0, 0.0)
    bi = lax.broadcasted_iota(jnp.int32, (LANES, LANES), 0)
    bj = lax.broadcasted_iota(jnp.int32, (LANES, LANES), 1)
    blk = (bi >> HEAD_SHIFT) == (bj >> HEAD_SHIFT)
    eye_bd = bi == bj
    zero_bd = jnp.zeros((LANES, LANES), BF16)

    def op(i, u):
        d, p, c = u
        return ops[d][i][c * CHUNK:(c + 1) * CHUNK, p * LANES:(p + 1) * LANES]

    g_mat, w_mat, pg, q_mat = {}, {}, {}, {}

    def local_stages(units):
        low, nak, mrbk = {}, {}, {}
        for u in units:
            lhs = jnp.concatenate([op(OP_A, u), op(OP_R, u)], axis=0)
            rhs = jnp.concatenate([_stack(op(OP_B, u)), _stack(op(OP_K, u))], axis=0)
            gram = _bdot(lhs, rhs, NT)
            d = u[0]
            low[u] = jnp.where(m_strict[d], gram[:CHUNK, :LANES], 0.0)
            nak[u] = jnp.where(m_strict[d], gram[:CHUNK, LANES:], 0.0).astype(BF16)
            mrbk[u] = jnp.concatenate([jnp.where(m_incl[d], gram[CHUNK:, :LANES], 0.0),
                                       jnp.where(m_incl[d], gram[CHUNK:, LANES:], 0.0)], axis=1).astype(BF16)

        inv = {u: eye_cat + low[u] for u in units}
        pwb = {u: low[u].astype(BF16) for u in units}
        for u in units:
            pwb[u] = _bdot(pwb[u], _stack(pwb[u])).astype(BF16)
        for _ in range(4):
            for u in units:
                both = _bdot(pwb[u], jnp.concatenate([_stack(pwb[u]), _stack(inv[u].astype(BF16))], axis=1))
                pwb[u] = both[:, :LANES].astype(BF16)
                inv[u] = inv[u] + both[:, LANES:]
        for u in units:
            inv[u] = inv[u] + _bdot(pwb[u], _stack(inv[u].astype(BF16)))

        sv = {u: _stack(op(OP_V, u)) for u in units}
        nv = {}
        for u in units:
            nv[u] = _bdot(nak[u], sv[u]).astype(BF16)
        x1, x2 = {}, {}
        for u in units:
            x12 = _bdot(inv[u].astype(BF16), jnp.concatenate([_stack(op(OP_A, u)), _stack(nv[u])], axis=1))
            x1[u] = x12[:, :LANES].astype(BF16)
            x2[u] = x12[:, LANES:].astype(BF16)
        for u in units:
            rhs = jnp.concatenate([jnp.concatenate([_stack(x1[u]), _stack(x2[u])], axis=1),
                                   jnp.concatenate([zero_bd, sv[u]], axis=1)], axis=0)
            gw = _bdot(mrbk[u], rhs)
            g_mat[u] = (op(OP_R, u).astype(F32) + gw[:, :LANES]).astype(BF16)
            w_mat[u] = gw[:, LANES:]
        for u in units:
            d, p, c = u
            ct = c % N_CHUNK
            gam = jnp.exp(ends_ref[d][c // N_CHUNK, ct:ct + 1, p * LANES:(p + 1) * LANES])
            rhs = jnp.concatenate([jnp.concatenate([x1[u], x2[u]], axis=1),
                                   jnp.concatenate([jnp.zeros_like(x1[u]), op(OP_V, u)], axis=1)], axis=0)
            pq = _bdot(jnp.concatenate([op(OP_BH, u), op(OP_KH, u)], axis=0), rhs, TN)
            p_mat = (jnp.where(blk, pq[:, :LANES], 0.0) + jnp.where(eye_bd, gam, 0.0)).astype(BF16)
            pg[u] = jnp.concatenate([p_mat, g_mat[u]], axis=0)
            q_mat[u] = jnp.where(blk, pq[:, LANES:], 0.0)

    n_c = SCAN_TILES * N_CHUNK
    state = {(d, p): jnp.where(is_lat, st_ref[d, p], 0.0) for d in (0, 1) for p in range(N_PAIR)}
    finals = {}
    y_refs = (yf_ref, yb_ref)

    def chunk_of(d, step):
        return step if d == 0 else n_c - 1 - step

    def chain(step, d, p):
        c = chunk_of(d, step)
        u = (d, p, c)
        both = _bdot(pg[u], state[d, p].astype(BF16))
        y_refs[d][c * CHUNK:(c + 1) * CHUNK, p * LANES:(p + 1) * LANES] = both[LANES:, :] + w_mat[u]
        state[d, p] = both[:LANES, :] + q_mat[u]
        if (step + 1) % N_CHUNK == 0:
            finals[d, c // N_CHUNK, p] = state[d, p]
            if step + 1 < n_c:
                state[d, p] = jnp.where(is_lat, state[d, p], 0.0)

    local_stages([(d, p, c) for d in (0, 1) for p in range(N_PAIR) for c in range(n_c)])
    for step in range(n_c):
        for d in (0, 1):
            for p in range(N_PAIR):
                chain(step, d, p)
    for (d, p), s in state.items():
        st_ref[d, p] = s

    @pl.when(jnp.logical_not(is_lat))
    def _():
        for (d, tile, p), h in finals.items():
            s = h.T
            sout_ref[tile, 0, d, 2 * p] = s[:HEAD, :HEAD]
            sout_ref[tile, 0, d, 2 * p + 1] = s[HEAD:, HEAD:]


def _scan(ops_f, ops_b, v, ends_f, ends_b, s0, side):
    assert len(side) == N_SCAN_SIDE
    def mirror(s):
        u = s - N_SCAN_CTX_STEPS
        return jnp.where(s < N_SCAN_CTX_STEPS, s,
                         N_SCAN_CTX_STEPS + (u // SCAN_LAT_STEPS) * SCAN_LAT_STEPS
                         + (SCAN_LAT_STEPS - 1 - u % SCAN_LAT_STEPS))

    def lat_seq(s):
        return jnp.maximum(s - N_SCAN_CTX_STEPS, 0) // SCAN_LAT_STEPS

    fwd = pl.BlockSpec((SCAN_TM, D_RWKV), lambda s: (s, 0))
    bwd = pl.BlockSpec((SCAN_TM, D_RWKV), lambda s: (mirror(s), 0))
    e_block = (SCAN_TILES, N_CHUNK, D_RWKV)
    st_block = (None, 1, 2, 2 * N_PAIR, HEAD, HEAD)
    out_y = jax.ShapeDtypeStruct((N_TOK, D_RWKV), F32)
    n_steps = N_TOK // SCAN_TM
    return pl.pallas_call(
        _scan_kernel,
        grid=(n_steps,),
        in_specs=[fwd] * (N_DIR_OPS + 1) + [bwd] * (N_DIR_OPS + 1)
                 + [pl.BlockSpec(e_block, lambda s: (s, 0, 0)),
                    pl.BlockSpec(e_block, lambda s: (mirror(s), 0, 0)),
                    pl.BlockSpec(st_block, lambda s: (lat_seq(s), 0, 0, 0, 0, 0))]
                 + [_slab_spec(w, n_steps) for w in side],
        out_specs=[fwd, bwd,
                   pl.BlockSpec((SCAN_TILES, 1, 2, 2 * N_PAIR, HEAD, HEAD),
                                lambda s: (jnp.minimum(s, N_SCAN_CTX_STEPS - 1), 0, 0, 0, 0, 0))]
                  + [_slab_spec(w, n_steps) for w in side],
        out_shape=[out_y, out_y,
                   jax.ShapeDtypeStruct((N_CTX_SEQ, 1, 2, 2 * N_PAIR, HEAD, HEAD), F32)]
                  + [jax.ShapeDtypeStruct(w.shape, BF16) for w in side],
        scratch_shapes=[pltpu.VMEM((2, N_PAIR, LANES, LANES), F32)],
        compiler_params=_params(),
        name="rwkv7_scan",
    )(*ops_f, v, *ops_b, v, ends_f, ends_b, s0, *side)


def _back_ffn_kernel(x_ref, yf_ref, yb_ref, bv_ref, sg_ref, sa_ref, mb_ref, mod_ref, g_ref,
                     gng_ref, gnb_ref, wba_ref, wout_ref, ones_ref, w13_ref, w2_ref, oc_ref, ol_ref):
    t = pl.program_id(0)
    tile = t * (FFN_TM // TM)
    gate, = _mod_chunks(mod_ref, tile, MOD_BACK, 1)
    ones = ones_ref[...]
    groups = [slice(i * TM, (i + 1) * TM) for i in range(FFN_TM // TM)]
    ys = [yf_ref[r, :] + yb_ref[r, :] for r in groups]
    ycs = [y - _head_sum(y, ones) * (1.0 / HEAD) for y in ys]
    vs = [_head_sum(yc * yc, ones) * (1.0 / HEAD) for yc in ycs]
    yns = [yc * lax.rsqrt(v + EPS_GN) * gng_ref[...] + gnb_ref[...] for yc, v in zip(ycs, vs)]
    yas = [_dot((yn + bv_ref[r, :]) * sg_ref[r, :], wba_ref[...]) for r, yn in zip(groups, yns)]
    outs = [_dot(sa_ref[r, :] * ya + mb_ref[r, :], wout_ref[...]) for r, ya in zip(groups, yas)]
    x2 = jnp.concatenate([x_ref[r, :] + gate * _rms(out, g_ref[3:4, :]) for r, out in zip(groups, outs)], axis=0)

    mod = _mod_chunks(mod_ref, tile, MOD_FFN2, 3)
    out = _ffn_body(x2, mod, g_ref, w13_ref, w2_ref, 4)

    @pl.when(t < N_FFN_CTX_TILES)
    def _():
        oc_ref[...] = out

    @pl.when(t >= N_FFN_CTX_TILES)
    def _():
        ol_ref[...] = out


def _back_ffn(x, yf, yb, bv, sg, sa, mb, mod3, norm_g, gng, gnb, wba_bf, wout_bf, ones, w13_bf, w2_bf):
    tok = lambda t: (t, 0)
    spec_r = pl.BlockSpec((FFN_TM, D_RWKV), tok)
    spec_d = pl.BlockSpec((FFN_TM, D_MODEL), tok)
    consts = [mod3, norm_g, gng, gnb, wba_bf, wout_bf, ones, w13_bf, w2_bf]
    n_ctx = N_CTX_SEQ * CTX_LEN
    return pl.pallas_call(
        _back_ffn_kernel,
        grid=(N_FFN_TILES,),
        in_specs=[spec_d, spec_r, spec_r, spec_r, spec_r, spec_d, spec_d]
                 + [_const_spec(c.shape) for c in consts],
        out_specs=[pl.BlockSpec((FFN_TM, D_MODEL), _ctx_tile),
                   pl.BlockSpec((FFN_TM, D_MODEL), _lat_tile)],
        out_shape=[jax.ShapeDtypeStruct((n_ctx, D_MODEL), F32),
                   jax.ShapeDtypeStruct((N_TOK - n_ctx, D_MODEL), F32)],
        compiler_params=_params(),
        name="back_ffn2",
    )(x, yf, yb, bv, sg, sa, mb, *consts)


def kernel(x_prompt, x_sample, c, state_rwkv, c_ctx, w_mod, b_mod, norm_g, ffn1_w13, ffn1_w2,
           ffn2_w13, ffn2_w2, w_in, mu_shift, decay_w0, decay_w1, decay_w2, iclr_a0, iclr_a1,
           iclr_a2, k_k, k_a, r_k, gn_gain, gn_bias, conv_w, conv_b, w_branch_a, w_branch_b, w_out):
    assert x_prompt.shape == (N_CTX_SEQ, CTX_LEN, D_MODEL) and x_sample.shape == (N_LAT_SEQ, LAT_LEN, D_MODEL)
    assert w_mod.shape[0] == 1, "single trunk layer"

    g = norm_g[0]
    x, w_in_bf, mod3 = _ffn_first(x_prompt.reshape(-1, D_MODEL), x_sample.reshape(-1, D_MODEL), g,
                                  ffn1_w13[0], ffn1_w2[0], w_in[0], c_ctx.reshape(1, D_MODEL), c,
                                  w_mod[0], b_mod)

    row = lambda p: p.reshape(1, -1)
    w_aug, w2_aug, b2_aug = _lora_prep(mu_shift[0].reshape(4, D_MODEL).T, decay_w1[0], iclr_a1[0],
                                       decay_w2[0], iclr_a2[0], decay_w0[0], iclr_a0[0])
    ones = _head_ones()
    front = _front(x, mod3, g, w_in_bf, w_aug,
                   [w2_aug, b2_aug, row(k_k[0]), row(k_a[0]), row(r_k[0]), conv_w[0], row(conv_b[0]),
                    w_branch_b[0], ones])
    ops_f, ops_b = front[0:N_DIR_OPS], front[N_DIR_OPS:2 * N_DIR_OPS]
    v, ends_f, ends_b, bv, sg, sa, mb = front[2 * N_DIR_OPS:]

    yf, yb, s_fin, w13_bf, w2_bf, wba_bf, wout_bf = _scan(
        ops_f, ops_b, v, ends_f, ends_b, state_rwkv, [ffn2_w13[0], ffn2_w2[0], w_branch_a[0], w_out[0]])

    y_ctx, y_lat = _back_ffn(x, yf, yb, bv, sg, sa, mb, mod3, g, row(gn_gain[0]), row(gn_bias[0]),
                             wba_bf, wout_bf, ones, w13_bf, w2_bf)

    y_prompt = y_ctx.reshape(N_CTX_SEQ, CTX_LEN, D_MODEL)
    y_sample = y_lat.reshape(N_LAT_SEQ, LAT_LEN, D_MODEL)
    return y_prompt, y_sample, s_fin
```
